```python
import math, functools
import jax, jax.numpy as jnp
from jax import lax
import numpy as np

D_MODEL = 1024
BATCH = 32
SEQ = 256
DEPTH = 1
DEC_BATCH = 8
DEC_SEQ = 2048
PAST_LEN = 512

GRID_W = 64
MIX_WIDTH = D_MODEL
HG_WIDTH = MIX_WIDTH // 2
HG_HEAD_DIM = 128
HG_HEADS = HG_WIDTH // HG_HEAD_DIM
ATT_WIDTH = MIX_WIDTH - HG_WIDTH
HEAD_DIM = 64
N_HEADS = ATT_WIDTH // HEAD_DIM
N_KV_HEADS = 2
KV_WIDTH = N_KV_HEADS * HEAD_DIM
IN_WIDTH = 5 * HG_WIDTH + ATT_WIDTH + 2 * KV_WIDTH
D_FF = 128 * ((8 * D_MODEL // 3 + 127) // 128)
CHUNK = 16
Q_BLOCK = 128
ROPE_THETA = 10000.0
N_MOD = 9
ALPHA = (2.0 * DEPTH) ** 0.25
BETA = (8.0 * DEPTH) ** -0.25
LN_EPS = 1e-6
RMS_EPS = 1e-6
f32 = jnp.float32

kernel_name = "hymba_hgrn2_gqa_macaron_deepnorm_prefix_dit_step"


def layer_norm(x, g, b):
    xf = x.astype(f32)
    mu = jnp.mean(xf, -1, keepdims=True)
    var = jnp.mean(jnp.square(xf - mu), -1, keepdims=True)
    return ((xf - mu) * lax.rsqrt(var + LN_EPS) * g + b).astype(x.dtype)


def rms_norm(x, g):
    xf = x.astype(f32)
    return (xf * lax.rsqrt(jnp.mean(xf * xf, -1, keepdims=True) + RMS_EPS) * g).astype(x.dtype)


def modulation(cvec, w_mod, b_mod):
    m = jax.nn.silu(cvec) @ w_mod + b_mod
    m = m.reshape(cvec.shape[:-1] + (N_MOD, D_MODEL))
    return [m[..., i, None, :] for i in range(N_MOD)]


def swiglu(h, w_up, w_down):
    a, u = jnp.split(h @ w_up, 2, axis=-1)
    return (jax.nn.silu(a) * u) @ w_down


def split_in(z):
    cuts = [HG_WIDTH, 2 * HG_WIDTH, 3 * HG_WIDTH, 4 * HG_WIDTH, 5 * HG_WIDTH,
            5 * HG_WIDTH + ATT_WIDTH, 5 * HG_WIDTH + ATT_WIDTH + KV_WIDTH]
    return jnp.split(z, cuts, axis=-1)


def axial_rope_angles(n_tokens):
    rows = n_tokens // GRID_W
    t_row = jnp.repeat(jnp.arange(rows), GRID_W).astype(f32)
    t_col = jnp.tile(jnp.arange(GRID_W), rows).astype(f32)
    half = HEAD_DIM // 2
    inv = ROPE_THETA ** (-jnp.arange(0, half, 2, dtype=f32) / half)
    return t_row[:, None] * inv, t_col[:, None] * inv


def rope_rotate(x, ang):
    x1, x2 = jnp.split(x, 2, axis=-1)
    cos = jnp.cos(ang)[:, None, :]
    sin = jnp.sin(ang)[:, None, :]
    return jnp.concatenate([x1 * cos - x2 * sin, x1 * sin + x2 * cos], axis=-1)


def apply_axial_rope(x, ang_row, ang_col):
    xr, xc = jnp.split(x.astype(f32), 2, axis=-1)
    return jnp.concatenate([rope_rotate(xr, ang_row), rope_rotate(xc, ang_col)], -1).astype(x.dtype)


def block_attention(q, k, v):
    b, lq, h, dh = q.shape
    nb = lq // Q_BLOCK
    grp = h // N_KV_HEADS
    qb = q.reshape(b, nb, Q_BLOCK, N_KV_HEADS, grp, dh).transpose(1, 0, 2, 3, 4, 5)
    scale = dh ** -0.5

    def one_block(qi):
        s = jnp.einsum("bqkgd,bskd->bkgqs", qi, k).astype(f32) * scale
        p = jax.nn.softmax(s, axis=-1).astype(v.dtype)
        return jnp.einsum("bkgqs,bskd->bqkgd", p, v)

    o = lax.map(one_block, qb)
    return o.transpose(1, 0, 2, 3, 4, 5).reshape(b, lq, h * dh)


def hgrn2_scan(q, k, log_f, v, s0):
    b, l, h, _ = q.shape
    n = l // CHUNK
    rs = lambda t: t.reshape(b, n, CHUNK, h, t.shape[-1])
    q, k, log_f, v = rs(q), rs(k), rs(log_f), rs(v)
    g = jnp.cumsum(log_f, axis=2)
    g_last = g[:, :, -1]
    mask = jnp.tril(jnp.ones((CHUNK, CHUNK), dtype=bool))
    rel = g[:, :, :, None] - g[:, :, None, :]
    decay = jnp.exp(jnp.where(mask[:, :, None, None], rel, -jnp.inf))
    a = jnp.einsum("bnihk,bnjhk,bnijhk->bnhij", q, k, decay)
    o_intra = jnp.einsum("bnhij,bnjhv->bnihv", a, v)
    k_end = k * jnp.exp(g_last[:, :, None] - g)
    upd = jnp.einsum("bnjhk,bnjhv->bnhkv", k_end, v)
    tot = jnp.exp(g_last)

    def step(s, inp):
        t, u = inp
        return t[..., None] * s + u, s

    s_fin, s_prev = lax.scan(step, s0, (jnp.moveaxis(tot, 1, 0), jnp.moveaxis(upd, 1, 0)))
    s_prev = jnp.moveaxis(s_prev, 0, 1)
    o_inter = jnp.einsum("bnihk,bnhkv->bnihv", q * jnp.exp(g), s_prev)
    return (o_intra + o_inter).reshape(b, l, h, -1), s_fin


def hgrn2_mixer(hq, hi, hf_fwd, hf_bwd, hgate, lb_fwd, lb_bwd, norm_g, s_fwd0, s_bwd0):
    b, l, _ = hq.shape
    heads = lambda t: t.reshape(b, l, HG_HEADS, HG_HEAD_DIM)
    q = heads(jax.nn.silu(hq.astype(f32)))
    v = heads(hi.astype(f32))

    def direction(hf, lb, s0, flip):
        f = lb + (1.0 - lb) * jax.nn.sigmoid(hf.astype(f32))
        args = (q, heads(1.0 - f), heads(jnp.log(f)), v)
        if flip:
            args = tuple(t[:, ::-1] for t in args)
        o, s = hgrn2_scan(*args, s0.astype(f32))
        if flip:
            o = o[:, ::-1]
        return o, s

    o_f, s_f = direction(hf_fwd, lb_fwd, s_fwd0, False)
    o_b, s_b = direction(hf_bwd, lb_bwd, s_bwd0, True)
    o = rms_norm(o_f + o_b, norm_g.reshape(HG_HEADS, HG_HEAD_DIM)).reshape(b, l, HG_WIDTH)
    o = o * jax.nn.silu(hgate.astype(f32))
    return o.astype(hq.dtype), s_f, s_b


def attn_proj(aq, ak, av, q_g, k_g):
    b, l, _ = aq.shape
    q = rms_norm(aq.reshape(b, l, N_HEADS, HEAD_DIM), q_g)
    k = rms_norm(ak.reshape(b, l, N_KV_HEADS, HEAD_DIM), k_g)
    v = av.reshape(b, l, N_KV_HEADS, HEAD_DIM)
    return q, k, v


def mixer_context(h, *, w_in, w_out, q_g, k_g, hg_g, lb_f, lb_b):
    hq, hi, hff, hfb, hgate, aq, ak, av = split_in(h @ w_in)
    s_zero = jnp.zeros((h.shape[0], HG_HEADS, HG_HEAD_DIM, HG_HEAD_DIM), f32)
    o_hg, s_f, s_b = hgrn2_mixer(hq, hi, hff, hfb, hgate, lb_f, lb_b, hg_g, s_zero, s_zero)
    q, k, v = attn_proj(aq, ak, av, q_g, k_g)
    o_att = block_attention(q, k, v)
    y = jnp.concatenate([o_hg, o_att.astype(o_hg.dtype)], axis=-1) @ w_out
    return y, (k, v, s_f, s_b)


def mixer_latent(h, *, k_ctx, v_ctx, s_f0, s_b0, w_in, w_out, q_g, k_g, hg_g, lb_f, lb_b):
    hq, hi, hff, hfb, hgate, aq, ak, av = split_in(h @ w_in)
    o_hg, _, _ = hgrn2_mixer(hq, hi, hff, hfb, hgate, lb_f, lb_b, hg_g, s_f0, s_b0)
    q, k, v = attn_proj(aq, ak, av, q_g, k_g)
    ang_row, ang_col = axial_rope_angles(h.shape[1])
    q = apply_axial_rope(q, ang_row, ang_col)
    k = apply_axial_rope(k, ang_row, ang_col)
    k_all = jnp.concatenate([k_ctx.astype(k.dtype), k], axis=1)
    v_all = jnp.concatenate([v_ctx.astype(v.dtype), v], axis=1)
    o_att = block_attention(q, k_all, v_all)
    y = jnp.concatenate([o_hg, o_att.astype(o_hg.dtype)], axis=-1) @ w_out
    return y, ()


def trunk_layer(x, cvec, mixer, w_mod, b_mod, w_f1_in, w_f1_out, w_f2_in, w_f2_out, ln_g, ln_b):
    sh1, sc1, g1, sh2, sc2, g2, sh3, sc3, g3 = modulation(cvec, w_mod, b_mod)
    x = layer_norm(ALPHA * x + 0.5 * g1 * swiglu(x * (1.0 + sc1) + sh1, w_f1_in, w_f1_out), ln_g[0], ln_b[0])
    y, aux = mixer(x * (1.0 + sc2) + sh2)
    x = layer_norm(ALPHA * x + g2 * y, ln_g[1], ln_b[1])
    x = layer_norm(ALPHA * x + 0.5 * g3 * swiglu(x * (1.0 + sc3) + sh3, w_f2_in, w_f2_out), ln_g[2], ln_b[2])
    return x, aux


def setup_inputs(seed: int = 0) -> dict:
    key = jax.random.key(seed)
    ks = jax.random.split(key, 24)
    nrm = lambda k, shape, scale: scale * jax.random.normal(k, shape, f32)
    d = D_MODEL
    col_scale = jnp.ones((IN_WIDTH,), f32)
    col_scale = col_scale.at[HG_WIDTH:2 * HG_WIDTH].set(BETA).at[IN_WIDTH - KV_WIDTH:].set(BETA)
    return {
        "x_prompt": nrm(ks[0], (BATCH, SEQ, d), 1.0),
        "x_sample": nrm(ks[1], (DEC_BATCH, DEC_SEQ, d), 1.0),
        "cache_k": nrm(ks[2], (DEC_BATCH, DEPTH, PAST_LEN, N_KV_HEADS, HEAD_DIM), 1.0),
        "cache_v": nrm(ks[3], (DEC_BATCH, DEPTH, PAST_LEN, N_KV_HEADS, HEAD_DIM), 0.5),
        "state_hgrn_fwd": nrm(ks[4], (DEC_BATCH, DEPTH, HG_HEADS, HG_HEAD_DIM, HG_HEAD_DIM), 0.5),
        "state_hgrn_bwd": nrm(ks[5], (DEC_BATCH, DEPTH, HG_HEADS, HG_HEAD_DIM, HG_HEAD_DIM), 0.5),
        "c": nrm(ks[6], (DEC_BATCH, d), 1.0),
        "c_ctx": nrm(ks[7], (d,), 1.0),
        "w_mod": nrm(ks[8], (DEPTH, d, N_MOD * d), 0.5 * d ** -0.5),
        "b_mod": nrm(ks[9], (DEPTH, N_MOD * d), 0.02),
        "w_ffn1_in": nrm(ks[10], (DEPTH, d, 2 * D_FF), BETA * d ** -0.5),
        "w_ffn1_out": nrm(ks[11], (DEPTH, D_FF, d), BETA * D_FF ** -0.5),
        "w_ffn2_in": nrm(ks[12], (DEPTH, d, 2 * D_FF), BETA * d ** -0.5),
        "w_ffn2_out": nrm(ks[13], (DEPTH, D_FF, d), BETA * D_FF ** -0.5),
        "w_in": nrm(ks[14], (DEPTH, d, IN_WIDTH), d ** -0.5) * col_scale,
        "w_out": nrm(ks[15], (DEPTH, MIX_WIDTH, d), BETA * MIX_WIDTH ** -0.5),
        "q_norm_g": 1.0 + nrm(ks[16], (DEPTH, HEAD_DIM), 0.05),
        "k_norm_g": 1.0 + nrm(ks[17], (DEPTH, HEAD_DIM), 0.05),
        "hg_norm_g": 1.0 + nrm(ks[18], (DEPTH, HG_WIDTH), 0.05),
        "lb_logits_fwd": nrm(ks[19], (DEPTH + 1, HG_WIDTH), 0.5),
        "lb_logits_bwd": nrm(ks[20], (DEPTH + 1, HG_WIDTH), 0.5),
        "ln_g": 1.0 + nrm(ks[21], (DEPTH, 3, d), 0.05),
        "ln_b": nrm(ks[22], (DEPTH, 3, d), 0.02),
    }


def reference(x_prompt, x_sample, cache_k, cache_v, state_hgrn_fwd, state_hgrn_bwd, c, c_ctx,
              w_mod, b_mod, w_ffn1_in, w_ffn1_out, w_ffn2_in, w_ffn2_out, w_in, w_out,
              q_norm_g, k_norm_g, hg_norm_g, lb_logits_fwd, lb_logits_bwd, ln_g, ln_b):
    lb_f_all = jnp.cumsum(jax.nn.softmax(lb_logits_fwd.astype(f32), axis=0), axis=0)
    lb_b_all = jnp.cumsum(jax.nn.softmax(lb_logits_bwd.astype(f32), axis=0), axis=0)

    y_prompt = x_prompt
    ks_l, vs_l, sf_l, sb_l = [], [], [], []
    for l in range(DEPTH):
        mix = functools.partial(mixer_context, w_in=w_in[l], w_out=w_out[l], q_g=q_norm_g[l],
                                k_g=k_norm_g[l], hg_g=hg_norm_g[l], lb_f=lb_f_all[l], lb_b=lb_b_all[l])
        y_prompt, (k_l, v_l, s_f, s_b) = trunk_layer(
            y_prompt, c_ctx, mix, w_mod[l], b_mod[l], w_ffn1_in[l], w_ffn1_out[l],
            w_ffn2_in[l], w_ffn2_out[l], ln_g[l], ln_b[l])
        ks_l.append(k_l)
        vs_l.append(v_l)
        sf_l.append(s_f)
        sb_l.append(s_b)
    new_cache_k = jnp.stack(ks_l, axis=1)
    new_cache_v = jnp.stack(vs_l, axis=1)
    new_state_fwd = jnp.stack(sf_l, axis=1)
    new_state_bwd = jnp.stack(sb_l, axis=1)

    y_sample = x_sample
    for l in range(DEPTH):
        mix = functools.partial(mixer_latent, k_ctx=cache_k[:, l], v_ctx=cache_v[:, l],
                                s_f0=state_hgrn_fwd[:, l], s_b0=state_hgrn_bwd[:, l],
                                w_in=w_in[l], w_out=w_out[l], q_g=q_norm_g[l], k_g=k_norm_g[l],
                                hg_g=hg_norm_g[l], lb_f=lb_f_all[l], lb_b=lb_b_all[l])
        y_sample, _ = trunk_layer(
            y_sample, c, mix, w_mod[l], b_mod[l], w_ffn1_in[l], w_ffn1_out[l],
            w_ffn2_in[l], w_ffn2_out[l], ln_g[l], ln_b[l])

    return (y_prompt, y_sample, new_cache_k, new_cache_v, new_state_fwd, new_state_bwd)
```

```python
import functools

import jax
import jax.numpy as jnp
from jax import lax
from jax.experimental import pallas as pl
from jax.experimental.pallas import tpu as pltpu

f32 = jnp.float32
bf16 = jnp.bfloat16

D_MODEL = 1024
N_MOD = 9
HG_WIDTH = 512
HG_HEAD_DIM = 128
HG_HEADS = 4
ATT_WIDTH = 512
HEAD_DIM = 64
N_HEADS = 8
N_KV_HEADS = 2
KV_WIDTH = 128
IN_WIDTH = 5 * HG_WIDTH + ATT_WIDTH + 2 * KV_WIDTH
D_FF = 2816
GRID_W = 64
ROPE_THETA = 10000.0
DEPTH = 1
ALPHA = (2.0 * DEPTH) ** 0.25
LN_EPS = 1e-6
RMS_EPS = 1e-6

LANES = 128
VMEM_LIMIT_BYTES = 56 * 1024 * 1024

FF_CHUNK = 256
TOKEN_TILE = 512
HG_TILE = 128
Q_TILE = 256


def _silu(x):
    return x * jax.nn.sigmoid(x)


def _layer_norm(r, g, b):
    mu = jnp.mean(r, axis=-1, keepdims=True)
    c = r - mu
    var = jnp.mean(c * c, axis=-1, keepdims=True)
    return c * lax.rsqrt(var + LN_EPS) * g + b


def _split_bf16(x, parts):
    out = []
    r = x
    for _ in range(parts - 1):
        p = r.astype(bf16)
        out.append(p)
        r = r - p.astype(f32)
    out.append(r.astype(bf16))
    return out


def _dot01(mat01, x, parts, *, mat_on_left):
    acc = None
    for p in _split_bf16(x, parts):
        d = (jnp.dot(mat01, p, preferred_element_type=f32) if mat_on_left
             else jnp.dot(p, mat01, preferred_element_type=f32))
        acc = d if acc is None else acc + d
    return acc


def _dot_nt(a, b):
    return lax.dot_general(a, b, (((1,), (1,)), ((), ())), preferred_element_type=f32)


def _dot_tn(a, b):
    return lax.dot_general(a, b, (((0,), (0,)), ((), ())), preferred_element_type=f32)


def _mod_kernel(c_ref, w_ref, b_ref, o_ref):
    a = _silu(c_ref[...]).astype(bf16)
    o_ref[...] = jnp.dot(a, w_ref[...].astype(bf16), preferred_element_type=f32) + b_ref[...]


def _modulation(cvecs, w_mod, b_mod):
    rows = cvecs.shape[0]
    n_out = w_mod.shape[1]
    tn = D_MODEL
    return pl.pallas_call(
        _mod_kernel,
        grid=(n_out // tn,),
        in_specs=[
            pl.BlockSpec((rows, D_MODEL), lambda j: (0, 0)),
            pl.BlockSpec((D_MODEL, tn), lambda j: (0, j)),
            pl.BlockSpec((1, tn), lambda j: (0, j)),
        ],
        out_specs=pl.BlockSpec((rows, tn), lambda j: (0, j)),
        out_shape=jax.ShapeDtypeStruct((rows, n_out), f32),
        compiler_params=pltpu.CompilerParams(dimension_semantics=("arbitrary",)),
        name="modulation",
    )(cvecs, w_mod, b_mod.reshape(1, n_out))


def _ffn_kernel(x_ref, m_ref, wa_ref, wu_ref, wd_ref, g_ref, b_ref, o_ref, act_ref, *, mod_base, ln_row):
    x = x_ref[0]
    shift = m_ref[0, mod_base:mod_base + 1, :]
    scale = m_ref[0, mod_base + 1:mod_base + 2, :]
    gate = m_ref[0, mod_base + 2:mod_base + 3, :]
    h = (x * (1.0 + scale) + shift).astype(bf16)
    for j in range(D_FF // FF_CHUNK):
        cols = slice(j * FF_CHUNK, (j + 1) * FF_CHUNK)
        a = jnp.dot(h, wa_ref[:, cols], preferred_element_type=f32)
        u = jnp.dot(h, wu_ref[:, cols], preferred_element_type=f32)
        act_ref[:, cols] = (_silu(a) * u).astype(bf16)
    y = jnp.dot(act_ref[...], wd_ref[...], preferred_element_type=f32)
    r = ALPHA * x + 0.5 * gate * y
    o_ref[0] = _layer_norm(r, g_ref[ln_row:ln_row + 1, :], b_ref[ln_row:ln_row + 1, :])


def _resident(shape):
    return pl.BlockSpec(shape, lambda *_: (0,) * len(shape), pipeline_mode=pl.Buffered(1))


def _ffn(x, mod, mod_group, wa, wu, wd, ln_g, ln_b, *, mod_base, ln_row):
    groups, length, _ = x.shape
    tm = TOKEN_TILE
    return pl.pallas_call(
        functools.partial(_ffn_kernel, mod_base=mod_base, ln_row=ln_row),
        grid=(groups, length // tm),
        in_specs=[
            pl.BlockSpec((1, tm, D_MODEL), lambda g, t: (g, t, 0)),
            pl.BlockSpec((1, N_MOD, D_MODEL), lambda g, t: (mod_group(g), 0, 0)),
            _resident((D_MODEL, D_FF)),
            _resident((D_MODEL, D_FF)),
            _resident((D_FF, D_MODEL)),
            _resident((3, D_MODEL)),
            _resident((3, D_MODEL)),
        ],
        out_specs=pl.BlockSpec((1, tm, D_MODEL), lambda g, t: (g, t, 0)),
        out_shape=jax.ShapeDtypeStruct(x.shape, f32),
        scratch_shapes=[pltpu.VMEM((tm, D_FF), bf16)],
        compiler_params=pltpu.CompilerParams(
            dimension_semantics=("arbitrary", "arbitrary"), vmem_limit_bytes=VMEM_LIMIT_BYTES),
        name="ffn",
    )(x, mod, wa, wu, wd, ln_g, ln_b)


def _head_rms_norm(x, ones_bd, gain):
    ss = _dot01(ones_bd, x * x, 2, mat_on_left=False)
    return x * lax.rsqrt(ss * (1.0 / HEAD_DIM) + RMS_EPS) * gain


def _rope(x, cos, sin_signed):
    width = x.shape[-1]
    lane = lax.broadcasted_iota(jnp.int32, x.shape, 1)
    from_right = pltpu.roll(x, width - 16, 1)
    from_left = pltpu.roll(x, 16, 1)
    partner = jnp.where((lane & 31) < 16, from_right, from_left)
    return x * cos + partner * sin_signed


def _proj_kernel(*refs, rope):
    if rope:
        (x_ref, m_ref, w_ref, lbf_ref, lbb_ref, qg_ref, kg_ref, bd_ref, cos_ref, sin_ref,
         qs_ref, vh_ref, ff_ref, fb_ref, sg_ref, qa_ref, ka_ref, va_ref) = refs
    else:
        (x_ref, m_ref, w_ref, lbf_ref, lbb_ref, qg_ref, kg_ref, bd_ref,
         qs_ref, vh_ref, ff_ref, fb_ref, sg_ref, qa_ref, ka_ref, va_ref) = refs
    x = x_ref[0]
    shift = m_ref[0, 3:4, :]
    scale = m_ref[0, 4:5, :]
    h = (x * (1.0 + scale) + shift).astype(bf16)

    def cols(lo, width):
        return jnp.dot(h, w_ref[:, lo:lo + width], preferred_element_type=f32)

    def lower_bound(lb_ref):
        l0 = lb_ref[0:1, :]
        l1 = lb_ref[1:2, :]
        m = jnp.maximum(l0, l1)
        e0 = jnp.exp(l0 - m)
        e1 = jnp.exp(l1 - m)
        return e0 / (e0 + e1)

    qs_ref[0] = _silu(cols(0, HG_WIDTH))
    vh_ref[0] = cols(HG_WIDTH, HG_WIDTH)
    lb = lower_bound(lbf_ref)
    ff_ref[0] = lb + (1.0 - lb) * jax.nn.sigmoid(cols(2 * HG_WIDTH, HG_WIDTH))
    lb = lower_bound(lbb_ref)
    fb_ref[0] = lb + (1.0 - lb) * jax.nn.sigmoid(cols(3 * HG_WIDTH, HG_WIDTH))
    sg_ref[0] = _silu(cols(4 * HG_WIDTH, HG_WIDTH))

    base = 5 * HG_WIDTH
    q = _head_rms_norm(cols(base, ATT_WIDTH), bd_ref[...], qg_ref[...])
    k = _head_rms_norm(cols(base + ATT_WIDTH, KV_WIDTH), bd_ref[0:KV_WIDTH, 0:KV_WIDTH], kg_ref[...])
    if rope:
        q = _rope(q, cos_ref[...], sin_ref[...])
        k = _rope(k, cos_ref[:, 0:KV_WIDTH], sin_ref[:, 0:KV_WIDTH])
    qa_ref[0] = (q * (HEAD_DIM ** -0.5)).astype(bf16)
    ka_ref[0] = k
    va_ref[0] = cols(base + ATT_WIDTH + KV_WIDTH, KV_WIDTH)


def _proj(x, mod, mod_group, w_in, lb_f, lb_b, q_gain, k_gain, ones_bd, rope_tables):
    groups, length, _ = x.shape
    tm = TOKEN_TILE
    rope = rope_tables is not None
    tok = lambda width: pl.BlockSpec((1, tm, width), lambda g, t: (g, t, 0))
    in_specs = [
        tok(D_MODEL),
        pl.BlockSpec((1, N_MOD, D_MODEL), lambda g, t: (mod_group(g), 0, 0)),
        _resident((D_MODEL, IN_WIDTH)),
        _resident((2, HG_WIDTH)),
        _resident((2, HG_WIDTH)),
        _resident((1, ATT_WIDTH)),
        _resident((1, KV_WIDTH)),
        _resident((ATT_WIDTH, ATT_WIDTH)),
    ]
    args = [x, mod, w_in, lb_f, lb_b, q_gain, k_gain, ones_bd]
    if rope:
        in_specs += [pl.BlockSpec((tm, ATT_WIDTH), lambda g, t: (t, 0))] * 2
        args += list(rope_tables)
    shape = lambda width, dt: jax.ShapeDtypeStruct((groups, length, width), dt)
    return pl.pallas_call(
        functools.partial(_proj_kernel, rope=rope),
        grid=(groups, length // tm),
        in_specs=in_specs,
        out_specs=[tok(HG_WIDTH)] * 5 + [tok(ATT_WIDTH), tok(KV_WIDTH), tok(KV_WIDTH)],
        out_shape=[shape(HG_WIDTH, f32)] * 5 + [shape(ATT_WIDTH, bf16), shape(KV_WIDTH, f32),
                                                 shape(KV_WIDTH, f32)],
        compiler_params=pltpu.CompilerParams(
            dimension_semantics=("arbitrary", "arbitrary"), vmem_limit_bytes=VMEM_LIMIT_BYTES),
        name="mixer_proj",
    )(*args)


def _hgrn_direction(qs, v, f, st_ref, o_ref, *, rev):
    tc = qs.shape[0]
    width = qs.shape[1]
    lf = jnp.log(f)
    kk = 1.0 - f
    row = lax.broadcasted_iota(jnp.int32, (tc, tc), 0)
    col = lax.broadcasted_iota(jnp.int32, (tc, tc), 1)
    ridx = lax.broadcasted_iota(jnp.int32, (tc, width), 0)
    tri = jnp.where((col >= row) if rev else (col <= row), 1.0, 0.0).astype(bf16)
    cum = _dot01(tri, lf, 3, mat_on_left=True)
    total = cum[0:1] if rev else cum[tc - 1:tc]
    query_half = 0 if rev else 1

    def level_exponent(s):
        if s == 1:
            return jnp.where((ridx & 1) == query_half, lf, 0.0)
        if s == 2:
            nxt = pltpu.roll(lf, tc - 1, 0)
            prv = pltpu.roll(lf, 1, 0)
            m4 = ridx & 3
            if rev:
                return jnp.where(m4 == 0, lf + nxt, jnp.where(m4 == 1, lf, jnp.where(m4 == 2, 0.0, prv)))
            return jnp.where(m4 == 0, nxt, jnp.where(m4 == 1, 0.0, jnp.where(m4 == 2, lf, lf + prv)))
        blocks = []
        for b in range(tc // (2 * s)):
            lo = b * 2 * s
            anchor = lo + (s if rev else s - 1)
            blocks.append(cum[lo:lo + 2 * s] - cum[anchor:anchor + 1])
        d = jnp.concatenate(blocks, axis=0)
        is_query = ((ridx >> (s.bit_length() - 1)) & 1) == query_half
        return jnp.where(is_query, d, -d)

    ordered = (row < col) if rev else (row > col)
    differ = row ^ col
    intra = [None] * HG_HEADS
    s = 1
    while s < tc:
        shift = s.bit_length() - 1
        is_query = ((ridx >> shift) & 1) == query_half
        mixed = (jnp.where(is_query, qs, kk) * jnp.exp(level_exponent(s))).astype(bf16)
        pair = jnp.logical_and((differ >> shift) == 1, ordered)
        for hd in range(HG_HEADS):
            mh = mixed[:, hd * HG_HEAD_DIM:(hd + 1) * HG_HEAD_DIM]
            a = _dot_nt(mh, mh)
            intra[hd] = jnp.where(pair, a, 0.0 if intra[hd] is None else intra[hd])
        s *= 2

    q_dec = (qs * jnp.exp(cum)).astype(bf16)
    k_end = (kk * jnp.exp(total - cum)).astype(bf16)
    carry = jnp.exp(total)
    vb = v.astype(bf16)
    qk = qs * kk
    for hd in range(HG_HEADS):
        hs = slice(hd * HG_HEAD_DIM, (hd + 1) * HG_HEAD_DIM)
        st = st_ref[hd]
        own = jnp.sum(qk[:, hs], axis=-1, keepdims=True)
        o = (jnp.dot(intra[hd].astype(bf16), vb[:, hs], preferred_element_type=f32)
             + _dot_nt(q_dec[:, hs], st.astype(bf16))
             + own * v[:, hs])
        o_ref[0, :, hs] = o
        st_ref[hd] = st * carry[:, hs] + _dot_tn(vb[:, hs], k_end[:, hs])


def _hgrn_kernel(*refs, has_init):
    if has_init:
        (qsf_ref, vf_ref, ff_ref, qsb_ref, vb_ref, fb_ref, s0f_ref, s0b_ref,
         of_ref, ob_ref, stf_ref, stb_ref) = refs
    else:
        (qsf_ref, vf_ref, ff_ref, qsb_ref, vb_ref, fb_ref,
         of_ref, ob_ref, sf_out_ref, sb_out_ref, stf_ref, stb_ref) = refs
    t = pl.program_id(1)

    @pl.when(t == 0)
    def _():
        for hd in range(HG_HEADS):
            if has_init:
                stf_ref[hd] = s0f_ref[0, hd].T
                stb_ref[hd] = s0b_ref[0, hd].T
            else:
                stf_ref[hd] = jnp.zeros((HG_HEAD_DIM, HG_HEAD_DIM), f32)
                stb_ref[hd] = jnp.zeros((HG_HEAD_DIM, HG_HEAD_DIM), f32)

    _hgrn_direction(qsf_ref[0], vf_ref[0], ff_ref[0], stf_ref, of_ref, rev=False)
    _hgrn_direction(qsb_ref[0], vb_ref[0], fb_ref[0], stb_ref, ob_ref, rev=True)

    if not has_init:
        @pl.when(t == pl.num_programs(1) - 1)
        def _():
            for hd in range(HG_HEADS):
                sf_out_ref[0, hd] = stf_ref[hd].T
                sb_out_ref[0, hd] = stb_ref[hd].T


def _hgrn(qs, vh, f_fwd, f_bwd, init_states):
    groups, length, _ = qs.shape
    nt = length // HG_TILE
    has_init = init_states is not None
    fwd = pl.BlockSpec((1, HG_TILE, HG_WIDTH), lambda g, t: (g, t, 0))
    bwd = pl.BlockSpec((1, HG_TILE, HG_WIDTH), lambda g, t: (g, nt - 1 - t, 0))
    state = pl.BlockSpec((1, HG_HEADS, HG_HEAD_DIM, HG_HEAD_DIM), lambda g, t: (g, 0, 0, 0))
    in_specs = [fwd, fwd, fwd, bwd, bwd, bwd]
    args = [qs, vh, f_fwd, qs, vh, f_bwd]
    out_specs = [fwd, bwd]
    out_shape = [jax.ShapeDtypeStruct(qs.shape, f32)] * 2
    if has_init:
        in_specs += [state, state]
        args += list(init_states)
    else:
        out_specs += [state, state]
        out_shape += [jax.ShapeDtypeStruct((groups, HG_HEADS, HG_HEAD_DIM, HG_HEAD_DIM), f32)] * 2
    return pl.pallas_call(
        functools.partial(_hgrn_kernel, has_init=has_init),
        grid=(groups, nt),
        in_specs=in_specs,
        out_specs=out_specs,
        out_shape=out_shape,
        scratch_shapes=[pltpu.VMEM((HG_HEADS, HG_HEAD_DIM, HG_HEAD_DIM), f32)] * 2,
        compiler_params=pltpu.CompilerParams(
            dimension_semantics=("arbitrary", "arbitrary"), vmem_limit_bytes=VMEM_LIMIT_BYTES),
        name="hgrn2",
    )(*args)


def _attn_kernel(*refs, n_ctx):
    if n_ctx:
        (qa_ref, ka_ref, va_ref, ck_ref, cv_ref, x_ref, of_ref, ob_ref, sg_ref, m_ref, wo_ref, hg_ref,
         g_ref, b_ref, o_ref, kk_ref, vv_ref) = refs
    else:
        (qa_ref, ka_ref, va_ref, x_ref, of_ref, ob_ref, sg_ref, m_ref, wo_ref, hg_ref,
         g_ref, b_ref, o_ref, kk_ref, vv_ref) = refs
    tq = qa_ref.shape[1]

    @pl.when(pl.program_id(1) == 0)
    def _():
        def fill(lo, k, v):
            n = k.shape[0]
            lane = lax.broadcasted_iota(jnp.int32, k.shape, 1)
            low = lane < HEAD_DIM
            k_sw = pltpu.roll(k, HEAD_DIM, 1)
            v_sw = pltpu.roll(v, HEAD_DIM, 1)
            kk_ref[lo:lo + n, 0:LANES] = jnp.where(low, k, k_sw).astype(bf16)
            kk_ref[lo:lo + n, LANES:2 * LANES] = jnp.where(low, k_sw, k).astype(bf16)
            vv_ref[lo:lo + n, 0:LANES] = jnp.where(low, v, 0.0).astype(bf16)
            vv_ref[lo:lo + n, LANES:2 * LANES] = jnp.where(low, 0.0, v_sw).astype(bf16)
            vv_ref[lo:lo + n, 2 * LANES:3 * LANES] = jnp.where(low, v_sw, 0.0).astype(bf16)
            vv_ref[lo:lo + n, 3 * LANES:4 * LANES] = jnp.where(low, 0.0, v).astype(bf16)

        if n_ctx:
            fill(0, ck_ref[0], cv_ref[0])
        fill(n_ctx, ka_ref[0], va_ref[0])

    lane = lax.broadcasted_iota(jnp.int32, (tq, LANES), 1)
    low = lane < HEAD_DIM
    pairs = []
    for kv in range(N_KV_HEADS):
        keys = kk_ref[:, kv * LANES:(kv + 1) * LANES]
        v_lo = vv_ref[:, 2 * kv * LANES:(2 * kv + 1) * LANES]
        v_hi = vv_ref[:, (2 * kv + 1) * LANES:(2 * kv + 2) * LANES]
        for p in range(N_HEADS // N_KV_HEADS // 2):
            tile = 2 * kv + p
            qp = qa_ref[0, :, tile * LANES:(tile + 1) * LANES]
            zero = jnp.zeros_like(qp)
            q2 = jnp.concatenate([jnp.where(low, qp, zero), jnp.where(low, zero, qp)], axis=0)
            s = _dot_nt(q2, keys)
            m = jnp.max(s, axis=-1, keepdims=True)
            e = jnp.exp(s - m)
            inv = 1.0 / jnp.sum(e, axis=-1, keepdims=True)
            eb = e.astype(bf16)
            o = (jnp.dot(eb[:tq], v_lo, preferred_element_type=f32)
                 + jnp.dot(eb[tq:], v_hi, preferred_element_type=f32))
            pairs.append(o * jnp.where(low, inv[:tq], inv[tq:]))
    o_att = jnp.concatenate(pairs, axis=-1)

    o_sum = of_ref[0] + ob_ref[0]
    normed = []
    for hd in range(HG_HEADS):
        oh = o_sum[:, hd * HG_HEAD_DIM:(hd + 1) * HG_HEAD_DIM]
        ms = jnp.mean(oh * oh, axis=-1, keepdims=True)
        normed.append(oh * lax.rsqrt(ms + RMS_EPS))
    o_hg = jnp.concatenate(normed, axis=-1) * hg_ref[...] * sg_ref[0]

    cat = jnp.concatenate([o_hg, o_att], axis=-1).astype(bf16)
    y = jnp.dot(cat, wo_ref[...], preferred_element_type=f32)
    r = ALPHA * x_ref[0] + m_ref[0, 5:6, :] * y
    o_ref[0] = _layer_norm(r, g_ref[1:2, :], b_ref[1:2, :])


def _attn(qa, ka, va, ctx_kv, x, o_f, o_b, sg, mod, mod_group, w_out, hg_gain, ln_g, ln_b):
    groups, length, _ = qa.shape
    tq = Q_TILE
    n_ctx = 0 if ctx_kv is None else ctx_kv[0].shape[1]
    n_keys = n_ctx + length
    tok = lambda width: pl.BlockSpec((1, tq, width), lambda g, t: (g, t, 0))
    whole = lambda n: pl.BlockSpec((1, n, KV_WIDTH), lambda g, t: (g, 0, 0))
    in_specs = [tok(ATT_WIDTH), whole(length), whole(length)]
    args = [qa, ka, va]
    if n_ctx:
        in_specs += [whole(n_ctx), whole(n_ctx)]
        args += list(ctx_kv)
    in_specs += [
        tok(D_MODEL), tok(HG_WIDTH), tok(HG_WIDTH), tok(HG_WIDTH),
        pl.BlockSpec((1, N_MOD, D_MODEL), lambda g, t: (mod_group(g), 0, 0)),
        _resident((D_MODEL, D_MODEL)),
        _resident((1, HG_WIDTH)),
        _resident((3, D_MODEL)),
        _resident((3, D_MODEL)),
    ]
    args += [x, o_f, o_b, sg, mod, w_out, hg_gain, ln_g, ln_b]
    return pl.pallas_call(
        functools.partial(_attn_kernel, n_ctx=n_ctx),
        grid=(groups, length // tq),
        in_specs=in_specs,
        out_specs=tok(D_MODEL),
        out_shape=jax.ShapeDtypeStruct(x.shape, f32),
        scratch_shapes=[pltpu.VMEM((n_keys, N_KV_HEADS * LANES), bf16),
                        pltpu.VMEM((n_keys, 2 * N_KV_HEADS * LANES), bf16)],
        compiler_params=pltpu.CompilerParams(
            dimension_semantics=("arbitrary", "arbitrary"), vmem_limit_bytes=VMEM_LIMIT_BYTES),
        name="attn_out",
    )(*args)


def _rope_tables(n_tokens):
    half = HEAD_DIM // 2
    t = jnp.arange(n_tokens)
    inv = ROPE_THETA ** (-jnp.arange(0, half, 2, dtype=f32) / half)
    ang_row = (t // GRID_W).astype(f32)[:, None] * inv
    ang_col = (t % GRID_W).astype(f32)[:, None] * inv
    cos = jnp.concatenate([jnp.cos(ang_row)] * 2 + [jnp.cos(ang_col)] * 2, axis=-1)
    sin = jnp.concatenate([-jnp.sin(ang_row), jnp.sin(ang_row), -jnp.sin(ang_col), jnp.sin(ang_col)], axis=-1)
    return jnp.tile(cos, (1, N_HEADS)), jnp.tile(sin, (1, N_HEADS))


def kernel(x_prompt, x_sample, cache_k, cache_v, state_hgrn_fwd, state_hgrn_bwd, c, c_ctx, w_mod, b_mod,
           w_ffn1_in, w_ffn1_out, w_ffn2_in, w_ffn2_out, w_in, w_out, q_norm_g, k_norm_g, hg_norm_g,
           lb_logits_fwd, lb_logits_bwd, ln_g, ln_b):
    assert w_mod.shape[0] == DEPTH and lb_logits_fwd.shape[0] == DEPTH + 1
    batch, seq, _ = x_prompt.shape
    dec_batch, dec_seq, _ = x_sample.shape
    past = cache_k.shape[2]

    ctx_row = dec_batch
    rows = 16
    cvecs = jnp.concatenate([c, c_ctx[None, :], jnp.zeros((rows - dec_batch - 1, D_MODEL), f32)], axis=0)
    mod = _modulation(cvecs, w_mod[0], b_mod[0]).reshape(rows, N_MOD, D_MODEL)

    to_bf16 = lambda w: w.astype(bf16)
    w1a, w1u = to_bf16(w_ffn1_in[0][:, :D_FF]), to_bf16(w_ffn1_in[0][:, D_FF:])
    w2a, w2u = to_bf16(w_ffn2_in[0][:, :D_FF]), to_bf16(w_ffn2_in[0][:, D_FF:])
    w1d, w2d = to_bf16(w_ffn1_out[0]), to_bf16(w_ffn2_out[0])
    w_in_b, w_out_b = to_bf16(w_in[0]), to_bf16(w_out[0])
    gains = ln_g[0], ln_b[0]
    q_gain = jnp.tile(q_norm_g[0], N_HEADS).reshape(1, ATT_WIDTH)
    k_gain = jnp.tile(k_norm_g[0], N_KV_HEADS).reshape(1, KV_WIDTH)
    hg_gain = hg_norm_g[0].reshape(1, HG_WIDTH)
    head_of = jnp.arange(ATT_WIDTH) // HEAD_DIM
    ones_bd = (head_of[:, None] == head_of[None, :]).astype(bf16)

    def trunk(x, mod_group, rope_tables, ctx_kv, init_states, hg_groups):
        shape = x.shape
        per_seq = lambda a: a.reshape(hg_groups, -1, a.shape[-1])
        x = _ffn(x, mod, mod_group, w1a, w1u, w1d, *gains, mod_base=0, ln_row=0)
        qs, vh, f_f, f_b, sg, qa, ka, va = _proj(x, mod, mod_group, w_in_b, lb_logits_fwd, lb_logits_bwd,
                                                 q_gain, k_gain, ones_bd, rope_tables)
        qs, vh, f_f, f_b, sg, qa, ka, va = map(per_seq, (qs, vh, f_f, f_b, sg, qa, ka, va))
        scans = _hgrn(qs, vh, f_f, f_b, init_states)
        x = _attn(qa, ka, va, ctx_kv, per_seq(x), scans[0], scans[1], sg, mod, mod_group, w_out_b, hg_gain,
                  *gains).reshape(shape)
        x = _ffn(x, mod, mod_group, w2a, w2u, w2d, *gains, mod_base=6, ln_row=2)
        return x, ka, va, scans[2:]

    ctx_group = lambda g: ctx_row
    y_prompt, k_new, v_new, states = trunk(
        x_prompt.reshape(1, batch * seq, D_MODEL), ctx_group, None, None, None, batch)
    y_prompt = y_prompt.reshape(batch, seq, D_MODEL)
    new_cache_k = k_new.reshape(batch, DEPTH, seq, N_KV_HEADS, HEAD_DIM)
    new_cache_v = v_new.reshape(batch, DEPTH, seq, N_KV_HEADS, HEAD_DIM)
    new_state_fwd = states[0].reshape(batch, DEPTH, HG_HEADS, HG_HEAD_DIM, HG_HEAD_DIM)
    new_state_bwd = states[1].reshape(batch, DEPTH, HG_HEADS, HG_HEAD_DIM, HG_HEAD_DIM)

    ctx_kv = (cache_k[:, 0].reshape(dec_batch, past, KV_WIDTH), cache_v[:, 0].reshape(dec_batch, past, KV_WIDTH))
    init_states = (state_hgrn_fwd[:, 0], state_hgrn_bwd[:, 0])
    y_sample, _, _, _ = trunk(x_sample, lambda g: g, _rope_tables(dec_seq), ctx_kv, init_states, dec_batch)

    return (y_prompt, y_sample, new_cache_k, new_cache_v, new_state_fwd, new_state_bwd)
```

```python
import functools

import jax
import jax.numpy as jnp
from jax import lax
from jax.experimental import pallas as pl
from jax.experimental.pallas import tpu as pltpu

f32 = jnp.float32
bf16 = jnp.bfloat16

D_MODEL = 1024
N_MOD = 9
HG_WIDTH = 512
HG_HEAD_DIM = 128
HG_HEADS = 4
ATT_WIDTH = 512
HEAD_DIM = 64
N_HEADS = 8
N_KV_HEADS = 2
KV_WIDTH = 128
IN_WIDTH = 5 * HG_WIDTH + ATT_WIDTH + 2 * KV_WIDTH
D_FF = 2816
GRID_W = 64
ROPE_THETA = 10000.0
DEPTH = 1
ALPHA = (2.0 * DEPTH) ** 0.25
LOG2_E = 1.4426950408889634
LN_EPS = 1e-6
RMS_EPS = 1e-6

LANES = 128
VMEM_LIMIT_BYTES = 56 * 1024 * 1024

FF_CHUNK = 256
TOKEN_TILE = 512
HG_TILE = 128
Q_TILE = 256
HG_FAST_BLOCK = 64
HG_FAST_SPAN_LOG2 = 100.0


def _silu(x):
    return x * jax.nn.sigmoid(x)


def _layer_norm(r, g, b):
    mu = jnp.mean(r, axis=-1, keepdims=True)
    c = r - mu
    var = jnp.mean(c * c, axis=-1, keepdims=True)
    return c * lax.rsqrt(var + LN_EPS) * g + b


def _split_bf16(x, parts):
    out = []
    r = x
    for _ in range(parts - 1):
        p = r.astype(bf16)
        out.append(p)
        r = r - p.astype(f32)
    out.append(r.astype(bf16))
    return out


def _dot01(mat01, x, parts, *, mat_on_left):
    acc = None
    for p in _split_bf16(x, parts):
        d = (jnp.dot(mat01, p, preferred_element_type=f32) if mat_on_left
             else jnp.dot(p, mat01, preferred_element_type=f32))
        acc = d if acc is None else acc + d
    return acc


def _dot_nt(a, b):
    return lax.dot_general(a, b, (((1,), (1,)), ((), ())), preferred_element_type=f32)


def _dot_tn(a, b):
    return lax.dot_general(a, b, (((0,), (0,)), ((), ())), preferred_element_type=f32)


def _mod_kernel(c_ref, w_ref, b_ref, o_ref):
    a = _silu(c_ref[...]).astype(bf16)
    o_ref[...] = jnp.dot(a, w_ref[...].astype(bf16), preferred_element_type=f32) + b_ref[...]


def _modulation(cvecs, w_mod, b_mod):
    rows = cvecs.shape[0]
    n_out = w_mod.shape[1]
    tn = D_MODEL
    return pl.pallas_call(
        _mod_kernel,
        grid=(n_out // tn,),
        in_specs=[
            pl.BlockSpec((rows, D_MODEL), lambda j: (0, 0)),
            pl.BlockSpec((D_MODEL, tn), lambda j: (0, j)),
            pl.BlockSpec((1, tn), lambda j: (0, j)),
        ],
        out_specs=pl.BlockSpec((rows, tn), lambda j: (0, j)),
        out_shape=jax.ShapeDtypeStruct((rows, n_out), f32),
        compiler_params=pltpu.CompilerParams(dimension_semantics=("arbitrary",)),
        name="modulation",
    )(cvecs, w_mod, b_mod.reshape(1, n_out))


def _ffn_kernel(x_ref, m_ref, wa_ref, wu_ref, wd_ref, g_ref, b_ref, o_ref, act_ref, *, mod_base, ln_row):
    x = x_ref[0]
    shift = m_ref[0, mod_base:mod_base + 1, :]
    scale = m_ref[0, mod_base + 1:mod_base + 2, :]
    gate = m_ref[0, mod_base + 2:mod_base + 3, :]
    h = (x * (1.0 + scale) + shift).astype(bf16)
    for j in range(D_FF // FF_CHUNK):
        cols = slice(j * FF_CHUNK, (j + 1) * FF_CHUNK)
        a = jnp.dot(h, wa_ref[:, cols], preferred_element_type=f32)
        u = jnp.dot(h, wu_ref[:, cols], preferred_element_type=f32)
        act_ref[:, cols] = (_silu(a) * u).astype(bf16)
    y = jnp.dot(act_ref[...], wd_ref[...], preferred_element_type=f32)
    r = ALPHA * x + 0.5 * gate * y
    o_ref[0] = _layer_norm(r, g_ref[ln_row:ln_row + 1, :], b_ref[ln_row:ln_row + 1, :])


def _resident(shape):
    return pl.BlockSpec(shape, lambda *_: (0,) * len(shape), pipeline_mode=pl.Buffered(1))


def _ffn(x, mod, mod_group, wa, wu, wd, ln_g, ln_b, *, mod_base, ln_row):
    groups, length, _ = x.shape
    tm = TOKEN_TILE
    return pl.pallas_call(
        functools.partial(_ffn_kernel, mod_base=mod_base, ln_row=ln_row),
        grid=(groups, length // tm),
        in_specs=[
            pl.BlockSpec((1, tm, D_MODEL), lambda g, t: (g, t, 0)),
            pl.BlockSpec((1, N_MOD, D_MODEL), lambda g, t: (mod_group(g), 0, 0)),
            _resident((D_MODEL, D_FF)),
            _resident((D_MODEL, D_FF)),
            _resident((D_FF, D_MODEL)),
            _resident((3, D_MODEL)),
            _resident((3, D_MODEL)),
        ],
        out_specs=pl.BlockSpec((1, tm, D_MODEL), lambda g, t: (g, t, 0)),
        out_shape=jax.ShapeDtypeStruct(x.shape, f32),
        scratch_shapes=[pltpu.VMEM((tm, D_FF), bf16)],
        compiler_params=pltpu.CompilerParams(
            dimension_semantics=("arbitrary", "arbitrary"), vmem_limit_bytes=VMEM_LIMIT_BYTES),
        name="ffn",
    )(x, mod, wa, wu, wd, ln_g, ln_b)


def _head_rms_norm(x, ones_bd, gain):
    ss = _dot01(ones_bd, x * x, 2, mat_on_left=False)
    return x * lax.rsqrt(ss * (1.0 / HEAD_DIM) + RMS_EPS) * gain


def _rope(x, cos, sin_signed):
    width = x.shape[-1]
    lane = lax.broadcasted_iota(jnp.int32, x.shape, 1)
    from_right = pltpu.roll(x, width - 16, 1)
    from_left = pltpu.roll(x, 16, 1)
    partner = jnp.where((lane & 31) < 16, from_right, from_left)
    return x * cos + partner * sin_signed


def _proj_kernel(*refs, rope):
    if rope:
        (x_ref, m_ref, w_ref, lbf_ref, lbb_ref, qg_ref, kg_ref, bd_ref, cos_ref, sin_ref,
         qs_ref, vh_ref, lff_ref, kkf_ref, lfb_ref, kkb_ref, sg_ref, qa_ref, ka_ref, va_ref,
         spf_ref, spb_ref) = refs
    else:
        (x_ref, m_ref, w_ref, lbf_ref, lbb_ref, qg_ref, kg_ref, bd_ref,
         qs_ref, vh_ref, lff_ref, kkf_ref, lfb_ref, kkb_ref, sg_ref, qa_ref, ka_ref, va_ref,
         spf_ref, spb_ref) = refs
    x = x_ref[0]
    shift = m_ref[0, 3:4, :]
    scale = m_ref[0, 4:5, :]
    h = (x * (1.0 + scale) + shift).astype(bf16)

    def cols(lo, width):
        return jnp.dot(h, w_ref[:, lo:lo + width], preferred_element_type=f32)

    def lower_bound(lb_ref):
        l0 = lb_ref[0:1, :]
        l1 = lb_ref[1:2, :]
        m = jnp.maximum(l0, l1)
        e0 = jnp.exp(l0 - m)
        e1 = jnp.exp(l1 - m)
        return e0 / (e0 + e1)

    def forget_gate(lb_ref, lo, lf_ref, kk_ref, span_ref):
        lb = lower_bound(lb_ref)
        f = lb + (1.0 - lb) * jax.nn.sigmoid(cols(lo, HG_WIDTH))
        lf = jnp.log2(f)
        lf_ref[0] = lf
        kk_ref[0] = 1.0 - f
        half = HG_FAST_BLOCK // 2
        sums = jnp.sum(lf.reshape(lf.shape[0] // half, half, HG_WIDTH), axis=1)
        span_ref[0] = jnp.broadcast_to(jnp.max(jnp.abs(sums), axis=-1, keepdims=True), span_ref.shape[1:])

    qs_ref[0] = _silu(cols(0, HG_WIDTH))
    vh_ref[0] = cols(HG_WIDTH, HG_WIDTH)
    forget_gate(lbf_ref, 2 * HG_WIDTH, lff_ref, kkf_ref, spf_ref)
    forget_gate(lbb_ref, 3 * HG_WIDTH, lfb_ref, kkb_ref, spb_ref)
    sg_ref[0] = _silu(cols(4 * HG_WIDTH, HG_WIDTH))

    base = 5 * HG_WIDTH
    q = _head_rms_norm(cols(base, ATT_WIDTH), bd_ref[...], qg_ref[...])
    k = _head_rms_norm(cols(base + ATT_WIDTH, KV_WIDTH), bd_ref[0:KV_WIDTH, 0:KV_WIDTH], kg_ref[...])
    if rope:
        q = _rope(q, cos_ref[...], sin_ref[...])
        k = _rope(k, cos_ref[:, 0:KV_WIDTH], sin_ref[:, 0:KV_WIDTH])
    qa_ref[0] = (q * (HEAD_DIM ** -0.5 * LOG2_E)).astype(bf16)
    ka_ref[0] = k
    va_ref[0] = cols(base + ATT_WIDTH + KV_WIDTH, KV_WIDTH)


def _proj(x, mod, mod_group, w_in, lb_f, lb_b, q_gain, k_gain, ones_bd, rope_tables):
    groups, length, _ = x.shape
    tm = TOKEN_TILE
    rope = rope_tables is not None
    tok = lambda width: pl.BlockSpec((1, tm, width), lambda g, t: (g, t, 0))
    in_specs = [
        tok(D_MODEL),
        pl.BlockSpec((1, N_MOD, D_MODEL), lambda g, t: (mod_group(g), 0, 0)),
        _resident((D_MODEL, IN_WIDTH)),
        _resident((2, HG_WIDTH)),
        _resident((2, HG_WIDTH)),
        _resident((1, ATT_WIDTH)),
        _resident((1, KV_WIDTH)),
        _resident((ATT_WIDTH, ATT_WIDTH)),
    ]
    args = [x, mod, w_in, lb_f, lb_b, q_gain, k_gain, ones_bd]
    if rope:
        in_specs += [pl.BlockSpec((tm, ATT_WIDTH), lambda g, t: (t, 0))] * 2
        args += list(rope_tables)
    shape = lambda width, dt: jax.ShapeDtypeStruct((groups, length, width), dt)
    half = HG_FAST_BLOCK // 2
    span_spec = pl.BlockSpec((1, tm // half, LANES), lambda g, t: (g, t, 0))
    span_shape = jax.ShapeDtypeStruct((groups, length // half, LANES), f32)
    return pl.pallas_call(
        functools.partial(_proj_kernel, rope=rope),
        grid=(groups, length // tm),
        in_specs=in_specs,
        out_specs=[tok(HG_WIDTH)] * 7 + [tok(ATT_WIDTH), tok(KV_WIDTH), tok(KV_WIDTH), span_spec, span_spec],
        out_shape=[shape(HG_WIDTH, f32)] * 7 + [shape(ATT_WIDTH, bf16), shape(KV_WIDTH, f32),
                                                 shape(KV_WIDTH, f32), span_shape, span_shape],
        compiler_params=pltpu.CompilerParams(
            dimension_semantics=("arbitrary", "arbitrary"), vmem_limit_bytes=VMEM_LIMIT_BYTES),
        name="mixer_proj",
    )(*args)


def _hgrn_direction(qs, v, lf, kk, st_ref, o_ref, *, rev):
    tc = qs.shape[0]
    row = lax.broadcasted_iota(jnp.int32, (tc, tc), 0)
    col = lax.broadcasted_iota(jnp.int32, (tc, tc), 1)
    ordered = (row < col) if rev else (row > col)
    differ = row ^ col
    tri = jnp.where((col >= row) if rev else (col <= row), 1.0, 0.0).astype(bf16)
    cum = _dot01(tri, lf, 3, mat_on_left=True)
    total = cum[0:1] if rev else cum[tc - 1:tc]
    vb = v.astype(bf16)
    carry = jnp.exp2(total)
    query_half = 0 if rev else 1
    heads = [slice(hd * HG_HEAD_DIM, (hd + 1) * HG_HEAD_DIM) for hd in range(HG_HEADS)]

    def level_exponent(s, ridx):
        if s == 1:
            return jnp.where((ridx & 1) == query_half, lf, 0.0)
        if s == 2:
            nxt = pltpu.roll(lf, tc - 1, 0)
            prv = pltpu.roll(lf, 1, 0)
            m4 = ridx & 3
            if rev:
                return jnp.where(m4 == 0, lf + nxt, jnp.where(m4 == 1, lf, jnp.where(m4 == 2, 0.0, prv)))
            return jnp.where(m4 == 0, nxt, jnp.where(m4 == 1, 0.0, jnp.where(m4 == 2, lf, lf + prv)))
        blocks = []
        for lo in range(0, tc, 2 * s):
            anchor = lo + (s if rev else s - 1)
            blocks.append(cum[lo:lo + 2 * s] - cum[anchor:anchor + 1])
        d = jnp.concatenate(blocks, axis=0)
        is_query = ((ridx >> (s.bit_length() - 1)) & 1) == query_half
        return jnp.where(is_query, d, -d)

    def add_levels(intra, first):
        ridx = lax.broadcasted_iota(jnp.int32, qs.shape, 0)
        s = first
        while s < tc:
            shift = s.bit_length() - 1
            is_query = ((ridx >> shift) & 1) == query_half
            mixed = (jnp.where(is_query, qs, kk) * jnp.exp2(level_exponent(s, ridx))).astype(bf16)
            pair = jnp.logical_and((differ >> shift) == 1, ordered)
            for hd, hs in enumerate(heads):
                intra[hd] = jnp.where(pair, _dot_nt(mixed[:, hs], mixed[:, hs]), intra[hd])
            s *= 2
        return intra

    def finish(intra, q_dec, k_end, extra):
        for hd, hs in enumerate(heads):
            st = st_ref[hd]
            o = (jnp.dot(intra[hd].astype(bf16), vb[:, hs], preferred_element_type=f32)
                 + _dot_nt(q_dec[:, hs], st.astype(bf16)))
            o_ref[0, :, hs] = o if extra is None else o + extra[:, hs]
            st_ref[hd] = st * carry[:, hs] + _dot_tn(vb[:, hs], k_end[:, hs])

    blk = HG_FAST_BLOCK
    anchors = [lo + (blk // 2 if rev else blk // 2 - 1) for lo in range(0, tc, blk)]

    def mid_split_blocks():
        q_mid, k_mid, q_dec, k_end = [], [], [], []
        for lo, a in zip(range(0, tc, blk), anchors):
            rel = cum[lo:lo + blk] - cum[a:a + 1]
            q_mid.append(qs[lo:lo + blk] * jnp.exp2(rel))
            k_mid.append(kk[lo:lo + blk] * jnp.exp2(-rel))
            q_dec.append(q_mid[-1] * jnp.exp2(cum[a:a + 1]))
            k_end.append(k_mid[-1] * jnp.exp2(total - cum[a:a + 1]))
        q_mid, k_mid, q_dec, k_end = (jnp.concatenate(p, axis=0).astype(bf16) for p in (q_mid, k_mid, q_dec, k_end))
        same_block = (differ >> (blk.bit_length() - 1)) == 0
        keep = jnp.logical_and(same_block, (row <= col) if rev else (row >= col))
        intra = [jnp.where(keep, _dot_nt(q_mid[:, hs], k_mid[:, hs]), 0.0) for hs in heads]
        finish(add_levels(intra, blk), q_dec, k_end, None)

    def all_levels():
        intra = add_levels([jnp.zeros((tc, tc), f32)] * HG_HEADS, 1)
        q_dec = (qs * jnp.exp2(cum)).astype(bf16)
        k_end = (kk * jnp.exp2(total - cum)).astype(bf16)
        qk = qs * kk
        own = jnp.concatenate(
            [jnp.sum(qk[:, hs], axis=-1, keepdims=True) * v[:, hs] for hs in heads], axis=-1)
        finish(intra, q_dec, k_end, own)

    return mid_split_blocks, all_levels


def _hgrn_kernel(slow_ref, *refs, has_init):
    if has_init:
        (qsf_ref, vf_ref, lff_ref, kkf_ref, qsb_ref, vb_ref, lfb_ref, kkb_ref, s0f_ref, s0b_ref,
         of_ref, ob_ref, stf_ref, stb_ref) = refs
    else:
        (qsf_ref, vf_ref, lff_ref, kkf_ref, qsb_ref, vb_ref, lfb_ref, kkb_ref,
         of_ref, ob_ref, sf_out_ref, sb_out_ref, stf_ref, stb_ref) = refs
    t = pl.program_id(1)

    @pl.when(t == 0)
    def _():
        for hd in range(HG_HEADS):
            if has_init:
                stf_ref[hd] = s0f_ref[0, hd].T
                stb_ref[hd] = s0b_ref[0, hd].T
            else:
                stf_ref[hd] = jnp.zeros((HG_HEAD_DIM, HG_HEAD_DIM), f32)
                stb_ref[hd] = jnp.zeros((HG_HEAD_DIM, HG_HEAD_DIM), f32)

    fwd = _hgrn_direction(qsf_ref[0], vf_ref[0], lff_ref[0], kkf_ref[0], stf_ref, of_ref, rev=False)
    bwd = _hgrn_direction(qsb_ref[0], vb_ref[0], lfb_ref[0], kkb_ref[0], stb_ref, ob_ref, rev=True)
    slow = slow_ref[pl.program_id(0) * pl.num_programs(1) + t] != 0

    @pl.when(jnp.logical_not(slow))
    def _():
        fwd[0]()
        bwd[0]()

    @pl.when(slow)
    def _():
        fwd[1]()
        bwd[1]()

    if not has_init:
        @pl.when(t == pl.num_programs(1) - 1)
        def _():
            for hd in range(HG_HEADS):
                sf_out_ref[0, hd] = stf_ref[hd].T
                sb_out_ref[0, hd] = stb_ref[hd].T


def _hgrn(qs, vh, lf_fwd, kk_fwd, lf_bwd, kk_bwd, span_fwd, span_bwd, init_states):
    groups, length, _ = qs.shape
    nt = length // HG_TILE
    has_init = init_states is not None
    per_chunk = lambda span: jnp.max(span[:, :, 0].reshape(groups, nt, -1), axis=-1) > HG_FAST_SPAN_LOG2
    slow = jnp.logical_or(per_chunk(span_fwd), per_chunk(span_bwd)[:, ::-1]).astype(jnp.int32).reshape(-1)
    fwd = pl.BlockSpec((1, HG_TILE, HG_WIDTH), lambda g, t, _: (g, t, 0))
    bwd = pl.BlockSpec((1, HG_TILE, HG_WIDTH), lambda g, t, _: (g, nt - 1 - t, 0))
    state = pl.BlockSpec((1, HG_HEADS, HG_HEAD_DIM, HG_HEAD_DIM), lambda g, t, _: (g, 0, 0, 0))
    in_specs = [fwd, fwd, fwd, fwd, bwd, bwd, bwd, bwd]
    args = [qs, vh, lf_fwd, kk_fwd, qs, vh, lf_bwd, kk_bwd]
    out_specs = [fwd, bwd]
    out_shape = [jax.ShapeDtypeStruct(qs.shape, f32)] * 2
    if has_init:
        in_specs += [state, state]
        args += list(init_states)
    else:
        out_specs += [state, state]
        out_shape += [jax.ShapeDtypeStruct((groups, HG_HEADS, HG_HEAD_DIM, HG_HEAD_DIM), f32)] * 2
    return pl.pallas_call(
        functools.partial(_hgrn_kernel, has_init=has_init),
        grid_spec=pltpu.PrefetchScalarGridSpec(
            num_scalar_prefetch=1,
            grid=(groups, nt),
            in_specs=in_specs,
            out_specs=out_specs,
            scratch_shapes=[pltpu.VMEM((HG_HEADS, HG_HEAD_DIM, HG_HEAD_DIM), f32)] * 2),
        out_shape=out_shape,
        compiler_params=pltpu.CompilerParams(
            dimension_semantics=("arbitrary", "arbitrary"), vmem_limit_bytes=VMEM_LIMIT_BYTES),
        name="hgrn2",
    )(slow, *args)


def _attn_kernel(*refs, n_ctx):
    if n_ctx:
        (qa_ref, ka_ref, va_ref, ck_ref, cv_ref, x_ref, of_ref, ob_ref, sg_ref, m_ref, wo_ref, hg_ref,
         g_ref, b_ref, o_ref, kk_ref, vv_ref) = refs
    else:
        (qa_ref, ka_ref, va_ref, x_ref, of_ref, ob_ref, sg_ref, m_ref, wo_ref, hg_ref,
         g_ref, b_ref, o_ref, kk_ref, vv_ref) = refs
    tq = qa_ref.shape[1]

    @pl.when(pl.program_id(1) == 0)
    def _():
        def fill(lo, k, v):
            n = k.shape[0]
            lane = lax.broadcasted_iota(jnp.int32, k.shape, 1)
            low = lane < HEAD_DIM
            k_sw = pltpu.roll(k, HEAD_DIM, 1)
            v_sw = pltpu.roll(v, HEAD_DIM, 1)
            ones_hi = jnp.where(lane == HEAD_DIM, 1.0, 0.0)
            ones_lo = jnp.where(lane == 0, 1.0, 0.0)
            kk_ref[lo:lo + n, 0:LANES] = jnp.where(low, k, k_sw).astype(bf16)
            kk_ref[lo:lo + n, LANES:2 * LANES] = jnp.where(low, k_sw, k).astype(bf16)
            vv_ref[lo:lo + n, 0:LANES] = jnp.where(low, v, ones_hi).astype(bf16)
            vv_ref[lo:lo + n, LANES:2 * LANES] = jnp.where(low, ones_lo, v_sw).astype(bf16)
            vv_ref[lo:lo + n, 2 * LANES:3 * LANES] = jnp.where(low, v_sw, ones_hi).astype(bf16)
            vv_ref[lo:lo + n, 3 * LANES:4 * LANES] = jnp.where(low, ones_lo, v).astype(bf16)

        if n_ctx:
            fill(0, ck_ref[0], cv_ref[0])
        fill(n_ctx, ka_ref[0], va_ref[0])

    lane = lax.broadcasted_iota(jnp.int32, (tq, LANES), 1)
    low = lane < HEAD_DIM
    pairs = []
    for kv in range(N_KV_HEADS):
        keys = kk_ref[:, kv * LANES:(kv + 1) * LANES]
        v_lo = vv_ref[:, 2 * kv * LANES:(2 * kv + 1) * LANES]
        v_hi = vv_ref[:, (2 * kv + 1) * LANES:(2 * kv + 2) * LANES]
        for p in range(N_HEADS // N_KV_HEADS // 2):
            tile = 2 * kv + p
            qp = qa_ref[0, :, tile * LANES:(tile + 1) * LANES]
            zero = jnp.zeros_like(qp)
            q2 = jnp.concatenate([jnp.where(low, qp, zero), jnp.where(low, zero, qp)], axis=0)
            s = _dot_nt(q2, keys)
            e = jnp.exp2(s - jnp.max(s, axis=-1, keepdims=True)).astype(bf16)
            o_lo = jnp.dot(e[:tq], v_lo, preferred_element_type=f32)
            o_hi = jnp.dot(e[tq:], v_hi, preferred_element_type=f32)
            pairs.append(jnp.where(low, o_lo * (1.0 / o_lo[:, HEAD_DIM:HEAD_DIM + 1]),
                                   o_hi * (1.0 / o_hi[:, 0:1])))
    o_att = jnp.concatenate(pairs, axis=-1)

    o_sum = of_ref[0] + ob_ref[0]
    normed = []
    for hd in range(HG_HEADS):
        oh = o_sum[:, hd * HG_HEAD_DIM:(hd + 1) * HG_HEAD_DIM]
        ms = jnp.mean(oh * oh, axis=-1, keepdims=True)
        normed.append(oh * lax.rsqrt(ms + RMS_EPS))
    o_hg = jnp.concatenate(normed, axis=-1) * hg_ref[...] * sg_ref[0]

    cat = jnp.concatenate([o_hg, o_att], axis=-1).astype(bf16)
    y = jnp.dot(cat, wo_ref[...], preferred_element_type=f32)
    r = ALPHA * x_ref[0] + m_ref[0, 5:6, :] * y
    o_ref[0] = _layer_norm(r, g_ref[1:2, :], b_ref[1:2, :])


def _attn(qa, ka, va, ctx_kv, x, o_f, o_b, sg, mod, mod_group, w_out, hg_gain, ln_g, ln_b):
    groups, length, _ = qa.shape
    tq = Q_TILE
    n_ctx = 0 if ctx_kv is None else ctx_kv[0].shape[1]
    n_keys = n_ctx + length
    tok = lambda width: pl.BlockSpec((1, tq, width), lambda g, t: (g, t, 0))
    whole = lambda n: pl.BlockSpec((1, n, KV_WIDTH), lambda g, t: (g, 0, 0))
    in_specs = [tok(ATT_WIDTH), whole(length), whole(length)]
    args = [qa, ka, va]
    if n_ctx:
        in_specs += [whole(n_ctx), whole(n_ctx)]
        args += list(ctx_kv)
    in_specs += [
        tok(D_MODEL), tok(HG_WIDTH), tok(HG_WIDTH), tok(HG_WIDTH),
        pl.BlockSpec((1, N_MOD, D_MODEL), lambda g, t: (mod_group(g), 0, 0)),
        _resident((D_MODEL, D_MODEL)),
        _resident((1, HG_WIDTH)),
        _resident((3, D_MODEL)),
        _resident((3, D_MODEL)),
    ]
    args += [x, o_f, o_b, sg, mod, w_out, hg_gain, ln_g, ln_b]
    return pl.pallas_call(
        functools.partial(_attn_kernel, n_ctx=n_ctx),
        grid=(groups, length // tq),
        in_specs=in_specs,
        out_specs=tok(D_MODEL),
        out_shape=jax.ShapeDtypeStruct(x.shape, f32),
        scratch_shapes=[pltpu.VMEM((n_keys, N_KV_HEADS * LANES), bf16),
                        pltpu.VMEM((n_keys, 2 * N_KV_HEADS * LANES), bf16)],
        compiler_params=pltpu.CompilerParams(
            dimension_semantics=("arbitrary", "arbitrary"), vmem_limit_bytes=VMEM_LIMIT_BYTES),
        name="attn_out",
    )(*args)


def _rope_tables(n_tokens):
    half = HEAD_DIM // 2
    t = jnp.arange(n_tokens)
    inv = ROPE_THETA ** (-jnp.arange(0, half, 2, dtype=f32) / half)
    ang_row = (t // GRID_W).astype(f32)[:, None] * inv
    ang_col = (t % GRID_W).astype(f32)[:, None] * inv
    cos = jnp.concatenate([jnp.cos(ang_row)] * 2 + [jnp.cos(ang_col)] * 2, axis=-1)
    sin = jnp.concatenate([-jnp.sin(ang_row), jnp.sin(ang_row), -jnp.sin(ang_col), jnp.sin(ang_col)], axis=-1)
    return jnp.tile(cos, (1, N_HEADS)), jnp.tile(sin, (1, N_HEADS))


def kernel(x_prompt, x_sample, cache_k, cache_v, state_hgrn_fwd, state_hgrn_bwd, c, c_ctx, w_mod, b_mod,
           w_ffn1_in, w_ffn1_out, w_ffn2_in, w_ffn2_out, w_in, w_out, q_norm_g, k_norm_g, hg_norm_g,
           lb_logits_fwd, lb_logits_bwd, ln_g, ln_b):
    assert w_mod.shape[0] == DEPTH and lb_logits_fwd.shape[0] == DEPTH + 1
    batch, seq, _ = x_prompt.shape
    dec_batch, dec_seq, _ = x_sample.shape
    past = cache_k.shape[2]

    ctx_row = dec_batch
    rows = 16
    cvecs = jnp.concatenate([c, c_ctx[None, :], jnp.zeros((rows - dec_batch - 1, D_MODEL), f32)], axis=0)
    mod = _modulation(cvecs, w_mod[0], b_mod[0]).reshape(rows, N_MOD, D_MODEL)

    to_bf16 = lambda w: w.astype(bf16)
    w1a, w1u = to_bf16(w_ffn1_in[0][:, :D_FF]), to_bf16(w_ffn1_in[0][:, D_FF:])
    w2a, w2u = to_bf16(w_ffn2_in[0][:, :D_FF]), to_bf16(w_ffn2_in[0][:, D_FF:])
    w1d, w2d = to_bf16(w_ffn1_out[0]), to_bf16(w_ffn2_out[0])
    w_in_b, w_out_b = to_bf16(w_in[0]), to_bf16(w_out[0])
    gains = ln_g[0], ln_b[0]
    q_gain = jnp.tile(q_norm_g[0], N_HEADS).reshape(1, ATT_WIDTH)
    k_gain = jnp.tile(k_norm_g[0], N_KV_HEADS).reshape(1, KV_WIDTH)
    hg_gain = hg_norm_g[0].reshape(1, HG_WIDTH)
    head_of = jnp.arange(ATT_WIDTH) // HEAD_DIM
    ones_bd = (head_of[:, None] == head_of[None, :]).astype(bf16)

    def trunk(x, mod_group, rope_tables, ctx_kv, init_states, hg_groups):
        shape = x.shape
        per_seq = lambda a: a.reshape(hg_groups, -1, a.shape[-1])
        x = _ffn(x, mod, mod_group, w1a, w1u, w1d, *gains, mod_base=0, ln_row=0)
        proj = _proj(x, mod, mod_group, w_in_b, lb_logits_fwd, lb_logits_bwd, q_gain, k_gain, ones_bd, rope_tables)
        qs, vh, lf_f, kk_f, lf_b, kk_b, sg, qa, ka, va, span_f, span_b = map(per_seq, proj)
        scans = _hgrn(qs, vh, lf_f, kk_f, lf_b, kk_b, span_f, span_b, init_states)
        x = _attn(qa, ka, va, ctx_kv, per_seq(x), scans[0], scans[1], sg, mod, mod_group, w_out_b, hg_gain,
                  *gains).reshape(shape)
        x = _ffn(x, mod, mod_group, w2a, w2u, w2d, *gains, mod_base=6, ln_row=2)
        return x, ka, va, scans[2:]

    ctx_group = lambda g: ctx_row
    y_prompt, k_new, v_new, states = trunk(
        x_prompt.reshape(1, batch * seq, D_MODEL), ctx_group, None, None, None, batch)
    y_prompt = y_prompt.reshape(batch, seq, D_MODEL)
    new_cache_k = k_new.reshape(batch, DEPTH, seq, N_KV_HEADS, HEAD_DIM)
    new_cache_v = v_new.reshape(batch, DEPTH, seq, N_KV_HEADS, HEAD_DIM)
    new_state_fwd = states[0].reshape(batch, DEPTH, HG_HEADS, HG_HEAD_DIM, HG_HEAD_DIM)
    new_state_bwd = states[1].reshape(batch, DEPTH, HG_HEADS, HG_HEAD_DIM, HG_HEAD_DIM)

    ctx_kv = (cache_k[:, 0].reshape(dec_batch, past, KV_WIDTH), cache_v[:, 0].reshape(dec_batch, past, KV_WIDTH))
    init_states = (state_hgrn_fwd[:, 0], state_hgrn_bwd[:, 0])
    y_sample, _, _, _ = trunk(x_sample, lambda g: g, _rope_tables(dec_seq), ctx_kv, init_states, dec_batch)

    return (y_prompt, y_sample, new_cache_k, new_cache_v, new_state_fwd, new_state_bwd)
```

```python
import functools

import jax
import jax.numpy as jnp
from jax import lax
from jax.experimental import pallas as pl
from jax.experimental.pallas import tpu as pltpu

f32 = jnp.float32
bf16 = jnp.bfloat16

D_MODEL = 1024
N_MOD = 9
HG_WIDTH = 512
HG_HEAD_DIM = 128
HG_HEADS = 4
ATT_WIDTH = 512
HEAD_DIM = 64
N_HEADS = 8
N_KV_HEADS = 2
KV_WIDTH = 128
IN_WIDTH = 5 * HG_WIDTH + ATT_WIDTH + 2 * KV_WIDTH
D_FF = 2816
GRID_W = 64
ROPE_THETA = 10000.0
DEPTH = 1
ALPHA = (2.0 * DEPTH) ** 0.25
LOG2_E = 1.4426950408889634
LN_EPS = 1e-6
RMS_EPS = 1e-6

LANES = 128
VMEM_LIMIT_BYTES = 56 * 1024 * 1024

FF_CHUNK = 256
TOKEN_TILE = 512
HG_TILE = 128
Q_TILE = 256
KEY_CHUNK = 512
V_ROWS = 80
HG_FAST_BLOCK = 64
HG_FAST_SPAN_LOG2 = 100.0


def _silu(x):
    return x * jax.nn.sigmoid(x)


def _layer_norm(r, g, b):
    mu = jnp.mean(r, axis=-1, keepdims=True)
    c = r - mu
    var = jnp.mean(c * c, axis=-1, keepdims=True)
    return c * lax.rsqrt(var + LN_EPS) * g + b


def _split_bf16(x, parts):
    out = []
    r = x
    for _ in range(parts - 1):
        p = r.astype(bf16)
        out.append(p)
        r = r - p.astype(f32)
    out.append(r.astype(bf16))
    return out


def _dot01(mat01, x, parts, *, mat_on_left):
    acc = None
    for p in _split_bf16(x, parts):
        d = (jnp.dot(mat01, p, preferred_element_type=f32) if mat_on_left
             else jnp.dot(p, mat01, preferred_element_type=f32))
        acc = d if acc is None else acc + d
    return acc


def _dot_nt(a, b):
    return lax.dot_general(a, b, (((1,), (1,)), ((), ())), preferred_element_type=f32)


def _dot_tn(a, b):
    return lax.dot_general(a, b, (((0,), (0,)), ((), ())), preferred_element_type=f32)


def _mod_kernel(c_ref, w_ref, b_ref, o_ref):
    a = _silu(c_ref[...]).astype(bf16)
    o_ref[...] = jnp.dot(a, w_ref[...].astype(bf16), preferred_element_type=f32) + b_ref[...]


def _modulation(cvecs, w_mod, b_mod):
    rows = cvecs.shape[0]
    n_out = w_mod.shape[1]
    tn = D_MODEL
    return pl.pallas_call(
        _mod_kernel,
        grid=(n_out // tn,),
        in_specs=[
            pl.BlockSpec((rows, D_MODEL), lambda j: (0, 0)),
            pl.BlockSpec((D_MODEL, tn), lambda j: (0, j)),
            pl.BlockSpec((1, tn), lambda j: (0, j)),
        ],
        out_specs=pl.BlockSpec((rows, tn), lambda j: (0, j)),
        out_shape=jax.ShapeDtypeStruct((rows, n_out), f32),
        compiler_params=pltpu.CompilerParams(dimension_semantics=("arbitrary",)),
        name="modulation",
    )(cvecs, w_mod, b_mod.reshape(1, n_out))


def _ffn_kernel(x_ref, m_ref, wup_ref, wd_ref, g_ref, b_ref, o_ref, act_ref, *, mod_base, ln_row):
    x = x_ref[0]
    shift = m_ref[0, mod_base:mod_base + 1, :]
    scale = m_ref[0, mod_base + 1:mod_base + 2, :]
    gate = m_ref[0, mod_base + 2:mod_base + 3, :]
    h = (x * (1.0 + scale) + shift).astype(bf16)
    for j in range(D_FF // FF_CHUNK):
        cols = slice(j * FF_CHUNK, (j + 1) * FF_CHUNK)
        a = jnp.dot(h, wup_ref[:, cols], preferred_element_type=f32)
        u = jnp.dot(h, wup_ref[:, D_FF + j * FF_CHUNK:D_FF + (j + 1) * FF_CHUNK], preferred_element_type=f32)
        act_ref[:, cols] = (_silu(a) * u).astype(bf16)
    y = jnp.dot(act_ref[...], wd_ref[...], preferred_element_type=f32)
    r = ALPHA * x + 0.5 * gate * y
    o_ref[0] = _layer_norm(r, g_ref[ln_row:ln_row + 1, :], b_ref[ln_row:ln_row + 1, :])


def _resident(shape):
    return pl.BlockSpec(shape, lambda *_: (0,) * len(shape), pipeline_mode=pl.Buffered(1))


def _ffn(x, mod, mod_group, w_up, wd, ln_g, ln_b, *, mod_base, ln_row):
    groups, length, _ = x.shape
    tm = TOKEN_TILE
    return pl.pallas_call(
        functools.partial(_ffn_kernel, mod_base=mod_base, ln_row=ln_row),
        grid=(groups, length // tm),
        in_specs=[
            pl.BlockSpec((1, tm, D_MODEL), lambda g, t: (g, t, 0)),
            pl.BlockSpec((1, N_MOD, D_MODEL), lambda g, t: (mod_group(g), 0, 0)),
            _resident((D_MODEL, 2 * D_FF)),
            _resident((D_FF, D_MODEL)),
            _resident((3, D_MODEL)),
            _resident((3, D_MODEL)),
        ],
        out_specs=pl.BlockSpec((1, tm, D_MODEL), lambda g, t: (g, t, 0)),
        out_shape=jax.ShapeDtypeStruct(x.shape, f32),
        scratch_shapes=[pltpu.VMEM((tm, D_FF), bf16)],
        compiler_params=pltpu.CompilerParams(
            dimension_semantics=("arbitrary", "arbitrary"), vmem_limit_bytes=VMEM_LIMIT_BYTES),
        name="ffn",
    )(x, mod, w_up, wd, ln_g, ln_b)


def _head_rms_norm(x, ones_bd, gain):
    ss = _dot01(ones_bd, x * x, 2, mat_on_left=False)
    return x * lax.rsqrt(ss * (1.0 / HEAD_DIM) + RMS_EPS) * gain


def _rope(x, cos, sin_signed):
    width = x.shape[-1]
    lane = lax.broadcasted_iota(jnp.int32, x.shape, 1)
    from_right = pltpu.roll(x, width - 16, 1)
    from_left = pltpu.roll(x, 16, 1)
    partner = jnp.where((lane & 31) < 16, from_right, from_left)
    return x * cos + partner * sin_signed


def _proj_kernel(*refs, rope):
    if rope:
        (x_ref, m_ref, w_ref, lbf_ref, lbb_ref, qg_ref, kg_ref, bd_ref, cos_ref, sin_ref,
         qs_ref, vh_ref, lff_ref, kkf_ref, lfb_ref, kkb_ref, sg_ref, qa_ref, ka_ref, vt_ref,
         spf_ref, spb_ref) = refs
    else:
        (x_ref, m_ref, w_ref, lbf_ref, lbb_ref, qg_ref, kg_ref, bd_ref,
         qs_ref, vh_ref, lff_ref, kkf_ref, lfb_ref, kkb_ref, sg_ref, qa_ref, ka_ref, vt_ref,
         spf_ref, spb_ref, kt_ref) = refs
    x = x_ref[0]
    shift = m_ref[0, 3:4, :]
    scale = m_ref[0, 4:5, :]
    h = (x * (1.0 + scale) + shift).astype(bf16)

    def cols(lo, width):
        return jnp.dot(h, w_ref[:, lo:lo + width], preferred_element_type=f32)

    def lower_bound(lb_ref):
        l0 = lb_ref[0:1, :]
        l1 = lb_ref[1:2, :]
        m = jnp.maximum(l0, l1)
        e0 = jnp.exp(l0 - m)
        e1 = jnp.exp(l1 - m)
        return e0 / (e0 + e1)

    def forget_gate(lb_ref, lo, lf_ref, kk_ref, span_ref):
        lb = lower_bound(lb_ref)
        f = lb + (1.0 - lb) * jax.nn.sigmoid(cols(lo, HG_WIDTH))
        lf = jnp.log2(f)
        lf_ref[0] = lf
        kk_ref[0] = 1.0 - f
        half = HG_FAST_BLOCK // 2
        sums = jnp.sum(lf.reshape(lf.shape[0] // half, half, HG_WIDTH), axis=1)
        span_ref[0] = jnp.broadcast_to(jnp.max(jnp.abs(sums), axis=-1, keepdims=True), span_ref.shape[1:])

    qs_ref[0] = _silu(cols(0, HG_WIDTH))
    vh_ref[0] = cols(HG_WIDTH, HG_WIDTH)
    forget_gate(lbf_ref, 2 * HG_WIDTH, lff_ref, kkf_ref, spf_ref)
    forget_gate(lbb_ref, 3 * HG_WIDTH, lfb_ref, kkb_ref, spb_ref)
    sg_ref[0] = _silu(cols(4 * HG_WIDTH, HG_WIDTH))

    base = 5 * HG_WIDTH
    q = _head_rms_norm(cols(base, ATT_WIDTH), bd_ref[...], qg_ref[...])
    k = _head_rms_norm(cols(base + ATT_WIDTH, KV_WIDTH), bd_ref[0:KV_WIDTH, 0:KV_WIDTH], kg_ref[...])
    if rope:
        q = _rope(q, cos_ref[...], sin_ref[...])
        k = _rope(k, cos_ref[:, 0:KV_WIDTH], sin_ref[:, 0:KV_WIDTH])
    qa_ref[0] = (q * (HEAD_DIM ** -0.5 * LOG2_E)).astype(bf16)
    ka_ref[0] = k
    v = cols(base + ATT_WIDTH + KV_WIDTH, KV_WIDTH)
    n_piece, _, piece = vt_ref.shape
    for i in range(n_piece):
        vt_ref[i] = v[i * piece:(i + 1) * piece].T
        if not rope:
            kt_ref[i] = k[i * piece:(i + 1) * piece].T


def _proj(x, mod, mod_group, w_in, lb_f, lb_b, q_gain, k_gain, ones_bd, rope_tables, seq_len):
    groups, length, _ = x.shape
    tm = TOKEN_TILE
    rope = rope_tables is not None
    n_seq = groups * length // seq_len
    if seq_len >= tm:
        tiles_per_seq = seq_len // tm
        t_spec = pl.BlockSpec((1, KV_WIDTH, tm), lambda g, t: (g * (length // seq_len) + t // tiles_per_seq, 0,
                                                               t % tiles_per_seq))
    else:
        t_spec = pl.BlockSpec((tm // seq_len, KV_WIDTH, seq_len), lambda g, t: (g * (length // tm) + t, 0, 0))
    t_shape = jax.ShapeDtypeStruct((n_seq, KV_WIDTH, seq_len), f32)
    tok = lambda width: pl.BlockSpec((1, tm, width), lambda g, t: (g, t, 0))
    in_specs = [
        tok(D_MODEL),
        pl.BlockSpec((1, N_MOD, D_MODEL), lambda g, t: (mod_group(g), 0, 0)),
        _resident((D_MODEL, IN_WIDTH)),
        _resident((2, HG_WIDTH)),
        _resident((2, HG_WIDTH)),
        _resident((1, ATT_WIDTH)),
        _resident((1, KV_WIDTH)),
        _resident((ATT_WIDTH, ATT_WIDTH)),
    ]
    args = [x, mod, w_in, lb_f, lb_b, q_gain, k_gain, ones_bd]
    if rope:
        in_specs += [pl.BlockSpec((tm, ATT_WIDTH), lambda g, t: (t, 0))] * 2
        args += list(rope_tables)
    shape = lambda width, dt: jax.ShapeDtypeStruct((groups, length, width), dt)
    half = HG_FAST_BLOCK // 2
    span_spec = pl.BlockSpec((1, tm // half, LANES), lambda g, t: (g, t, 0))
    span_shape = jax.ShapeDtypeStruct((groups, length // half, LANES), f32)
    return pl.pallas_call(
        functools.partial(_proj_kernel, rope=rope),
        grid=(groups, length // tm),
        in_specs=in_specs,
        out_specs=([tok(HG_WIDTH)] * 7 + [tok(ATT_WIDTH), tok(KV_WIDTH), t_spec, span_spec, span_spec]
                   + ([] if rope else [t_spec])),
        out_shape=([shape(HG_WIDTH, f32)] * 7 + [shape(ATT_WIDTH, bf16), shape(KV_WIDTH, f32), t_shape,
                                                  span_shape, span_shape] + ([] if rope else [t_shape])),
        compiler_params=pltpu.CompilerParams(
            dimension_semantics=("arbitrary", "arbitrary"), vmem_limit_bytes=VMEM_LIMIT_BYTES),
        name="mixer_proj",
    )(*args)


def _hgrn_direction(qs, v, lf, kk, st_ref, o_ref, *, rev):
    tc = qs.shape[0]
    row = lax.broadcasted_iota(jnp.int32, (tc, tc), 0)
    col = lax.broadcasted_iota(jnp.int32, (tc, tc), 1)
    ordered = (row < col) if rev else (row > col)
    differ = row ^ col
    tri = jnp.where((col >= row) if rev else (col <= row), 1.0, 0.0).astype(bf16)
    cum = _dot01(tri, lf, 3, mat_on_left=True)
    total = cum[0:1] if rev else cum[tc - 1:tc]
    vb = v.astype(bf16)
    carry = jnp.exp2(total)
    query_half = 0 if rev else 1
    heads = [slice(hd * HG_HEAD_DIM, (hd + 1) * HG_HEAD_DIM) for hd in range(HG_HEADS)]

    def level_exponent(s, ridx):
        if s == 1:
            return jnp.where((ridx & 1) == query_half, lf, 0.0)
        if s == 2:
            nxt = pltpu.roll(lf, tc - 1, 0)
            prv = pltpu.roll(lf, 1, 0)
            m4 = ridx & 3
            if rev:
                return jnp.where(m4 == 0, lf + nxt, jnp.where(m4 == 1, lf, jnp.where(m4 == 2, 0.0, prv)))
            return jnp.where(m4 == 0, nxt, jnp.where(m4 == 1, 0.0, jnp.where(m4 == 2, lf, lf + prv)))
        blocks = []
        for lo in range(0, tc, 2 * s):
            anchor = lo + (s if rev else s - 1)
            blocks.append(cum[lo:lo + 2 * s] - cum[anchor:anchor + 1])
        d = jnp.concatenate(blocks, axis=0)
        is_query = ((ridx >> (s.bit_length() - 1)) & 1) == query_half
        return jnp.where(is_query, d, -d)

    def add_levels(intra, first):
        ridx = lax.broadcasted_iota(jnp.int32, qs.shape, 0)
        s = first
        while s < tc:
            shift = s.bit_length() - 1
            is_query = ((ridx >> shift) & 1) == query_half
            mixed = (jnp.where(is_query, qs, kk) * jnp.exp2(level_exponent(s, ridx))).astype(bf16)
            pair = jnp.logical_and((differ >> shift) == 1, ordered)
            for hd, hs in enumerate(heads):
                intra[hd] = jnp.where(pair, _dot_nt(mixed[:, hs], mixed[:, hs]), intra[hd])
            s *= 2
        return intra

    def finish(intra, q_dec, k_end, extra):
        for hd, hs in enumerate(heads):
            st = st_ref[hd]
            o = (jnp.dot(intra[hd].astype(bf16), vb[:, hs], preferred_element_type=f32)
                 + _dot_nt(q_dec[:, hs], st.astype(bf16)))
            o_ref[0, :, hs] = o if extra is None else o + extra[:, hs]
            st_ref[hd] = st * carry[:, hs] + _dot_tn(vb[:, hs], k_end[:, hs])

    blk = HG_FAST_BLOCK
    anchors = [lo + (blk // 2 if rev else blk // 2 - 1) for lo in range(0, tc, blk)]

    def mid_split_blocks():
        q_mid, k_mid, q_dec, k_end = [], [], [], []
        for lo, a in zip(range(0, tc, blk), anchors):
            rel = cum[lo:lo + blk] - cum[a:a + 1]
            q_mid.append(qs[lo:lo + blk] * jnp.exp2(rel))
            k_mid.append(kk[lo:lo + blk] * jnp.exp2(-rel))
            q_dec.append(q_mid[-1] * jnp.exp2(cum[a:a + 1]))
            k_end.append(k_mid[-1] * jnp.exp2(total - cum[a:a + 1]))
        q_mid, k_mid, q_dec, k_end = (jnp.concatenate(p, axis=0).astype(bf16) for p in (q_mid, k_mid, q_dec, k_end))
        same_block = (differ >> (blk.bit_length() - 1)) == 0
        keep = jnp.logical_and(same_block, (row <= col) if rev else (row >= col))
        intra = [jnp.where(keep, _dot_nt(q_mid[:, hs], k_mid[:, hs]), 0.0) for hs in heads]
        finish(add_levels(intra, blk), q_dec, k_end, None)

    def all_levels():
        intra = add_levels([jnp.zeros((tc, tc), f32)] * HG_HEADS, 1)
        q_dec = (qs * jnp.exp2(cum)).astype(bf16)
        k_end = (kk * jnp.exp2(total - cum)).astype(bf16)
        qk = qs * kk
        own = jnp.concatenate(
            [jnp.sum(qk[:, hs], axis=-1, keepdims=True) * v[:, hs] for hs in heads], axis=-1)
        finish(intra, q_dec, k_end, own)

    return mid_split_blocks, all_levels


def _hgrn_kernel(slow_ref, *refs, has_init):
    if has_init:
        (qsf_ref, vf_ref, lff_ref, kkf_ref, qsb_ref, vb_ref, lfb_ref, kkb_ref, s0f_ref, s0b_ref,
         of_ref, ob_ref, stf_ref, stb_ref) = refs
    else:
        (qsf_ref, vf_ref, lff_ref, kkf_ref, qsb_ref, vb_ref, lfb_ref, kkb_ref,
         of_ref, ob_ref, sf_out_ref, sb_out_ref, stf_ref, stb_ref) = refs
    t = pl.program_id(1)

    @pl.when(t == 0)
    def _():
        for hd in range(HG_HEADS):
            if has_init:
                stf_ref[hd] = s0f_ref[0, hd].T
                stb_ref[hd] = s0b_ref[0, hd].T
            else:
                stf_ref[hd] = jnp.zeros((HG_HEAD_DIM, HG_HEAD_DIM), f32)
                stb_ref[hd] = jnp.zeros((HG_HEAD_DIM, HG_HEAD_DIM), f32)

    fwd = _hgrn_direction(qsf_ref[0], vf_ref[0], lff_ref[0], kkf_ref[0], stf_ref, of_ref, rev=False)
    bwd = _hgrn_direction(qsb_ref[0], vb_ref[0], lfb_ref[0], kkb_ref[0], stb_ref, ob_ref, rev=True)
    slow = slow_ref[pl.program_id(0) * pl.num_programs(1) + t] != 0

    @pl.when(jnp.logical_not(slow))
    def _():
        fwd[0]()
        bwd[0]()

    @pl.when(slow)
    def _():
        fwd[1]()
        bwd[1]()

    if not has_init:
        @pl.when(t == pl.num_programs(1) - 1)
        def _():
            for hd in range(HG_HEADS):
                sf_out_ref[0, hd] = stf_ref[hd].T
                sb_out_ref[0, hd] = stb_ref[hd].T


def _hgrn(qs, vh, lf_fwd, kk_fwd, lf_bwd, kk_bwd, span_fwd, span_bwd, init_states):
    groups, length, _ = qs.shape
    nt = length // HG_TILE
    has_init = init_states is not None
    per_chunk = lambda span: jnp.max(span[:, :, 0].reshape(groups, nt, -1), axis=-1) > HG_FAST_SPAN_LOG2
    slow = jnp.logical_or(per_chunk(span_fwd), per_chunk(span_bwd)[:, ::-1]).astype(jnp.int32).reshape(-1)
    fwd = pl.BlockSpec((1, HG_TILE, HG_WIDTH), lambda g, t, _: (g, t, 0))
    bwd = pl.BlockSpec((1, HG_TILE, HG_WIDTH), lambda g, t, _: (g, nt - 1 - t, 0))
    state = pl.BlockSpec((1, HG_HEADS, HG_HEAD_DIM, HG_HEAD_DIM), lambda g, t, _: (g, 0, 0, 0))
    in_specs = [fwd, fwd, fwd, fwd, bwd, bwd, bwd, bwd]
    args = [qs, vh, lf_fwd, kk_fwd, qs, vh, lf_bwd, kk_bwd]
    out_specs = [fwd, bwd]
    out_shape = [jax.ShapeDtypeStruct(qs.shape, f32)] * 2
    if has_init:
        in_specs += [state, state]
        args += list(init_states)
    else:
        out_specs += [state, state]
        out_shape += [jax.ShapeDtypeStruct((groups, HG_HEADS, HG_HEAD_DIM, HG_HEAD_DIM), f32)] * 2
    return pl.pallas_call(
        functools.partial(_hgrn_kernel, has_init=has_init),
        grid_spec=pltpu.PrefetchScalarGridSpec(
            num_scalar_prefetch=1,
            grid=(groups, nt),
            in_specs=in_specs,
            out_specs=out_specs,
            scratch_shapes=[pltpu.VMEM((HG_HEADS, HG_HEAD_DIM, HG_HEAD_DIM), f32)] * 2),
        out_shape=out_shape,
        compiler_params=pltpu.CompilerParams(
            dimension_semantics=("arbitrary", "arbitrary"), vmem_limit_bytes=VMEM_LIMIT_BYTES),
        name="hgrn2",
    )(slow, *args)


def _attn_kernel(*refs, n_ctx):
    if n_ctx:
        (qa_ref, ka_ref, vt_ref, ckt_ref, cvt_ref, x_ref, of_ref, ob_ref, sg_ref, m_ref, wo_ref, hg_ref,
         g_ref, b_ref, o_ref, kk_ref, vta_ref) = refs
    else:
        (qa_ref, ka_ref, vt_ref, x_ref, of_ref, ob_ref, sg_ref, m_ref, wo_ref, hg_ref,
         g_ref, b_ref, o_ref, kk_ref, vta_ref) = refs
    tq = qa_ref.shape[1]

    @pl.when(pl.program_id(1) == 0)
    def _():
        def fill_keys(lo, k):
            n = k.shape[0]
            low = lax.broadcasted_iota(jnp.int32, k.shape, 1) < HEAD_DIM
            k_sw = pltpu.roll(k, HEAD_DIM, 1)
            kk_ref[lo:lo + n, 0:LANES] = jnp.where(low, k, k_sw).astype(bf16)
            kk_ref[lo:lo + n, LANES:2 * LANES] = jnp.where(low, k_sw, k).astype(bf16)

        def fill_values(lo, vt):
            n = vt.shape[1]
            ones_row = jnp.where(lax.broadcasted_iota(jnp.int32, (V_ROWS - HEAD_DIM, n), 0) == 0, 1.0, 0.0)
            for kv in range(N_KV_HEADS):
                vta_ref[kv * V_ROWS:kv * V_ROWS + HEAD_DIM, lo:lo + n] = (
                    vt[kv * HEAD_DIM:(kv + 1) * HEAD_DIM].astype(bf16))
                vta_ref[kv * V_ROWS + HEAD_DIM:(kv + 1) * V_ROWS, lo:lo + n] = ones_row.astype(bf16)

        if n_ctx:
            fill_keys(0, ckt_ref[0].T)
            fill_values(0, cvt_ref[0])
        fill_keys(n_ctx, ka_ref[0])
        fill_values(n_ctx, vt_ref[0])

    low = lax.broadcasted_iota(jnp.int32, (tq, LANES), 1) < HEAD_DIM
    n_keys = kk_ref.shape[0]
    kc = min(KEY_CHUNK, n_keys)
    def masked_pair(tile):
        qp = qa_ref[0, :, tile * LANES:(tile + 1) * LANES]
        zero = jnp.zeros_like(qp)
        return jnp.concatenate([jnp.where(low, qp, zero), jnp.where(low, zero, qp)], axis=0)

    q_pairs = [masked_pair(tile) for tile in range(N_HEADS // 2)]
    pairs_per_kv = N_HEADS // N_KV_HEADS // 2
    work = [(tile, lo) for tile in range(N_HEADS // 2) for lo in range(0, n_keys, kc)]

    def scores(tile, lo):
        kv = tile // pairs_per_kv
        return _dot_nt(kk_ref[lo:lo + kc, kv * LANES:(kv + 1) * LANES], q_pairs[tile])

    heads_t = []
    st_next = scores(*work[0])
    m = acc = None
    for i, (tile, lo) in enumerate(work):
        st = st_next
        if i + 1 < len(work):
            st_next = scores(*work[i + 1])
        kv = tile // pairs_per_kv
        m_chunk = jnp.max(st, axis=0, keepdims=True)
        m_new = m_chunk if m is None else jnp.maximum(m, m_chunk)
        e = jnp.exp2(st - m_new).astype(bf16)
        pv = jnp.dot(vta_ref[kv * V_ROWS:(kv + 1) * V_ROWS, lo:lo + kc], e,
                     preferred_element_type=f32)
        acc = pv if acc is None else acc * jnp.exp2(m - m_new) + pv
        m = m_new
        if lo + kc == n_keys:
            on = acc[0:HEAD_DIM] * (1.0 / acc[HEAD_DIM:HEAD_DIM + 1])
            heads_t += [on[:, :tq], on[:, tq:]]
            m = acc = None
    o_att_t = jnp.concatenate(heads_t, axis=0).astype(bf16)

    o_sum = of_ref[0] + ob_ref[0]
    normed = []
    for hd in range(HG_HEADS):
        oh = o_sum[:, hd * HG_HEAD_DIM:(hd + 1) * HG_HEAD_DIM]
        ms = jnp.mean(oh * oh, axis=-1, keepdims=True)
        normed.append(oh * lax.rsqrt(ms + RMS_EPS))
    o_hg = (jnp.concatenate(normed, axis=-1) * hg_ref[...] * sg_ref[0]).astype(bf16)

    y = (jnp.dot(o_hg, wo_ref[0:HG_WIDTH, :], preferred_element_type=f32)
         + _dot_tn(o_att_t, wo_ref[HG_WIDTH:HG_WIDTH + ATT_WIDTH, :]))
    r = ALPHA * x_ref[0] + m_ref[0, 5:6, :] * y
    o_ref[0] = _layer_norm(r, g_ref[1:2, :], b_ref[1:2, :])


def _attn(qa, ka, vt, ctx_kv_t, x, o_f, o_b, sg, mod, mod_group, w_out, hg_gain, ln_g, ln_b):
    groups, length, _ = qa.shape
    tq = Q_TILE
    n_ctx = 0 if ctx_kv_t is None else ctx_kv_t[0].shape[2]
    n_keys = n_ctx + length
    tok = lambda width: pl.BlockSpec((1, tq, width), lambda g, t: (g, t, 0))
    whole_t = lambda n: pl.BlockSpec((1, KV_WIDTH, n), lambda g, t: (g, 0, 0))
    in_specs = [tok(ATT_WIDTH), pl.BlockSpec((1, length, KV_WIDTH), lambda g, t: (g, 0, 0)), whole_t(length)]
    args = [qa, ka, vt]
    if n_ctx:
        in_specs += [whole_t(n_ctx), whole_t(n_ctx)]
        args += list(ctx_kv_t)
    in_specs += [
        tok(D_MODEL), tok(HG_WIDTH), tok(HG_WIDTH), tok(HG_WIDTH),
        pl.BlockSpec((1, N_MOD, D_MODEL), lambda g, t: (mod_group(g), 0, 0)),
        _resident((D_MODEL, D_MODEL)),
        _resident((1, HG_WIDTH)),
        _resident((3, D_MODEL)),
        _resident((3, D_MODEL)),
    ]
    args += [x, o_f, o_b, sg, mod, w_out, hg_gain, ln_g, ln_b]
    return pl.pallas_call(
        functools.partial(_attn_kernel, n_ctx=n_ctx),
        grid=(groups, length // tq),
        in_specs=in_specs,
        out_specs=tok(D_MODEL),
        out_shape=jax.ShapeDtypeStruct(x.shape, f32),
        scratch_shapes=[pltpu.VMEM((n_keys, N_KV_HEADS * LANES), bf16),
                        pltpu.VMEM((N_KV_HEADS * V_ROWS, n_keys), bf16)],
        compiler_params=pltpu.CompilerParams(
            dimension_semantics=("arbitrary", "arbitrary"), vmem_limit_bytes=VMEM_LIMIT_BYTES),
        name="attn_out",
    )(*args)


def _rope_tables(n_tokens):
    half = HEAD_DIM // 2
    t = jnp.arange(n_tokens)
    inv = ROPE_THETA ** (-jnp.arange(0, half, 2, dtype=f32) / half)
    ang_row = (t // GRID_W).astype(f32)[:, None] * inv
    ang_col = (t % GRID_W).astype(f32)[:, None] * inv
    cos = jnp.concatenate([jnp.cos(ang_row)] * 2 + [jnp.cos(ang_col)] * 2, axis=-1)
    sin = jnp.concatenate([-jnp.sin(ang_row), jnp.sin(ang_row), -jnp.sin(ang_col), jnp.sin(ang_col)], axis=-1)
    return jnp.tile(cos, (1, N_HEADS)), jnp.tile(sin, (1, N_HEADS))


def kernel(x_prompt, x_sample, cache_k, cache_v, state_hgrn_fwd, state_hgrn_bwd, c, c_ctx, w_mod, b_mod,
           w_ffn1_in, w_ffn1_out, w_ffn2_in, w_ffn2_out, w_in, w_out, q_norm_g, k_norm_g, hg_norm_g,
           lb_logits_fwd, lb_logits_bwd, ln_g, ln_b):
    assert w_mod.shape[0] == DEPTH and lb_logits_fwd.shape[0] == DEPTH + 1
    batch, seq, _ = x_prompt.shape
    dec_batch, dec_seq, _ = x_sample.shape
    past = cache_k.shape[2]

    ctx_row = dec_batch
    rows = 16
    cvecs = jnp.concatenate([c, c_ctx[None, :], jnp.zeros((rows - dec_batch - 1, D_MODEL), f32)], axis=0)
    mod = _modulation(cvecs, w_mod[0], b_mod[0]).reshape(rows, N_MOD, D_MODEL)

    to_bf16 = lambda w: w.astype(bf16)
    w1u, w2u = to_bf16(w_ffn1_in[0]), to_bf16(w_ffn2_in[0])
    w1d, w2d = to_bf16(w_ffn1_out[0]), to_bf16(w_ffn2_out[0])
    w_in_b, w_out_b = to_bf16(w_in[0]), to_bf16(w_out[0])
    gains = ln_g[0], ln_b[0]
    q_gain = jnp.tile(q_norm_g[0], N_HEADS).reshape(1, ATT_WIDTH)
    k_gain = jnp.tile(k_norm_g[0], N_KV_HEADS).reshape(1, KV_WIDTH)
    hg_gain = hg_norm_g[0].reshape(1, HG_WIDTH)
    head_of = jnp.arange(ATT_WIDTH) // HEAD_DIM
    ones_bd = (head_of[:, None] == head_of[None, :]).astype(bf16)

    def trunk(x, mod_group, rope_tables, ctx_kv_t, init_states, hg_groups):
        shape = x.shape
        seq_len = shape[0] * shape[1] // hg_groups
        per_seq = lambda a: a.reshape(hg_groups, seq_len, a.shape[-1])
        x = _ffn(x, mod, mod_group, w1u, w1d, *gains, mod_base=0, ln_row=0)
        proj = _proj(x, mod, mod_group, w_in_b, lb_logits_fwd, lb_logits_bwd, q_gain, k_gain, ones_bd, rope_tables,
                     seq_len)
        qs, vh, lf_f, kk_f, lf_b, kk_b, sg, qa, ka = map(per_seq, proj[:9])
        vt, span_f, span_b = proj[9], proj[10].reshape(hg_groups, -1, LANES), proj[11].reshape(hg_groups, -1, LANES)
        scans = _hgrn(qs, vh, lf_f, kk_f, lf_b, kk_b, span_f, span_b, init_states)
        x = _attn(qa, ka, vt, ctx_kv_t, per_seq(x), scans[0], scans[1], sg, mod, mod_group, w_out_b, hg_gain,
                  *gains).reshape(shape)
        x = _ffn(x, mod, mod_group, w2u, w2d, *gains, mod_base=6, ln_row=2)
        return x, proj[12:], vt, scans[2:]

    ctx_group = lambda g: ctx_row
    y_prompt, (kt_new,), vt_new, states = trunk(
        x_prompt.reshape(1, batch * seq, D_MODEL), ctx_group, None, None, None, batch)
    y_prompt = y_prompt.reshape(batch, seq, D_MODEL)
    to_cache = lambda t: t.reshape(batch, DEPTH, N_KV_HEADS, HEAD_DIM, seq).transpose(0, 1, 4, 2, 3)
    new_cache_k, new_cache_v = to_cache(kt_new), to_cache(vt_new)
    new_state_fwd = states[0].reshape(batch, DEPTH, HG_HEADS, HG_HEAD_DIM, HG_HEAD_DIM)
    new_state_bwd = states[1].reshape(batch, DEPTH, HG_HEADS, HG_HEAD_DIM, HG_HEAD_DIM)

    from_cache = lambda t: t[:, 0].transpose(0, 2, 3, 1).reshape(dec_batch, KV_WIDTH, past)
    init_states = (state_hgrn_fwd[:, 0], state_hgrn_bwd[:, 0])
    y_sample, _, _, _ = trunk(x_sample, lambda g: g, _rope_tables(dec_seq), (from_cache(cache_k), from_cache(cache_v)),
                              init_states, dec_batch)

    return (y_prompt, y_sample, new_cache_k, new_cache_v, new_state_fwd, new_state_bwd)
```

```python
import functools

import jax
import jax.numpy as jnp
from jax import lax
from jax.experimental import pallas as pl
from jax.experimental.pallas import tpu as pltpu

f32 = jnp.float32
bf16 = jnp.bfloat16

D_MODEL = 1024
N_MOD = 9
HG_WIDTH = 512
HG_HEAD_DIM = 128
HG_HEADS = 4
ATT_WIDTH = 512
HEAD_DIM = 64
N_HEADS = 8
N_KV_HEADS = 2
KV_WIDTH = 128
IN_WIDTH = 5 * HG_WIDTH + ATT_WIDTH + 2 * KV_WIDTH
D_FF = 2816
GRID_W = 64
ROPE_THETA = 10000.0
DEPTH = 1
ALPHA = (2.0 * DEPTH) ** 0.25
LOG2_E = 1.4426950408889634
LN_EPS = 1e-6
RMS_EPS = 1e-6

LANES = 128
VMEM_LIMIT_BYTES = 56 * 1024 * 1024

FF_CHUNK = 256
FFN_TILE = 1024
FFN_SUB = 512
TOKEN_TILE = 512
HG_TILE = 128
HG_STEP = 256
Q_TILE = 256
KEY_CHUNK = 512
V_ROWS = 80
HG_FAST_BLOCK = 64
HG_FAST_SPAN_LOG2 = 100.0


def _silu(x):
    return x * jax.nn.sigmoid(x)


def _layer_norm(r, g, b):
    mu = jnp.mean(r, axis=-1, keepdims=True)
    c = r - mu
    var = jnp.mean(c * c, axis=-1, keepdims=True)
    return c * lax.rsqrt(var + LN_EPS) * g + b


def _split_bf16(x, parts):
    out = []
    r = x
    for _ in range(parts - 1):
        p = r.astype(bf16)
        out.append(p)
        r = r - p.astype(f32)
    out.append(r.astype(bf16))
    return out


def _dot01(mat01, x, parts, *, mat_on_left):
    acc = None
    for p in _split_bf16(x, parts):
        d = (jnp.dot(mat01, p, preferred_element_type=f32) if mat_on_left
             else jnp.dot(p, mat01, preferred_element_type=f32))
        acc = d if acc is None else acc + d
    return acc


def _dot_nt(a, b):
    return lax.dot_general(a, b, (((1,), (1,)), ((), ())), preferred_element_type=f32)


def _dot_tn(a, b):
    return lax.dot_general(a, b, (((0,), (0,)), ((), ())), preferred_element_type=f32)


def _mod_kernel(c_ref, w_ref, b_ref, o_ref):
    a = _silu(c_ref[...]).astype(bf16)
    o_ref[...] = jnp.dot(a, w_ref[...].astype(bf16), preferred_element_type=f32) + b_ref[...]


def _modulation(cvecs, w_mod, b_mod):
    rows = cvecs.shape[0]
    n_out = w_mod.shape[1]
    tn = D_MODEL
    return pl.pallas_call(
        _mod_kernel,
        grid=(n_out // tn,),
        in_specs=[
            pl.BlockSpec((rows, D_MODEL), lambda j: (0, 0)),
            pl.BlockSpec((D_MODEL, tn), lambda j: (0, j)),
            pl.BlockSpec((1, tn), lambda j: (0, j)),
        ],
        out_specs=pl.BlockSpec((rows, tn), lambda j: (0, j)),
        out_shape=jax.ShapeDtypeStruct((rows, n_out), f32),
        compiler_params=pltpu.CompilerParams(dimension_semantics=("arbitrary",)),
        name="modulation",
    )(cvecs, w_mod, b_mod.reshape(1, n_out))


def _ffn_kernel(x_ref, m_ref, wup_ref, wd_ref, g_ref, b_ref, o_ref, act_ref, *, mod_base, ln_row):
    shift = m_ref[0, mod_base:mod_base + 1, :]
    scale = m_ref[0, mod_base + 1:mod_base + 2, :]
    gate = m_ref[0, mod_base + 2:mod_base + 3, :]
    subs = [slice(lo, lo + FFN_SUB) for lo in range(0, x_ref.shape[1], FFN_SUB)]
    xs = [x_ref[0, rows] for rows in subs]
    hs = [(x * (1.0 + scale) + shift).astype(bf16) for x in xs]
    for j in range(D_FF // FF_CHUNK):
        cols = slice(j * FF_CHUNK, (j + 1) * FF_CHUNK)
        for rows, h in zip(subs, hs):
            a = jnp.dot(h, wup_ref[:, cols], preferred_element_type=f32)
            u = jnp.dot(h, wup_ref[:, D_FF + j * FF_CHUNK:D_FF + (j + 1) * FF_CHUNK],
                        preferred_element_type=f32)
            act_ref[rows, cols] = (_silu(a) * u).astype(bf16)
    ys = [jnp.dot(act_ref[rows, :], wd_ref[...], preferred_element_type=f32) for rows in subs]
    for rows, x, y in zip(subs, xs, ys):
        r = ALPHA * x + 0.5 * gate * y
        o_ref[0, rows] = _layer_norm(r, g_ref[ln_row:ln_row + 1, :], b_ref[ln_row:ln_row + 1, :])


def _resident(shape):
    return pl.BlockSpec(shape, lambda *_: (0,) * len(shape), pipeline_mode=pl.Buffered(1))


def _ffn(x, mod, mod_group, w_up, wd, ln_g, ln_b, *, mod_base, ln_row):
    groups, length, _ = x.shape
    tm = FFN_TILE
    return pl.pallas_call(
        functools.partial(_ffn_kernel, mod_base=mod_base, ln_row=ln_row),
        grid=(groups, length // tm),
        in_specs=[
            pl.BlockSpec((1, tm, D_MODEL), lambda g, t: (g, t, 0)),
            pl.BlockSpec((1, N_MOD, D_MODEL), lambda g, t: (mod_group(g), 0, 0)),
            _resident((D_MODEL, 2 * D_FF)),
            _resident((D_FF, D_MODEL)),
            _resident((3, D_MODEL)),
            _resident((3, D_MODEL)),
        ],
        out_specs=pl.BlockSpec((1, tm, D_MODEL), lambda g, t: (g, t, 0)),
        out_shape=jax.ShapeDtypeStruct(x.shape, f32),
        scratch_shapes=[pltpu.VMEM((tm, D_FF), bf16)],
        compiler_params=pltpu.CompilerParams(
            dimension_semantics=("arbitrary", "arbitrary"), vmem_limit_bytes=VMEM_LIMIT_BYTES),
        name="ffn",
    )(x, mod, w_up, wd, ln_g, ln_b)


def _head_rms_norm(x, ones_bd, gain):
    ss = _dot01(ones_bd, x * x, 2, mat_on_left=False)
    return x * lax.rsqrt(ss * (1.0 / HEAD_DIM) + RMS_EPS) * gain


def _rope(x, cos, sin_signed):
    width = x.shape[-1]
    lane = lax.broadcasted_iota(jnp.int32, x.shape, 1)
    from_right = pltpu.roll(x, width - 16, 1)
    from_left = pltpu.roll(x, 16, 1)
    partner = jnp.where((lane & 31) < 16, from_right, from_left)
    return x * cos + partner * sin_signed


def _proj_kernel(*refs, rope):
    if rope:
        (x_ref, m_ref, w_ref, lbf_ref, lbb_ref, qg_ref, kg_ref, bd_ref, cos_ref, sin_ref,
         qs_ref, vh_ref, lff_ref, kkf_ref, lfb_ref, kkb_ref, sg_ref, qa_ref, ka_ref, vt_ref,
         spf_ref, spb_ref) = refs
    else:
        (x_ref, m_ref, w_ref, lbf_ref, lbb_ref, qg_ref, kg_ref, bd_ref,
         qs_ref, vh_ref, lff_ref, kkf_ref, lfb_ref, kkb_ref, sg_ref, qa_ref, ka_ref, vt_ref,
         spf_ref, spb_ref, kt_ref) = refs
    x = x_ref[0]
    shift = m_ref[0, 3:4, :]
    scale = m_ref[0, 4:5, :]
    h = (x * (1.0 + scale) + shift).astype(bf16)

    def cols(lo, width):
        return jnp.dot(h, w_ref[:, lo:lo + width], preferred_element_type=f32)

    def lower_bound(lb_ref):
        l0 = lb_ref[0:1, :]
        l1 = lb_ref[1:2, :]
        m = jnp.maximum(l0, l1)
        e0 = jnp.exp(l0 - m)
        e1 = jnp.exp(l1 - m)
        return e0 / (e0 + e1)

    def forget_gate(lb_ref, lo, lf_ref, kk_ref, span_ref):
        lb = lower_bound(lb_ref)
        f = lb + (1.0 - lb) * jax.nn.sigmoid(cols(lo, HG_WIDTH))
        lf = jnp.log2(f)
        lf_ref[0] = lf
        kk_ref[0] = 1.0 - f
        half = HG_FAST_BLOCK // 2
        sums = jnp.sum(lf.reshape(lf.shape[0] // half, half, HG_WIDTH), axis=1)
        span_ref[0] = jnp.broadcast_to(jnp.max(jnp.abs(sums), axis=-1, keepdims=True), span_ref.shape[1:])

    qs_ref[0] = _silu(cols(0, HG_WIDTH))
    vh_ref[0] = cols(HG_WIDTH, HG_WIDTH)
    forget_gate(lbf_ref, 2 * HG_WIDTH, lff_ref, kkf_ref, spf_ref)
    forget_gate(lbb_ref, 3 * HG_WIDTH, lfb_ref, kkb_ref, spb_ref)
    sg_ref[0] = _silu(cols(4 * HG_WIDTH, HG_WIDTH))

    base = 5 * HG_WIDTH
    q = _head_rms_norm(cols(base, ATT_WIDTH), bd_ref[...], qg_ref[...])
    k = _head_rms_norm(cols(base + ATT_WIDTH, KV_WIDTH), bd_ref[0:KV_WIDTH, 0:KV_WIDTH], kg_ref[...])
    if rope:
        q = _rope(q, cos_ref[...], sin_ref[...])
        k = _rope(k, cos_ref[:, 0:KV_WIDTH], sin_ref[:, 0:KV_WIDTH])
    qa_ref[0] = (q * (HEAD_DIM ** -0.5 * LOG2_E)).astype(bf16)
    ka_ref[0] = k
    v = cols(base + ATT_WIDTH + KV_WIDTH, KV_WIDTH)
    n_piece, _, piece = vt_ref.shape
    for i in range(n_piece):
        vt_ref[i] = v[i * piece:(i + 1) * piece].T
        if not rope:
            kt_ref[i] = k[i * piece:(i + 1) * piece].T


def _proj(x, mod, mod_group, w_in, lb_f, lb_b, q_gain, k_gain, ones_bd, rope_tables, seq_len):
    groups, length, _ = x.shape
    tm = TOKEN_TILE
    rope = rope_tables is not None
    n_seq = groups * length // seq_len
    if seq_len >= tm:
        tiles_per_seq = seq_len // tm
        t_spec = pl.BlockSpec((1, KV_WIDTH, tm), lambda g, t: (g * (length // seq_len) + t // tiles_per_seq, 0,
                                                               t % tiles_per_seq))
    else:
        t_spec = pl.BlockSpec((tm // seq_len, KV_WIDTH, seq_len), lambda g, t: (g * (length // tm) + t, 0, 0))
    t_shape = jax.ShapeDtypeStruct((n_seq, KV_WIDTH, seq_len), f32)
    tok = lambda width: pl.BlockSpec((1, tm, width), lambda g, t: (g, t, 0))
    in_specs = [
        tok(D_MODEL),
        pl.BlockSpec((1, N_MOD, D_MODEL), lambda g, t: (mod_group(g), 0, 0)),
        _resident((D_MODEL, IN_WIDTH)),
        _resident((2, HG_WIDTH)),
        _resident((2, HG_WIDTH)),
        _resident((1, ATT_WIDTH)),
        _resident((1, KV_WIDTH)),
        _resident((ATT_WIDTH, ATT_WIDTH)),
    ]
    args = [x, mod, w_in, lb_f, lb_b, q_gain, k_gain, ones_bd]
    if rope:
        in_specs += [pl.BlockSpec((tm, ATT_WIDTH), lambda g, t: (t, 0))] * 2
        args += list(rope_tables)
    shape = lambda width, dt: jax.ShapeDtypeStruct((groups, length, width), dt)
    half = HG_FAST_BLOCK // 2
    span_spec = pl.BlockSpec((1, tm // half, LANES), lambda g, t: (g, t, 0))
    span_shape = jax.ShapeDtypeStruct((groups, length // half, LANES), f32)
    return pl.pallas_call(
        functools.partial(_proj_kernel, rope=rope),
        grid=(groups, length // tm),
        in_specs=in_specs,
        out_specs=([tok(HG_WIDTH)] * 7 + [tok(ATT_WIDTH), tok(KV_WIDTH), t_spec, span_spec, span_spec]
                   + ([] if rope else [t_spec])),
        out_shape=([shape(HG_WIDTH, f32)] * 7 + [shape(ATT_WIDTH, bf16), shape(KV_WIDTH, f32), t_shape,
                                                  span_shape, span_shape] + ([] if rope else [t_shape])),
        compiler_params=pltpu.CompilerParams(
            dimension_semantics=("arbitrary", "arbitrary"), vmem_limit_bytes=VMEM_LIMIT_BYTES),
        name="mixer_proj",
    )(*args)


def _hgrn_direction(qs, v, lf, kk, st_ref, o_ref, rows, *, rev):
    tc = qs.shape[0]
    row = lax.broadcasted_iota(jnp.int32, (tc, tc), 0)
    col = lax.broadcasted_iota(jnp.int32, (tc, tc), 1)
    ordered = (row < col) if rev else (row > col)
    differ = row ^ col
    tri = jnp.where((col >= row) if rev else (col <= row), 1.0, 0.0).astype(bf16)
    cum = _dot01(tri, lf, 3, mat_on_left=True)
    total = cum[0:1] if rev else cum[tc - 1:tc]
    vb = v.astype(bf16)
    carry = jnp.exp2(total)
    query_half = 0 if rev else 1
    heads = [slice(hd * HG_HEAD_DIM, (hd + 1) * HG_HEAD_DIM) for hd in range(HG_HEADS)]

    def level_exponent(s, ridx):
        if s == 1:
            return jnp.where((ridx & 1) == query_half, lf, 0.0)
        if s == 2:
            nxt = pltpu.roll(lf, tc - 1, 0)
            prv = pltpu.roll(lf, 1, 0)
            m4 = ridx & 3
            if rev:
                return jnp.where(m4 == 0, lf + nxt, jnp.where(m4 == 1, lf, jnp.where(m4 == 2, 0.0, prv)))
            return jnp.where(m4 == 0, nxt, jnp.where(m4 == 1, 0.0, jnp.where(m4 == 2, lf, lf + prv)))
        blocks = []
        for lo in range(0, tc, 2 * s):
            anchor = lo + (s if rev else s - 1)
            blocks.append(cum[lo:lo + 2 * s] - cum[anchor:anchor + 1])
        d = jnp.concatenate(blocks, axis=0)
        is_query = ((ridx >> (s.bit_length() - 1)) & 1) == query_half
        return jnp.where(is_query, d, -d)

    def add_levels(intra, first):
        ridx = lax.broadcasted_iota(jnp.int32, qs.shape, 0)
        s = first
        while s < tc:
            shift = s.bit_length() - 1
            is_query = ((ridx >> shift) & 1) == query_half
            mixed = (jnp.where(is_query, qs, kk) * jnp.exp2(level_exponent(s, ridx))).astype(bf16)
            pair = jnp.logical_and((differ >> shift) == 1, ordered)
            for hd, hs in enumerate(heads):
                intra[hd] = jnp.where(pair, _dot_nt(mixed[:, hs], mixed[:, hs]), intra[hd])
            s *= 2
        return intra

    def finish(intra, q_dec, k_end, extra):
        for hd, hs in enumerate(heads):
            st = st_ref[hd]
            o = (jnp.dot(intra[hd].astype(bf16), vb[:, hs], preferred_element_type=f32)
                 + _dot_nt(q_dec[:, hs], st.astype(bf16)))
            o_ref[0, rows, hs] = o if extra is None else o + extra[:, hs]
            st_ref[hd] = st * carry[:, hs] + _dot_tn(vb[:, hs], k_end[:, hs])

    blk = HG_FAST_BLOCK
    anchors = [lo + (blk // 2 if rev else blk // 2 - 1) for lo in range(0, tc, blk)]

    def mid_split_blocks():
        q_mid, k_mid, q_dec, k_end = [], [], [], []
        for lo, a in zip(range(0, tc, blk), anchors):
            rel = cum[lo:lo + blk] - cum[a:a + 1]
            q_mid.append(qs[lo:lo + blk] * jnp.exp2(rel))
            k_mid.append(kk[lo:lo + blk] * jnp.exp2(-rel))
            q_dec.append(q_mid[-1] * jnp.exp2(cum[a:a + 1]))
            k_end.append(k_mid[-1] * jnp.exp2(total - cum[a:a + 1]))
        q_mid, k_mid, q_dec, k_end = (jnp.concatenate(p, axis=0).astype(bf16) for p in (q_mid, k_mid, q_dec, k_end))
        same_block = (differ >> (blk.bit_length() - 1)) == 0
        keep = jnp.logical_and(same_block, (row <= col) if rev else (row >= col))
        intra = [jnp.where(keep, _dot_nt(q_mid[:, hs], k_mid[:, hs]), 0.0) for hs in heads]
        finish(add_levels(intra, blk), q_dec, k_end, None)

    def all_levels():
        intra = add_levels([jnp.zeros((tc, tc), f32)] * HG_HEADS, 1)
        q_dec = (qs * jnp.exp2(cum)).astype(bf16)
        k_end = (kk * jnp.exp2(total - cum)).astype(bf16)
        qk = qs * kk
        own = jnp.concatenate(
            [jnp.sum(qk[:, hs], axis=-1, keepdims=True) * v[:, hs] for hs in heads], axis=-1)
        finish(intra, q_dec, k_end, own)

    return mid_split_blocks, all_levels


def _hgrn_kernel(slow_ref, *refs, has_init):
    if has_init:
        (qsf_ref, vf_ref, lff_ref, kkf_ref, qsb_ref, vb_ref, lfb_ref, kkb_ref, s0f_ref, s0b_ref,
         of_ref, ob_ref, stf_ref, stb_ref) = refs
    else:
        (qsf_ref, vf_ref, lff_ref, kkf_ref, qsb_ref, vb_ref, lfb_ref, kkb_ref,
         of_ref, ob_ref, sf_out_ref, sb_out_ref, stf_ref, stb_ref) = refs
    t = pl.program_id(1)

    @pl.when(t == 0)
    def _():
        for hd in range(HG_HEADS):
            if has_init:
                stf_ref[hd] = s0f_ref[0, hd].T
                stb_ref[hd] = s0b_ref[0, hd].T
            else:
                stf_ref[hd] = jnp.zeros((HG_HEAD_DIM, HG_HEAD_DIM), f32)
                stb_ref[hd] = jnp.zeros((HG_HEAD_DIM, HG_HEAD_DIM), f32)

    chunks = [slice(lo, lo + HG_TILE) for lo in range(0, qsf_ref.shape[1], HG_TILE)]
    scans = []
    for r_f, r_b in zip(chunks, reversed(chunks)):
        scans.append(_hgrn_direction(qsf_ref[0, r_f], vf_ref[0, r_f], lff_ref[0, r_f], kkf_ref[0, r_f],
                                     stf_ref, of_ref, r_f, rev=False))
        scans.append(_hgrn_direction(qsb_ref[0, r_b], vb_ref[0, r_b], lfb_ref[0, r_b], kkb_ref[0, r_b],
                                     stb_ref, ob_ref, r_b, rev=True))
    slow = slow_ref[pl.program_id(0) * pl.num_programs(1) + t] != 0

    @pl.when(jnp.logical_not(slow))
    def _():
        for mid_split_blocks, _ in scans:
            mid_split_blocks()

    @pl.when(slow)
    def _():
        for _, all_levels in scans:
            all_levels()

    if not has_init:
        @pl.when(t == pl.num_programs(1) - 1)
        def _():
            for hd in range(HG_HEADS):
                sf_out_ref[0, hd] = stf_ref[hd].T
                sb_out_ref[0, hd] = stb_ref[hd].T


def _hgrn(qs, vh, lf_fwd, kk_fwd, lf_bwd, kk_bwd, span_fwd, span_bwd, init_states):
    groups, length, _ = qs.shape
    nt = length // HG_STEP
    has_init = init_states is not None
    per_step = lambda span: jnp.max(span[:, :, 0].reshape(groups, nt, -1), axis=-1) > HG_FAST_SPAN_LOG2
    slow = jnp.logical_or(per_step(span_fwd), per_step(span_bwd)[:, ::-1]).astype(jnp.int32).reshape(-1)
    fwd = pl.BlockSpec((1, HG_STEP, HG_WIDTH), lambda g, t, _: (g, t, 0))
    bwd = pl.BlockSpec((1, HG_STEP, HG_WIDTH), lambda g, t, _: (g, nt - 1 - t, 0))
    state = pl.BlockSpec((1, HG_HEADS, HG_HEAD_DIM, HG_HEAD_DIM), lambda g, t, _: (g, 0, 0, 0))
    in_specs = [fwd, fwd, fwd, fwd, bwd, bwd, bwd, bwd]
    args = [qs, vh, lf_fwd, kk_fwd, qs, vh, lf_bwd, kk_bwd]
    out_specs = [fwd, bwd]
    out_shape = [jax.ShapeDtypeStruct(qs.shape, f32)] * 2
    if has_init:
        in_specs += [state, state]
        args += list(init_states)
    else:
        out_specs += [state, state]
        out_shape += [jax.ShapeDtypeStruct((groups, HG_HEADS, HG_HEAD_DIM, HG_HEAD_DIM), f32)] * 2
    return pl.pallas_call(
        functools.partial(_hgrn_kernel, has_init=has_init),
        grid_spec=pltpu.PrefetchScalarGridSpec(
            num_scalar_prefetch=1,
            grid=(groups, nt),
            in_specs=in_specs,
            out_specs=out_specs,
            scratch_shapes=[pltpu.VMEM((HG_HEADS, HG_HEAD_DIM, HG_HEAD_DIM), f32)] * 2),
        out_shape=out_shape,
        compiler_params=pltpu.CompilerParams(
            dimension_semantics=("arbitrary", "arbitrary"), vmem_limit_bytes=VMEM_LIMIT_BYTES),
        name="hgrn2",
    )(slow, *args)


def _attn_kernel(*refs, n_ctx):
    if n_ctx:
        (qa_ref, ka_ref, vt_ref, ckt_ref, cvt_ref, x_ref, of_ref, ob_ref, sg_ref, m_ref, wo_ref, hg_ref,
         g_ref, b_ref, o_ref, kk_ref, vta_ref) = refs
    else:
        (qa_ref, ka_ref, vt_ref, x_ref, of_ref, ob_ref, sg_ref, m_ref, wo_ref, hg_ref,
         g_ref, b_ref, o_ref, kk_ref, vta_ref) = refs
    tq = qa_ref.shape[1]

    @pl.when(pl.program_id(1) == 0)
    def _():
        def fill_keys(lo, k):
            n = k.shape[0]
            low = lax.broadcasted_iota(jnp.int32, k.shape, 1) < HEAD_DIM
            k_sw = pltpu.roll(k, HEAD_DIM, 1)
            kk_ref[lo:lo + n, 0:LANES] = jnp.where(low, k, k_sw).astype(bf16)
            kk_ref[lo:lo + n, LANES:2 * LANES] = jnp.where(low, k_sw, k).astype(bf16)

        def fill_values(lo, vt):
            n = vt.shape[1]
            ones_row = jnp.where(lax.broadcasted_iota(jnp.int32, (V_ROWS - HEAD_DIM, n), 0) == 0, 1.0, 0.0)
            for kv in range(N_KV_HEADS):
                vta_ref[kv * V_ROWS:kv * V_ROWS + HEAD_DIM, lo:lo + n] = (
                    vt[kv * HEAD_DIM:(kv + 1) * HEAD_DIM].astype(bf16))
                vta_ref[kv * V_ROWS + HEAD_DIM:(kv + 1) * V_ROWS, lo:lo + n] = ones_row.astype(bf16)

        if n_ctx:
            fill_keys(0, ckt_ref[0].T)
            fill_values(0, cvt_ref[0])
        fill_keys(n_ctx, ka_ref[0])
        fill_values(n_ctx, vt_ref[0])

    low = lax.broadcasted_iota(jnp.int32, (tq, LANES), 1) < HEAD_DIM
    n_keys = kk_ref.shape[0]
    kc = min(KEY_CHUNK, n_keys)
    def masked_pair(tile):
        qp = qa_ref[0, :, tile * LANES:(tile + 1) * LANES]
        zero = jnp.zeros_like(qp)
        return jnp.concatenate([jnp.where(low, qp, zero), jnp.where(low, zero, qp)], axis=0)

    q_pairs = [masked_pair(tile) for tile in range(N_HEADS // 2)]
    pairs_per_kv = N_HEADS // N_KV_HEADS // 2
    work = [(tile, lo) for tile in range(N_HEADS // 2) for lo in range(0, n_keys, kc)]

    def scores(tile, lo):
        kv = tile // pairs_per_kv
        return _dot_nt(kk_ref[lo:lo + kc, kv * LANES:(kv + 1) * LANES], q_pairs[tile])

    heads_t = []
    st_next = scores(*work[0])
    m = acc = None
    for i, (tile, lo) in enumerate(work):
        st = st_next
        if i + 1 < len(work):
            st_next = scores(*work[i + 1])
        kv = tile // pairs_per_kv
        m_chunk = jnp.max(st, axis=0, keepdims=True)
        m_new = m_chunk if m is None else jnp.maximum(m, m_chunk)
        e = jnp.exp2(st - m_new).astype(bf16)
        pv = jnp.dot(vta_ref[kv * V_ROWS:(kv + 1) * V_ROWS, lo:lo + kc], e,
                     preferred_element_type=f32)
        acc = pv if acc is None else acc * jnp.exp2(m - m_new) + pv
        m = m_new
        if lo + kc == n_keys:
            on = acc[0:HEAD_DIM] * (1.0 / acc[HEAD_DIM:HEAD_DIM + 1])
            heads_t += [on[:, :tq], on[:, tq:]]
            m = acc = None
    o_att_t = jnp.concatenate(heads_t, axis=0).astype(bf16)

    o_sum = of_ref[0] + ob_ref[0]
    normed = []
    for hd in range(HG_HEADS):
        oh = o_sum[:, hd * HG_HEAD_DIM:(hd + 1) * HG_HEAD_DIM]
        ms = jnp.mean(oh * oh, axis=-1, keepdims=True)
        normed.append(oh * lax.rsqrt(ms + RMS_EPS))
    o_hg = (jnp.concatenate(normed, axis=-1) * hg_ref[...] * sg_ref[0]).astype(bf16)

    y = (jnp.dot(o_hg, wo_ref[0:HG_WIDTH, :], preferred_element_type=f32)
         + _dot_tn(o_att_t, wo_ref[HG_WIDTH:HG_WIDTH + ATT_WIDTH, :]))
    r = ALPHA * x_ref[0] + m_ref[0, 5:6, :] * y
    o_ref[0] = _layer_norm(r, g_ref[1:2, :], b_ref[1:2, :])


def _attn(qa, ka, vt, ctx_kv_t, x, o_f, o_b, sg, mod, mod_group, w_out, hg_gain, ln_g, ln_b):
    groups, length, _ = qa.shape
    tq = Q_TILE
    n_ctx = 0 if ctx_kv_t is None else ctx_kv_t[0].shape[2]
    n_keys = n_ctx + length
    tok = lambda width: pl.BlockSpec((1, tq, width), lambda g, t: (g, t, 0))
    whole_t = lambda n: pl.BlockSpec((1, KV_WIDTH, n), lambda g, t: (g, 0, 0))
    in_specs = [tok(ATT_WIDTH), pl.BlockSpec((1, length, KV_WIDTH), lambda g, t: (g, 0, 0)), whole_t(length)]
    args = [qa, ka, vt]
    if n_ctx:
        in_specs += [whole_t(n_ctx), whole_t(n_ctx)]
        args += list(ctx_kv_t)
    in_specs += [
        tok(D_MODEL), tok(HG_WIDTH), tok(HG_WIDTH), tok(HG_WIDTH),
        pl.BlockSpec((1, N_MOD, D_MODEL), lambda g, t: (mod_group(g), 0, 0)),
        _resident((D_MODEL, D_MODEL)),
        _resident((1, HG_WIDTH)),
        _resident((3, D_MODEL)),
        _resident((3, D_MODEL)),
    ]
    args += [x, o_f, o_b, sg, mod, w_out, hg_gain, ln_g, ln_b]
    return pl.pallas_call(
        functools.partial(_attn_kernel, n_ctx=n_ctx),
        grid=(groups, length // tq),
        in_specs=in_specs,
        out_specs=tok(D_MODEL),
        out_shape=jax.ShapeDtypeStruct(x.shape, f32),
        scratch_shapes=[pltpu.VMEM((n_keys, N_KV_HEADS * LANES), bf16),
                        pltpu.VMEM((N_KV_HEADS * V_ROWS, n_keys), bf16)],
        compiler_params=pltpu.CompilerParams(
            dimension_semantics=("arbitrary", "arbitrary"), vmem_limit_bytes=VMEM_LIMIT_BYTES),
        name="attn_out",
    )(*args)


def _rope_tables(n_tokens):
    half = HEAD_DIM // 2
    t = jnp.arange(n_tokens)
    inv = ROPE_THETA ** (-jnp.arange(0, half, 2, dtype=f32) / half)
    ang_row = (t // GRID_W).astype(f32)[:, None] * inv
    ang_col = (t % GRID_W).astype(f32)[:, None] * inv
    cos = jnp.concatenate([jnp.cos(ang_row)] * 2 + [jnp.cos(ang_col)] * 2, axis=-1)
    sin = jnp.concatenate([-jnp.sin(ang_row), jnp.sin(ang_row), -jnp.sin(ang_col), jnp.sin(ang_col)], axis=-1)
    return jnp.tile(cos, (1, N_HEADS)), jnp.tile(sin, (1, N_HEADS))


def kernel(x_prompt, x_sample, cache_k, cache_v, state_hgrn_fwd, state_hgrn_bwd, c, c_ctx, w_mod, b_mod,
           w_ffn1_in, w_ffn1_out, w_ffn2_in, w_ffn2_out, w_in, w_out, q_norm_g, k_norm_g, hg_norm_g,
           lb_logits_fwd, lb_logits_bwd, ln_g, ln_b):
    assert w_mod.shape[0] == DEPTH and lb_logits_fwd.shape[0] == DEPTH + 1
    batch, seq, _ = x_prompt.shape
    dec_batch, dec_seq, _ = x_sample.shape
    past = cache_k.shape[2]

    ctx_row = dec_batch
    rows = 16
    cvecs = jnp.concatenate([c, c_ctx[None, :], jnp.zeros((rows - dec_batch - 1, D_MODEL), f32)], axis=0)
    mod = _modulation(cvecs, w_mod[0], b_mod[0]).reshape(rows, N_MOD, D_MODEL)

    to_bf16 = lambda w: w.astype(bf16)
    w1u, w2u = to_bf16(w_ffn1_in[0]), to_bf16(w_ffn2_in[0])
    w1d, w2d = to_bf16(w_ffn1_out[0]), to_bf16(w_ffn2_out[0])
    w_in_b, w_out_b = to_bf16(w_in[0]), to_bf16(w_out[0])
    gains = ln_g[0], ln_b[0]
    q_gain = jnp.tile(q_norm_g[0], N_HEADS).reshape(1, ATT_WIDTH)
    k_gain = jnp.tile(k_norm_g[0], N_KV_HEADS).reshape(1, KV_WIDTH)
    hg_gain = hg_norm_g[0].reshape(1, HG_WIDTH)
    head_of = jnp.arange(ATT_WIDTH) // HEAD_DIM
    ones_bd = (head_of[:, None] == head_of[None, :]).astype(bf16)

    def trunk(x, mod_group, rope_tables, ctx_kv_t, init_states, hg_groups):
        shape = x.shape
        seq_len = shape[0] * shape[1] // hg_groups
        per_seq = lambda a: a.reshape(hg_groups, seq_len, a.shape[-1])
        x = _ffn(x, mod, mod_group, w1u, w1d, *gains, mod_base=0, ln_row=0)
        proj = _proj(x, mod, mod_group, w_in_b, lb_logits_fwd, lb_logits_bwd, q_gain, k_gain, ones_bd, rope_tables,
                     seq_len)
        qs, vh, lf_f, kk_f, lf_b, kk_b, sg, qa, ka = map(per_seq, proj[:9])
        vt, span_f, span_b = proj[9], proj[10].reshape(hg_groups, -1, LANES), proj[11].reshape(hg_groups, -1, LANES)
        scans = _hgrn(qs, vh, lf_f, kk_f, lf_b, kk_b, span_f, span_b, init_states)
        x = _attn(qa, ka, vt, ctx_kv_t, per_seq(x), scans[0], scans[1], sg, mod, mod_group, w_out_b, hg_gain,
                  *gains).reshape(shape)
        x = _ffn(x, mod, mod_group, w2u, w2d, *gains, mod_base=6, ln_row=2)
        return x, proj[12:], vt, scans[2:]

    ctx_group = lambda g: ctx_row
    y_prompt, (kt_new,), vt_new, states = trunk(
        x_prompt.reshape(1, batch * seq, D_MODEL), ctx_group, None, None, None, batch)
    y_prompt = y_prompt.reshape(batch, seq, D_MODEL)
    to_cache = lambda t: t.reshape(batch, DEPTH, N_KV_HEADS, HEAD_DIM, seq).transpose(0, 1, 4, 2, 3)
    new_cache_k, new_cache_v = to_cache(kt_new), to_cache(vt_new)
    new_state_fwd = states[0].reshape(batch, DEPTH, HG_HEADS, HG_HEAD_DIM, HG_HEAD_DIM)
    new_state_bwd = states[1].reshape(batch, DEPTH, HG_HEADS, HG_HEAD_DIM, HG_HEAD_DIM)

    from_cache = lambda t: t[:, 0].transpose(0, 2, 3, 1).reshape(dec_batch, KV_WIDTH, past)
    init_states = (state_hgrn_fwd[:, 0], state_hgrn_bwd[:, 0])
    y_sample, _, _, _ = trunk(x_sample, lambda g: g, _rope_tables(dec_seq), (from_cache(cache_k), from_cache(cache_v)),
                              init_states, dec_batch)

    return (y_prompt, y_sample, new_cache_k, new_cache_v, new_state_fwd, new_state_bwd)
```

```python
import functools

import jax
import jax.numpy as jnp
from jax import lax
from jax.experimental import pallas as pl
from jax.experimental.pallas import tpu as pltpu

f32 = jnp.float32
bf16 = jnp.bfloat16

D_MODEL = 1024
N_MOD = 9
HG_WIDTH = 512
HG_HEAD_DIM = 128
HG_HEADS = 4
ATT_WIDTH = 512
HEAD_DIM = 64
N_HEADS = 8
N_KV_HEADS = 2
KV_WIDTH = 128
IN_WIDTH = 5 * HG_WIDTH + ATT_WIDTH + 2 * KV_WIDTH
D_FF = 2816
GRID_W = 64
ROPE_THETA = 10000.0
DEPTH = 1
ALPHA = (2.0 * DEPTH) ** 0.25
LOG2_E = 1.4426950408889634
LN_EPS = 1e-6
RMS_EPS = 1e-6

LANES = 128
VMEM_LIMIT_BYTES = 56 * 1024 * 1024

FF_CHUNK = 256
FFN_TILE = 1024
FFN_SUB = 512
TOKEN_TILE = 512
PROJ_SUB = 256
HG_TILE = 128
HG_STEP = 256
Q_TILE = 256
KEY_CHUNK = 512
V_ROWS = 80
HG_FAST_BLOCK = 64
HG_FAST_SPAN_LOG2 = 100.0


def _silu(x):
    return x * jax.nn.sigmoid(x)


def _layer_norm(r, g, b):
    mu = jnp.mean(r, axis=-1, keepdims=True)
    c = r - mu
    var = jnp.mean(c * c, axis=-1, keepdims=True)
    return c * lax.rsqrt(var + LN_EPS) * g + b


def _split_bf16(x, parts):
    out = []
    r = x
    for _ in range(parts - 1):
        p = r.astype(bf16)
        out.append(p)
        r = r - p.astype(f32)
    out.append(r.astype(bf16))
    return out


def _dot01(mat01, x, parts, *, mat_on_left):
    acc = None
    for p in _split_bf16(x, parts):
        d = (jnp.dot(mat01, p, preferred_element_type=f32) if mat_on_left
             else jnp.dot(p, mat01, preferred_element_type=f32))
        acc = d if acc is None else acc + d
    return acc


def _dot_nt(a, b):
    return lax.dot_general(a, b, (((1,), (1,)), ((), ())), preferred_element_type=f32)


def _dot_tn(a, b):
    return lax.dot_general(a, b, (((0,), (0,)), ((), ())), preferred_element_type=f32)


def _mod_kernel(c_ref, w_ref, b_ref, o_ref):
    a = _silu(c_ref[...]).astype(bf16)
    o_ref[...] = jnp.dot(a, w_ref[...].astype(bf16), preferred_element_type=f32) + b_ref[...]


def _modulation(cvecs, w_mod, b_mod):
    rows = cvecs.shape[0]
    n_out = w_mod.shape[1]
    tn = D_MODEL
    return pl.pallas_call(
        _mod_kernel,
        grid=(n_out // tn,),
        in_specs=[
            pl.BlockSpec((rows, D_MODEL), lambda j: (0, 0)),
            pl.BlockSpec((D_MODEL, tn), lambda j: (0, j)),
            pl.BlockSpec((1, tn), lambda j: (0, j)),
        ],
        out_specs=pl.BlockSpec((rows, tn), lambda j: (0, j)),
        out_shape=jax.ShapeDtypeStruct((rows, n_out), f32),
        compiler_params=pltpu.CompilerParams(dimension_semantics=("arbitrary",)),
        name="modulation",
    )(cvecs, w_mod, b_mod.reshape(1, n_out))


def _ffn_kernel(x_ref, m_ref, wup_ref, wd_ref, g_ref, b_ref, o_ref, act_ref, *, mod_base, ln_row):
    shift = m_ref[0, mod_base:mod_base + 1, :]
    scale = m_ref[0, mod_base + 1:mod_base + 2, :]
    gate = m_ref[0, mod_base + 2:mod_base + 3, :]
    subs = [slice(lo, lo + FFN_SUB) for lo in range(0, x_ref.shape[1], FFN_SUB)]
    xs = [x_ref[0, rows] for rows in subs]
    hs = [(x * (1.0 + scale) + shift).astype(bf16) for x in xs]
    for j in range(D_FF // FF_CHUNK):
        cols = slice(j * FF_CHUNK, (j + 1) * FF_CHUNK)
        for rows, h in zip(subs, hs):
            a = jnp.dot(h, wup_ref[:, cols], preferred_element_type=f32)
            u = jnp.dot(h, wup_ref[:, D_FF + j * FF_CHUNK:D_FF + (j + 1) * FF_CHUNK],
                        preferred_element_type=f32)
            act_ref[rows, cols] = (_silu(a) * u).astype(bf16)
    ys = [jnp.dot(act_ref[rows, :], wd_ref[...], preferred_element_type=f32) for rows in subs]
    for rows, x, y in zip(subs, xs, ys):
        r = ALPHA * x + 0.5 * gate * y
        o_ref[0, rows] = _layer_norm(r, g_ref[ln_row:ln_row + 1, :], b_ref[ln_row:ln_row + 1, :])


def _resident(shape):
    return pl.BlockSpec(shape, lambda *_: (0,) * len(shape), pipeline_mode=pl.Buffered(1))


def _ffn(x, mod, mod_group, w_up, wd, ln_g, ln_b, *, mod_base, ln_row):
    groups, length, _ = x.shape
    tm = FFN_TILE
    return pl.pallas_call(
        functools.partial(_ffn_kernel, mod_base=mod_base, ln_row=ln_row),
        grid=(groups, length // tm),
        in_specs=[
            pl.BlockSpec((1, tm, D_MODEL), lambda g, t: (g, t, 0)),
            pl.BlockSpec((1, N_MOD, D_MODEL), lambda g, t: (mod_group(g), 0, 0)),
            _resident((D_MODEL, 2 * D_FF)),
            _resident((D_FF, D_MODEL)),
            _resident((3, D_MODEL)),
            _resident((3, D_MODEL)),
        ],
        out_specs=pl.BlockSpec((1, tm, D_MODEL), lambda g, t: (g, t, 0)),
        out_shape=jax.ShapeDtypeStruct(x.shape, f32),
        scratch_shapes=[pltpu.VMEM((tm, D_FF), bf16)],
        compiler_params=pltpu.CompilerParams(
            dimension_semantics=("arbitrary", "arbitrary"), vmem_limit_bytes=VMEM_LIMIT_BYTES),
        name="ffn",
    )(x, mod, w_up, wd, ln_g, ln_b)


def _head_rms_norm(x, ones_bd, gain):
    ss = _dot01(ones_bd, x * x, 2, mat_on_left=False)
    return x * lax.rsqrt(ss * (1.0 / HEAD_DIM) + RMS_EPS) * gain


def _rope(x, cos, sin_signed):
    width = x.shape[-1]
    lane = lax.broadcasted_iota(jnp.int32, x.shape, 1)
    from_right = pltpu.roll(x, width - 16, 1)
    from_left = pltpu.roll(x, 16, 1)
    partner = jnp.where((lane & 31) < 16, from_right, from_left)
    return x * cos + partner * sin_signed


def _proj_kernel(*refs, rope):
    if rope:
        (x_ref, m_ref, w_ref, lbf_ref, lbb_ref, qg_ref, kg_ref, bd_ref, cos_ref, sin_ref,
         qs_ref, vh_ref, lff_ref, kkf_ref, lfb_ref, kkb_ref, sg_ref, qa_ref, ka_ref, vt_ref,
         spf_ref, spb_ref) = refs
    else:
        (x_ref, m_ref, w_ref, lbf_ref, lbb_ref, qg_ref, kg_ref, bd_ref,
         qs_ref, vh_ref, lff_ref, kkf_ref, lfb_ref, kkb_ref, sg_ref, qa_ref, ka_ref, vt_ref,
         spf_ref, spb_ref, kt_ref) = refs
    shift = m_ref[0, 3:4, :]
    scale = m_ref[0, 4:5, :]
    subs = [slice(lo, lo + PROJ_SUB) for lo in range(0, x_ref.shape[1], PROJ_SUB)]
    hs = [(x_ref[0, rows] * (1.0 + scale) + shift).astype(bf16) for rows in subs]

    def cols(h, lo, width):
        return jnp.dot(h, w_ref[:, lo:lo + width], preferred_element_type=f32)

    def lower_bound(lb_ref):
        l0 = lb_ref[0:1, :]
        l1 = lb_ref[1:2, :]
        m = jnp.maximum(l0, l1)
        e0 = jnp.exp(l0 - m)
        e1 = jnp.exp(l1 - m)
        return e0 / (e0 + e1)

    half = HG_FAST_BLOCK // 2

    def forget_gate(lb_ref, lo, lf_ref, kk_ref, span_ref):
        lb = lower_bound(lb_ref)
        for rows, h in zip(subs, hs):
            f = lb + (1.0 - lb) * jax.nn.sigmoid(cols(h, lo, HG_WIDTH))
            lf = jnp.log2(f)
            lf_ref[0, rows] = lf
            kk_ref[0, rows] = 1.0 - f
            sums = jnp.sum(lf.reshape(PROJ_SUB // half, half, HG_WIDTH), axis=1)
            span_ref[0, rows.start // half:rows.stop // half] = jnp.broadcast_to(
                jnp.max(jnp.abs(sums), axis=-1, keepdims=True), (PROJ_SUB // half, LANES))

    for rows, h in zip(subs, hs):
        qs_ref[0, rows] = _silu(cols(h, 0, HG_WIDTH))
    for rows, h in zip(subs, hs):
        vh_ref[0, rows] = cols(h, HG_WIDTH, HG_WIDTH)
    forget_gate(lbf_ref, 2 * HG_WIDTH, lff_ref, kkf_ref, spf_ref)
    forget_gate(lbb_ref, 3 * HG_WIDTH, lfb_ref, kkb_ref, spb_ref)
    for rows, h in zip(subs, hs):
        sg_ref[0, rows] = _silu(cols(h, 4 * HG_WIDTH, HG_WIDTH))

    base = 5 * HG_WIDTH
    piece = vt_ref.shape[2]
    for rows, h in zip(subs, hs):
        q = _head_rms_norm(cols(h, base, ATT_WIDTH), bd_ref[...], qg_ref[...])
        k = _head_rms_norm(cols(h, base + ATT_WIDTH, KV_WIDTH), bd_ref[0:KV_WIDTH, 0:KV_WIDTH], kg_ref[...])
        if rope:
            q = _rope(q, cos_ref[rows, :], sin_ref[rows, :])
            k = _rope(k, cos_ref[rows, 0:KV_WIDTH], sin_ref[rows, 0:KV_WIDTH])
        qa_ref[0, rows] = (q * (HEAD_DIM ** -0.5 * LOG2_E)).astype(bf16)
        ka_ref[0, rows] = k
        v = cols(h, base + ATT_WIDTH + KV_WIDTH, KV_WIDTH)
        for lo in range(rows.start, rows.stop, min(piece, PROJ_SUB)):
            n = min(piece, PROJ_SUB)
            dst = (lo // piece, slice(None), slice(lo % piece, lo % piece + n))
            vt_ref[dst] = v[lo - rows.start:lo - rows.start + n].T
            if not rope:
                kt_ref[dst] = k[lo - rows.start:lo - rows.start + n].T


def _proj(x, mod, mod_group, w_in, lb_f, lb_b, q_gain, k_gain, ones_bd, rope_tables, seq_len):
    groups, length, _ = x.shape
    tm = TOKEN_TILE
    rope = rope_tables is not None
    n_seq = groups * length // seq_len
    if seq_len >= tm:
        tiles_per_seq = seq_len // tm
        t_spec = pl.BlockSpec((1, KV_WIDTH, tm), lambda g, t: (g * (length // seq_len) + t // tiles_per_seq, 0,
                                                               t % tiles_per_seq))
    else:
        t_spec = pl.BlockSpec((tm // seq_len, KV_WIDTH, seq_len), lambda g, t: (g * (length // tm) + t, 0, 0))
    t_shape = jax.ShapeDtypeStruct((n_seq, KV_WIDTH, seq_len), f32)
    tok = lambda width: pl.BlockSpec((1, tm, width), lambda g, t: (g, t, 0))
    in_specs = [
        tok(D_MODEL),
        pl.BlockSpec((1, N_MOD, D_MODEL), lambda g, t: (mod_group(g), 0, 0)),
        _resident((D_MODEL, IN_WIDTH)),
        _resident((2, HG_WIDTH)),
        _resident((2, HG_WIDTH)),
        _resident((1, ATT_WIDTH)),
        _resident((1, KV_WIDTH)),
        _resident((ATT_WIDTH, ATT_WIDTH)),
    ]
    args = [x, mod, w_in, lb_f, lb_b, q_gain, k_gain, ones_bd]
    if rope:
        in_specs += [pl.BlockSpec((tm, ATT_WIDTH), lambda g, t: (t, 0))] * 2
        args += list(rope_tables)
    shape = lambda width, dt: jax.ShapeDtypeStruct((groups, length, width), dt)
    half = HG_FAST_BLOCK // 2
    span_spec = pl.BlockSpec((1, tm // half, LANES), lambda g, t: (g, t, 0))
    span_shape = jax.ShapeDtypeStruct((groups, length // half, LANES), f32)
    return pl.pallas_call(
        functools.partial(_proj_kernel, rope=rope),
        grid=(groups, length // tm),
        in_specs=in_specs,
        out_specs=([tok(HG_WIDTH)] * 7 + [tok(ATT_WIDTH), tok(KV_WIDTH), t_spec, span_spec, span_spec]
                   + ([] if rope else [t_spec])),
        out_shape=([shape(HG_WIDTH, f32)] * 7 + [shape(ATT_WIDTH, bf16), shape(KV_WIDTH, f32), t_shape,
                                                  span_shape, span_shape] + ([] if rope else [t_shape])),
        compiler_params=pltpu.CompilerParams(
            dimension_semantics=("arbitrary", "arbitrary"), vmem_limit_bytes=VMEM_LIMIT_BYTES),
        name="mixer_proj",
    )(*args)


def _hgrn_direction(qs, v, lf, kk, st_ref, o_ref, rows, *, rev):
    tc = qs.shape[0]
    row = lax.broadcasted_iota(jnp.int32, (tc, tc), 0)
    col = lax.broadcasted_iota(jnp.int32, (tc, tc), 1)
    ordered = (row < col) if rev else (row > col)
    differ = row ^ col
    tri = jnp.where((col >= row) if rev else (col <= row), 1.0, 0.0).astype(bf16)
    cum = _dot01(tri, lf, 3, mat_on_left=True)
    total = cum[0:1] if rev else cum[tc - 1:tc]
    vb = v.astype(bf16)
    carry = jnp.exp2(total)
    query_half = 0 if rev else 1
    heads = [slice(hd * HG_HEAD_DIM, (hd + 1) * HG_HEAD_DIM) for hd in range(HG_HEADS)]

    def level_exponent(s, ridx):
        if s == 1:
            return jnp.where((ridx & 1) == query_half, lf, 0.0)
        if s == 2:
            nxt = pltpu.roll(lf, tc - 1, 0)
            prv = pltpu.roll(lf, 1, 0)
            m4 = ridx & 3
            if rev:
                return jnp.where(m4 == 0, lf + nxt, jnp.where(m4 == 1, lf, jnp.where(m4 == 2, 0.0, prv)))
            return jnp.where(m4 == 0, nxt, jnp.where(m4 == 1, 0.0, jnp.where(m4 == 2, lf, lf + prv)))
        blocks = []
        for lo in range(0, tc, 2 * s):
            anchor = lo + (s if rev else s - 1)
            blocks.append(cum[lo:lo + 2 * s] - cum[anchor:anchor + 1])
        d = jnp.concatenate(blocks, axis=0)
        is_query = ((ridx >> (s.bit_length() - 1)) & 1) == query_half
        return jnp.where(is_query, d, -d)

    def add_levels(intra, first):
        ridx = lax.broadcasted_iota(jnp.int32, qs.shape, 0)
        s = first
        while s < tc:
            shift = s.bit_length() - 1
            is_query = ((ridx >> shift) & 1) == query_half
            mixed = (jnp.where(is_query, qs, kk) * jnp.exp2(level_exponent(s, ridx))).astype(bf16)
            pair = jnp.logical_and((differ >> shift) == 1, ordered)
            for hd, hs in enumerate(heads):
                intra[hd] = jnp.where(pair, _dot_nt(mixed[:, hs], mixed[:, hs]), intra[hd])
            s *= 2
        return intra

    def finish(intra, q_dec, k_end, extra):
        for hd, hs in enumerate(heads):
            st = st_ref[hd]
            o = (jnp.dot(intra[hd].astype(bf16), vb[:, hs], preferred_element_type=f32)
                 + _dot_nt(q_dec[:, hs], st.astype(bf16)))
            o_ref[0, rows, hs] = o if extra is None else o + extra[:, hs]
            st_ref[hd] = st * carry[:, hs] + _dot_tn(vb[:, hs], k_end[:, hs])

    blk = HG_FAST_BLOCK
    anchors = [lo + (blk // 2 if rev else blk // 2 - 1) for lo in range(0, tc, blk)]

    def mid_split_blocks():
        q_mid, k_mid, q_dec, k_end = [], [], [], []
        for lo, a in zip(range(0, tc, blk), anchors):
            rel = cum[lo:lo + blk] - cum[a:a + 1]
            q_mid.append(qs[lo:lo + blk] * jnp.exp2(rel))
            k_mid.append(kk[lo:lo + blk] * jnp.exp2(-rel))
            q_dec.append(q_mid[-1] * jnp.exp2(cum[a:a + 1]))
            k_end.append(k_mid[-1] * jnp.exp2(total - cum[a:a + 1]))
        q_mid, k_mid, q_dec, k_end = (jnp.concatenate(p, axis=0).astype(bf16) for p in (q_mid, k_mid, q_dec, k_end))
        same_block = (differ >> (blk.bit_length() - 1)) == 0
        keep = jnp.logical_and(same_block, (row <= col) if rev else (row >= col))
        intra = [jnp.where(keep, _dot_nt(q_mid[:, hs], k_mid[:, hs]), 0.0) for hs in heads]
        finish(add_levels(intra, blk), q_dec, k_end, None)

    def all_levels():
        intra = add_levels([jnp.zeros((tc, tc), f32)] * HG_HEADS, 1)
        q_dec = (qs * jnp.exp2(cum)).astype(bf16)
        k_end = (kk * jnp.exp2(total - cum)).astype(bf16)
        qk = qs * kk
        own = jnp.concatenate(
            [jnp.sum(qk[:, hs], axis=-1, keepdims=True) * v[:, hs] for hs in heads], axis=-1)
        finish(intra, q_dec, k_end, own)

    return mid_split_blocks, all_levels


def _hgrn_kernel(slow_ref, *refs, has_init):
    if has_init:
        (qsf_ref, vf_ref, lff_ref, kkf_ref, qsb_ref, vb_ref, lfb_ref, kkb_ref, s0f_ref, s0b_ref,
         of_ref, ob_ref, stf_ref, stb_ref) = refs
    else:
        (qsf_ref, vf_ref, lff_ref, kkf_ref, qsb_ref, vb_ref, lfb_ref, kkb_ref,
         of_ref, ob_ref, sf_out_ref, sb_out_ref, stf_ref, stb_ref) = refs
    t = pl.program_id(1)

    @pl.when(t == 0)
    def _():
        for hd in range(HG_HEADS):
            if has_init:
                stf_ref[hd] = s0f_ref[0, hd].T
                stb_ref[hd] = s0b_ref[0, hd].T
            else:
                stf_ref[hd] = jnp.zeros((HG_HEAD_DIM, HG_HEAD_DIM), f32)
                stb_ref[hd] = jnp.zeros((HG_HEAD_DIM, HG_HEAD_DIM), f32)

    chunks = [slice(lo, lo + HG_TILE) for lo in range(0, qsf_ref.shape[1], HG_TILE)]
    scans = []
    for r_f, r_b in zip(chunks, reversed(chunks)):
        scans.append(_hgrn_direction(qsf_ref[0, r_f], vf_ref[0, r_f], lff_ref[0, r_f], kkf_ref[0, r_f],
                                     stf_ref, of_ref, r_f, rev=False))
        scans.append(_hgrn_direction(qsb_ref[0, r_b], vb_ref[0, r_b], lfb_ref[0, r_b], kkb_ref[0, r_b],
                                     stb_ref, ob_ref, r_b, rev=True))
    slow = slow_ref[pl.program_id(0) * pl.num_programs(1) + t] != 0

    @pl.when(jnp.logical_not(slow))
    def _():
        for mid_split_blocks, _ in scans:
            mid_split_blocks()

    @pl.when(slow)
    def _():
        for _, all_levels in scans:
            all_levels()

    if not has_init:
        @pl.when(t == pl.num_programs(1) - 1)
        def _():
            for hd in range(HG_HEADS):
                sf_out_ref[0, hd] = stf_ref[hd].T
                sb_out_ref[0, hd] = stb_ref[hd].T


def _hgrn(qs, vh, lf_fwd, kk_fwd, lf_bwd, kk_bwd, span_fwd, span_bwd, init_states):
    groups, length, _ = qs.shape
    nt = length // HG_STEP
    has_init = init_states is not None
    per_step = lambda span: jnp.max(span[:, :, 0].reshape(groups, nt, -1), axis=-1) > HG_FAST_SPAN_LOG2
    slow = jnp.logical_or(per_step(span_fwd), per_step(span_bwd)[:, ::-1]).astype(jnp.int32).reshape(-1)
    fwd = pl.BlockSpec((1, HG_STEP, HG_WIDTH), lambda g, t, _: (g, t, 0))
    bwd = pl.BlockSpec((1, HG_STEP, HG_WIDTH), lambda g, t, _: (g, nt - 1 - t, 0))
    state = pl.BlockSpec((1, HG_HEADS, HG_HEAD_DIM, HG_HEAD_DIM), lambda g, t, _: (g, 0, 0, 0))
    in_specs = [fwd, fwd, fwd, fwd, bwd, bwd, bwd, bwd]
    args = [qs, vh, lf_fwd, kk_fwd, qs, vh, lf_bwd, kk_bwd]
    out_specs = [fwd, bwd]
    out_shape = [jax.ShapeDtypeStruct(qs.shape, f32)] * 2
    if has_init:
        in_specs += [state, state]
        args += list(init_states)
    else:
        out_specs += [state, state]
        out_shape += [jax.ShapeDtypeStruct((groups, HG_HEADS, HG_HEAD_DIM, HG_HEAD_DIM), f32)] * 2
    return pl.pallas_call(
        functools.partial(_hgrn_kernel, has_init=has_init),
        grid_spec=pltpu.PrefetchScalarGridSpec(
            num_scalar_prefetch=1,
            grid=(groups, nt),
            in_specs=in_specs,
            out_specs=out_specs,
            scratch_shapes=[pltpu.VMEM((HG_HEADS, HG_HEAD_DIM, HG_HEAD_DIM), f32)] * 2),
        out_shape=out_shape,
        compiler_params=pltpu.CompilerParams(
            dimension_semantics=("arbitrary", "arbitrary"), vmem_limit_bytes=VMEM_LIMIT_BYTES),
        name="hgrn2",
    )(slow, *args)


def _attn_kernel(*refs, n_ctx):
    if n_ctx:
        (qa_ref, ka_ref, vt_ref, ckt_ref, cvt_ref, x_ref, of_ref, ob_ref, sg_ref, m_ref, wo_ref, hg_ref,
         g_ref, b_ref, o_ref, kk_ref, vta_ref) = refs
    else:
        (qa_ref, ka_ref, vt_ref, x_ref, of_ref, ob_ref, sg_ref, m_ref, wo_ref, hg_ref,
         g_ref, b_ref, o_ref, kk_ref, vta_ref) = refs
    tq = qa_ref.shape[1]

    @pl.when(pl.program_id(1) == 0)
    def _():
        def fill_keys(lo, k):
            n = k.shape[0]
            low = lax.broadcasted_iota(jnp.int32, k.shape, 1) < HEAD_DIM
            k_sw = pltpu.roll(k, HEAD_DIM, 1)
            kk_ref[lo:lo + n, 0:LANES] = jnp.where(low, k, k_sw).astype(bf16)
            kk_ref[lo:lo + n, LANES:2 * LANES] = jnp.where(low, k_sw, k).astype(bf16)

        def fill_values(lo, vt):
            n = vt.shape[1]
            ones_row = jnp.where(lax.broadcasted_iota(jnp.int32, (V_ROWS - HEAD_DIM, n), 0) == 0, 1.0, 0.0)
            for kv in range(N_KV_HEADS):
                vta_ref[kv * V_ROWS:kv * V_ROWS + HEAD_DIM, lo:lo + n] = (
                    vt[kv * HEAD_DIM:(kv + 1) * HEAD_DIM].astype(bf16))
                vta_ref[kv * V_ROWS + HEAD_DIM:(kv + 1) * V_ROWS, lo:lo + n] = ones_row.astype(bf16)

        if n_ctx:
            fill_keys(0, ckt_ref[0].T)
            fill_values(0, cvt_ref[0])
        fill_keys(n_ctx, ka_ref[0])
        fill_values(n_ctx, vt_ref[0])

    low = lax.broadcasted_iota(jnp.int32, (tq, LANES), 1) < HEAD_DIM
    n_keys = kk_ref.shape[0]
    kc = min(KEY_CHUNK, n_keys)
    def masked_pair(tile):
        qp = qa_ref[0, :, tile * LANES:(tile + 1) * LANES]
        zero = jnp.zeros_like(qp)
        return jnp.concatenate([jnp.where(low, qp, zero), jnp.where(low, zero, qp)], axis=0)

    q_pairs = [masked_pair(tile) for tile in range(N_HEADS // 2)]
    pairs_per_kv = N_HEADS // N_KV_HEADS // 2
    work = [(tile, lo) for tile in range(N_HEADS // 2) for lo in range(0, n_keys, kc)]

    def scores(tile, lo):
        kv = tile // pairs_per_kv
        return _dot_nt(kk_ref[lo:lo + kc, kv * LANES:(kv + 1) * LANES], q_pairs[tile])

    heads_t = []
    st_next = scores(*work[0])
    m = acc = None
    for i, (tile, lo) in enumerate(work):
        st = st_next
        if i + 1 < len(work):
            st_next = scores(*work[i + 1])
        kv = tile // pairs_per_kv
        m_chunk = jnp.max(st, axis=0, keepdims=True)
        m_new = m_chunk if m is None else jnp.maximum(m, m_chunk)
        e = jnp.exp2(st - m_new).astype(bf16)
        pv = jnp.dot(vta_ref[kv * V_ROWS:(kv + 1) * V_ROWS, lo:lo + kc], e,
                     preferred_element_type=f32)
        acc = pv if acc is None else acc * jnp.exp2(m - m_new) + pv
        m = m_new
        if lo + kc == n_keys:
            on = acc[0:HEAD_DIM] * (1.0 / acc[HEAD_DIM:HEAD_DIM + 1])
            heads_t += [on[:, :tq], on[:, tq:]]
            m = acc = None
    o_att_t = jnp.concatenate(heads_t, axis=0).astype(bf16)

    o_sum = of_ref[0] + ob_ref[0]
    normed = []
    for hd in range(HG_HEADS):
        oh = o_sum[:, hd * HG_HEAD_DIM:(hd + 1) * HG_HEAD_DIM]
        ms = jnp.mean(oh * oh, axis=-1, keepdims=True)
        normed.append(oh * lax.rsqrt(ms + RMS_EPS))
    o_hg = (jnp.concatenate(normed, axis=-1) * hg_ref[...] * sg_ref[0]).astype(bf16)

    y = (jnp.dot(o_hg, wo_ref[0:HG_WIDTH, :], preferred_element_type=f32)
         + _dot_tn(o_att_t, wo_ref[HG_WIDTH:HG_WIDTH + ATT_WIDTH, :]))
    r = ALPHA * x_ref[0] + m_ref[0, 5:6, :] * y
    o_ref[0] = _layer_norm(r, g_ref[1:2, :], b_ref[1:2, :])


def _attn(qa, ka, vt, ctx_kv_t, x, o_f, o_b, sg, mod, mod_group, w_out, hg_gain, ln_g, ln_b):
    groups, length, _ = qa.shape
    tq = Q_TILE
    n_ctx = 0 if ctx_kv_t is None else ctx_kv_t[0].shape[2]
    n_keys = n_ctx + length
    tok = lambda width: pl.BlockSpec((1, tq, width), lambda g, t: (g, t, 0))
    whole_t = lambda n: pl.BlockSpec((1, KV_WIDTH, n), lambda g, t: (g, 0, 0))
    in_specs = [tok(ATT_WIDTH), pl.BlockSpec((1, length, KV_WIDTH), lambda g, t: (g, 0, 0)), whole_t(length)]
    args = [qa, ka, vt]
    if n_ctx:
        in_specs += [whole_t(n_ctx), whole_t(n_ctx)]
        args += list(ctx_kv_t)
    in_specs += [
        tok(D_MODEL), tok(HG_WIDTH), tok(HG_WIDTH), tok(HG_WIDTH),
        pl.BlockSpec((1, N_MOD, D_MODEL), lambda g, t: (mod_group(g), 0, 0)),
        _resident((D_MODEL, D_MODEL)),
        _resident((1, HG_WIDTH)),
        _resident((3, D_MODEL)),
        _resident((3, D_MODEL)),
    ]
    args += [x, o_f, o_b, sg, mod, w_out, hg_gain, ln_g, ln_b]
    return pl.pallas_call(
        functools.partial(_attn_kernel, n_ctx=n_ctx),
        grid=(groups, length // tq),
        in_specs=in_specs,
        out_specs=tok(D_MODEL),
        out_shape=jax.ShapeDtypeStruct(x.shape, f32),
        scratch_shapes=[pltpu.VMEM((n_keys, N_KV_HEADS * LANES), bf16),
                        pltpu.VMEM((N_KV_HEADS * V_ROWS, n_keys), bf16)],
        compiler_params=pltpu.CompilerParams(
            dimension_semantics=("arbitrary", "arbitrary"), vmem_limit_bytes=VMEM_LIMIT_BYTES),
        name="attn_out",
    )(*args)


def _rope_tables(n_tokens):
    half = HEAD_DIM // 2
    t = jnp.arange(n_tokens)
    inv = ROPE_THETA ** (-jnp.arange(0, half, 2, dtype=f32) / half)
    ang_row = (t // GRID_W).astype(f32)[:, None] * inv
    ang_col = (t % GRID_W).astype(f32)[:, None] * inv
    cos = jnp.concatenate([jnp.cos(ang_row)] * 2 + [jnp.cos(ang_col)] * 2, axis=-1)
    sin = jnp.concatenate([-jnp.sin(ang_row), jnp.sin(ang_row), -jnp.sin(ang_col), jnp.sin(ang_col)], axis=-1)
    return jnp.tile(cos, (1, N_HEADS)), jnp.tile(sin, (1, N_HEADS))


def kernel(x_prompt, x_sample, cache_k, cache_v, state_hgrn_fwd, state_hgrn_bwd, c, c_ctx, w_mod, b_mod,
           w_ffn1_in, w_ffn1_out, w_ffn2_in, w_ffn2_out, w_in, w_out, q_norm_g, k_norm_g, hg_norm_g,
           lb_logits_fwd, lb_logits_bwd, ln_g, ln_b):
    assert w_mod.shape[0] == DEPTH and lb_logits_fwd.shape[0] == DEPTH + 1
    batch, seq, _ = x_prompt.shape
    dec_batch, dec_seq, _ = x_sample.shape
    past = cache_k.shape[2]

    ctx_row = dec_batch
    rows = 16
    cvecs = jnp.concatenate([c, c_ctx[None, :], jnp.zeros((rows - dec_batch - 1, D_MODEL), f32)], axis=0)
    mod = _modulation(cvecs, w_mod[0], b_mod[0]).reshape(rows, N_MOD, D_MODEL)

    to_bf16 = lambda w: w.astype(bf16)
    w1u, w2u = to_bf16(w_ffn1_in[0]), to_bf16(w_ffn2_in[0])
    w1d, w2d = to_bf16(w_ffn1_out[0]), to_bf16(w_ffn2_out[0])
    w_in_b, w_out_b = to_bf16(w_in[0]), to_bf16(w_out[0])
    gains = ln_g[0], ln_b[0]
    q_gain = jnp.tile(q_norm_g[0], N_HEADS).reshape(1, ATT_WIDTH)
    k_gain = jnp.tile(k_norm_g[0], N_KV_HEADS).reshape(1, KV_WIDTH)
    hg_gain = hg_norm_g[0].reshape(1, HG_WIDTH)
    head_of = jnp.arange(ATT_WIDTH) // HEAD_DIM
    ones_bd = (head_of[:, None] == head_of[None, :]).astype(bf16)

    def trunk(x, mod_group, rope_tables, ctx_kv_t, init_states, hg_groups):
        shape = x.shape
        seq_len = shape[0] * shape[1] // hg_groups
        per_seq = lambda a: a.reshape(hg_groups, seq_len, a.shape[-1])
        x = _ffn(x, mod, mod_group, w1u, w1d, *gains, mod_base=0, ln_row=0)
        proj = _proj(x, mod, mod_group, w_in_b, lb_logits_fwd, lb_logits_bwd, q_gain, k_gain, ones_bd, rope_tables,
                     seq_len)
        qs, vh, lf_f, kk_f, lf_b, kk_b, sg, qa, ka = map(per_seq, proj[:9])
        vt, span_f, span_b = proj[9], proj[10].reshape(hg_groups, -1, LANES), proj[11].reshape(hg_groups, -1, LANES)
        scans = _hgrn(qs, vh, lf_f, kk_f, lf_b, kk_b, span_f, span_b, init_states)
        x = _attn(qa, ka, vt, ctx_kv_t, per_seq(x), scans[0], scans[1], sg, mod, mod_group, w_out_b, hg_gain,
                  *gains).reshape(shape)
        x = _ffn(x, mod, mod_group, w2u, w2d, *gains, mod_base=6, ln_row=2)
        return x, proj[12:], vt, scans[2:]

    ctx_group = lambda g: ctx_row
    y_prompt, (kt_new,), vt_new, states = trunk(
        x_prompt.reshape(1, batch * seq, D_MODEL), ctx_group, None, None, None, batch)
    y_prompt = y_prompt.reshape(batch, seq, D_MODEL)
    to_cache = lambda t: t.reshape(batch, DEPTH, N_KV_HEADS, HEAD_DIM, seq).transpose(0, 1, 4, 2, 3)
    new_cache_k, new_cache_v = to_cache(kt_new), to_cache(vt_new)
    new_state_fwd = states[0].reshape(batch, DEPTH, HG_HEADS, HG_HEAD_DIM, HG_HEAD_DIM)
    new_state_bwd = states[1].reshape(batch, DEPTH, HG_HEADS, HG_HEAD_DIM, HG_HEAD_DIM)

    from_cache = lambda t: t[:, 0].transpose(0, 2, 3, 1).reshape(dec_batch, KV_WIDTH, past)
    init_states = (state_hgrn_fwd[:, 0], state_hgrn_bwd[:, 0])
    y_sample, _, _, _ = trunk(x_sample, lambda g: g, _rope_tables(dec_seq), (from_cache(cache_k), from_cache(cache_v)),
                              init_states, dec_batch)

    return (y_prompt, y_sample, new_cache_k, new_cache_v, new_state_fwd, new_state_bwd)
```

```python
import functools

import jax
import jax.numpy as jnp
from jax import lax
from jax.experimental import pallas as pl
from jax.experimental.pallas import tpu as pltpu

f32 = jnp.float32
bf16 = jnp.bfloat16

D_MODEL = 1024
N_MOD = 9
HG_WIDTH = 512
HG_HEAD_DIM = 128
HG_HEADS = 4
ATT_WIDTH = 512
HEAD_DIM = 64
N_HEADS = 8
N_KV_HEADS = 2
KV_WIDTH = 128
IN_WIDTH = 5 * HG_WIDTH + ATT_WIDTH + 2 * KV_WIDTH
D_FF = 2816
GRID_W = 64
ROPE_THETA = 10000.0
DEPTH = 1
ALPHA = (2.0 * DEPTH) ** 0.25
LOG2_E = 1.4426950408889634
LN_EPS = 1e-6
RMS_EPS = 1e-6

LANES = 128
VMEM_LIMIT_BYTES = 56 * 1024 * 1024

FF_CHUNK = 256
FFN_TILE = 1024
FFN_SUB = 512
TOKEN_TILE = 512
PROJ_SUB = 256
HG_TILE = 128
HG_STEP = 256
Q_TILE = 256
KEY_CHUNK = 512
SCORE_BOUND_LIMIT = 40.0
SCORE_BOUND_SLACK = 1.01
V_ROWS = 80
HG_FAST_BLOCK = 64
HG_FAST_SPAN_LOG2 = 100.0


def _silu(x):
    return x * jax.nn.sigmoid(x)


def _layer_norm(r, g, b):
    mu = jnp.mean(r, axis=-1, keepdims=True)
    c = r - mu
    var = jnp.mean(c * c, axis=-1, keepdims=True)
    return c * lax.rsqrt(var + LN_EPS) * g + b


def _split_bf16(x, parts):
    out = []
    r = x
    for _ in range(parts - 1):
        p = r.astype(bf16)
        out.append(p)
        r = r - p.astype(f32)
    out.append(r.astype(bf16))
    return out


def _dot01(mat01, x, parts, *, mat_on_left):
    acc = None
    for p in _split_bf16(x, parts):
        d = (jnp.dot(mat01, p, preferred_element_type=f32) if mat_on_left
             else jnp.dot(p, mat01, preferred_element_type=f32))
        acc = d if acc is None else acc + d
    return acc


def _dot_nt(a, b):
    return lax.dot_general(a, b, (((1,), (1,)), ((), ())), preferred_element_type=f32)


def _dot_tn(a, b):
    return lax.dot_general(a, b, (((0,), (0,)), ((), ())), preferred_element_type=f32)


def _mod_kernel(c_ref, w_ref, b_ref, o_ref):
    a = _silu(c_ref[...]).astype(bf16)
    o_ref[...] = jnp.dot(a, w_ref[...].astype(bf16), preferred_element_type=f32) + b_ref[...]


def _modulation(cvecs, w_mod, b_mod):
    rows = cvecs.shape[0]
    n_out = w_mod.shape[1]
    tn = D_MODEL
    return pl.pallas_call(
        _mod_kernel,
        grid=(n_out // tn,),
        in_specs=[
            pl.BlockSpec((rows, D_MODEL), lambda j: (0, 0)),
            pl.BlockSpec((D_MODEL, tn), lambda j: (0, j)),
            pl.BlockSpec((1, tn), lambda j: (0, j)),
        ],
        out_specs=pl.BlockSpec((rows, tn), lambda j: (0, j)),
        out_shape=jax.ShapeDtypeStruct((rows, n_out), f32),
        compiler_params=pltpu.CompilerParams(dimension_semantics=("arbitrary",)),
        name="modulation",
    )(cvecs, w_mod, b_mod.reshape(1, n_out))


def _ffn_kernel(x_ref, m_ref, wup_ref, wd_ref, g_ref, b_ref, o_ref, act_ref, *, mod_base, ln_row):
    shift = m_ref[0, mod_base:mod_base + 1, :]
    scale = m_ref[0, mod_base + 1:mod_base + 2, :]
    gate = m_ref[0, mod_base + 2:mod_base + 3, :]
    subs = [slice(lo, lo + FFN_SUB) for lo in range(0, x_ref.shape[1], FFN_SUB)]
    xs = [x_ref[0, rows] for rows in subs]
    hs = [(x * (1.0 + scale) + shift).astype(bf16) for x in xs]
    for j in range(D_FF // FF_CHUNK):
        cols = slice(j * FF_CHUNK, (j + 1) * FF_CHUNK)
        for rows, h in zip(subs, hs):
            a = jnp.dot(h, wup_ref[:, cols], preferred_element_type=f32)
            u = jnp.dot(h, wup_ref[:, D_FF + j * FF_CHUNK:D_FF + (j + 1) * FF_CHUNK],
                        preferred_element_type=f32)
            act_ref[rows, cols] = (_silu(a) * u).astype(bf16)
    ys = [jnp.dot(act_ref[rows, :], wd_ref[...], preferred_element_type=f32) for rows in subs]
    for rows, x, y in zip(subs, xs, ys):
        r = ALPHA * x + 0.5 * gate * y
        o_ref[0, rows] = _layer_norm(r, g_ref[ln_row:ln_row + 1, :], b_ref[ln_row:ln_row + 1, :])


def _resident(shape):
    return pl.BlockSpec(shape, lambda *_: (0,) * len(shape), pipeline_mode=pl.Buffered(1))


def _ffn(x, mod, mod_group, w_up, wd, ln_g, ln_b, *, mod_base, ln_row):
    groups, length, _ = x.shape
    tm = FFN_TILE
    return pl.pallas_call(
        functools.partial(_ffn_kernel, mod_base=mod_base, ln_row=ln_row),
        grid=(groups, length // tm),
        in_specs=[
            pl.BlockSpec((1, tm, D_MODEL), lambda g, t: (g, t, 0)),
            pl.BlockSpec((1, N_MOD, D_MODEL), lambda g, t: (mod_group(g), 0, 0)),
            _resident((D_MODEL, 2 * D_FF)),
            _resident((D_FF, D_MODEL)),
            _resident((3, D_MODEL)),
            _resident((3, D_MODEL)),
        ],
        out_specs=pl.BlockSpec((1, tm, D_MODEL), lambda g, t: (g, t, 0)),
        out_shape=jax.ShapeDtypeStruct(x.shape, f32),
        scratch_shapes=[pltpu.VMEM((tm, D_FF), bf16)],
        compiler_params=pltpu.CompilerParams(
            dimension_semantics=("arbitrary", "arbitrary"), vmem_limit_bytes=VMEM_LIMIT_BYTES),
        name="ffn",
    )(x, mod, w_up, wd, ln_g, ln_b)


def _head_rms_norm(x, ones_bd, gain):
    ss = _dot01(ones_bd, x * x, 2, mat_on_left=False)
    return x * lax.rsqrt(ss * (1.0 / HEAD_DIM) + RMS_EPS) * gain


def _rope(x, cos, sin_signed):
    width = x.shape[-1]
    lane = lax.broadcasted_iota(jnp.int32, x.shape, 1)
    from_right = pltpu.roll(x, width - 16, 1)
    from_left = pltpu.roll(x, 16, 1)
    partner = jnp.where((lane & 31) < 16, from_right, from_left)
    return x * cos + partner * sin_signed


def _proj_kernel(*refs, rope):
    if rope:
        (x_ref, m_ref, w_ref, lbf_ref, lbb_ref, qg_ref, kg_ref, bd_ref, cos_ref, sin_ref,
         qs_ref, vh_ref, lff_ref, kkf_ref, lfb_ref, kkb_ref, sg_ref, qa_ref, ka_ref, vt_ref,
         spf_ref, spb_ref) = refs
    else:
        (x_ref, m_ref, w_ref, lbf_ref, lbb_ref, qg_ref, kg_ref, bd_ref,
         qs_ref, vh_ref, lff_ref, kkf_ref, lfb_ref, kkb_ref, sg_ref, qa_ref, ka_ref, vt_ref,
         spf_ref, spb_ref, kt_ref) = refs
    shift = m_ref[0, 3:4, :]
    scale = m_ref[0, 4:5, :]
    subs = [slice(lo, lo + PROJ_SUB) for lo in range(0, x_ref.shape[1], PROJ_SUB)]
    hs = [(x_ref[0, rows] * (1.0 + scale) + shift).astype(bf16) for rows in subs]

    def cols(h, lo, width):
        return jnp.dot(h, w_ref[:, lo:lo + width], preferred_element_type=f32)

    def lower_bound(lb_ref):
        l0 = lb_ref[0:1, :]
        l1 = lb_ref[1:2, :]
        m = jnp.maximum(l0, l1)
        e0 = jnp.exp(l0 - m)
        e1 = jnp.exp(l1 - m)
        return e0 / (e0 + e1)

    half = HG_FAST_BLOCK // 2

    def forget_gate(lb_ref, lo, lf_ref, kk_ref, span_ref):
        lb = lower_bound(lb_ref)
        for rows, h in zip(subs, hs):
            f = lb + (1.0 - lb) * jax.nn.sigmoid(cols(h, lo, HG_WIDTH))
            lf = jnp.log2(f)
            lf_ref[0, rows] = lf
            kk_ref[0, rows] = 1.0 - f
            sums = jnp.sum(lf.reshape(PROJ_SUB // half, half, HG_WIDTH), axis=1)
            span_ref[0, rows.start // half:rows.stop // half] = jnp.broadcast_to(
                jnp.max(jnp.abs(sums), axis=-1, keepdims=True), (PROJ_SUB // half, LANES))

    for rows, h in zip(subs, hs):
        qs_ref[0, rows] = _silu(cols(h, 0, HG_WIDTH))
    for rows, h in zip(subs, hs):
        vh_ref[0, rows] = cols(h, HG_WIDTH, HG_WIDTH)
    forget_gate(lbf_ref, 2 * HG_WIDTH, lff_ref, kkf_ref, spf_ref)
    forget_gate(lbb_ref, 3 * HG_WIDTH, lfb_ref, kkb_ref, spb_ref)
    for rows, h in zip(subs, hs):
        sg_ref[0, rows] = _silu(cols(h, 4 * HG_WIDTH, HG_WIDTH))

    base = 5 * HG_WIDTH
    piece = vt_ref.shape[2]
    for rows, h in zip(subs, hs):
        q = _head_rms_norm(cols(h, base, ATT_WIDTH), bd_ref[...], qg_ref[...])
        k = _head_rms_norm(cols(h, base + ATT_WIDTH, KV_WIDTH), bd_ref[0:KV_WIDTH, 0:KV_WIDTH], kg_ref[...])
        if rope:
            q = _rope(q, cos_ref[rows, :], sin_ref[rows, :])
            k = _rope(k, cos_ref[rows, 0:KV_WIDTH], sin_ref[rows, 0:KV_WIDTH])
        qa_ref[0, rows] = (q * (HEAD_DIM ** -0.5 * LOG2_E)).astype(bf16)
        ka_ref[0, rows] = k
        v = cols(h, base + ATT_WIDTH + KV_WIDTH, KV_WIDTH)
        for lo in range(rows.start, rows.stop, min(piece, PROJ_SUB)):
            n = min(piece, PROJ_SUB)
            dst = (lo // piece, slice(None), slice(lo % piece, lo % piece + n))
            vt_ref[dst] = v[lo - rows.start:lo - rows.start + n].T
            if not rope:
                kt_ref[dst] = k[lo - rows.start:lo - rows.start + n].T


def _proj(x, mod, mod_group, w_in, lb_f, lb_b, q_gain, k_gain, ones_bd, rope_tables, seq_len):
    groups, length, _ = x.shape
    tm = TOKEN_TILE
    rope = rope_tables is not None
    n_seq = groups * length // seq_len
    if seq_len >= tm:
        tiles_per_seq = seq_len // tm
        t_spec = pl.BlockSpec((1, KV_WIDTH, tm), lambda g, t: (g * (length // seq_len) + t // tiles_per_seq, 0,
                                                               t % tiles_per_seq))
    else:
        t_spec = pl.BlockSpec((tm // seq_len, KV_WIDTH, seq_len), lambda g, t: (g * (length // tm) + t, 0, 0))
    t_shape = jax.ShapeDtypeStruct((n_seq, KV_WIDTH, seq_len), f32)
    tok = lambda width: pl.BlockSpec((1, tm, width), lambda g, t: (g, t, 0))
    in_specs = [
        tok(D_MODEL),
        pl.BlockSpec((1, N_MOD, D_MODEL), lambda g, t: (mod_group(g), 0, 0)),
        _resident((D_MODEL, IN_WIDTH)),
        _resident((2, HG_WIDTH)),
        _resident((2, HG_WIDTH)),
        _resident((1, ATT_WIDTH)),
        _resident((1, KV_WIDTH)),
        _resident((ATT_WIDTH, ATT_WIDTH)),
    ]
    args = [x, mod, w_in, lb_f, lb_b, q_gain, k_gain, ones_bd]
    if rope:
        in_specs += [pl.BlockSpec((tm, ATT_WIDTH), lambda g, t: (t, 0))] * 2
        args += list(rope_tables)
    shape = lambda width, dt: jax.ShapeDtypeStruct((groups, length, width), dt)
    half = HG_FAST_BLOCK // 2
    span_spec = pl.BlockSpec((1, tm // half, LANES), lambda g, t: (g, t, 0))
    span_shape = jax.ShapeDtypeStruct((groups, length // half, LANES), f32)
    return pl.pallas_call(
        functools.partial(_proj_kernel, rope=rope),
        grid=(groups, length // tm),
        in_specs=in_specs,
        out_specs=([tok(HG_WIDTH)] * 7 + [tok(ATT_WIDTH), tok(KV_WIDTH), t_spec, span_spec, span_spec]
                   + ([] if rope else [t_spec])),
        out_shape=([shape(HG_WIDTH, f32)] * 7 + [shape(ATT_WIDTH, bf16), shape(KV_WIDTH, f32), t_shape,
                                                  span_shape, span_shape] + ([] if rope else [t_shape])),
        compiler_params=pltpu.CompilerParams(
            dimension_semantics=("arbitrary", "arbitrary"), vmem_limit_bytes=VMEM_LIMIT_BYTES),
        name="mixer_proj",
    )(*args)


def _hgrn_direction(qs, v, lf, kk, st_ref, o_ref, rows, *, rev):
    tc = qs.shape[0]
    row = lax.broadcasted_iota(jnp.int32, (tc, tc), 0)
    col = lax.broadcasted_iota(jnp.int32, (tc, tc), 1)
    ordered = (row < col) if rev else (row > col)
    differ = row ^ col
    tri = jnp.where((col >= row) if rev else (col <= row), 1.0, 0.0).astype(bf16)
    cum = _dot01(tri, lf, 3, mat_on_left=True)
    total = cum[0:1] if rev else cum[tc - 1:tc]
    vb = v.astype(bf16)
    carry = jnp.exp2(total)
    query_half = 0 if rev else 1
    heads = [slice(hd * HG_HEAD_DIM, (hd + 1) * HG_HEAD_DIM) for hd in range(HG_HEADS)]

    def level_exponent(s, ridx):
        if s == 1:
            return jnp.where((ridx & 1) == query_half, lf, 0.0)
        if s == 2:
            nxt = pltpu.roll(lf, tc - 1, 0)
            prv = pltpu.roll(lf, 1, 0)
            m4 = ridx & 3
            if rev:
                return jnp.where(m4 == 0, lf + nxt, jnp.where(m4 == 1, lf, jnp.where(m4 == 2, 0.0, prv)))
            return jnp.where(m4 == 0, nxt, jnp.where(m4 == 1, 0.0, jnp.where(m4 == 2, lf, lf + prv)))
        blocks = []
        for lo in range(0, tc, 2 * s):
            anchor = lo + (s if rev else s - 1)
            blocks.append(cum[lo:lo + 2 * s] - cum[anchor:anchor + 1])
        d = jnp.concatenate(blocks, axis=0)
        is_query = ((ridx >> (s.bit_length() - 1)) & 1) == query_half
        return jnp.where(is_query, d, -d)

    def add_levels(intra, first):
        ridx = lax.broadcasted_iota(jnp.int32, qs.shape, 0)
        s = first
        while s < tc:
            shift = s.bit_length() - 1
            is_query = ((ridx >> shift) & 1) == query_half
            mixed = (jnp.where(is_query, qs, kk) * jnp.exp2(level_exponent(s, ridx))).astype(bf16)
            pair = jnp.logical_and((differ >> shift) == 1, ordered)
            for hd, hs in enumerate(heads):
                intra[hd] = jnp.where(pair, _dot_nt(mixed[:, hs], mixed[:, hs]), intra[hd])
            s *= 2
        return intra

    def finish(intra, q_dec, k_end, extra):
        for hd, hs in enumerate(heads):
            st = st_ref[hd]
            o = (jnp.dot(intra[hd].astype(bf16), vb[:, hs], preferred_element_type=f32)
                 + _dot_nt(q_dec[:, hs], st.astype(bf16)))
            o_ref[0, rows, hs] = o if extra is None else o + extra[:, hs]
            st_ref[hd] = st * carry[:, hs] + _dot_tn(vb[:, hs], k_end[:, hs])

    blk = HG_FAST_BLOCK
    anchors = [lo + (blk // 2 if rev else blk // 2 - 1) for lo in range(0, tc, blk)]

    def mid_split_blocks():
        q_mid, k_mid, q_dec, k_end = [], [], [], []
        for lo, a in zip(range(0, tc, blk), anchors):
            rel = cum[lo:lo + blk] - cum[a:a + 1]
            q_mid.append(qs[lo:lo + blk] * jnp.exp2(rel))
            k_mid.append(kk[lo:lo + blk] * jnp.exp2(-rel))
            q_dec.append(q_mid[-1] * jnp.exp2(cum[a:a + 1]))
            k_end.append(k_mid[-1] * jnp.exp2(total - cum[a:a + 1]))
        q_mid, k_mid, q_dec, k_end = (jnp.concatenate(p, axis=0).astype(bf16) for p in (q_mid, k_mid, q_dec, k_end))
        same_block = (differ >> (blk.bit_length() - 1)) == 0
        keep = jnp.logical_and(same_block, (row <= col) if rev else (row >= col))
        intra = [jnp.where(keep, _dot_nt(q_mid[:, hs], k_mid[:, hs]), 0.0) for hs in heads]
        finish(add_levels(intra, blk), q_dec, k_end, None)

    def all_levels():
        intra = add_levels([jnp.zeros((tc, tc), f32)] * HG_HEADS, 1)
        q_dec = (qs * jnp.exp2(cum)).astype(bf16)
        k_end = (kk * jnp.exp2(total - cum)).astype(bf16)
        qk = qs * kk
        own = jnp.concatenate(
            [jnp.sum(qk[:, hs], axis=-1, keepdims=True) * v[:, hs] for hs in heads], axis=-1)
        finish(intra, q_dec, k_end, own)

    return mid_split_blocks, all_levels


def _hgrn_kernel(slow_ref, *refs, has_init):
    if has_init:
        (qsf_ref, vf_ref, lff_ref, kkf_ref, qsb_ref, vb_ref, lfb_ref, kkb_ref, s0f_ref, s0b_ref,
         of_ref, ob_ref, stf_ref, stb_ref) = refs
    else:
        (qsf_ref, vf_ref, lff_ref, kkf_ref, qsb_ref, vb_ref, lfb_ref, kkb_ref,
         of_ref, ob_ref, sf_out_ref, sb_out_ref, stf_ref, stb_ref) = refs
    t = pl.program_id(1)

    @pl.when(t == 0)
    def _():
        for hd in range(HG_HEADS):
            if has_init:
                stf_ref[hd] = s0f_ref[0, hd].T
                stb_ref[hd] = s0b_ref[0, hd].T
            else:
                stf_ref[hd] = jnp.zeros((HG_HEAD_DIM, HG_HEAD_DIM), f32)
                stb_ref[hd] = jnp.zeros((HG_HEAD_DIM, HG_HEAD_DIM), f32)

    chunks = [slice(lo, lo + HG_TILE) for lo in range(0, qsf_ref.shape[1], HG_TILE)]
    scans = []
    for r_f, r_b in zip(chunks, reversed(chunks)):
        scans.append(_hgrn_direction(qsf_ref[0, r_f], vf_ref[0, r_f], lff_ref[0, r_f], kkf_ref[0, r_f],
                                     stf_ref, of_ref, r_f, rev=False))
        scans.append(_hgrn_direction(qsb_ref[0, r_b], vb_ref[0, r_b], lfb_ref[0, r_b], kkb_ref[0, r_b],
                                     stb_ref, ob_ref, r_b, rev=True))
    slow = slow_ref[pl.program_id(0) * pl.num_programs(1) + t] != 0

    @pl.when(jnp.logical_not(slow))
    def _():
        for mid_split_blocks, _ in scans:
            mid_split_blocks()

    @pl.when(slow)
    def _():
        for _, all_levels in scans:
            all_levels()

    if not has_init:
        @pl.when(t == pl.num_programs(1) - 1)
        def _():
            for hd in range(HG_HEADS):
                sf_out_ref[0, hd] = stf_ref[hd].T
                sb_out_ref[0, hd] = stb_ref[hd].T


def _hgrn(qs, vh, lf_fwd, kk_fwd, lf_bwd, kk_bwd, span_fwd, span_bwd, init_states):
    groups, length, _ = qs.shape
    nt = length // HG_STEP
    has_init = init_states is not None
    per_step = lambda span: jnp.max(span[:, :, 0].reshape(groups, nt, -1), axis=-1) > HG_FAST_SPAN_LOG2
    slow = jnp.logical_or(per_step(span_fwd), per_step(span_bwd)[:, ::-1]).astype(jnp.int32).reshape(-1)
    fwd = pl.BlockSpec((1, HG_STEP, HG_WIDTH), lambda g, t, _: (g, t, 0))
    bwd = pl.BlockSpec((1, HG_STEP, HG_WIDTH), lambda g, t, _: (g, nt - 1 - t, 0))
    state = pl.BlockSpec((1, HG_HEADS, HG_HEAD_DIM, HG_HEAD_DIM), lambda g, t, _: (g, 0, 0, 0))
    in_specs = [fwd, fwd, fwd, fwd, bwd, bwd, bwd, bwd]
    args = [qs, vh, lf_fwd, kk_fwd, qs, vh, lf_bwd, kk_bwd]
    out_specs = [fwd, bwd]
    out_shape = [jax.ShapeDtypeStruct(qs.shape, f32)] * 2
    if has_init:
        in_specs += [state, state]
        args += list(init_states)
    else:
        out_specs += [state, state]
        out_shape += [jax.ShapeDtypeStruct((groups, HG_HEADS, HG_HEAD_DIM, HG_HEAD_DIM), f32)] * 2
    return pl.pallas_call(
        functools.partial(_hgrn_kernel, has_init=has_init),
        grid_spec=pltpu.PrefetchScalarGridSpec(
            num_scalar_prefetch=1,
            grid=(groups, nt),
            in_specs=in_specs,
            out_specs=out_specs,
            scratch_shapes=[pltpu.VMEM((HG_HEADS, HG_HEAD_DIM, HG_HEAD_DIM), f32)] * 2),
        out_shape=out_shape,
        compiler_params=pltpu.CompilerParams(
            dimension_semantics=("arbitrary", "arbitrary"), vmem_limit_bytes=VMEM_LIMIT_BYTES),
        name="hgrn2",
    )(slow, *args)


def _attn_kernel(*refs, n_ctx):
    if n_ctx:
        (qa_ref, ka_ref, vt_ref, ckt_ref, cvt_ref, x_ref, of_ref, ob_ref, sg_ref, m_ref, wo_ref, hg_ref, qg_ref,
         g_ref, b_ref, o_ref, kk_ref, vta_ref, oatt_ref, bound_ref, fixed_ref) = refs
    else:
        (qa_ref, ka_ref, vt_ref, x_ref, of_ref, ob_ref, sg_ref, m_ref, wo_ref, hg_ref, qg_ref,
         g_ref, b_ref, o_ref, kk_ref, vta_ref, oatt_ref, bound_ref, fixed_ref) = refs
    tq = qa_ref.shape[1]
    n_keys = kk_ref.shape[0]

    @pl.when(pl.program_id(1) == 0)
    def _():
        def fill_keys(lo, k):
            n = k.shape[0]
            low = lax.broadcasted_iota(jnp.int32, k.shape, 1) < HEAD_DIM
            k_sw = pltpu.roll(k, HEAD_DIM, 1)
            kk_ref[lo:lo + n, 0:LANES] = jnp.where(low, k, k_sw).astype(bf16)
            kk_ref[lo:lo + n, LANES:2 * LANES] = jnp.where(low, k_sw, k).astype(bf16)

        def fill_values(lo, vt):
            n = vt.shape[1]
            ones_row = jnp.where(lax.broadcasted_iota(jnp.int32, (V_ROWS - HEAD_DIM, n), 0) == 0, 1.0, 0.0)
            for kv in range(N_KV_HEADS):
                vta_ref[kv * V_ROWS:kv * V_ROWS + HEAD_DIM, lo:lo + n] = (
                    vt[kv * HEAD_DIM:(kv + 1) * HEAD_DIM].astype(bf16))
                vta_ref[kv * V_ROWS + HEAD_DIM:(kv + 1) * V_ROWS, lo:lo + n] = ones_row.astype(bf16)

        if n_ctx:
            fill_keys(0, ckt_ref[0].T)
            fill_values(0, cvt_ref[0])
        fill_keys(n_ctx, ka_ref[0])
        fill_values(n_ctx, vt_ref[0])

        gain_max = jnp.max(jnp.abs(qg_ref[...]), axis=1, keepdims=True)
        all_small = None
        for kv in range(N_KV_HEADS):
            kt = kk_ref[:, kv * LANES:(kv + 1) * LANES].astype(f32)
            k_norm2 = jnp.max(0.5 * jnp.sum(kt * kt, axis=1, keepdims=True), axis=0, keepdims=True)
            bound = (LOG2_E * SCORE_BOUND_SLACK) * gain_max * jnp.sqrt(k_norm2)
            bound_ref[kv] = bound[0, 0]
            small = jnp.where(bound <= SCORE_BOUND_LIMIT, 1, 0)
            all_small = small if all_small is None else all_small * small
        fixed_ref[0] = all_small[0, 0]

    low = lax.broadcasted_iota(jnp.int32, (tq, LANES), 1) < HEAD_DIM
    kc = min(KEY_CHUNK, n_keys)

    def masked_pair(tile):
        qp = qa_ref[0, :, tile * LANES:(tile + 1) * LANES]
        zero = jnp.zeros_like(qp)
        return jnp.concatenate([jnp.where(low, qp, zero), jnp.where(low, zero, qp)], axis=0)

    q_pairs = [masked_pair(tile) for tile in range(N_HEADS // 2)]
    pairs_per_kv = N_HEADS // N_KV_HEADS // 2
    work = [(tile, lo) for tile in range(N_HEADS // 2) for lo in range(0, n_keys, kc)]

    def scores(tile, lo):
        kv = tile // pairs_per_kv
        return _dot_nt(kk_ref[lo:lo + kc, kv * LANES:(kv + 1) * LANES], q_pairs[tile])

    def attend(fixed_shift):
        st_next = scores(*work[0])
        m = acc = None
        for i, (tile, lo) in enumerate(work):
            st = st_next
            if i + 1 < len(work):
                st_next = scores(*work[i + 1])
            kv = tile // pairs_per_kv
            values_t = vta_ref[kv * V_ROWS:(kv + 1) * V_ROWS, lo:lo + kc]
            if fixed_shift:
                e = jnp.exp2(st - bound_ref[kv]).astype(bf16)
                pv = jnp.dot(values_t, e, preferred_element_type=f32)
                acc = pv if acc is None else acc + pv
            else:
                m_chunk = jnp.max(st, axis=0, keepdims=True)
                m_new = m_chunk if m is None else jnp.maximum(m, m_chunk)
                e = jnp.exp2(st - m_new).astype(bf16)
                pv = jnp.dot(values_t, e, preferred_element_type=f32)
                acc = pv if acc is None else acc * jnp.exp2(m - m_new) + pv
                m = m_new
            if lo + kc == n_keys:
                on = (acc[0:HEAD_DIM] * (1.0 / acc[HEAD_DIM:HEAD_DIM + 1])).astype(bf16)
                oatt_ref[2 * tile * HEAD_DIM:(2 * tile + 1) * HEAD_DIM, :] = on[:, :tq]
                oatt_ref[(2 * tile + 1) * HEAD_DIM:(2 * tile + 2) * HEAD_DIM, :] = on[:, tq:]
                m = acc = None

    use_bound = fixed_ref[0] != 0
    pl.when(use_bound)(functools.partial(attend, True))
    pl.when(jnp.logical_not(use_bound))(functools.partial(attend, False))
    o_att_t = oatt_ref[...]

    o_sum = of_ref[0] + ob_ref[0]
    normed = []
    for hd in range(HG_HEADS):
        oh = o_sum[:, hd * HG_HEAD_DIM:(hd + 1) * HG_HEAD_DIM]
        ms = jnp.mean(oh * oh, axis=-1, keepdims=True)
        normed.append(oh * lax.rsqrt(ms + RMS_EPS))
    o_hg = (jnp.concatenate(normed, axis=-1) * hg_ref[...] * sg_ref[0]).astype(bf16)

    y = (jnp.dot(o_hg, wo_ref[0:HG_WIDTH, :], preferred_element_type=f32)
         + _dot_tn(o_att_t, wo_ref[HG_WIDTH:HG_WIDTH + ATT_WIDTH, :]))
    r = ALPHA * x_ref[0] + m_ref[0, 5:6, :] * y
    o_ref[0] = _layer_norm(r, g_ref[1:2, :], b_ref[1:2, :])


def _attn(qa, ka, vt, ctx_kv_t, x, o_f, o_b, sg, mod, mod_group, w_out, hg_gain, q_gain, ln_g, ln_b):
    groups, length, _ = qa.shape
    tq = Q_TILE
    n_ctx = 0 if ctx_kv_t is None else ctx_kv_t[0].shape[2]
    n_keys = n_ctx + length
    tok = lambda width: pl.BlockSpec((1, tq, width), lambda g, t: (g, t, 0))
    whole_t = lambda n: pl.BlockSpec((1, KV_WIDTH, n), lambda g, t: (g, 0, 0))
    in_specs = [tok(ATT_WIDTH), pl.BlockSpec((1, length, KV_WIDTH), lambda g, t: (g, 0, 0)), whole_t(length)]
    args = [qa, ka, vt]
    if n_ctx:
        in_specs += [whole_t(n_ctx), whole_t(n_ctx)]
        args += list(ctx_kv_t)
    in_specs += [
        tok(D_MODEL), tok(HG_WIDTH), tok(HG_WIDTH), tok(HG_WIDTH),
        pl.BlockSpec((1, N_MOD, D_MODEL), lambda g, t: (mod_group(g), 0, 0)),
        _resident((D_MODEL, D_MODEL)),
        _resident((1, HG_WIDTH)),
        _resident((1, ATT_WIDTH)),
        _resident((3, D_MODEL)),
        _resident((3, D_MODEL)),
    ]
    args += [x, o_f, o_b, sg, mod, w_out, hg_gain, q_gain, ln_g, ln_b]
    return pl.pallas_call(
        functools.partial(_attn_kernel, n_ctx=n_ctx),
        grid=(groups, length // tq),
        in_specs=in_specs,
        out_specs=tok(D_MODEL),
        out_shape=jax.ShapeDtypeStruct(x.shape, f32),
        scratch_shapes=[pltpu.VMEM((n_keys, N_KV_HEADS * LANES), bf16),
                        pltpu.VMEM((N_KV_HEADS * V_ROWS, n_keys), bf16),
                        pltpu.VMEM((ATT_WIDTH, tq), bf16),
                        pltpu.SMEM((N_KV_HEADS,), f32),
                        pltpu.SMEM((1,), jnp.int32)],
        compiler_params=pltpu.CompilerParams(
            dimension_semantics=("arbitrary", "arbitrary"), vmem_limit_bytes=VMEM_LIMIT_BYTES),
        name="attn_out",
    )(*args)


def _rope_tables(n_tokens):
    half = HEAD_DIM // 2
    t = jnp.arange(n_tokens)
    inv = ROPE_THETA ** (-jnp.arange(0, half, 2, dtype=f32) / half)
    ang_row = (t // GRID_W).astype(f32)[:, None] * inv
    ang_col = (t % GRID_W).astype(f32)[:, None] * inv
    cos = jnp.concatenate([jnp.cos(ang_row)] * 2 + [jnp.cos(ang_col)] * 2, axis=-1)
    sin = jnp.concatenate([-jnp.sin(ang_row), jnp.sin(ang_row), -jnp.sin(ang_col), jnp.sin(ang_col)], axis=-1)
    return jnp.tile(cos, (1, N_HEADS)), jnp.tile(sin, (1, N_HEADS))


def kernel(x_prompt, x_sample, cache_k, cache_v, state_hgrn_fwd, state_hgrn_bwd, c, c_ctx, w_mod, b_mod,
           w_ffn1_in, w_ffn1_out, w_ffn2_in, w_ffn2_out, w_in, w_out, q_norm_g, k_norm_g, hg_norm_g,
           lb_logits_fwd, lb_logits_bwd, ln_g, ln_b):
    assert w_mod.shape[0] == DEPTH and lb_logits_fwd.shape[0] == DEPTH + 1
    batch, seq, _ = x_prompt.shape
    dec_batch, dec_seq, _ = x_sample.shape
    past = cache_k.shape[2]

    ctx_row = dec_batch
    rows = 16
    cvecs = jnp.concatenate([c, c_ctx[None, :], jnp.zeros((rows - dec_batch - 1, D_MODEL), f32)], axis=0)
    mod = _modulation(cvecs, w_mod[0], b_mod[0]).reshape(rows, N_MOD, D_MODEL)

    to_bf16 = lambda w: w.astype(bf16)
    w1u, w2u = to_bf16(w_ffn1_in[0]), to_bf16(w_ffn2_in[0])
    w1d, w2d = to_bf16(w_ffn1_out[0]), to_bf16(w_ffn2_out[0])
    w_in_b, w_out_b = to_bf16(w_in[0]), to_bf16(w_out[0])
    gains = ln_g[0], ln_b[0]
    q_gain = jnp.tile(q_norm_g[0], N_HEADS).reshape(1, ATT_WIDTH)
    k_gain = jnp.tile(k_norm_g[0], N_KV_HEADS).reshape(1, KV_WIDTH)
    hg_gain = hg_norm_g[0].reshape(1, HG_WIDTH)
    head_of = jnp.arange(ATT_WIDTH) // HEAD_DIM
    ones_bd = (head_of[:, None] == head_of[None, :]).astype(bf16)

    def trunk(x, mod_group, rope_tables, ctx_kv_t, init_states, hg_groups):
        shape = x.shape
        seq_len = shape[0] * shape[1] // hg_groups
        per_seq = lambda a: a.reshape(hg_groups, seq_len, a.shape[-1])
        x = _ffn(x, mod, mod_group, w1u, w1d, *gains, mod_base=0, ln_row=0)
        proj = _proj(x, mod, mod_group, w_in_b, lb_logits_fwd, lb_logits_bwd, q_gain, k_gain, ones_bd, rope_tables,
                     seq_len)
        qs, vh, lf_f, kk_f, lf_b, kk_b, sg, qa, ka = map(per_seq, proj[:9])
        vt, span_f, span_b = proj[9], proj[10].reshape(hg_groups, -1, LANES), proj[11].reshape(hg_groups, -1, LANES)
        scans = _hgrn(qs, vh, lf_f, kk_f, lf_b, kk_b, span_f, span_b, init_states)
        x = _attn(qa, ka, vt, ctx_kv_t, per_seq(x), scans[0], scans[1], sg, mod, mod_group, w_out_b, hg_gain,
                  q_gain, *gains).reshape(shape)
        x = _ffn(x, mod, mod_group, w2u, w2d, *gains, mod_base=6, ln_row=2)
        return x, proj[12:], vt, scans[2:]

    ctx_group = lambda g: ctx_row
    y_prompt, (kt_new,), vt_new, states = trunk(
        x_prompt.reshape(1, batch * seq, D_MODEL), ctx_group, None, None, None, batch)
    y_prompt = y_prompt.reshape(batch, seq, D_MODEL)
    to_cache = lambda t: t.reshape(batch, DEPTH, N_KV_HEADS, HEAD_DIM, seq).transpose(0, 1, 4, 2, 3)
    new_cache_k, new_cache_v = to_cache(kt_new), to_cache(vt_new)
    new_state_fwd = states[0].reshape(batch, DEPTH, HG_HEADS, HG_HEAD_DIM, HG_HEAD_DIM)
    new_state_bwd = states[1].reshape(batch, DEPTH, HG_HEADS, HG_HEAD_DIM, HG_HEAD_DIM)

    from_cache = lambda t: t[:, 0].transpose(0, 2, 3, 1).reshape(dec_batch, KV_WIDTH, past)
    init_states = (state_hgrn_fwd[:, 0], state_hgrn_bwd[:, 0])
    y_sample, _, _, _ = trunk(x_sample, lambda g: g, _rope_tables(dec_seq), (from_cache(cache_k), from_cache(cache_v)),
                              init_states, dec_batch)

    return (y_prompt, y_sample, new_cache_k, new_cache_v, new_state_fwd, new_state_bwd)
```

```python
import functools

import jax
import jax.numpy as jnp
from jax import lax
from jax.experimental import pallas as pl
from jax.experimental.pallas import tpu as pltpu

f32 = jnp.float32
bf16 = jnp.bfloat16

D_MODEL = 1024
N_MOD = 9
HG_WIDTH = 512
HG_HEAD_DIM = 128
HG_HEADS = 4
ATT_WIDTH = 512
HEAD_DIM = 64
N_HEADS = 8
N_KV_HEADS = 2
KV_WIDTH = 128
IN_WIDTH = 5 * HG_WIDTH + ATT_WIDTH + 2 * KV_WIDTH
D_FF = 2816
GRID_W = 64
ROPE_THETA = 10000.0
DEPTH = 1
ALPHA = (2.0 * DEPTH) ** 0.25
LOG2_E = 1.4426950408889634
LN_EPS = 1e-6
RMS_EPS = 1e-6

LANES = 128
VMEM_LIMIT_BYTES = 56 * 1024 * 1024

FF_CHUNK = 256
FFN_TILE = 1024
FFN_SUB = 512
TOKEN_TILE = 512
PROJ_SUB = 256
HG_TILE = 128
HG_STEP = 256
Q_TILE = 256
KEY_CHUNK = 512
SCORE_BOUND_LIMIT = 40.0
SCORE_BOUND_SLACK = 1.01
V_ROWS = 80
HG_FAST_BLOCK = 64
HG_FAST_SPAN_LOG2 = 100.0


def _silu(x):
    return x * jax.nn.sigmoid(x)


def _layer_norm(r, g, b):
    mu = jnp.mean(r, axis=-1, keepdims=True)
    c = r - mu
    var = jnp.mean(c * c, axis=-1, keepdims=True)
    return c * lax.rsqrt(var + LN_EPS) * g + b


def _split_bf16(x, parts):
    out = []
    r = x
    for _ in range(parts - 1):
        p = r.astype(bf16)
        out.append(p)
        r = r - p.astype(f32)
    out.append(r.astype(bf16))
    return out


def _dot01(mat01, x, parts, *, mat_on_left):
    pieces = _split_bf16(x, parts)
    if mat_on_left:
        return jnp.dot(jnp.concatenate([mat01] * parts, axis=1), jnp.concatenate(pieces, axis=0),
                       preferred_element_type=f32)
    return jnp.dot(jnp.concatenate(pieces, axis=1), jnp.concatenate([mat01] * parts, axis=0),
                   preferred_element_type=f32)


def _dot_nt(a, b):
    return lax.dot_general(a, b, (((1,), (1,)), ((), ())), preferred_element_type=f32)


def _dot_tn(a, b):
    return lax.dot_general(a, b, (((0,), (0,)), ((), ())), preferred_element_type=f32)


def _mod_kernel(c_ref, w_ref, b_ref, o_ref):
    a = _silu(c_ref[...]).astype(bf16)
    o_ref[...] = jnp.dot(a, w_ref[...].astype(bf16), preferred_element_type=f32) + b_ref[...]


def _modulation(cvecs, w_mod, b_mod):
    rows = cvecs.shape[0]
    n_out = w_mod.shape[1]
    tn = D_MODEL
    return pl.pallas_call(
        _mod_kernel,
        grid=(n_out // tn,),
        in_specs=[
            pl.BlockSpec((rows, D_MODEL), lambda j: (0, 0)),
            pl.BlockSpec((D_MODEL, tn), lambda j: (0, j)),
            pl.BlockSpec((1, tn), lambda j: (0, j)),
        ],
        out_specs=pl.BlockSpec((rows, tn), lambda j: (0, j)),
        out_shape=jax.ShapeDtypeStruct((rows, n_out), f32),
        compiler_params=pltpu.CompilerParams(dimension_semantics=("arbitrary",)),
        name="modulation",
    )(cvecs, w_mod, b_mod.reshape(1, n_out))


def _ffn_kernel(x_ref, m_ref, wup_ref, wd_ref, g_ref, b_ref, o_ref, act_ref, *, mod_base, ln_row):
    shift = m_ref[0, mod_base:mod_base + 1, :]
    scale = m_ref[0, mod_base + 1:mod_base + 2, :]
    gate = m_ref[0, mod_base + 2:mod_base + 3, :]
    subs = [slice(lo, lo + FFN_SUB) for lo in range(0, x_ref.shape[1], FFN_SUB)]
    xs = [x_ref[0, rows] for rows in subs]
    hs = [(x * (1.0 + scale) + shift).astype(bf16) for x in xs]
    for j in range(D_FF // FF_CHUNK):
        cols = slice(j * FF_CHUNK, (j + 1) * FF_CHUNK)
        for rows, h in zip(subs, hs):
            a = jnp.dot(h, wup_ref[:, cols], preferred_element_type=f32)
            u = jnp.dot(h, wup_ref[:, D_FF + j * FF_CHUNK:D_FF + (j + 1) * FF_CHUNK],
                        preferred_element_type=f32)
            act_ref[rows, cols] = (_silu(a) * u).astype(bf16)
    ys = [jnp.dot(act_ref[rows, :], wd_ref[...], preferred_element_type=f32) for rows in subs]
    for rows, x, y in zip(subs, xs, ys):
        r = ALPHA * x + 0.5 * gate * y
        o_ref[0, rows] = _layer_norm(r, g_ref[ln_row:ln_row + 1, :], b_ref[ln_row:ln_row + 1, :])


def _resident(shape):
    return pl.BlockSpec(shape, lambda *_: (0,) * len(shape), pipeline_mode=pl.Buffered(1))


def _ffn(x, mod, mod_group, w_up, wd, ln_g, ln_b, *, mod_base, ln_row):
    groups, length, _ = x.shape
    tm = FFN_TILE
    return pl.pallas_call(
        functools.partial(_ffn_kernel, mod_base=mod_base, ln_row=ln_row),
        grid=(groups, length // tm),
        in_specs=[
            pl.BlockSpec((1, tm, D_MODEL), lambda g, t: (g, t, 0)),
            pl.BlockSpec((1, N_MOD, D_MODEL), lambda g, t: (mod_group(g), 0, 0)),
            _resident((D_MODEL, 2 * D_FF)),
            _resident((D_FF, D_MODEL)),
            _resident((3, D_MODEL)),
            _resident((3, D_MODEL)),
        ],
        out_specs=pl.BlockSpec((1, tm, D_MODEL), lambda g, t: (g, t, 0)),
        out_shape=jax.ShapeDtypeStruct(x.shape, f32),
        scratch_shapes=[pltpu.VMEM((tm, D_FF), bf16)],
        compiler_params=pltpu.CompilerParams(
            dimension_semantics=("arbitrary", "arbitrary"), vmem_limit_bytes=VMEM_LIMIT_BYTES),
        name="ffn",
    )(x, mod, w_up, wd, ln_g, ln_b)


def _head_rms_norm(x, ones_bd, gain):
    ss = _dot01(ones_bd, x * x, 2, mat_on_left=False)
    return x * lax.rsqrt(ss * (1.0 / HEAD_DIM) + RMS_EPS) * gain


def _rope(x, cos, sin_signed):
    width = x.shape[-1]
    lane = lax.broadcasted_iota(jnp.int32, x.shape, 1)
    from_right = pltpu.roll(x, width - 16, 1)
    from_left = pltpu.roll(x, 16, 1)
    partner = jnp.where((lane & 31) < 16, from_right, from_left)
    return x * cos + partner * sin_signed


def _proj_kernel(*refs, rope):
    if rope:
        (x_ref, m_ref, w_ref, lbf_ref, lbb_ref, qg_ref, kg_ref, bd_ref, cos_ref, sin_ref,
         qs_ref, vh_ref, lff_ref, kkf_ref, lfb_ref, kkb_ref, sg_ref, qa_ref, ka_ref, vt_ref,
         spf_ref, spb_ref) = refs
    else:
        (x_ref, m_ref, w_ref, lbf_ref, lbb_ref, qg_ref, kg_ref, bd_ref,
         qs_ref, vh_ref, lff_ref, kkf_ref, lfb_ref, kkb_ref, sg_ref, qa_ref, ka_ref, vt_ref,
         spf_ref, spb_ref, kt_ref) = refs
    shift = m_ref[0, 3:4, :]
    scale = m_ref[0, 4:5, :]
    subs = [slice(lo, lo + PROJ_SUB) for lo in range(0, x_ref.shape[1], PROJ_SUB)]
    hs = [(x_ref[0, rows] * (1.0 + scale) + shift).astype(bf16) for rows in subs]

    def cols(h, lo, width):
        return jnp.dot(h, w_ref[:, lo:lo + width], preferred_element_type=f32)

    def lower_bound(lb_ref):
        l0 = lb_ref[0:1, :]
        l1 = lb_ref[1:2, :]
        m = jnp.maximum(l0, l1)
        e0 = jnp.exp(l0 - m)
        e1 = jnp.exp(l1 - m)
        return e0 / (e0 + e1)

    half = HG_FAST_BLOCK // 2

    def forget_gate(lb_ref, lo, lf_ref, kk_ref, span_ref):
        lb = lower_bound(lb_ref)
        for rows, h in zip(subs, hs):
            f = lb + (1.0 - lb) * jax.nn.sigmoid(cols(h, lo, HG_WIDTH))
            lf = jnp.log2(f)
            lf_ref[0, rows] = lf
            kk_ref[0, rows] = 1.0 - f
            sums = jnp.sum(lf.reshape(PROJ_SUB // half, half, HG_WIDTH), axis=1)
            span_ref[0, rows.start // half:rows.stop // half] = jnp.broadcast_to(
                jnp.max(jnp.abs(sums), axis=-1, keepdims=True), (PROJ_SUB // half, LANES))

    for rows, h in zip(subs, hs):
        qs_ref[0, rows] = _silu(cols(h, 0, HG_WIDTH))
    for rows, h in zip(subs, hs):
        vh_ref[0, rows] = cols(h, HG_WIDTH, HG_WIDTH)
    forget_gate(lbf_ref, 2 * HG_WIDTH, lff_ref, kkf_ref, spf_ref)
    forget_gate(lbb_ref, 3 * HG_WIDTH, lfb_ref, kkb_ref, spb_ref)
    for rows, h in zip(subs, hs):
        sg_ref[0, rows] = _silu(cols(h, 4 * HG_WIDTH, HG_WIDTH))

    base = 5 * HG_WIDTH
    piece = vt_ref.shape[2]
    for rows, h in zip(subs, hs):
        q = _head_rms_norm(cols(h, base, ATT_WIDTH), bd_ref[...], qg_ref[...])
        k = _head_rms_norm(cols(h, base + ATT_WIDTH, KV_WIDTH), bd_ref[0:KV_WIDTH, 0:KV_WIDTH], kg_ref[...])
        if rope:
            q = _rope(q, cos_ref[rows, :], sin_ref[rows, :])
            k = _rope(k, cos_ref[rows, 0:KV_WIDTH], sin_ref[rows, 0:KV_WIDTH])
        qa_ref[0, rows] = (q * (HEAD_DIM ** -0.5 * LOG2_E)).astype(bf16)
        ka_ref[0, rows] = k
        v = cols(h, base + ATT_WIDTH + KV_WIDTH, KV_WIDTH)
        for lo in range(rows.start, rows.stop, min(piece, PROJ_SUB)):
            n = min(piece, PROJ_SUB)
            dst = (lo // piece, slice(None), slice(lo % piece, lo % piece + n))
            vt_ref[dst] = v[lo - rows.start:lo - rows.start + n].T
            if not rope:
                kt_ref[dst] = k[lo - rows.start:lo - rows.start + n].T


def _proj(x, mod, mod_group, w_in, lb_f, lb_b, q_gain, k_gain, ones_bd, rope_tables, seq_len):
    groups, length, _ = x.shape
    tm = TOKEN_TILE
    rope = rope_tables is not None
    n_seq = groups * length // seq_len
    if seq_len >= tm:
        tiles_per_seq = seq_len // tm
        t_spec = pl.BlockSpec((1, KV_WIDTH, tm), lambda g, t: (g * (length // seq_len) + t // tiles_per_seq, 0,
                                                               t % tiles_per_seq))
    else:
        t_spec = pl.BlockSpec((tm // seq_len, KV_WIDTH, seq_len), lambda g, t: (g * (length // tm) + t, 0, 0))
    t_shape = jax.ShapeDtypeStruct((n_seq, KV_WIDTH, seq_len), f32)
    tok = lambda width: pl.BlockSpec((1, tm, width), lambda g, t: (g, t, 0))
    in_specs = [
        tok(D_MODEL),
        pl.BlockSpec((1, N_MOD, D_MODEL), lambda g, t: (mod_group(g), 0, 0)),
        _resident((D_MODEL, IN_WIDTH)),
        _resident((2, HG_WIDTH)),
        _resident((2, HG_WIDTH)),
        _resident((1, ATT_WIDTH)),
        _resident((1, KV_WIDTH)),
        _resident((ATT_WIDTH, ATT_WIDTH)),
    ]
    args = [x, mod, w_in, lb_f, lb_b, q_gain, k_gain, ones_bd]
    if rope:
        in_specs += [pl.BlockSpec((tm, ATT_WIDTH), lambda g, t: (t, 0))] * 2
        args += list(rope_tables)
    shape = lambda width, dt: jax.ShapeDtypeStruct((groups, length, width), dt)
    half = HG_FAST_BLOCK // 2
    span_spec = pl.BlockSpec((1, tm // half, LANES), lambda g, t: (g, t, 0))
    span_shape = jax.ShapeDtypeStruct((groups, length // half, LANES), f32)
    return pl.pallas_call(
        functools.partial(_proj_kernel, rope=rope),
        grid=(groups, length // tm),
        in_specs=in_specs,
        out_specs=([tok(HG_WIDTH)] * 7 + [tok(ATT_WIDTH), tok(KV_WIDTH), t_spec, span_spec, span_spec]
                   + ([] if rope else [t_spec])),
        out_shape=([shape(HG_WIDTH, f32)] * 7 + [shape(ATT_WIDTH, bf16), shape(KV_WIDTH, f32), t_shape,
                                                  span_shape, span_shape] + ([] if rope else [t_shape])),
        compiler_params=pltpu.CompilerParams(
            dimension_semantics=("arbitrary", "arbitrary"), vmem_limit_bytes=VMEM_LIMIT_BYTES),
        name="mixer_proj",
    )(*args)


def _hgrn_direction(qs, v, lf, kk, st_ref, o_ref, rows, *, rev):
    tc = qs.shape[0]
    row = lax.broadcasted_iota(jnp.int32, (tc, tc), 0)
    col = lax.broadcasted_iota(jnp.int32, (tc, tc), 1)
    ordered = (row < col) if rev else (row > col)
    differ = row ^ col
    tri = jnp.where((col >= row) if rev else (col <= row), 1.0, 0.0).astype(bf16)
    cum = _dot01(tri, lf, 2, mat_on_left=True)
    total = cum[0:1] if rev else cum[tc - 1:tc]
    vb = v.astype(bf16)
    carry = jnp.exp2(total)
    query_half = 0 if rev else 1
    heads = [slice(hd * HG_HEAD_DIM, (hd + 1) * HG_HEAD_DIM) for hd in range(HG_HEADS)]

    def level_exponent(s, ridx):
        if s == 1:
            return jnp.where((ridx & 1) == query_half, lf, 0.0)
        if s == 2:
            nxt = pltpu.roll(lf, tc - 1, 0)
            prv = pltpu.roll(lf, 1, 0)
            m4 = ridx & 3
            if rev:
                return jnp.where(m4 == 0, lf + nxt, jnp.where(m4 == 1, lf, jnp.where(m4 == 2, 0.0, prv)))
            return jnp.where(m4 == 0, nxt, jnp.where(m4 == 1, 0.0, jnp.where(m4 == 2, lf, lf + prv)))
        blocks = []
        for lo in range(0, tc, 2 * s):
            anchor = lo + (s if rev else s - 1)
            blocks.append(cum[lo:lo + 2 * s] - cum[anchor:anchor + 1])
        d = jnp.concatenate(blocks, axis=0)
        is_query = ((ridx >> (s.bit_length() - 1)) & 1) == query_half
        return jnp.where(is_query, d, -d)

    def add_levels(intra, first):
        ridx = lax.broadcasted_iota(jnp.int32, qs.shape, 0)
        s = first
        while s < tc:
            shift = s.bit_length() - 1
            is_query = ((ridx >> shift) & 1) == query_half
            mixed = (jnp.where(is_query, qs, kk) * jnp.exp2(level_exponent(s, ridx))).astype(bf16)
            pair = jnp.logical_and((differ >> shift) == 1, ordered)
            for hd, hs in enumerate(heads):
                intra[hd] = jnp.where(pair, _dot_nt(mixed[:, hs], mixed[:, hs]), intra[hd])
            s *= 2
        return intra

    def finish(intra, q_dec, k_end, extra):
        for hd, hs in enumerate(heads):
            st = st_ref[hd]
            o = jnp.dot(jnp.concatenate([intra[hd].astype(bf16), q_dec[:, hs]], axis=1),
                        jnp.concatenate([vb[:, hs], st.T.astype(bf16)], axis=0), preferred_element_type=f32)
            o_ref[0, rows, hs] = o if extra is None else o + extra[:, hs]
            st_ref[hd] = st * carry[:, hs] + _dot_tn(vb[:, hs], k_end[:, hs])

    blk = HG_FAST_BLOCK
    anchors = [lo + (blk // 2 if rev else blk // 2 - 1) for lo in range(0, tc, blk)]

    def mid_split_blocks():
        q_mid, k_mid, q_dec, k_end = [], [], [], []
        for lo, a in zip(range(0, tc, blk), anchors):
            rel = cum[lo:lo + blk] - cum[a:a + 1]
            q_mid.append(qs[lo:lo + blk] * jnp.exp2(rel))
            k_mid.append(kk[lo:lo + blk] * jnp.exp2(-rel))
            q_dec.append(q_mid[-1] * jnp.exp2(cum[a:a + 1]))
            k_end.append(k_mid[-1] * jnp.exp2(total - cum[a:a + 1]))
        q_mid, k_mid, q_dec, k_end = (jnp.concatenate(p, axis=0).astype(bf16) for p in (q_mid, k_mid, q_dec, k_end))
        same_block = (differ >> (blk.bit_length() - 1)) == 0
        keep = jnp.logical_and(same_block, (row <= col) if rev else (row >= col))
        intra = [jnp.where(keep, _dot_nt(q_mid[:, hs], k_mid[:, hs]), 0.0) for hs in heads]
        finish(add_levels(intra, blk), q_dec, k_end, None)

    def all_levels():
        intra = add_levels([jnp.zeros((tc, tc), f32)] * HG_HEADS, 1)
        q_dec = (qs * jnp.exp2(cum)).astype(bf16)
        k_end = (kk * jnp.exp2(total - cum)).astype(bf16)
        qk = qs * kk
        own = jnp.concatenate(
            [jnp.sum(qk[:, hs], axis=-1, keepdims=True) * v[:, hs] for hs in heads], axis=-1)
        finish(intra, q_dec, k_end, own)

    return mid_split_blocks, all_levels


def _hgrn_kernel(slow_ref, *refs, has_init):
    if has_init:
        (qsf_ref, vf_ref, lff_ref, kkf_ref, qsb_ref, vb_ref, lfb_ref, kkb_ref, s0f_ref, s0b_ref,
         of_ref, ob_ref, stf_ref, stb_ref) = refs
    else:
        (qsf_ref, vf_ref, lff_ref, kkf_ref, qsb_ref, vb_ref, lfb_ref, kkb_ref,
         of_ref, ob_ref, sf_out_ref, sb_out_ref, stf_ref, stb_ref) = refs
    t = pl.program_id(1)

    @pl.when(t == 0)
    def _():
        for hd in range(HG_HEADS):
            if has_init:
                stf_ref[hd] = s0f_ref[0, hd].T
                stb_ref[hd] = s0b_ref[0, hd].T
            else:
                stf_ref[hd] = jnp.zeros((HG_HEAD_DIM, HG_HEAD_DIM), f32)
                stb_ref[hd] = jnp.zeros((HG_HEAD_DIM, HG_HEAD_DIM), f32)

    chunks = [slice(lo, lo + HG_TILE) for lo in range(0, qsf_ref.shape[1], HG_TILE)]
    scans = []
    for r_f, r_b in zip(chunks, reversed(chunks)):
        scans.append(_hgrn_direction(qsf_ref[0, r_f], vf_ref[0, r_f], lff_ref[0, r_f], kkf_ref[0, r_f],
                                     stf_ref, of_ref, r_f, rev=False))
        scans.append(_hgrn_direction(qsb_ref[0, r_b], vb_ref[0, r_b], lfb_ref[0, r_b], kkb_ref[0, r_b],
                                     stb_ref, ob_ref, r_b, rev=True))
    slow = slow_ref[pl.program_id(0) * pl.num_programs(1) + t] != 0

    @pl.when(jnp.logical_not(slow))
    def _():
        for mid_split_blocks, _ in scans:
            mid_split_blocks()

    @pl.when(slow)
    def _():
        for _, all_levels in scans:
            all_levels()

    if not has_init:
        @pl.when(t == pl.num_programs(1) - 1)
        def _():
            for hd in range(HG_HEADS):
                sf_out_ref[0, hd] = stf_ref[hd].T
                sb_out_ref[0, hd] = stb_ref[hd].T


def _hgrn(qs, vh, lf_fwd, kk_fwd, lf_bwd, kk_bwd, span_fwd, span_bwd, init_states):
    groups, length, _ = qs.shape
    nt = length // HG_STEP
    has_init = init_states is not None
    per_step = lambda span: jnp.max(span[:, :, 0].reshape(groups, nt, -1), axis=-1) > HG_FAST_SPAN_LOG2
    slow = jnp.logical_or(per_step(span_fwd), per_step(span_bwd)[:, ::-1]).astype(jnp.int32).reshape(-1)
    fwd = pl.BlockSpec((1, HG_STEP, HG_WIDTH), lambda g, t, _: (g, t, 0))
    bwd = pl.BlockSpec((1, HG_STEP, HG_WIDTH), lambda g, t, _: (g, nt - 1 - t, 0))
    state = pl.BlockSpec((1, HG_HEADS, HG_HEAD_DIM, HG_HEAD_DIM), lambda g, t, _: (g, 0, 0, 0))
    in_specs = [fwd, fwd, fwd, fwd, bwd, bwd, bwd, bwd]
    args = [qs, vh, lf_fwd, kk_fwd, qs, vh, lf_bwd, kk_bwd]
    out_specs = [fwd, bwd]
    out_shape = [jax.ShapeDtypeStruct(qs.shape, f32)] * 2
    if has_init:
        in_specs += [state, state]
        args += list(init_states)
    else:
        out_specs += [state, state]
        out_shape += [jax.ShapeDtypeStruct((groups, HG_HEADS, HG_HEAD_DIM, HG_HEAD_DIM), f32)] * 2
    return pl.pallas_call(
        functools.partial(_hgrn_kernel, has_init=has_init),
        grid_spec=pltpu.PrefetchScalarGridSpec(
            num_scalar_prefetch=1,
            grid=(groups, nt),
            in_specs=in_specs,
            out_specs=out_specs,
            scratch_shapes=[pltpu.VMEM((HG_HEADS, HG_HEAD_DIM, HG_HEAD_DIM), f32)] * 2),
        out_shape=out_shape,
        compiler_params=pltpu.CompilerParams(
            dimension_semantics=("arbitrary", "arbitrary"), vmem_limit_bytes=VMEM_LIMIT_BYTES),
        name="hgrn2",
    )(slow, *args)


def _attn_kernel(*refs, n_ctx):
    if n_ctx:
        (qa_ref, ka_ref, vt_ref, ckt_ref, cvt_ref, x_ref, of_ref, ob_ref, sg_ref, m_ref, wo_ref, hg_ref, qg_ref,
         g_ref, b_ref, o_ref, kk_ref, vta_ref, oatt_ref, bound_ref, fixed_ref) = refs
    else:
        (qa_ref, ka_ref, vt_ref, x_ref, of_ref, ob_ref, sg_ref, m_ref, wo_ref, hg_ref, qg_ref,
         g_ref, b_ref, o_ref, kk_ref, vta_ref, oatt_ref, bound_ref, fixed_ref) = refs
    tq = qa_ref.shape[1]
    n_keys = kk_ref.shape[0]

    @pl.when(pl.program_id(1) == 0)
    def _():
        def fill_keys(lo, k):
            n = k.shape[0]
            low = lax.broadcasted_iota(jnp.int32, k.shape, 1) < HEAD_DIM
            k_sw = pltpu.roll(k, HEAD_DIM, 1)
            kk_ref[lo:lo + n, 0:LANES] = jnp.where(low, k, k_sw).astype(bf16)
            kk_ref[lo:lo + n, LANES:2 * LANES] = jnp.where(low, k_sw, k).astype(bf16)

        def fill_values(lo, vt):
            n = vt.shape[1]
            ones_row = jnp.where(lax.broadcasted_iota(jnp.int32, (V_ROWS - HEAD_DIM, n), 0) == 0, 1.0, 0.0)
            for kv in range(N_KV_HEADS):
                vta_ref[kv * V_ROWS:kv * V_ROWS + HEAD_DIM, lo:lo + n] = (
                    vt[kv * HEAD_DIM:(kv + 1) * HEAD_DIM].astype(bf16))
                vta_ref[kv * V_ROWS + HEAD_DIM:(kv + 1) * V_ROWS, lo:lo + n] = ones_row.astype(bf16)

        if n_ctx:
            fill_keys(0, ckt_ref[0].T)
            fill_values(0, cvt_ref[0])
        fill_keys(n_ctx, ka_ref[0])
        fill_values(n_ctx, vt_ref[0])

        gain_max = jnp.max(jnp.abs(qg_ref[...]), axis=1, keepdims=True)
        all_small = None
        for kv in range(N_KV_HEADS):
            kt = kk_ref[:, kv * LANES:(kv + 1) * LANES].astype(f32)
            k_norm2 = jnp.max(0.5 * jnp.sum(kt * kt, axis=1, keepdims=True), axis=0, keepdims=True)
            bound = (LOG2_E * SCORE_BOUND_SLACK) * gain_max * jnp.sqrt(k_norm2)
            bound_ref[kv] = bound[0, 0]
            small = jnp.where(bound <= SCORE_BOUND_LIMIT, 1, 0)
            all_small = small if all_small is None else all_small * small
        fixed_ref[0] = all_small[0, 0]

    low = lax.broadcasted_iota(jnp.int32, (tq, LANES), 1) < HEAD_DIM
    kc = min(KEY_CHUNK, n_keys)

    def masked_pair(tile):
        qp = qa_ref[0, :, tile * LANES:(tile + 1) * LANES]
        zero = jnp.zeros_like(qp)
        return jnp.concatenate([jnp.where(low, qp, zero), jnp.where(low, zero, qp)], axis=0)

    q_pairs = [masked_pair(tile) for tile in range(N_HEADS // 2)]
    pairs_per_kv = N_HEADS // N_KV_HEADS // 2
    work = [(tile, lo) for tile in range(N_HEADS // 2) for lo in range(0, n_keys, kc)]

    def scores(tile, lo):
        kv = tile // pairs_per_kv
        return _dot_nt(kk_ref[lo:lo + kc, kv * LANES:(kv + 1) * LANES], q_pairs[tile])

    def attend(fixed_shift):
        st_next = scores(*work[0])
        m = acc = None
        for i, (tile, lo) in enumerate(work):
            st = st_next
            if i + 1 < len(work):
                st_next = scores(*work[i + 1])
            kv = tile // pairs_per_kv
            values_t = vta_ref[kv * V_ROWS:(kv + 1) * V_ROWS, lo:lo + kc]
            if fixed_shift:
                e = jnp.exp2(st - bound_ref[kv]).astype(bf16)
                pv = jnp.dot(values_t, e, preferred_element_type=f32)
                acc = pv if acc is None else acc + pv
            else:
                m_chunk = jnp.max(st, axis=0, keepdims=True)
                m_new = m_chunk if m is None else jnp.maximum(m, m_chunk)
                e = jnp.exp2(st - m_new).astype(bf16)
                pv = jnp.dot(values_t, e, preferred_element_type=f32)
                acc = pv if acc is None else acc * jnp.exp2(m - m_new) + pv
                m = m_new
            if lo + kc == n_keys:
                on = (acc[0:HEAD_DIM] * (1.0 / acc[HEAD_DIM:HEAD_DIM + 1])).astype(bf16)
                oatt_ref[2 * tile * HEAD_DIM:(2 * tile + 1) * HEAD_DIM, :] = on[:, :tq]
                oatt_ref[(2 * tile + 1) * HEAD_DIM:(2 * tile + 2) * HEAD_DIM, :] = on[:, tq:]
                m = acc = None

    use_bound = fixed_ref[0] != 0
    pl.when(use_bound)(functools.partial(attend, True))
    pl.when(jnp.logical_not(use_bound))(functools.partial(attend, False))
    o_att_t = oatt_ref[...]

    o_sum = of_ref[0] + ob_ref[0]
    normed = []
    for hd in range(HG_HEADS):
        oh = o_sum[:, hd * HG_HEAD_DIM:(hd + 1) * HG_HEAD_DIM]
        ms = jnp.mean(oh * oh, axis=-1, keepdims=True)
        normed.append(oh * lax.rsqrt(ms + RMS_EPS))
    o_hg = (jnp.concatenate(normed, axis=-1) * hg_ref[...] * sg_ref[0]).astype(bf16)

    y = (jnp.dot(o_hg, wo_ref[0:HG_WIDTH, :], preferred_element_type=f32)
         + _dot_tn(o_att_t, wo_ref[HG_WIDTH:HG_WIDTH + ATT_WIDTH, :]))
    r = ALPHA * x_ref[0] + m_ref[0, 5:6, :] * y
    o_ref[0] = _layer_norm(r, g_ref[1:2, :], b_ref[1:2, :])


def _attn(qa, ka, vt, ctx_kv_t, x, o_f, o_b, sg, mod, mod_group, w_out, hg_gain, q_gain, ln_g, ln_b):
    groups, length, _ = qa.shape
    tq = Q_TILE
    n_ctx = 0 if ctx_kv_t is None else ctx_kv_t[0].shape[2]
    n_keys = n_ctx + length
    tok = lambda width: pl.BlockSpec((1, tq, width), lambda g, t: (g, t, 0))
    whole_t = lambda n: pl.BlockSpec((1, KV_WIDTH, n), lambda g, t: (g, 0, 0))
    in_specs = [tok(ATT_WIDTH), pl.BlockSpec((1, length, KV_WIDTH), lambda g, t: (g, 0, 0)), whole_t(length)]
    args = [qa, ka, vt]
    if n_ctx:
        in_specs += [whole_t(n_ctx), whole_t(n_ctx)]
        args += list(ctx_kv_t)
    in_specs += [
        tok(D_MODEL), tok(HG_WIDTH), tok(HG_WIDTH), tok(HG_WIDTH),
        pl.BlockSpec((1, N_MOD, D_MODEL), lambda g, t: (mod_group(g), 0, 0)),
        _resident((D_MODEL, D_MODEL)),
        _resident((1, HG_WIDTH)),
        _resident((1, ATT_WIDTH)),
        _resident((3, D_MODEL)),
        _resident((3, D_MODEL)),
    ]
    args += [x, o_f, o_b, sg, mod, w_out, hg_gain, q_gain, ln_g, ln_b]
    return pl.pallas_call(
        functools.partial(_attn_kernel, n_ctx=n_ctx),
        grid=(groups, length // tq),
        in_specs=in_specs,
        out_specs=tok(D_MODEL),
        out_shape=jax.ShapeDtypeStruct(x.shape, f32),
        scratch_shapes=[pltpu.VMEM((n_keys, N_KV_HEADS * LANES), bf16),
                        pltpu.VMEM((N_KV_HEADS * V_ROWS, n_keys), bf16),
                        pltpu.VMEM((ATT_WIDTH, tq), bf16),
                        pltpu.SMEM((N_KV_HEADS,), f32),
                        pltpu.SMEM((1,), jnp.int32)],
        compiler_params=pltpu.CompilerParams(
            dimension_semantics=("arbitrary", "arbitrary"), vmem_limit_bytes=VMEM_LIMIT_BYTES),
        name="attn_out",
    )(*args)


def _rope_tables(n_tokens):
    half = HEAD_DIM // 2
    t = jnp.arange(n_tokens)
    inv = ROPE_THETA ** (-jnp.arange(0, half, 2, dtype=f32) / half)
    ang_row = (t // GRID_W).astype(f32)[:, None] * inv
    ang_col = (t % GRID_W).astype(f32)[:, None] * inv
    cos = jnp.concatenate([jnp.cos(ang_row)] * 2 + [jnp.cos(ang_col)] * 2, axis=-1)
    sin = jnp.concatenate([-jnp.sin(ang_row), jnp.sin(ang_row), -jnp.sin(ang_col), jnp.sin(ang_col)], axis=-1)
    return jnp.tile(cos, (1, N_HEADS)), jnp.tile(sin, (1, N_HEADS))


def kernel(x_prompt, x_sample, cache_k, cache_v, state_hgrn_fwd, state_hgrn_bwd, c, c_ctx, w_mod, b_mod,
           w_ffn1_in, w_ffn1_out, w_ffn2_in, w_ffn2_out, w_in, w_out, q_norm_g, k_norm_g, hg_norm_g,
           lb_logits_fwd, lb_logits_bwd, ln_g, ln_b):
    assert w_mod.shape[0] == DEPTH and lb_logits_fwd.shape[0] == DEPTH + 1
    batch, seq, _ = x_prompt.shape
    dec_batch, dec_seq, _ = x_sample.shape
    past = cache_k.shape[2]

    ctx_row = dec_batch
    rows = 16
    cvecs = jnp.concatenate([c, c_ctx[None, :], jnp.zeros((rows - dec_batch - 1, D_MODEL), f32)], axis=0)
    mod = _modulation(cvecs, w_mod[0], b_mod[0]).reshape(rows, N_MOD, D_MODEL)

    to_bf16 = lambda w: w.astype(bf16)
    w1u, w2u = to_bf16(w_ffn1_in[0]), to_bf16(w_ffn2_in[0])
    w1d, w2d = to_bf16(w_ffn1_out[0]), to_bf16(w_ffn2_out[0])
    w_in_b, w_out_b = to_bf16(w_in[0]), to_bf16(w_out[0])
    gains = ln_g[0], ln_b[0]
    q_gain = jnp.tile(q_norm_g[0], N_HEADS).reshape(1, ATT_WIDTH)
    k_gain = jnp.tile(k_norm_g[0], N_KV_HEADS).reshape(1, KV_WIDTH)
    hg_gain = hg_norm_g[0].reshape(1, HG_WIDTH)
    head_of = jnp.arange(ATT_WIDTH) // HEAD_DIM
    ones_bd = (head_of[:, None] == head_of[None, :]).astype(bf16)

    def trunk(x, mod_group, rope_tables, ctx_kv_t, init_states, hg_groups):
        shape = x.shape
        seq_len = shape[0] * shape[1] // hg_groups
        per_seq = lambda a: a.reshape(hg_groups, seq_len, a.shape[-1])
        x = _ffn(x, mod, mod_group, w1u, w1d, *gains, mod_base=0, ln_row=0)
        proj = _proj(x, mod, mod_group, w_in_b, lb_logits_fwd, lb_logits_bwd, q_gain, k_gain, ones_bd, rope_tables,
                     seq_len)
        qs, vh, lf_f, kk_f, lf_b, kk_b, sg, qa, ka = map(per_seq, proj[:9])
        vt, span_f, span_b = proj[9], proj[10].reshape(hg_groups, -1, LANES), proj[11].reshape(hg_groups, -1, LANES)
        scans = _hgrn(qs, vh, lf_f, kk_f, lf_b, kk_b, span_f, span_b, init_states)
        x = _attn(qa, ka, vt, ctx_kv_t, per_seq(x), scans[0], scans[1], sg, mod, mod_group, w_out_b, hg_gain,
                  q_gain, *gains).reshape(shape)
        x = _ffn(x, mod, mod_group, w2u, w2d, *gains, mod_base=6, ln_row=2)
        return x, proj[12:], vt, scans[2:]

    ctx_group = lambda g: ctx_row
    y_prompt, (kt_new,), vt_new, states = trunk(
        x_prompt.reshape(1, batch * seq, D_MODEL), ctx_group, None, None, None, batch)
    y_prompt = y_prompt.reshape(batch, seq, D_MODEL)
    to_cache = lambda t: t.reshape(batch, DEPTH, N_KV_HEADS, HEAD_DIM, seq).transpose(0, 1, 4, 2, 3)
    new_cache_k, new_cache_v = to_cache(kt_new), to_cache(vt_new)
    new_state_fwd = states[0].reshape(batch, DEPTH, HG_HEADS, HG_HEAD_DIM, HG_HEAD_DIM)
    new_state_bwd = states[1].reshape(batch, DEPTH, HG_HEADS, HG_HEAD_DIM, HG_HEAD_DIM)

    from_cache = lambda t: t[:, 0].transpose(0, 2, 3, 1).reshape(dec_batch, KV_WIDTH, past)
    init_states = (state_hgrn_fwd[:, 0], state_hgrn_bwd[:, 0])
    y_sample, _, _, _ = trunk(x_sample, lambda g: g, _rope_tables(dec_seq), (from_cache(cache_k), from_cache(cache_v)),
                              init_states, dec_batch)

    return (y_prompt, y_sample, new_cache_k, new_cache_v, new_state_fwd, new_state_bwd)
```

```python
import functools

import jax
import jax.numpy as jnp
from jax import lax
from jax.experimental import pallas as pl
from jax.experimental.pallas import tpu as pltpu

f32 = jnp.float32
bf16 = jnp.bfloat16

D_MODEL = 1024
N_MOD = 9
HG_WIDTH = 512
HG_HEAD_DIM = 128
HG_HEADS = 4
ATT_WIDTH = 512
HEAD_DIM = 64
N_HEADS = 8
N_KV_HEADS = 2
KV_WIDTH = 128
IN_WIDTH = 5 * HG_WIDTH + ATT_WIDTH + 2 * KV_WIDTH
D_FF = 2816
GRID_W = 64
ROPE_THETA = 10000.0
DEPTH = 1
ALPHA = (2.0 * DEPTH) ** 0.25
LOG2_E = 1.4426950408889634
LN_EPS = 1e-6
RMS_EPS = 1e-6

LANES = 128
VMEM_LIMIT_BYTES = 56 * 1024 * 1024

FF_CHUNK = 256
FFN_TILE = 1024
FFN_SUB = 512
TOKEN_TILE = 512
PROJ_SUB = 256
HG_TILE = 128
HG_STEP = 256
Q_TILE = 256
KEY_CHUNK = 512
SCORE_BOUND_LIMIT = 40.0
SCORE_BOUND_SLACK = 1.01
V_ROWS = 80
HG_FAST_BLOCK = 64
HG_FAST_SPAN_LOG2 = 100.0


def _silu(x):
    return x * jax.nn.sigmoid(x)


def _layer_norm(r, g, b):
    mu = jnp.mean(r, axis=-1, keepdims=True)
    c = r - mu
    var = jnp.mean(c * c, axis=-1, keepdims=True)
    return c * lax.rsqrt(var + LN_EPS) * g + b


def _split_bf16(x, parts):
    out = []
    r = x
    for _ in range(parts - 1):
        p = r.astype(bf16)
        out.append(p)
        r = r - p.astype(f32)
    out.append(r.astype(bf16))
    return out


def _dot01(mat01, x, parts, *, mat_on_left):
    pieces = _split_bf16(x, parts)
    if mat_on_left:
        return jnp.dot(jnp.concatenate([mat01] * parts, axis=1), jnp.concatenate(pieces, axis=0),
                       preferred_element_type=f32)
    return jnp.dot(jnp.concatenate(pieces, axis=1), jnp.concatenate([mat01] * parts, axis=0),
                   preferred_element_type=f32)


def _dot_nt(a, b):
    return lax.dot_general(a, b, (((1,), (1,)), ((), ())), preferred_element_type=f32)


def _dot_tn(a, b):
    return lax.dot_general(a, b, (((0,), (0,)), ((), ())), preferred_element_type=f32)


def _mod_kernel(c_ref, w_ref, b_ref, o_ref):
    a = _silu(c_ref[...]).astype(bf16)
    o_ref[...] = jnp.dot(a, w_ref[...].astype(bf16), preferred_element_type=f32) + b_ref[...]


def _modulation(cvecs, w_mod, b_mod):
    rows = cvecs.shape[0]
    n_out = w_mod.shape[1]
    tn = D_MODEL
    return pl.pallas_call(
        _mod_kernel,
        grid=(n_out // tn,),
        in_specs=[
            pl.BlockSpec((rows, D_MODEL), lambda j: (0, 0)),
            pl.BlockSpec((D_MODEL, tn), lambda j: (0, j)),
            pl.BlockSpec((1, tn), lambda j: (0, j)),
        ],
        out_specs=pl.BlockSpec((rows, tn), lambda j: (0, j)),
        out_shape=jax.ShapeDtypeStruct((rows, n_out), f32),
        compiler_params=pltpu.CompilerParams(dimension_semantics=("arbitrary",)),
        name="modulation",
    )(cvecs, w_mod, b_mod.reshape(1, n_out))


def _ffn_kernel(x_ref, m_ref, wup_ref, wd_ref, g_ref, b_ref, o_ref, act_ref, *, mod_base, ln_row):
    shift = m_ref[0, mod_base:mod_base + 1, :]
    scale = m_ref[0, mod_base + 1:mod_base + 2, :]
    gate = m_ref[0, mod_base + 2:mod_base + 3, :]
    subs = [slice(lo, lo + FFN_SUB) for lo in range(0, x_ref.shape[1], FFN_SUB)]
    xs = [x_ref[0, rows] for rows in subs]
    hs = [(x * (1.0 + scale) + shift).astype(bf16) for x in xs]
    for j in range(D_FF // FF_CHUNK):
        cols = slice(j * FF_CHUNK, (j + 1) * FF_CHUNK)
        for rows, h in zip(subs, hs):
            a = jnp.dot(h, wup_ref[:, cols], preferred_element_type=f32)
            u = jnp.dot(h, wup_ref[:, D_FF + j * FF_CHUNK:D_FF + (j + 1) * FF_CHUNK],
                        preferred_element_type=f32)
            act_ref[rows, cols] = (_silu(a) * u).astype(bf16)
    ys = [jnp.dot(act_ref[rows, :], wd_ref[...], preferred_element_type=f32) for rows in subs]
    for rows, x, y in zip(subs, xs, ys):
        r = ALPHA * x + 0.5 * gate * y
        o_ref[0, rows] = _layer_norm(r, g_ref[ln_row:ln_row + 1, :], b_ref[ln_row:ln_row + 1, :])


def _resident(shape):
    return pl.BlockSpec(shape, lambda *_: (0,) * len(shape), pipeline_mode=pl.Buffered(1))


def _ffn(x, mod, mod_group, w_up, wd, ln_g, ln_b, *, mod_base, ln_row):
    groups, length, _ = x.shape
    tm = FFN_TILE
    return pl.pallas_call(
        functools.partial(_ffn_kernel, mod_base=mod_base, ln_row=ln_row),
        grid=(groups, length // tm),
        in_specs=[
            pl.BlockSpec((1, tm, D_MODEL), lambda g, t: (g, t, 0)),
            pl.BlockSpec((1, N_MOD, D_MODEL), lambda g, t: (mod_group(g), 0, 0)),
            _resident((D_MODEL, 2 * D_FF)),
            _resident((D_FF, D_MODEL)),
            _resident((3, D_MODEL)),
            _resident((3, D_MODEL)),
        ],
        out_specs=pl.BlockSpec((1, tm, D_MODEL), lambda g, t: (g, t, 0)),
        out_shape=jax.ShapeDtypeStruct(x.shape, f32),
        scratch_shapes=[pltpu.VMEM((tm, D_FF), bf16)],
        compiler_params=pltpu.CompilerParams(
            dimension_semantics=("arbitrary", "arbitrary"), vmem_limit_bytes=VMEM_LIMIT_BYTES),
        name="ffn",
    )(x, mod, w_up, wd, ln_g, ln_b)


def _head_rms_norm(x, ones_bd, gain):
    ss = _dot01(ones_bd, x * x, 2, mat_on_left=False)
    return x * lax.rsqrt(ss * (1.0 / HEAD_DIM) + RMS_EPS) * gain


def _rope(x, cos, sin_signed):
    width = x.shape[-1]
    lane = lax.broadcasted_iota(jnp.int32, x.shape, 1)
    from_right = pltpu.roll(x, width - 16, 1)
    from_left = pltpu.roll(x, 16, 1)
    partner = jnp.where((lane & 31) < 16, from_right, from_left)
    return x * cos + partner * sin_signed


def _proj_kernel(*refs, rope):
    if rope:
        (x_ref, m_ref, w_ref, lbf_ref, lbb_ref, qg_ref, kg_ref, bd_ref, cos_ref, sin_ref,
         qs_ref, vh_ref, ff_ref, fb_ref, sg_ref, qa_ref, ka_ref, vt_ref,
         spf_ref, spb_ref) = refs
    else:
        (x_ref, m_ref, w_ref, lbf_ref, lbb_ref, qg_ref, kg_ref, bd_ref,
         qs_ref, vh_ref, ff_ref, fb_ref, sg_ref, qa_ref, ka_ref, vt_ref,
         spf_ref, spb_ref, kt_ref) = refs
    shift = m_ref[0, 3:4, :]
    scale = m_ref[0, 4:5, :]
    subs = [slice(lo, lo + PROJ_SUB) for lo in range(0, x_ref.shape[1], PROJ_SUB)]
    hs = [(x_ref[0, rows] * (1.0 + scale) + shift).astype(bf16) for rows in subs]

    def cols(h, lo, width):
        return jnp.dot(h, w_ref[:, lo:lo + width], preferred_element_type=f32)

    def lower_bound(lb_ref):
        l0 = lb_ref[0:1, :]
        l1 = lb_ref[1:2, :]
        m = jnp.maximum(l0, l1)
        e0 = jnp.exp(l0 - m)
        e1 = jnp.exp(l1 - m)
        return e0 / (e0 + e1)

    half = HG_FAST_BLOCK // 2

    def forget_gate(lb_ref, lo, f_ref, span_ref):
        lb = lower_bound(lb_ref)
        for rows, h in zip(subs, hs):
            f = lb + (1.0 - lb) * jax.nn.sigmoid(cols(h, lo, HG_WIDTH))
            f_ref[0, rows] = f
            sums = jnp.sum(jnp.log2(f).reshape(PROJ_SUB // half, half, HG_WIDTH), axis=1)
            span_ref[0, rows.start // half:rows.stop // half] = jnp.broadcast_to(
                jnp.max(jnp.abs(sums), axis=-1, keepdims=True), (PROJ_SUB // half, LANES))

    for rows, h in zip(subs, hs):
        qs_ref[0, rows] = _silu(cols(h, 0, HG_WIDTH))
    for rows, h in zip(subs, hs):
        vh_ref[0, rows] = cols(h, HG_WIDTH, HG_WIDTH).astype(bf16)
    forget_gate(lbf_ref, 2 * HG_WIDTH, ff_ref, spf_ref)
    forget_gate(lbb_ref, 3 * HG_WIDTH, fb_ref, spb_ref)
    for rows, h in zip(subs, hs):
        sg_ref[0, rows] = _silu(cols(h, 4 * HG_WIDTH, HG_WIDTH))

    base = 5 * HG_WIDTH
    piece = vt_ref.shape[2]
    for rows, h in zip(subs, hs):
        q = _head_rms_norm(cols(h, base, ATT_WIDTH), bd_ref[...], qg_ref[...])
        k = _head_rms_norm(cols(h, base + ATT_WIDTH, KV_WIDTH), bd_ref[0:KV_WIDTH, 0:KV_WIDTH], kg_ref[...])
        if rope:
            q = _rope(q, cos_ref[rows, :], sin_ref[rows, :])
            k = _rope(k, cos_ref[rows, 0:KV_WIDTH], sin_ref[rows, 0:KV_WIDTH])
        qa_ref[0, rows] = (q * (HEAD_DIM ** -0.5 * LOG2_E)).astype(bf16)
        ka_ref[0, rows] = k
        v = cols(h, base + ATT_WIDTH + KV_WIDTH, KV_WIDTH)
        for lo in range(rows.start, rows.stop, min(piece, PROJ_SUB)):
            n = min(piece, PROJ_SUB)
            dst = (lo // piece, slice(None), slice(lo % piece, lo % piece + n))
            vt_ref[dst] = v[lo - rows.start:lo - rows.start + n].T
            if not rope:
                kt_ref[dst] = k[lo - rows.start:lo - rows.start + n].T


def _proj(x, mod, mod_group, w_in, lb_f, lb_b, q_gain, k_gain, ones_bd, rope_tables, seq_len):
    groups, length, _ = x.shape
    tm = TOKEN_TILE
    rope = rope_tables is not None
    n_seq = groups * length // seq_len
    if seq_len >= tm:
        tiles_per_seq = seq_len // tm
        t_spec = pl.BlockSpec((1, KV_WIDTH, tm), lambda g, t: (g * (length // seq_len) + t // tiles_per_seq, 0,
                                                               t % tiles_per_seq))
    else:
        t_spec = pl.BlockSpec((tm // seq_len, KV_WIDTH, seq_len), lambda g, t: (g * (length // tm) + t, 0, 0))
    t_shape = jax.ShapeDtypeStruct((n_seq, KV_WIDTH, seq_len), f32)
    tok = lambda width: pl.BlockSpec((1, tm, width), lambda g, t: (g, t, 0))
    in_specs = [
        tok(D_MODEL),
        pl.BlockSpec((1, N_MOD, D_MODEL), lambda g, t: (mod_group(g), 0, 0)),
        _resident((D_MODEL, IN_WIDTH)),
        _resident((2, HG_WIDTH)),
        _resident((2, HG_WIDTH)),
        _resident((1, ATT_WIDTH)),
        _resident((1, KV_WIDTH)),
        _resident((ATT_WIDTH, ATT_WIDTH)),
    ]
    args = [x, mod, w_in, lb_f, lb_b, q_gain, k_gain, ones_bd]
    if rope:
        in_specs += [pl.BlockSpec((tm, ATT_WIDTH), lambda g, t: (t, 0))] * 2
        args += list(rope_tables)
    shape = lambda width, dt: jax.ShapeDtypeStruct((groups, length, width), dt)
    half = HG_FAST_BLOCK // 2
    span_spec = pl.BlockSpec((1, tm // half, LANES), lambda g, t: (g, t, 0))
    span_shape = jax.ShapeDtypeStruct((groups, length // half, LANES), f32)
    return pl.pallas_call(
        functools.partial(_proj_kernel, rope=rope),
        grid=(groups, length // tm),
        in_specs=in_specs,
        out_specs=([tok(HG_WIDTH)] * 5 + [tok(ATT_WIDTH), tok(KV_WIDTH), t_spec, span_spec, span_spec]
                   + ([] if rope else [t_spec])),
        out_shape=([shape(HG_WIDTH, f32), shape(HG_WIDTH, bf16)] + [shape(HG_WIDTH, f32)] * 3
                   + [shape(ATT_WIDTH, bf16), shape(KV_WIDTH, f32), t_shape, span_shape, span_shape]
                   + ([] if rope else [t_shape])),
        compiler_params=pltpu.CompilerParams(
            dimension_semantics=("arbitrary", "arbitrary"), vmem_limit_bytes=VMEM_LIMIT_BYTES),
        name="mixer_proj",
    )(*args)


def _hgrn_direction(qs, vb, f, st_ref, o_ref, rows, *, rev):
    tc = qs.shape[0]
    lf = jnp.log2(f)
    kk = 1.0 - f
    row = lax.broadcasted_iota(jnp.int32, (tc, tc), 0)
    col = lax.broadcasted_iota(jnp.int32, (tc, tc), 1)
    ordered = (row < col) if rev else (row > col)
    differ = row ^ col
    tri = jnp.where((col >= row) if rev else (col <= row), 1.0, 0.0).astype(bf16)
    cum = _dot01(tri, lf, 2, mat_on_left=True)
    total = cum[0:1] if rev else cum[tc - 1:tc]
    carry = jnp.exp2(total)
    query_half = 0 if rev else 1
    heads = [slice(hd * HG_HEAD_DIM, (hd + 1) * HG_HEAD_DIM) for hd in range(HG_HEADS)]

    def level_exponent(s, ridx):
        if s == 1:
            return jnp.where((ridx & 1) == query_half, lf, 0.0)
        if s == 2:
            nxt = pltpu.roll(lf, tc - 1, 0)
            prv = pltpu.roll(lf, 1, 0)
            m4 = ridx & 3
            if rev:
                return jnp.where(m4 == 0, lf + nxt, jnp.where(m4 == 1, lf, jnp.where(m4 == 2, 0.0, prv)))
            return jnp.where(m4 == 0, nxt, jnp.where(m4 == 1, 0.0, jnp.where(m4 == 2, lf, lf + prv)))
        blocks = []
        for lo in range(0, tc, 2 * s):
            anchor = lo + (s if rev else s - 1)
            blocks.append(cum[lo:lo + 2 * s] - cum[anchor:anchor + 1])
        d = jnp.concatenate(blocks, axis=0)
        is_query = ((ridx >> (s.bit_length() - 1)) & 1) == query_half
        return jnp.where(is_query, d, -d)

    def add_levels(intra, first):
        ridx = lax.broadcasted_iota(jnp.int32, qs.shape, 0)
        s = first
        while s < tc:
            shift = s.bit_length() - 1
            is_query = ((ridx >> shift) & 1) == query_half
            mixed = (jnp.where(is_query, qs, kk) * jnp.exp2(level_exponent(s, ridx))).astype(bf16)
            pair = jnp.logical_and((differ >> shift) == 1, ordered)
            for hd, hs in enumerate(heads):
                intra[hd] = jnp.where(pair, _dot_nt(mixed[:, hs], mixed[:, hs]), intra[hd])
            s *= 2
        return intra

    def finish(intra, q_dec, k_end, extra):
        for hd, hs in enumerate(heads):
            st = st_ref[hd]
            o = jnp.dot(jnp.concatenate([intra[hd].astype(bf16), q_dec[:, hs]], axis=1),
                        jnp.concatenate([vb[:, hs], st.T.astype(bf16)], axis=0), preferred_element_type=f32)
            o_ref[0, rows, hs] = o if extra is None else o + extra[:, hs]
            st_ref[hd] = st * carry[:, hs] + _dot_tn(vb[:, hs], k_end[:, hs])

    blk = HG_FAST_BLOCK
    anchors = [lo + (blk // 2 if rev else blk // 2 - 1) for lo in range(0, tc, blk)]

    def mid_split_blocks():
        q_mid, k_mid, q_dec, k_end = [], [], [], []
        for lo, a in zip(range(0, tc, blk), anchors):
            rel = cum[lo:lo + blk] - cum[a:a + 1]
            q_mid.append(qs[lo:lo + blk] * jnp.exp2(rel))
            k_mid.append(kk[lo:lo + blk] * jnp.exp2(-rel))
            q_dec.append(q_mid[-1] * jnp.exp2(cum[a:a + 1]))
            k_end.append(k_mid[-1] * jnp.exp2(total - cum[a:a + 1]))
        q_mid, k_mid, q_dec, k_end = (jnp.concatenate(p, axis=0).astype(bf16) for p in (q_mid, k_mid, q_dec, k_end))
        same_block = (differ >> (blk.bit_length() - 1)) == 0
        keep = jnp.logical_and(same_block, (row <= col) if rev else (row >= col))
        intra = [jnp.where(keep, _dot_nt(q_mid[:, hs], k_mid[:, hs]), 0.0) for hs in heads]
        finish(add_levels(intra, blk), q_dec, k_end, None)

    def all_levels():
        intra = add_levels([jnp.zeros((tc, tc), f32)] * HG_HEADS, 1)
        q_dec = (qs * jnp.exp2(cum)).astype(bf16)
        k_end = (kk * jnp.exp2(total - cum)).astype(bf16)
        qk = qs * kk
        own = jnp.concatenate(
            [jnp.sum(qk[:, hs], axis=-1, keepdims=True) * vb[:, hs].astype(f32) for hs in heads], axis=-1)
        finish(intra, q_dec, k_end, own)

    return mid_split_blocks, all_levels


def _hgrn_kernel(slow_ref, *refs, has_init):
    if has_init:
        (qsf_ref, vf_ref, ff_ref, qsb_ref, vb_ref, fb_ref, s0f_ref, s0b_ref,
         of_ref, ob_ref, stf_ref, stb_ref) = refs
    else:
        (qsf_ref, vf_ref, ff_ref, qsb_ref, vb_ref, fb_ref,
         of_ref, ob_ref, sf_out_ref, sb_out_ref, stf_ref, stb_ref) = refs
    t = pl.program_id(1)

    @pl.when(t == 0)
    def _():
        for hd in range(HG_HEADS):
            if has_init:
                stf_ref[hd] = s0f_ref[0, hd].T
                stb_ref[hd] = s0b_ref[0, hd].T
            else:
                stf_ref[hd] = jnp.zeros((HG_HEAD_DIM, HG_HEAD_DIM), f32)
                stb_ref[hd] = jnp.zeros((HG_HEAD_DIM, HG_HEAD_DIM), f32)

    chunks = [slice(lo, lo + HG_TILE) for lo in range(0, qsf_ref.shape[1], HG_TILE)]
    scans = []
    for r_f, r_b in zip(chunks, reversed(chunks)):
        scans.append(_hgrn_direction(qsf_ref[0, r_f], vf_ref[0, r_f], ff_ref[0, r_f], stf_ref, of_ref, r_f, rev=False))
        scans.append(_hgrn_direction(qsb_ref[0, r_b], vb_ref[0, r_b], fb_ref[0, r_b], stb_ref, ob_ref, r_b, rev=True))
    slow = slow_ref[pl.program_id(0) * pl.num_programs(1) + t] != 0

    @pl.when(jnp.logical_not(slow))
    def _():
        for mid_split_blocks, _ in scans:
            mid_split_blocks()

    @pl.when(slow)
    def _():
        for _, all_levels in scans:
            all_levels()

    if not has_init:
        @pl.when(t == pl.num_programs(1) - 1)
        def _():
            for hd in range(HG_HEADS):
                sf_out_ref[0, hd] = stf_ref[hd].T
                sb_out_ref[0, hd] = stb_ref[hd].T


def _hgrn(qs, vh, f_fwd, f_bwd, span_fwd, span_bwd, init_states):
    groups, length, _ = qs.shape
    nt = length // HG_STEP
    has_init = init_states is not None
    per_step = lambda span: jnp.max(span[:, :, 0].reshape(groups, nt, -1), axis=-1) > HG_FAST_SPAN_LOG2
    slow = jnp.logical_or(per_step(span_fwd), per_step(span_bwd)[:, ::-1]).astype(jnp.int32).reshape(-1)
    fwd = pl.BlockSpec((1, HG_STEP, HG_WIDTH), lambda g, t, _: (g, t, 0))
    bwd = pl.BlockSpec((1, HG_STEP, HG_WIDTH), lambda g, t, _: (g, nt - 1 - t, 0))
    state = pl.BlockSpec((1, HG_HEADS, HG_HEAD_DIM, HG_HEAD_DIM), lambda g, t, _: (g, 0, 0, 0))
    in_specs = [fwd, fwd, fwd, bwd, bwd, bwd]
    args = [qs, vh, f_fwd, qs, vh, f_bwd]
    out_specs = [fwd, bwd]
    out_shape = [jax.ShapeDtypeStruct(qs.shape, f32)] * 2
    if has_init:
        in_specs += [state, state]
        args += list(init_states)
    else:
        out_specs += [state, state]
        out_shape += [jax.ShapeDtypeStruct((groups, HG_HEADS, HG_HEAD_DIM, HG_HEAD_DIM), f32)] * 2
    return pl.pallas_call(
        functools.partial(_hgrn_kernel, has_init=has_init),
        grid_spec=pltpu.PrefetchScalarGridSpec(
            num_scalar_prefetch=1,
            grid=(groups, nt),
            in_specs=in_specs,
            out_specs=out_specs,
            scratch_shapes=[pltpu.VMEM((HG_HEADS, HG_HEAD_DIM, HG_HEAD_DIM), f32)] * 2),
        out_shape=out_shape,
        compiler_params=pltpu.CompilerParams(
            dimension_semantics=("arbitrary", "arbitrary"), vmem_limit_bytes=VMEM_LIMIT_BYTES),
        name="hgrn2",
    )(slow, *args)


def _attn_kernel(*refs, n_ctx):
    if n_ctx:
        (qa_ref, ka_ref, vt_ref, ckt_ref, cvt_ref, x_ref, of_ref, ob_ref, sg_ref, m_ref, wo_ref, hg_ref, qg_ref,
         g_ref, b_ref, o_ref, kk_ref, vta_ref, oatt_ref, bound_ref, fixed_ref) = refs
    else:
        (qa_ref, ka_ref, vt_ref, x_ref, of_ref, ob_ref, sg_ref, m_ref, wo_ref, hg_ref, qg_ref,
         g_ref, b_ref, o_ref, kk_ref, vta_ref, oatt_ref, bound_ref, fixed_ref) = refs
    tq = qa_ref.shape[1]
    n_keys = kk_ref.shape[0]

    @pl.when(pl.program_id(1) == 0)
    def _():
        def fill_keys(lo, k):
            n = k.shape[0]
            low = lax.broadcasted_iota(jnp.int32, k.shape, 1) < HEAD_DIM
            k_sw = pltpu.roll(k, HEAD_DIM, 1)
            kk_ref[lo:lo + n, 0:LANES] = jnp.where(low, k, k_sw).astype(bf16)
            kk_ref[lo:lo + n, LANES:2 * LANES] = jnp.where(low, k_sw, k).astype(bf16)

        def fill_values(lo, vt):
            n = vt.shape[1]
            ones_row = jnp.where(lax.broadcasted_iota(jnp.int32, (V_ROWS - HEAD_DIM, n), 0) == 0, 1.0, 0.0)
            for kv in range(N_KV_HEADS):
                vta_ref[kv * V_ROWS:kv * V_ROWS + HEAD_DIM, lo:lo + n] = (
                    vt[kv * HEAD_DIM:(kv + 1) * HEAD_DIM].astype(bf16))
                vta_ref[kv * V_ROWS + HEAD_DIM:(kv + 1) * V_ROWS, lo:lo + n] = ones_row.astype(bf16)

        if n_ctx:
            fill_keys(0, ckt_ref[0].T)
            fill_values(0, cvt_ref[0])
        fill_keys(n_ctx, ka_ref[0])
        fill_values(n_ctx, vt_ref[0])

        gain_max = jnp.max(jnp.abs(qg_ref[...]), axis=1, keepdims=True)
        all_small = None
        for kv in range(N_KV_HEADS):
            kt = kk_ref[:, kv * LANES:(kv + 1) * LANES].astype(f32)
            k_norm2 = jnp.max(0.5 * jnp.sum(kt * kt, axis=1, keepdims=True), axis=0, keepdims=True)
            bound = (LOG2_E * SCORE_BOUND_SLACK) * gain_max * jnp.sqrt(k_norm2)
            bound_ref[kv] = bound[0, 0]
            small = jnp.where(bound <= SCORE_BOUND_LIMIT, 1, 0)
            all_small = small if all_small is None else all_small * small
        fixed_ref[0] = all_small[0, 0]

    low = lax.broadcasted_iota(jnp.int32, (tq, LANES), 1) < HEAD_DIM
    kc = min(KEY_CHUNK, n_keys)

    def masked_pair(tile):
        qp = qa_ref[0, :, tile * LANES:(tile + 1) * LANES]
        zero = jnp.zeros_like(qp)
        return jnp.concatenate([jnp.where(low, qp, zero), jnp.where(low, zero, qp)], axis=0)

    q_pairs = [masked_pair(tile) for tile in range(N_HEADS // 2)]
    pairs_per_kv = N_HEADS // N_KV_HEADS // 2
    work = [(tile, lo) for tile in range(N_HEADS // 2) for lo in range(0, n_keys, kc)]

    def scores(tile, lo):
        kv = tile // pairs_per_kv
        return _dot_nt(kk_ref[lo:lo + kc, kv * LANES:(kv + 1) * LANES], q_pairs[tile])

    def attend(fixed_shift):
        st_next = scores(*work[0])
        m = acc = None
        for i, (tile, lo) in enumerate(work):
            st = st_next
            if i + 1 < len(work):
                st_next = scores(*work[i + 1])
            kv = tile // pairs_per_kv
            values_t = vta_ref[kv * V_ROWS:(kv + 1) * V_ROWS, lo:lo + kc]
            if fixed_shift:
                e = jnp.exp2(st - bound_ref[kv]).astype(bf16)
                pv = jnp.dot(values_t, e, preferred_element_type=f32)
                acc = pv if acc is None else acc + pv
            else:
                m_chunk = jnp.max(st, axis=0, keepdims=True)
                m_new = m_chunk if m is None else jnp.maximum(m, m_chunk)
                e = jnp.exp2(st - m_new).astype(bf16)
                pv = jnp.dot(values_t, e, preferred_element_type=f32)
                acc = pv if acc is None else acc * jnp.exp2(m - m_new) + pv
                m = m_new
            if lo + kc == n_keys:
                on = (acc[0:HEAD_DIM] * (1.0 / acc[HEAD_DIM:HEAD_DIM + 1])).astype(bf16)
                oatt_ref[2 * tile * HEAD_DIM:(2 * tile + 1) * HEAD_DIM, :] = on[:, :tq]
                oatt_ref[(2 * tile + 1) * HEAD_DIM:(2 * tile + 2) * HEAD_DIM, :] = on[:, tq:]
                m = acc = None

    use_bound = fixed_ref[0] != 0
    pl.when(use_bound)(functools.partial(attend, True))
    pl.when(jnp.logical_not(use_bound))(functools.partial(attend, False))
    o_att_t = oatt_ref[...]

    o_sum = of_ref[0] + ob_ref[0]
    normed = []
    for hd in range(HG_HEADS):
        oh = o_sum[:, hd * HG_HEAD_DIM:(hd + 1) * HG_HEAD_DIM]
        ms = jnp.mean(oh * oh, axis=-1, keepdims=True)
        normed.append(oh * lax.rsqrt(ms + RMS_EPS))
    o_hg = (jnp.concatenate(normed, axis=-1) * hg_ref[...] * sg_ref[0]).astype(bf16)

    y = (jnp.dot(o_hg, wo_ref[0:HG_WIDTH, :], preferred_element_type=f32)
         + _dot_tn(o_att_t, wo_ref[HG_WIDTH:HG_WIDTH + ATT_WIDTH, :]))
    r = ALPHA * x_ref[0] + m_ref[0, 5:6, :] * y
    o_ref[0] = _layer_norm(r, g_ref[1:2, :], b_ref[1:2, :])


def _attn(qa, ka, vt, ctx_kv_t, x, o_f, o_b, sg, mod, mod_group, w_out, hg_gain, q_gain, ln_g, ln_b):
    groups, length, _ = qa.shape
    tq = Q_TILE
    n_ctx = 0 if ctx_kv_t is None else ctx_kv_t[0].shape[2]
    n_keys = n_ctx + length
    tok = lambda width: pl.BlockSpec((1, tq, width), lambda g, t: (g, t, 0))
    whole_t = lambda n: pl.BlockSpec((1, KV_WIDTH, n), lambda g, t: (g, 0, 0))
    in_specs = [tok(ATT_WIDTH), pl.BlockSpec((1, length, KV_WIDTH), lambda g, t: (g, 0, 0)), whole_t(length)]
    args = [qa, ka, vt]
    if n_ctx:
        in_specs += [whole_t(n_ctx), whole_t(n_ctx)]
        args += list(ctx_kv_t)
    in_specs += [
        tok(D_MODEL), tok(HG_WIDTH), tok(HG_WIDTH), tok(HG_WIDTH),
        pl.BlockSpec((1, N_MOD, D_MODEL), lambda g, t: (mod_group(g), 0, 0)),
        _resident((D_MODEL, D_MODEL)),
        _resident((1, HG_WIDTH)),
        _resident((1, ATT_WIDTH)),
        _resident((3, D_MODEL)),
        _resident((3, D_MODEL)),
    ]
    args += [x, o_f, o_b, sg, mod, w_out, hg_gain, q_gain, ln_g, ln_b]
    return pl.pallas_call(
        functools.partial(_attn_kernel, n_ctx=n_ctx),
        grid=(groups, length // tq),
        in_specs=in_specs,
        out_specs=tok(D_MODEL),
        out_shape=jax.ShapeDtypeStruct(x.shape, f32),
        scratch_shapes=[pltpu.VMEM((n_keys, N_KV_HEADS * LANES), bf16),
                        pltpu.VMEM((N_KV_HEADS * V_ROWS, n_keys), bf16),
                        pltpu.VMEM((ATT_WIDTH, tq), bf16),
                        pltpu.SMEM((N_KV_HEADS,), f32),
                        pltpu.SMEM((1,), jnp.int32)],
        compiler_params=pltpu.CompilerParams(
            dimension_semantics=("arbitrary", "arbitrary"), vmem_limit_bytes=VMEM_LIMIT_BYTES),
        name="attn_out",
    )(*args)


def _rope_tables(n_tokens):
    half = HEAD_DIM // 2
    t = jnp.arange(n_tokens)
    inv = ROPE_THETA ** (-jnp.arange(0, half, 2, dtype=f32) / half)
    ang_row = (t // GRID_W).astype(f32)[:, None] * inv
    ang_col = (t % GRID_W).astype(f32)[:, None] * inv
    cos = jnp.concatenate([jnp.cos(ang_row)] * 2 + [jnp.cos(ang_col)] * 2, axis=-1)
    sin = jnp.concatenate([-jnp.sin(ang_row), jnp.sin(ang_row), -jnp.sin(ang_col), jnp.sin(ang_col)], axis=-1)
    return jnp.tile(cos, (1, N_HEADS)), jnp.tile(sin, (1, N_HEADS))


def kernel(x_prompt, x_sample, cache_k, cache_v, state_hgrn_fwd, state_hgrn_bwd, c, c_ctx, w_mod, b_mod,
           w_ffn1_in, w_ffn1_out, w_ffn2_in, w_ffn2_out, w_in, w_out, q_norm_g, k_norm_g, hg_norm_g,
           lb_logits_fwd, lb_logits_bwd, ln_g, ln_b):
    assert w_mod.shape[0] == DEPTH and lb_logits_fwd.shape[0] == DEPTH + 1
    batch, seq, _ = x_prompt.shape
    dec_batch, dec_seq, _ = x_sample.shape
    past = cache_k.shape[2]

    ctx_row = dec_batch
    rows = 16
    cvecs = jnp.concatenate([c, c_ctx[None, :], jnp.zeros((rows - dec_batch - 1, D_MODEL), f32)], axis=0)
    mod = _modulation(cvecs, w_mod[0], b_mod[0]).reshape(rows, N_MOD, D_MODEL)

    to_bf16 = lambda w: w.astype(bf16)
    w1u, w2u = to_bf16(w_ffn1_in[0]), to_bf16(w_ffn2_in[0])
    w1d, w2d = to_bf16(w_ffn1_out[0]), to_bf16(w_ffn2_out[0])
    w_in_b, w_out_b = to_bf16(w_in[0]), to_bf16(w_out[0])
    gains = ln_g[0], ln_b[0]
    q_gain = jnp.tile(q_norm_g[0], N_HEADS).reshape(1, ATT_WIDTH)
    k_gain = jnp.tile(k_norm_g[0], N_KV_HEADS).reshape(1, KV_WIDTH)
    hg_gain = hg_norm_g[0].reshape(1, HG_WIDTH)
    head_of = jnp.arange(ATT_WIDTH) // HEAD_DIM
    ones_bd = (head_of[:, None] == head_of[None, :]).astype(bf16)

    def trunk(x, mod_group, rope_tables, ctx_kv_t, init_states, hg_groups):
        shape = x.shape
        seq_len = shape[0] * shape[1] // hg_groups
        per_seq = lambda a: a.reshape(hg_groups, seq_len, a.shape[-1])
        x = _ffn(x, mod, mod_group, w1u, w1d, *gains, mod_base=0, ln_row=0)
        proj = _proj(x, mod, mod_group, w_in_b, lb_logits_fwd, lb_logits_bwd, q_gain, k_gain, ones_bd, rope_tables,
                     seq_len)
        qs, vh, f_f, f_b, sg, qa, ka = map(per_seq, proj[:7])
        vt, span_f, span_b = proj[7], proj[8].reshape(hg_groups, -1, LANES), proj[9].reshape(hg_groups, -1, LANES)
        scans = _hgrn(qs, vh, f_f, f_b, span_f, span_b, init_states)
        x = _attn(qa, ka, vt, ctx_kv_t, per_seq(x), scans[0], scans[1], sg, mod, mod_group, w_out_b, hg_gain,
                  q_gain, *gains).reshape(shape)
        x = _ffn(x, mod, mod_group, w2u, w2d, *gains, mod_base=6, ln_row=2)
        return x, proj[10:], vt, scans[2:]

    ctx_group = lambda g: ctx_row
    y_prompt, (kt_new,), vt_new, states = trunk(
        x_prompt.reshape(1, batch * seq, D_MODEL), ctx_group, None, None, None, batch)
    y_prompt = y_prompt.reshape(batch, seq, D_MODEL)
    to_cache = lambda t: t.reshape(batch, DEPTH, N_KV_HEADS, HEAD_DIM, seq).transpose(0, 1, 4, 2, 3)
    new_cache_k, new_cache_v = to_cache(kt_new), to_cache(vt_new)
    new_state_fwd = states[0].reshape(batch, DEPTH, HG_HEADS, HG_HEAD_DIM, HG_HEAD_DIM)
    new_state_bwd = states[1].reshape(batch, DEPTH, HG_HEADS, HG_HEAD_DIM, HG_HEAD_DIM)

    from_cache = lambda t: t[:, 0].transpose(0, 2, 3, 1).reshape(dec_batch, KV_WIDTH, past)
    init_states = (state_hgrn_fwd[:, 0], state_hgrn_bwd[:, 0])
    y_sample, _, _, _ = trunk(x_sample, lambda g: g, _rope_tables(dec_seq), (from_cache(cache_k), from_cache(cache_v)),
                              init_states, dec_batch)

    return (y_prompt, y_sample, new_cache_k, new_cache_v, new_state_fwd, new_state_bwd)
```

```python
import functools

import jax
import jax.numpy as jnp
from jax import lax
from jax.experimental import pallas as pl
from jax.experimental.pallas import tpu as pltpu

f32 = jnp.float32
bf16 = jnp.bfloat16

D_MODEL = 1024
N_MOD = 9
HG_WIDTH = 512
HG_HEAD_DIM = 128
HG_HEADS = 4
ATT_WIDTH = 512
HEAD_DIM = 64
N_HEADS = 8
N_KV_HEADS = 2
KV_WIDTH = 128
IN_WIDTH = 5 * HG_WIDTH + ATT_WIDTH + 2 * KV_WIDTH
D_FF = 2816
GRID_W = 64
ROPE_THETA = 10000.0
DEPTH = 1
ALPHA = (2.0 * DEPTH) ** 0.25
LOG2_E = 1.4426950408889634
LN_EPS = 1e-6
RMS_EPS = 1e-6

LANES = 128
VMEM_LIMIT_BYTES = 56 * 1024 * 1024

FF_CHUNK = 256
FFN_TILE = 1024
FFN_SUB = 512
TOKEN_TILE = 512
PROJ_SUB = 256
HG_TILE = 128
HG_STEP = 256
Q_TILE = 512
Q_SUB = 256
KEY_CHUNK = 512
SCORE_BOUND_LIMIT = 40.0
SCORE_BOUND_SLACK = 1.01
V_ROWS = 80
HG_FAST_BLOCK = 64
HG_FAST_SPAN_LOG2 = 100.0


def _silu(x):
    return x * jax.nn.sigmoid(x)


def _layer_norm(r, g, b):
    mu = jnp.mean(r, axis=-1, keepdims=True)
    c = r - mu
    var = jnp.mean(c * c, axis=-1, keepdims=True)
    return c * lax.rsqrt(var + LN_EPS) * g + b


def _split_bf16(x, parts):
    out = []
    r = x
    for _ in range(parts - 1):
        p = r.astype(bf16)
        out.append(p)
        r = r - p.astype(f32)
    out.append(r.astype(bf16))
    return out


def _dot01(mat01, x, parts, *, mat_on_left):
    pieces = _split_bf16(x, parts)
    if mat_on_left:
        return jnp.dot(jnp.concatenate([mat01] * parts, axis=1), jnp.concatenate(pieces, axis=0),
                       preferred_element_type=f32)
    return jnp.dot(jnp.concatenate(pieces, axis=1), jnp.concatenate([mat01] * parts, axis=0),
                   preferred_element_type=f32)


def _dot_nt(a, b):
    return lax.dot_general(a, b, (((1,), (1,)), ((), ())), preferred_element_type=f32)


def _dot_tn(a, b):
    return lax.dot_general(a, b, (((0,), (0,)), ((), ())), preferred_element_type=f32)


def _mod_kernel(c_ref, w_ref, b_ref, o_ref):
    a = _silu(c_ref[...]).astype(bf16)
    o_ref[...] = jnp.dot(a, w_ref[...].astype(bf16), preferred_element_type=f32) + b_ref[...]


def _modulation(cvecs, w_mod, b_mod):
    rows = cvecs.shape[0]
    n_out = w_mod.shape[1]
    tn = D_MODEL
    return pl.pallas_call(
        _mod_kernel,
        grid=(n_out // tn,),
        in_specs=[
            pl.BlockSpec((rows, D_MODEL), lambda j: (0, 0)),
            pl.BlockSpec((D_MODEL, tn), lambda j: (0, j)),
            pl.BlockSpec((1, tn), lambda j: (0, j)),
        ],
        out_specs=pl.BlockSpec((rows, tn), lambda j: (0, j)),
        out_shape=jax.ShapeDtypeStruct((rows, n_out), f32),
        compiler_params=pltpu.CompilerParams(dimension_semantics=("arbitrary",)),
        name="modulation",
    )(cvecs, w_mod, b_mod.reshape(1, n_out))


def _ffn_kernel(x_ref, m_ref, wup_ref, wd_ref, g_ref, b_ref, o_ref, act_ref, *, mod_base, ln_row):
    shift = m_ref[0, mod_base:mod_base + 1, :]
    scale = m_ref[0, mod_base + 1:mod_base + 2, :]
    gate = m_ref[0, mod_base + 2:mod_base + 3, :]
    subs = [slice(lo, lo + FFN_SUB) for lo in range(0, x_ref.shape[1], FFN_SUB)]
    xs = [x_ref[0, rows] for rows in subs]
    hs = [(x * (1.0 + scale) + shift).astype(bf16) for x in xs]
    for j in range(D_FF // FF_CHUNK):
        cols = slice(j * FF_CHUNK, (j + 1) * FF_CHUNK)
        for rows, h in zip(subs, hs):
            a = jnp.dot(h, wup_ref[:, cols], preferred_element_type=f32)
            u = jnp.dot(h, wup_ref[:, D_FF + j * FF_CHUNK:D_FF + (j + 1) * FF_CHUNK],
                        preferred_element_type=f32)
            act_ref[rows, cols] = (_silu(a) * u).astype(bf16)
    ys = [jnp.dot(act_ref[rows, :], wd_ref[...], preferred_element_type=f32) for rows in subs]
    for rows, x, y in zip(subs, xs, ys):
        r = ALPHA * x + 0.5 * gate * y
        o_ref[0, rows] = _layer_norm(r, g_ref[ln_row:ln_row + 1, :], b_ref[ln_row:ln_row + 1, :])


def _resident(shape):
    return pl.BlockSpec(shape, lambda *_: (0,) * len(shape), pipeline_mode=pl.Buffered(1))


def _ffn(x, mod, mod_group, w_up, wd, ln_g, ln_b, *, mod_base, ln_row):
    groups, length, _ = x.shape
    tm = FFN_TILE
    return pl.pallas_call(
        functools.partial(_ffn_kernel, mod_base=mod_base, ln_row=ln_row),
        grid=(groups, length // tm),
        in_specs=[
            pl.BlockSpec((1, tm, D_MODEL), lambda g, t: (g, t, 0)),
            pl.BlockSpec((1, N_MOD, D_MODEL), lambda g, t: (mod_group(g), 0, 0)),
            _resident((D_MODEL, 2 * D_FF)),
            _resident((D_FF, D_MODEL)),
            _resident((3, D_MODEL)),
            _resident((3, D_MODEL)),
        ],
        out_specs=pl.BlockSpec((1, tm, D_MODEL), lambda g, t: (g, t, 0)),
        out_shape=jax.ShapeDtypeStruct(x.shape, f32),
        scratch_shapes=[pltpu.VMEM((tm, D_FF), bf16)],
        compiler_params=pltpu.CompilerParams(
            dimension_semantics=("arbitrary", "arbitrary"), vmem_limit_bytes=VMEM_LIMIT_BYTES),
        name="ffn",
    )(x, mod, w_up, wd, ln_g, ln_b)


def _head_rms_norm(x, ones_bd, gain):
    ss = _dot01(ones_bd, x * x, 2, mat_on_left=False)
    return x * lax.rsqrt(ss * (1.0 / HEAD_DIM) + RMS_EPS) * gain


def _rope(x, cos, sin_signed):
    width = x.shape[-1]
    lane = lax.broadcasted_iota(jnp.int32, x.shape, 1)
    from_right = pltpu.roll(x, width - 16, 1)
    from_left = pltpu.roll(x, 16, 1)
    partner = jnp.where((lane & 31) < 16, from_right, from_left)
    return x * cos + partner * sin_signed


def _proj_kernel(*refs, rope):
    if rope:
        (x_ref, m_ref, w_ref, lbf_ref, lbb_ref, qg_ref, kg_ref, bd_ref, cos_ref, sin_ref,
         qs_ref, vh_ref, ff_ref, fb_ref, sg_ref, qa_ref, ka_ref, vt_ref,
         spf_ref, spb_ref) = refs
    else:
        (x_ref, m_ref, w_ref, lbf_ref, lbb_ref, qg_ref, kg_ref, bd_ref,
         qs_ref, vh_ref, ff_ref, fb_ref, sg_ref, qa_ref, ka_ref, vt_ref,
         spf_ref, spb_ref, kt_ref) = refs
    shift = m_ref[0, 3:4, :]
    scale = m_ref[0, 4:5, :]
    subs = [slice(lo, lo + PROJ_SUB) for lo in range(0, x_ref.shape[1], PROJ_SUB)]
    hs = [(x_ref[0, rows] * (1.0 + scale) + shift).astype(bf16) for rows in subs]

    def cols(h, lo, width):
        return jnp.dot(h, w_ref[:, lo:lo + width], preferred_element_type=f32)

    def lower_bound(lb_ref):
        l0 = lb_ref[0:1, :]
        l1 = lb_ref[1:2, :]
        m = jnp.maximum(l0, l1)
        e0 = jnp.exp(l0 - m)
        e1 = jnp.exp(l1 - m)
        return e0 / (e0 + e1)

    half = HG_FAST_BLOCK // 2

    def forget_gate(lb_ref, lo, f_ref, span_ref):
        lb = lower_bound(lb_ref)
        for rows, h in zip(subs, hs):
            f = lb + (1.0 - lb) * jax.nn.sigmoid(cols(h, lo, HG_WIDTH))
            f_ref[0, rows] = f
            sums = jnp.sum(jnp.log2(f).reshape(PROJ_SUB // half, half, HG_WIDTH), axis=1)
            span_ref[0, rows.start // half:rows.stop // half] = jnp.broadcast_to(
                jnp.max(jnp.abs(sums), axis=-1, keepdims=True), (PROJ_SUB // half, LANES))

    for rows, h in zip(subs, hs):
        qs_ref[0, rows] = _silu(cols(h, 0, HG_WIDTH))
    for rows, h in zip(subs, hs):
        vh_ref[0, rows] = cols(h, HG_WIDTH, HG_WIDTH).astype(bf16)
    forget_gate(lbf_ref, 2 * HG_WIDTH, ff_ref, spf_ref)
    forget_gate(lbb_ref, 3 * HG_WIDTH, fb_ref, spb_ref)
    for rows, h in zip(subs, hs):
        sg_ref[0, rows] = _silu(cols(h, 4 * HG_WIDTH, HG_WIDTH))

    base = 5 * HG_WIDTH
    piece = vt_ref.shape[2]
    for rows, h in zip(subs, hs):
        q = _head_rms_norm(cols(h, base, ATT_WIDTH), bd_ref[...], qg_ref[...])
        k = _head_rms_norm(cols(h, base + ATT_WIDTH, KV_WIDTH), bd_ref[0:KV_WIDTH, 0:KV_WIDTH], kg_ref[...])
        if rope:
            cos, sin = cos_ref[rows, :], sin_ref[rows, :]
            q = _rope(q, jnp.concatenate([cos] * (ATT_WIDTH // LANES), axis=1),
                      jnp.concatenate([sin] * (ATT_WIDTH // LANES), axis=1))
            k = _rope(k, cos, sin)
        qa_ref[0, rows] = (q * (HEAD_DIM ** -0.5 * LOG2_E)).astype(bf16)
        ka_ref[0, rows] = k
        v = cols(h, base + ATT_WIDTH + KV_WIDTH, KV_WIDTH)
        for lo in range(rows.start, rows.stop, min(piece, PROJ_SUB)):
            n = min(piece, PROJ_SUB)
            dst = (lo // piece, slice(None), slice(lo % piece, lo % piece + n))
            vt_ref[dst] = v[lo - rows.start:lo - rows.start + n].T
            if not rope:
                kt_ref[dst] = k[lo - rows.start:lo - rows.start + n].T


def _proj(x, mod, mod_group, w_in, lb_f, lb_b, q_gain, k_gain, ones_bd, rope_tables, seq_len):
    groups, length, _ = x.shape
    tm = TOKEN_TILE
    rope = rope_tables is not None
    n_seq = groups * length // seq_len
    if seq_len >= tm:
        tiles_per_seq = seq_len // tm
        t_spec = pl.BlockSpec((1, KV_WIDTH, tm), lambda g, t: (g * (length // seq_len) + t // tiles_per_seq, 0,
                                                               t % tiles_per_seq))
    else:
        t_spec = pl.BlockSpec((tm // seq_len, KV_WIDTH, seq_len), lambda g, t: (g * (length // tm) + t, 0, 0))
    t_shape = jax.ShapeDtypeStruct((n_seq, KV_WIDTH, seq_len), f32)
    tok = lambda width: pl.BlockSpec((1, tm, width), lambda g, t: (g, t, 0))
    in_specs = [
        tok(D_MODEL),
        pl.BlockSpec((1, N_MOD, D_MODEL), lambda g, t: (mod_group(g), 0, 0)),
        _resident((D_MODEL, IN_WIDTH)),
        _resident((2, HG_WIDTH)),
        _resident((2, HG_WIDTH)),
        _resident((1, ATT_WIDTH)),
        _resident((1, KV_WIDTH)),
        _resident((ATT_WIDTH, ATT_WIDTH)),
    ]
    args = [x, mod, w_in, lb_f, lb_b, q_gain, k_gain, ones_bd]
    if rope:
        in_specs += [pl.BlockSpec((tm, LANES), lambda g, t: (t, 0))] * 2
        args += list(rope_tables)
    shape = lambda width, dt: jax.ShapeDtypeStruct((groups, length, width), dt)
    half = HG_FAST_BLOCK // 2
    span_spec = pl.BlockSpec((1, tm // half, LANES), lambda g, t: (g, t, 0))
    span_shape = jax.ShapeDtypeStruct((groups, length // half, LANES), f32)
    return pl.pallas_call(
        functools.partial(_proj_kernel, rope=rope),
        grid=(groups, length // tm),
        in_specs=in_specs,
        out_specs=([tok(HG_WIDTH)] * 5 + [tok(ATT_WIDTH), tok(KV_WIDTH), t_spec, span_spec, span_spec]
                   + ([] if rope else [t_spec])),
        out_shape=([shape(HG_WIDTH, f32), shape(HG_WIDTH, bf16)] + [shape(HG_WIDTH, f32)] * 3
                   + [shape(ATT_WIDTH, bf16), shape(KV_WIDTH, f32), t_shape, span_shape, span_shape]
                   + ([] if rope else [t_shape])),
        compiler_params=pltpu.CompilerParams(
            dimension_semantics=("arbitrary", "arbitrary"), vmem_limit_bytes=VMEM_LIMIT_BYTES),
        name="mixer_proj",
    )(*args)


def _hgrn_direction(qs, vb, f, st_ref, o_ref, rows, *, rev):
    tc = qs.shape[0]
    lf = jnp.log2(f)
    kk = 1.0 - f
    row = lax.broadcasted_iota(jnp.int32, (tc, tc), 0)
    col = lax.broadcasted_iota(jnp.int32, (tc, tc), 1)
    ordered = (row < col) if rev else (row > col)
    differ = row ^ col
    tri = jnp.where((col >= row) if rev else (col <= row), 1.0, 0.0).astype(bf16)
    cum = _dot01(tri, lf, 2, mat_on_left=True)
    total = cum[0:1] if rev else cum[tc - 1:tc]
    carry = jnp.exp2(total)
    query_half = 0 if rev else 1
    heads = [slice(hd * HG_HEAD_DIM, (hd + 1) * HG_HEAD_DIM) for hd in range(HG_HEADS)]

    def level_exponent(s, ridx):
        if s == 1:
            return jnp.where((ridx & 1) == query_half, lf, 0.0)
        if s == 2:
            nxt = pltpu.roll(lf, tc - 1, 0)
            prv = pltpu.roll(lf, 1, 0)
            m4 = ridx & 3
            if rev:
                return jnp.where(m4 == 0, lf + nxt, jnp.where(m4 == 1, lf, jnp.where(m4 == 2, 0.0, prv)))
            return jnp.where(m4 == 0, nxt, jnp.where(m4 == 1, 0.0, jnp.where(m4 == 2, lf, lf + prv)))
        blocks = []
        for lo in range(0, tc, 2 * s):
            anchor = lo + (s if rev else s - 1)
            blocks.append(cum[lo:lo + 2 * s] - cum[anchor:anchor + 1])
        d = jnp.concatenate(blocks, axis=0)
        is_query = ((ridx >> (s.bit_length() - 1)) & 1) == query_half
        return jnp.where(is_query, d, -d)

    def add_levels(intra, first):
        ridx = lax.broadcasted_iota(jnp.int32, qs.shape, 0)
        s = first
        while s < tc:
            shift = s.bit_length() - 1
            is_query = ((ridx >> shift) & 1) == query_half
            mixed = (jnp.where(is_query, qs, kk) * jnp.exp2(level_exponent(s, ridx))).astype(bf16)
            pair = jnp.logical_and((differ >> shift) == 1, ordered)
            for hd, hs in enumerate(heads):
                intra[hd] = jnp.where(pair, _dot_nt(mixed[:, hs], mixed[:, hs]), intra[hd])
            s *= 2
        return intra

    def finish(intra, q_dec, k_end, extra):
        for hd, hs in enumerate(heads):
            st = st_ref[hd]
            o = jnp.dot(jnp.concatenate([intra[hd].astype(bf16), q_dec[:, hs]], axis=1),
                        jnp.concatenate([vb[:, hs], st.T.astype(bf16)], axis=0), preferred_element_type=f32)
            o_ref[0, rows, hs] = o if extra is None else o + extra[:, hs]
            st_ref[hd] = st * carry[:, hs] + _dot_tn(vb[:, hs], k_end[:, hs])

    blk = HG_FAST_BLOCK
    anchors = [lo + (blk // 2 if rev else blk // 2 - 1) for lo in range(0, tc, blk)]

    def mid_split_blocks():
        q_mid, k_mid, q_dec, k_end = [], [], [], []
        for lo, a in zip(range(0, tc, blk), anchors):
            rel = cum[lo:lo + blk] - cum[a:a + 1]
            q_mid.append(qs[lo:lo + blk] * jnp.exp2(rel))
            k_mid.append(kk[lo:lo + blk] * jnp.exp2(-rel))
            q_dec.append(q_mid[-1] * jnp.exp2(cum[a:a + 1]))
            k_end.append(k_mid[-1] * jnp.exp2(total - cum[a:a + 1]))
        q_mid, k_mid, q_dec, k_end = (jnp.concatenate(p, axis=0).astype(bf16) for p in (q_mid, k_mid, q_dec, k_end))
        same_block = (differ >> (blk.bit_length() - 1)) == 0
        keep = jnp.logical_and(same_block, (row <= col) if rev else (row >= col))
        intra = [jnp.where(keep, _dot_nt(q_mid[:, hs], k_mid[:, hs]), 0.0) for hs in heads]
        finish(add_levels(intra, blk), q_dec, k_end, None)

    def all_levels():
        intra = add_levels([jnp.zeros((tc, tc), f32)] * HG_HEADS, 1)
        q_dec = (qs * jnp.exp2(cum)).astype(bf16)
        k_end = (kk * jnp.exp2(total - cum)).astype(bf16)
        qk = qs * kk
        own = jnp.concatenate(
            [jnp.sum(qk[:, hs], axis=-1, keepdims=True) * vb[:, hs].astype(f32) for hs in heads], axis=-1)
        finish(intra, q_dec, k_end, own)

    return mid_split_blocks, all_levels


def _hgrn_kernel(slow_ref, *refs, has_init):
    if has_init:
        (qsf_ref, vf_ref, ff_ref, qsb_ref, vb_ref, fb_ref, s0f_ref, s0b_ref,
         of_ref, ob_ref, stf_ref, stb_ref) = refs
    else:
        (qsf_ref, vf_ref, ff_ref, qsb_ref, vb_ref, fb_ref,
         of_ref, ob_ref, sf_out_ref, sb_out_ref, stf_ref, stb_ref) = refs
    t = pl.program_id(1)

    @pl.when(t == 0)
    def _():
        for hd in range(HG_HEADS):
            if has_init:
                stf_ref[hd] = s0f_ref[0, hd].T
                stb_ref[hd] = s0b_ref[0, hd].T
            else:
                stf_ref[hd] = jnp.zeros((HG_HEAD_DIM, HG_HEAD_DIM), f32)
                stb_ref[hd] = jnp.zeros((HG_HEAD_DIM, HG_HEAD_DIM), f32)

    chunks = [slice(lo, lo + HG_TILE) for lo in range(0, qsf_ref.shape[1], HG_TILE)]
    scans = []
    for r_f, r_b in zip(chunks, reversed(chunks)):
        scans.append(_hgrn_direction(qsf_ref[0, r_f], vf_ref[0, r_f], ff_ref[0, r_f], stf_ref, of_ref, r_f, rev=False))
        scans.append(_hgrn_direction(qsb_ref[0, r_b], vb_ref[0, r_b], fb_ref[0, r_b], stb_ref, ob_ref, r_b, rev=True))
    slow = slow_ref[pl.program_id(0) * pl.num_programs(1) + t] != 0

    @pl.when(jnp.logical_not(slow))
    def _():
        for mid_split_blocks, _ in scans:
            mid_split_blocks()

    @pl.when(slow)
    def _():
        for _, all_levels in scans:
            all_levels()

    if not has_init:
        @pl.when(t == pl.num_programs(1) - 1)
        def _():
            for hd in range(HG_HEADS):
                sf_out_ref[0, hd] = stf_ref[hd].T
                sb_out_ref[0, hd] = stb_ref[hd].T


def _hgrn(qs, vh, f_fwd, f_bwd, span_fwd, span_bwd, init_states):
    groups, length, _ = qs.shape
    nt = length // HG_STEP
    has_init = init_states is not None
    per_step = lambda span: jnp.max(span[:, :, 0].reshape(groups, nt, -1), axis=-1) > HG_FAST_SPAN_LOG2
    slow = jnp.logical_or(per_step(span_fwd), per_step(span_bwd)[:, ::-1]).astype(jnp.int32).reshape(-1)
    fwd = pl.BlockSpec((1, HG_STEP, HG_WIDTH), lambda g, t, _: (g, t, 0))
    bwd = pl.BlockSpec((1, HG_STEP, HG_WIDTH), lambda g, t, _: (g, nt - 1 - t, 0))
    state = pl.BlockSpec((1, HG_HEADS, HG_HEAD_DIM, HG_HEAD_DIM), lambda g, t, _: (g, 0, 0, 0))
    in_specs = [fwd, fwd, fwd, bwd, bwd, bwd]
    args = [qs, vh, f_fwd, qs, vh, f_bwd]
    out_specs = [fwd, bwd]
    out_shape = [jax.ShapeDtypeStruct(qs.shape, f32)] * 2
    if has_init:
        in_specs += [state, state]
        args += list(init_states)
    else:
        out_specs += [state, state]
        out_shape += [jax.ShapeDtypeStruct((groups, HG_HEADS, HG_HEAD_DIM, HG_HEAD_DIM), f32)] * 2
    return pl.pallas_call(
        functools.partial(_hgrn_kernel, has_init=has_init),
        grid_spec=pltpu.PrefetchScalarGridSpec(
            num_scalar_prefetch=1,
            grid=(groups, nt),
            in_specs=in_specs,
            out_specs=out_specs,
            scratch_shapes=[pltpu.VMEM((HG_HEADS, HG_HEAD_DIM, HG_HEAD_DIM), f32)] * 2),
        out_shape=out_shape,
        compiler_params=pltpu.CompilerParams(
            dimension_semantics=("arbitrary", "arbitrary"), vmem_limit_bytes=VMEM_LIMIT_BYTES),
        name="hgrn2",
    )(slow, *args)


def _attn_kernel(*refs, n_ctx):
    if n_ctx:
        (qa_ref, ka_ref, vt_ref, ckt_ref, cvt_ref, x_ref, of_ref, ob_ref, sg_ref, m_ref, wo_ref, hg_ref, qg_ref,
         g_ref, b_ref, o_ref, kk_ref, vta_ref, bound_ref, fixed_ref) = refs
    else:
        (qa_ref, ka_ref, vt_ref, x_ref, of_ref, ob_ref, sg_ref, m_ref, wo_ref, hg_ref, qg_ref,
         g_ref, b_ref, o_ref, kk_ref, vta_ref, bound_ref, fixed_ref) = refs
    tq = qa_ref.shape[1]
    n_keys = kk_ref.shape[0]

    @pl.when(pl.program_id(1) == 0)
    def _():
        def fill_keys(lo, k):
            n = k.shape[0]
            low = lax.broadcasted_iota(jnp.int32, k.shape, 1) < HEAD_DIM
            k_sw = pltpu.roll(k, HEAD_DIM, 1)
            kk_ref[lo:lo + n, 0:LANES] = jnp.where(low, k, k_sw).astype(bf16)
            kk_ref[lo:lo + n, LANES:2 * LANES] = jnp.where(low, k_sw, k).astype(bf16)

        def fill_values(lo, vt):
            n = vt.shape[1]
            ones_row = jnp.where(lax.broadcasted_iota(jnp.int32, (V_ROWS - HEAD_DIM, n), 0) == 0, 1.0, 0.0)
            for kv in range(N_KV_HEADS):
                vta_ref[kv * V_ROWS:kv * V_ROWS + HEAD_DIM, lo:lo + n] = (
                    vt[kv * HEAD_DIM:(kv + 1) * HEAD_DIM].astype(bf16))
                vta_ref[kv * V_ROWS + HEAD_DIM:(kv + 1) * V_ROWS, lo:lo + n] = ones_row.astype(bf16)

        if n_ctx:
            fill_keys(0, ckt_ref[0].T)
            fill_values(0, cvt_ref[0])
        fill_keys(n_ctx, ka_ref[0])
        fill_values(n_ctx, vt_ref[0])

        gain_max = jnp.max(jnp.abs(qg_ref[...]), axis=1, keepdims=True)
        all_small = None
        for kv in range(N_KV_HEADS):
            kt = kk_ref[:, kv * LANES:(kv + 1) * LANES].astype(f32)
            k_norm2 = jnp.max(0.5 * jnp.sum(kt * kt, axis=1, keepdims=True), axis=0, keepdims=True)
            bound = (LOG2_E * SCORE_BOUND_SLACK) * gain_max * jnp.sqrt(k_norm2)
            bound_ref[kv] = bound[0, 0]
            small = jnp.where(bound <= SCORE_BOUND_LIMIT, 1, 0)
            all_small = small if all_small is None else all_small * small
        fixed_ref[0] = all_small[0, 0]

    ts = min(Q_SUB, tq)
    low = lax.broadcasted_iota(jnp.int32, (ts, LANES), 1) < HEAD_DIM
    kc = min(KEY_CHUNK, n_keys)
    pairs_per_kv = N_HEADS // N_KV_HEADS // 2

    def attend(rows, fixed_shift):
        def masked_pair(tile):
            qp = qa_ref[0, rows, tile * LANES:(tile + 1) * LANES]
            zero = jnp.zeros_like(qp)
            return jnp.concatenate([jnp.where(low, qp, zero), jnp.where(low, zero, qp)], axis=0)

        q_pairs = [masked_pair(tile) for tile in range(N_HEADS // 2)]
        work = [(tile, lo) for tile in range(N_HEADS // 2) for lo in range(0, n_keys, kc)]

        def scores(tile, lo):
            kv = tile // pairs_per_kv
            return _dot_nt(kk_ref[lo:lo + kc, kv * LANES:(kv + 1) * LANES], q_pairs[tile])

        heads_t = []
        st_next = scores(*work[0])
        m = acc = None
        for i, (tile, lo) in enumerate(work):
            st = st_next
            if i + 1 < len(work):
                st_next = scores(*work[i + 1])
            kv = tile // pairs_per_kv
            values_t = vta_ref[kv * V_ROWS:(kv + 1) * V_ROWS, lo:lo + kc]
            if fixed_shift:
                e = jnp.exp2(st - bound_ref[kv]).astype(bf16)
                pv = jnp.dot(values_t, e, preferred_element_type=f32)
                acc = pv if acc is None else acc + pv
            else:
                m_chunk = jnp.max(st, axis=0, keepdims=True)
                m_new = m_chunk if m is None else jnp.maximum(m, m_chunk)
                e = jnp.exp2(st - m_new).astype(bf16)
                pv = jnp.dot(values_t, e, preferred_element_type=f32)
                acc = pv if acc is None else acc * jnp.exp2(m - m_new) + pv
                m = m_new
            if lo + kc == n_keys:
                on = (acc[0:HEAD_DIM] * (1.0 / acc[HEAD_DIM:HEAD_DIM + 1])).astype(bf16)
                heads_t += [on[:, :ts], on[:, ts:]]
                m = acc = None
        return jnp.concatenate(heads_t, axis=0)

    def project(rows, o_att_t):
        o_sum = of_ref[0, rows] + ob_ref[0, rows]
        normed = []
        for hd in range(HG_HEADS):
            oh = o_sum[:, hd * HG_HEAD_DIM:(hd + 1) * HG_HEAD_DIM]
            ms = jnp.mean(oh * oh, axis=-1, keepdims=True)
            normed.append(oh * lax.rsqrt(ms + RMS_EPS))
        o_hg = (jnp.concatenate(normed, axis=-1) * hg_ref[...] * sg_ref[0, rows]).astype(bf16)
        y = (jnp.dot(o_hg, wo_ref[0:HG_WIDTH, :], preferred_element_type=f32)
             + _dot_tn(o_att_t, wo_ref[HG_WIDTH:HG_WIDTH + ATT_WIDTH, :]))
        r = ALPHA * x_ref[0, rows] + m_ref[0, 5:6, :] * y
        o_ref[0, rows] = _layer_norm(r, g_ref[1:2, :], b_ref[1:2, :])

    def run(fixed_shift):
        for lo in range(0, tq, ts):
            rows = slice(lo, lo + ts)
            project(rows, attend(rows, fixed_shift))

    use_bound = fixed_ref[0] != 0
    pl.when(use_bound)(functools.partial(run, True))
    pl.when(jnp.logical_not(use_bound))(functools.partial(run, False))


def _attn(qa, ka, vt, ctx_kv_t, x, o_f, o_b, sg, mod, mod_group, w_out, hg_gain, q_gain, ln_g, ln_b):
    groups, length, _ = qa.shape
    tq = min(Q_TILE, length)
    n_ctx = 0 if ctx_kv_t is None else ctx_kv_t[0].shape[2]
    n_keys = n_ctx + length
    tok = lambda width: pl.BlockSpec((1, tq, width), lambda g, t: (g, t, 0))
    whole_t = lambda n: pl.BlockSpec((1, KV_WIDTH, n), lambda g, t: (g, 0, 0))
    in_specs = [tok(ATT_WIDTH), pl.BlockSpec((1, length, KV_WIDTH), lambda g, t: (g, 0, 0)), whole_t(length)]
    args = [qa, ka, vt]
    if n_ctx:
        in_specs += [whole_t(n_ctx), whole_t(n_ctx)]
        args += list(ctx_kv_t)
    in_specs += [
        tok(D_MODEL), tok(HG_WIDTH), tok(HG_WIDTH), tok(HG_WIDTH),
        pl.BlockSpec((1, N_MOD, D_MODEL), lambda g, t: (mod_group(g), 0, 0)),
        _resident((D_MODEL, D_MODEL)),
        _resident((1, HG_WIDTH)),
        _resident((1, ATT_WIDTH)),
        _resident((3, D_MODEL)),
        _resident((3, D_MODEL)),
    ]
    args += [x, o_f, o_b, sg, mod, w_out, hg_gain, q_gain, ln_g, ln_b]
    return pl.pallas_call(
        functools.partial(_attn_kernel, n_ctx=n_ctx),
        grid=(groups, length // tq),
        in_specs=in_specs,
        out_specs=tok(D_MODEL),
        out_shape=jax.ShapeDtypeStruct(x.shape, f32),
        scratch_shapes=[pltpu.VMEM((n_keys, N_KV_HEADS * LANES), bf16),
                        pltpu.VMEM((N_KV_HEADS * V_ROWS, n_keys), bf16),
                        pltpu.SMEM((N_KV_HEADS,), f32),
                        pltpu.SMEM((1,), jnp.int32)],
        compiler_params=pltpu.CompilerParams(
            dimension_semantics=("arbitrary", "arbitrary"), vmem_limit_bytes=VMEM_LIMIT_BYTES),
        name="attn_out",
    )(*args)


def _rope_tables(n_tokens):
    half = HEAD_DIM // 2
    t = jnp.arange(n_tokens)
    inv = ROPE_THETA ** (-jnp.arange(0, half, 2, dtype=f32) / half)
    ang_row = (t // GRID_W).astype(f32)[:, None] * inv
    ang_col = (t % GRID_W).astype(f32)[:, None] * inv
    cos = jnp.concatenate([jnp.cos(ang_row)] * 2 + [jnp.cos(ang_col)] * 2, axis=-1)
    sin = jnp.concatenate([-jnp.sin(ang_row), jnp.sin(ang_row), -jnp.sin(ang_col), jnp.sin(ang_col)], axis=-1)
    return jnp.tile(cos, (1, LANES // HEAD_DIM)), jnp.tile(sin, (1, LANES // HEAD_DIM))


def kernel(x_prompt, x_sample, cache_k, cache_v, state_hgrn_fwd, state_hgrn_bwd, c, c_ctx, w_mod, b_mod,
           w_ffn1_in, w_ffn1_out, w_ffn2_in, w_ffn2_out, w_in, w_out, q_norm_g, k_norm_g, hg_norm_g,
           lb_logits_fwd, lb_logits_bwd, ln_g, ln_b):
    assert w_mod.shape[0] == DEPTH and lb_logits_fwd.shape[0] == DEPTH + 1
    batch, seq, _ = x_prompt.shape
    dec_batch, dec_seq, _ = x_sample.shape
    past = cache_k.shape[2]

    ctx_row = dec_batch
    rows = 16
    cvecs = jnp.concatenate([c, c_ctx[None, :], jnp.zeros((rows - dec_batch - 1, D_MODEL), f32)], axis=0)
    mod = _modulation(cvecs, w_mod[0], b_mod[0]).reshape(rows, N_MOD, D_MODEL)

    to_bf16 = lambda w: w.astype(bf16)
    w1u, w2u = to_bf16(w_ffn1_in[0]), to_bf16(w_ffn2_in[0])
    w1d, w2d = to_bf16(w_ffn1_out[0]), to_bf16(w_ffn2_out[0])
    w_in_b, w_out_b = to_bf16(w_in[0]), to_bf16(w_out[0])
    gains = ln_g[0], ln_b[0]
    q_gain = jnp.tile(q_norm_g[0], N_HEADS).reshape(1, ATT_WIDTH)
    k_gain = jnp.tile(k_norm_g[0], N_KV_HEADS).reshape(1, KV_WIDTH)
    hg_gain = hg_norm_g[0].reshape(1, HG_WIDTH)
    head_of = jnp.arange(ATT_WIDTH) // HEAD_DIM
    ones_bd = (head_of[:, None] == head_of[None, :]).astype(bf16)

    def trunk(x, mod_group, rope_tables, ctx_kv_t, init_states, hg_groups):
        shape = x.shape
        seq_len = shape[0] * shape[1] // hg_groups
        per_seq = lambda a: a.reshape(hg_groups, seq_len, a.shape[-1])
        x = _ffn(x, mod, mod_group, w1u, w1d, *gains, mod_base=0, ln_row=0)
        proj = _proj(x, mod, mod_group, w_in_b, lb_logits_fwd, lb_logits_bwd, q_gain, k_gain, ones_bd, rope_tables,
                     seq_len)
        qs, vh, f_f, f_b, sg, qa, ka = map(per_seq, proj[:7])
        vt, span_f, span_b = proj[7], proj[8].reshape(hg_groups, -1, LANES), proj[9].reshape(hg_groups, -1, LANES)
        scans = _hgrn(qs, vh, f_f, f_b, span_f, span_b, init_states)
        x = _attn(qa, ka, vt, ctx_kv_t, per_seq(x), scans[0], scans[1], sg, mod, mod_group, w_out_b, hg_gain,
                  q_gain, *gains).reshape(shape)
        x = _ffn(x, mod, mod_group, w2u, w2d, *gains, mod_base=6, ln_row=2)
        return x, proj[10:], vt, scans[2:]

    ctx_group = lambda g: ctx_row
    y_prompt, (kt_new,), vt_new, states = trunk(
        x_prompt.reshape(1, batch * seq, D_MODEL), ctx_group, None, None, None, batch)
    y_prompt = y_prompt.reshape(batch, seq, D_MODEL)
    to_cache = lambda t: t.reshape(batch, DEPTH, N_KV_HEADS, HEAD_DIM, seq).transpose(0, 1, 4, 2, 3)
    new_cache_k, new_cache_v = to_cache(kt_new), to_cache(vt_new)
    new_state_fwd = states[0].reshape(batch, DEPTH, HG_HEADS, HG_HEAD_DIM, HG_HEAD_DIM)
    new_state_bwd = states[1].reshape(batch, DEPTH, HG_HEADS, HG_HEAD_DIM, HG_HEAD_DIM)

    from_cache = lambda t: t[:, 0].transpose(0, 2, 3, 1).reshape(dec_batch, KV_WIDTH, past)
    init_states = (state_hgrn_fwd[:, 0], state_hgrn_bwd[:, 0])
    y_sample, _, _, _ = trunk(x_sample, lambda g: g, _rope_tables(dec_seq), (from_cache(cache_k), from_cache(cache_v)),
                              init_states, dec_batch)

    return (y_prompt, y_sample, new_cache_k, new_cache_v, new_state_fwd, new_state_bwd)
```

```python
import functools

import jax
import jax.numpy as jnp
from jax import lax
from jax.experimental import pallas as pl
from jax.experimental.pallas import tpu as pltpu

f32 = jnp.float32
bf16 = jnp.bfloat16

D_MODEL = 1024
N_MOD = 9
HG_WIDTH = 512
HG_HEAD_DIM = 128
HG_HEADS = 4
ATT_WIDTH = 512
HEAD_DIM = 64
N_HEADS = 8
N_KV_HEADS = 2
KV_WIDTH = 128
IN_WIDTH = 5 * HG_WIDTH + ATT_WIDTH + 2 * KV_WIDTH
D_FF = 2816
GRID_W = 64
ROPE_THETA = 10000.0
DEPTH = 1
ALPHA = (2.0 * DEPTH) ** 0.25
LOG2_E = 1.4426950408889634
LN_EPS = 1e-6
RMS_EPS = 1e-6

LANES = 128
VMEM_LIMIT_BYTES = 56 * 1024 * 1024

FF_CHUNK = 256
FFN_TILE = 1024
FFN_SUB = 512
TOKEN_TILE = 512
PROJ_SUB = 256
HG_TILE = 128
HG_STEP = 256
Q_TILE = 512
Q_SUB = 256
KEY_CHUNK = 512
SCORE_BOUND_LIMIT = 40.0
SCORE_BOUND_SLACK = 1.01
V_ROWS = 80
HG_FAST_BLOCK = 64
HG_FAST_SPAN_LOG2 = 100.0


def _silu(x):
    return x * jax.nn.sigmoid(x)


def _layer_norm(r, g, b):
    mu = jnp.mean(r, axis=-1, keepdims=True)
    c = r - mu
    var = jnp.mean(c * c, axis=-1, keepdims=True)
    return c * lax.rsqrt(var + LN_EPS) * g + b


def _split_bf16(x, parts):
    out = []
    r = x
    for _ in range(parts - 1):
        p = r.astype(bf16)
        out.append(p)
        r = r - p.astype(f32)
    out.append(r.astype(bf16))
    return out


def _dot01(mat01, x, parts, *, mat_on_left):
    pieces = _split_bf16(x, parts)
    if mat_on_left:
        return jnp.dot(jnp.concatenate([mat01] * parts, axis=1), jnp.concatenate(pieces, axis=0),
                       preferred_element_type=f32)
    return jnp.dot(jnp.concatenate(pieces, axis=1), jnp.concatenate([mat01] * parts, axis=0),
                   preferred_element_type=f32)


def _dot_nt(a, b):
    return lax.dot_general(a, b, (((1,), (1,)), ((), ())), preferred_element_type=f32)


def _dot_tn(a, b):
    return lax.dot_general(a, b, (((0,), (0,)), ((), ())), preferred_element_type=f32)


def _mod_kernel(c_ref, w_ref, b_ref, o_ref):
    a = _silu(c_ref[...]).astype(bf16)
    o_ref[...] = jnp.dot(a, w_ref[...].astype(bf16), preferred_element_type=f32) + b_ref[...]


def _modulation(cvecs, w_mod, b_mod):
    rows = cvecs.shape[0]
    n_out = w_mod.shape[1]
    tn = D_MODEL
    return pl.pallas_call(
        _mod_kernel,
        grid=(n_out // tn,),
        in_specs=[
            pl.BlockSpec((rows, D_MODEL), lambda j: (0, 0)),
            pl.BlockSpec((D_MODEL, tn), lambda j: (0, j)),
            pl.BlockSpec((1, tn), lambda j: (0, j)),
        ],
        out_specs=pl.BlockSpec((rows, tn), lambda j: (0, j)),
        out_shape=jax.ShapeDtypeStruct((rows, n_out), f32),
        compiler_params=pltpu.CompilerParams(dimension_semantics=("arbitrary",)),
        name="modulation",
    )(cvecs, w_mod, b_mod.reshape(1, n_out))


def _ffn_kernel(x_ref, m_ref, wup_hbm, wd_hbm, g_ref, b_ref, o_ref, act_ref, wup_ref, wd_ref, sem,
                *, mod_base, ln_row):
    n_chunks = wup_hbm.shape[0]

    def up_copy(j):
        return pltpu.make_async_copy(wup_hbm.at[j], wup_ref.at[j], sem.at[j])

    def down_copy(j):
        rows = pl.ds(j * FF_CHUNK, FF_CHUNK)
        return pltpu.make_async_copy(wd_hbm.at[rows, :], wd_ref.at[rows, :], sem.at[n_chunks + j])

    def sub_layer(first_step):
        if first_step:
            for j in range(n_chunks):
                up_copy(j).start()
            for j in range(n_chunks):
                down_copy(j).start()
        shift = m_ref[0, mod_base:mod_base + 1, :]
        scale = m_ref[0, mod_base + 1:mod_base + 2, :]
        gate = m_ref[0, mod_base + 2:mod_base + 3, :]
        subs = [slice(lo, lo + FFN_SUB) for lo in range(0, x_ref.shape[1], FFN_SUB)]
        xs = [x_ref[0, rows] for rows in subs]
        hs = [(x * (1.0 + scale) + shift).astype(bf16) for x in xs]
        for j in range(n_chunks):
            if first_step:
                up_copy(j).wait()
            cols = slice(j * FF_CHUNK, (j + 1) * FF_CHUNK)
            for rows, h in zip(subs, hs):
                a = jnp.dot(h, wup_ref[j, :, 0:FF_CHUNK], preferred_element_type=f32)
                u = jnp.dot(h, wup_ref[j, :, FF_CHUNK:2 * FF_CHUNK], preferred_element_type=f32)
                act_ref[rows, cols] = (_silu(a) * u).astype(bf16)
        if first_step:
            for j in range(n_chunks):
                down_copy(j).wait()
        ys = [jnp.dot(act_ref[rows, :], wd_ref[...], preferred_element_type=f32) for rows in subs]
        for rows, x, y in zip(subs, xs, ys):
            r = ALPHA * x + 0.5 * gate * y
            o_ref[0, rows] = _layer_norm(r, g_ref[ln_row:ln_row + 1, :], b_ref[ln_row:ln_row + 1, :])

    first = jnp.logical_and(pl.program_id(0) == 0, pl.program_id(1) == 0)
    pl.when(first)(functools.partial(sub_layer, True))
    pl.when(jnp.logical_not(first))(functools.partial(sub_layer, False))


def _resident(shape):
    return pl.BlockSpec(shape, lambda *_: (0,) * len(shape), pipeline_mode=pl.Buffered(1))


def _ffn(x, mod, mod_group, w_up, wd, ln_g, ln_b, *, mod_base, ln_row):
    groups, length, _ = x.shape
    tm = FFN_TILE
    n_chunks = D_FF // FF_CHUNK
    return pl.pallas_call(
        functools.partial(_ffn_kernel, mod_base=mod_base, ln_row=ln_row),
        grid=(groups, length // tm),
        in_specs=[
            pl.BlockSpec((1, tm, D_MODEL), lambda g, t: (g, t, 0)),
            pl.BlockSpec((1, N_MOD, D_MODEL), lambda g, t: (mod_group(g), 0, 0)),
            pl.BlockSpec(memory_space=pl.ANY),
            pl.BlockSpec(memory_space=pl.ANY),
            _resident((3, D_MODEL)),
            _resident((3, D_MODEL)),
        ],
        out_specs=pl.BlockSpec((1, tm, D_MODEL), lambda g, t: (g, t, 0)),
        out_shape=jax.ShapeDtypeStruct(x.shape, f32),
        scratch_shapes=[pltpu.VMEM((tm, D_FF), bf16),
                        pltpu.VMEM((n_chunks, D_MODEL, 2 * FF_CHUNK), bf16),
                        pltpu.VMEM((D_FF, D_MODEL), bf16),
                        pltpu.SemaphoreType.DMA((2 * n_chunks,))],
        compiler_params=pltpu.CompilerParams(
            dimension_semantics=("arbitrary", "arbitrary"), vmem_limit_bytes=VMEM_LIMIT_BYTES),
        name="ffn",
    )(x, mod, w_up, wd, ln_g, ln_b)


def _head_rms_norm(x, ones_bd, gain):
    ss = _dot01(ones_bd, x * x, 2, mat_on_left=False)
    return x * lax.rsqrt(ss * (1.0 / HEAD_DIM) + RMS_EPS) * gain


def _rope(x, cos, sin_signed):
    width = x.shape[-1]
    lane = lax.broadcasted_iota(jnp.int32, x.shape, 1)
    from_right = pltpu.roll(x, width - 16, 1)
    from_left = pltpu.roll(x, 16, 1)
    partner = jnp.where((lane & 31) < 16, from_right, from_left)
    return x * cos + partner * sin_signed


def _proj_kernel(*refs, rope):
    if rope:
        (x_ref, m_ref, w_ref, lbf_ref, lbb_ref, qg_ref, kg_ref, bd_ref, cos_ref, sin_ref,
         qs_ref, vh_ref, ff_ref, fb_ref, sg_ref, qa_ref, ka_ref, vt_ref,
         spf_ref, spb_ref) = refs
    else:
        (x_ref, m_ref, w_ref, lbf_ref, lbb_ref, qg_ref, kg_ref, bd_ref,
         qs_ref, vh_ref, ff_ref, fb_ref, sg_ref, qa_ref, ka_ref, vt_ref,
         spf_ref, spb_ref, kt_ref) = refs
    shift = m_ref[0, 3:4, :]
    scale = m_ref[0, 4:5, :]
    subs = [slice(lo, lo + PROJ_SUB) for lo in range(0, x_ref.shape[1], PROJ_SUB)]
    hs = [(x_ref[0, rows] * (1.0 + scale) + shift).astype(bf16) for rows in subs]

    def cols(h, lo, width):
        return jnp.dot(h, w_ref[:, lo:lo + width], preferred_element_type=f32)

    def lower_bound(lb_ref):
        l0 = lb_ref[0:1, :]
        l1 = lb_ref[1:2, :]
        m = jnp.maximum(l0, l1)
        e0 = jnp.exp(l0 - m)
        e1 = jnp.exp(l1 - m)
        return e0 / (e0 + e1)

    half = HG_FAST_BLOCK // 2

    def forget_gate(lb_ref, lo, f_ref, span_ref):
        lb = lower_bound(lb_ref)
        for rows, h in zip(subs, hs):
            f = lb + (1.0 - lb) * jax.nn.sigmoid(cols(h, lo, HG_WIDTH))
            f_ref[0, rows] = f
            sums = jnp.sum(jnp.log2(f).reshape(PROJ_SUB // half, half, HG_WIDTH), axis=1)
            span_ref[0, rows.start // half:rows.stop // half] = jnp.broadcast_to(
                jnp.max(jnp.abs(sums), axis=-1, keepdims=True), (PROJ_SUB // half, LANES))

    for rows, h in zip(subs, hs):
        qs_ref[0, rows] = _silu(cols(h, 0, HG_WIDTH))
    for rows, h in zip(subs, hs):
        vh_ref[0, rows] = cols(h, HG_WIDTH, HG_WIDTH).astype(bf16)
    forget_gate(lbf_ref, 2 * HG_WIDTH, ff_ref, spf_ref)
    forget_gate(lbb_ref, 3 * HG_WIDTH, fb_ref, spb_ref)
    for rows, h in zip(subs, hs):
        sg_ref[0, rows] = _silu(cols(h, 4 * HG_WIDTH, HG_WIDTH))

    base = 5 * HG_WIDTH
    piece = vt_ref.shape[2]
    for rows, h in zip(subs, hs):
        q = _head_rms_norm(cols(h, base, ATT_WIDTH), bd_ref[...], qg_ref[...])
        k = _head_rms_norm(cols(h, base + ATT_WIDTH, KV_WIDTH), bd_ref[0:KV_WIDTH, 0:KV_WIDTH], kg_ref[...])
        if rope:
            cos, sin = cos_ref[rows, :], sin_ref[rows, :]
            q = _rope(q, jnp.concatenate([cos] * (ATT_WIDTH // LANES), axis=1),
                      jnp.concatenate([sin] * (ATT_WIDTH // LANES), axis=1))
            k = _rope(k, cos, sin)
        qa_ref[0, rows] = (q * (HEAD_DIM ** -0.5 * LOG2_E)).astype(bf16)
        ka_ref[0, rows] = k
        v = cols(h, base + ATT_WIDTH + KV_WIDTH, KV_WIDTH)
        for lo in range(rows.start, rows.stop, min(piece, PROJ_SUB)):
            n = min(piece, PROJ_SUB)
            dst = (lo // piece, slice(None), slice(lo % piece, lo % piece + n))
            vt_ref[dst] = v[lo - rows.start:lo - rows.start + n].T
            if not rope:
                kt_ref[dst] = k[lo - rows.start:lo - rows.start + n].T


def _proj(x, mod, mod_group, w_in, lb_f, lb_b, q_gain, k_gain, ones_bd, rope_tables, seq_len):
    groups, length, _ = x.shape
    tm = TOKEN_TILE
    rope = rope_tables is not None
    n_seq = groups * length // seq_len
    if seq_len >= tm:
        tiles_per_seq = seq_len // tm
        t_spec = pl.BlockSpec((1, KV_WIDTH, tm), lambda g, t: (g * (length // seq_len) + t // tiles_per_seq, 0,
                                                               t % tiles_per_seq))
    else:
        t_spec = pl.BlockSpec((tm // seq_len, KV_WIDTH, seq_len), lambda g, t: (g * (length // tm) + t, 0, 0))
    t_shape = jax.ShapeDtypeStruct((n_seq, KV_WIDTH, seq_len), f32)
    tok = lambda width: pl.BlockSpec((1, tm, width), lambda g, t: (g, t, 0))
    in_specs = [
        tok(D_MODEL),
        pl.BlockSpec((1, N_MOD, D_MODEL), lambda g, t: (mod_group(g), 0, 0)),
        _resident((D_MODEL, IN_WIDTH)),
        _resident((2, HG_WIDTH)),
        _resident((2, HG_WIDTH)),
        _resident((1, ATT_WIDTH)),
        _resident((1, KV_WIDTH)),
        _resident((ATT_WIDTH, ATT_WIDTH)),
    ]
    args = [x, mod, w_in, lb_f, lb_b, q_gain, k_gain, ones_bd]
    if rope:
        in_specs += [pl.BlockSpec((tm, LANES), lambda g, t: (t, 0))] * 2
        args += list(rope_tables)
    shape = lambda width, dt: jax.ShapeDtypeStruct((groups, length, width), dt)
    half = HG_FAST_BLOCK // 2
    span_spec = pl.BlockSpec((1, tm // half, LANES), lambda g, t: (g, t, 0))
    span_shape = jax.ShapeDtypeStruct((groups, length // half, LANES), f32)
    return pl.pallas_call(
        functools.partial(_proj_kernel, rope=rope),
        grid=(groups, length // tm),
        in_specs=in_specs,
        out_specs=([tok(HG_WIDTH)] * 5 + [tok(ATT_WIDTH), tok(KV_WIDTH), t_spec, span_spec, span_spec]
                   + ([] if rope else [t_spec])),
        out_shape=([shape(HG_WIDTH, f32), shape(HG_WIDTH, bf16)] + [shape(HG_WIDTH, f32)] * 3
                   + [shape(ATT_WIDTH, bf16), shape(KV_WIDTH, f32), t_shape, span_shape, span_shape]
                   + ([] if rope else [t_shape])),
        compiler_params=pltpu.CompilerParams(
            dimension_semantics=("arbitrary", "arbitrary"), vmem_limit_bytes=VMEM_LIMIT_BYTES),
        name="mixer_proj",
    )(*args)


def _hgrn_direction(qs, vb, f, st_ref, o_ref, rows, *, rev):
    tc = qs.shape[0]
    lf = jnp.log2(f)
    kk = 1.0 - f
    row = lax.broadcasted_iota(jnp.int32, (tc, tc), 0)
    col = lax.broadcasted_iota(jnp.int32, (tc, tc), 1)
    ordered = (row < col) if rev else (row > col)
    differ = row ^ col
    tri = jnp.where((col >= row) if rev else (col <= row), 1.0, 0.0).astype(bf16)
    cum = _dot01(tri, lf, 2, mat_on_left=True)
    total = cum[0:1] if rev else cum[tc - 1:tc]
    carry = jnp.exp2(total)
    query_half = 0 if rev else 1
    heads = [slice(hd * HG_HEAD_DIM, (hd + 1) * HG_HEAD_DIM) for hd in range(HG_HEADS)]

    def level_exponent(s, ridx):
        if s == 1:
            return jnp.where((ridx & 1) == query_half, lf, 0.0)
        if s == 2:
            nxt = pltpu.roll(lf, tc - 1, 0)
            prv = pltpu.roll(lf, 1, 0)
            m4 = ridx & 3
            if rev:
                return jnp.where(m4 == 0, lf + nxt, jnp.where(m4 == 1, lf, jnp.where(m4 == 2, 0.0, prv)))
            return jnp.where(m4 == 0, nxt, jnp.where(m4 == 1, 0.0, jnp.where(m4 == 2, lf, lf + prv)))
        blocks = []
        for lo in range(0, tc, 2 * s):
            anchor = lo + (s if rev else s - 1)
            blocks.append(cum[lo:lo + 2 * s] - cum[anchor:anchor + 1])
        d = jnp.concatenate(blocks, axis=0)
        is_query = ((ridx >> (s.bit_length() - 1)) & 1) == query_half
        return jnp.where(is_query, d, -d)

    def add_levels(intra, first):
        ridx = lax.broadcasted_iota(jnp.int32, qs.shape, 0)
        s = first
        while s < tc:
            shift = s.bit_length() - 1
            is_query = ((ridx >> shift) & 1) == query_half
            mixed = (jnp.where(is_query, qs, kk) * jnp.exp2(level_exponent(s, ridx))).astype(bf16)
            pair = jnp.logical_and((differ >> shift) == 1, ordered)
            for hd, hs in enumerate(heads):
                intra[hd] = jnp.where(pair, _dot_nt(mixed[:, hs], mixed[:, hs]), intra[hd])
            s *= 2
        return intra

    def finish(intra, q_dec, k_end, extra):
        for hd, hs in enumerate(heads):
            st = st_ref[hd]
            o = jnp.dot(jnp.concatenate([intra[hd].astype(bf16), q_dec[:, hs]], axis=1),
                        jnp.concatenate([vb[:, hs], st.T.astype(bf16)], axis=0), preferred_element_type=f32)
            o_ref[0, rows, hs] = o if extra is None else o + extra[:, hs]
            st_ref[hd] = st * carry[:, hs] + _dot_tn(vb[:, hs], k_end[:, hs])

    blk = HG_FAST_BLOCK
    anchors = [lo + (blk // 2 if rev else blk // 2 - 1) for lo in range(0, tc, blk)]

    def mid_split_blocks():
        q_mid, k_mid, q_dec, k_end = [], [], [], []
        for lo, a in zip(range(0, tc, blk), anchors):
            rel = cum[lo:lo + blk] - cum[a:a + 1]
            q_mid.append(qs[lo:lo + blk] * jnp.exp2(rel))
            k_mid.append(kk[lo:lo + blk] * jnp.exp2(-rel))
            q_dec.append(q_mid[-1] * jnp.exp2(cum[a:a + 1]))
            k_end.append(k_mid[-1] * jnp.exp2(total - cum[a:a + 1]))
        q_mid, k_mid, q_dec, k_end = (jnp.concatenate(p, axis=0).astype(bf16) for p in (q_mid, k_mid, q_dec, k_end))
        same_block = (differ >> (blk.bit_length() - 1)) == 0
        keep = jnp.logical_and(same_block, (row <= col) if rev else (row >= col))
        intra = [jnp.where(keep, _dot_nt(q_mid[:, hs], k_mid[:, hs]), 0.0) for hs in heads]
        finish(add_levels(intra, blk), q_dec, k_end, None)

    def all_levels():
        intra = add_levels([jnp.zeros((tc, tc), f32)] * HG_HEADS, 1)
        q_dec = (qs * jnp.exp2(cum)).astype(bf16)
        k_end = (kk * jnp.exp2(total - cum)).astype(bf16)
        qk = qs * kk
        own = jnp.concatenate(
            [jnp.sum(qk[:, hs], axis=-1, keepdims=True) * vb[:, hs].astype(f32) for hs in heads], axis=-1)
        finish(intra, q_dec, k_end, own)

    return mid_split_blocks, all_levels


def _hgrn_kernel(slow_ref, *refs, has_init):
    if has_init:
        (qsf_ref, vf_ref, ff_ref, qsb_ref, vb_ref, fb_ref, s0f_ref, s0b_ref,
         of_ref, ob_ref, stf_ref, stb_ref) = refs
    else:
        (qsf_ref, vf_ref, ff_ref, qsb_ref, vb_ref, fb_ref,
         of_ref, ob_ref, sf_out_ref, sb_out_ref, stf_ref, stb_ref) = refs
    t = pl.program_id(1)

    @pl.when(t == 0)
    def _():
        for hd in range(HG_HEADS):
            if has_init:
                stf_ref[hd] = s0f_ref[0, hd].T
                stb_ref[hd] = s0b_ref[0, hd].T
            else:
                stf_ref[hd] = jnp.zeros((HG_HEAD_DIM, HG_HEAD_DIM), f32)
                stb_ref[hd] = jnp.zeros((HG_HEAD_DIM, HG_HEAD_DIM), f32)

    chunks = [slice(lo, lo + HG_TILE) for lo in range(0, qsf_ref.shape[1], HG_TILE)]
    scans = []
    for r_f, r_b in zip(chunks, reversed(chunks)):
        scans.append(_hgrn_direction(qsf_ref[0, r_f], vf_ref[0, r_f], ff_ref[0, r_f], stf_ref, of_ref, r_f, rev=False))
        scans.append(_hgrn_direction(qsb_ref[0, r_b], vb_ref[0, r_b], fb_ref[0, r_b], stb_ref, ob_ref, r_b, rev=True))
    slow = slow_ref[pl.program_id(0) * pl.num_programs(1) + t] != 0

    @pl.when(jnp.logical_not(slow))
    def _():
        for mid_split_blocks, _ in scans:
            mid_split_blocks()

    @pl.when(slow)
    def _():
        for _, all_levels in scans:
            all_levels()

    if not has_init:
        @pl.when(t == pl.num_programs(1) - 1)
        def _():
            for hd in range(HG_HEADS):
                sf_out_ref[0, hd] = stf_ref[hd].T
                sb_out_ref[0, hd] = stb_ref[hd].T


def _hgrn(qs, vh, f_fwd, f_bwd, span_fwd, span_bwd, init_states):
    groups, length, _ = qs.shape
    nt = length // HG_STEP
    has_init = init_states is not None
    per_step = lambda span: jnp.max(span[:, :, 0].reshape(groups, nt, -1), axis=-1) > HG_FAST_SPAN_LOG2
    slow = jnp.logical_or(per_step(span_fwd), per_step(span_bwd)[:, ::-1]).astype(jnp.int32).reshape(-1)
    fwd = pl.BlockSpec((1, HG_STEP, HG_WIDTH), lambda g, t, _: (g, t, 0))
    bwd = pl.BlockSpec((1, HG_STEP, HG_WIDTH), lambda g, t, _: (g, nt - 1 - t, 0))
    state = pl.BlockSpec((1, HG_HEADS, HG_HEAD_DIM, HG_HEAD_DIM), lambda g, t, _: (g, 0, 0, 0))
    in_specs = [fwd, fwd, fwd, bwd, bwd, bwd]
    args = [qs, vh, f_fwd, qs, vh, f_bwd]
    out_specs = [fwd, bwd]
    out_shape = [jax.ShapeDtypeStruct(qs.shape, f32)] * 2
    if has_init:
        in_specs += [state, state]
        args += list(init_states)
    else:
        out_specs += [state, state]
        out_shape += [jax.ShapeDtypeStruct((groups, HG_HEADS, HG_HEAD_DIM, HG_HEAD_DIM), f32)] * 2
    return pl.pallas_call(
        functools.partial(_hgrn_kernel, has_init=has_init),
        grid_spec=pltpu.PrefetchScalarGridSpec(
            num_scalar_prefetch=1,
            grid=(groups, nt),
            in_specs=in_specs,
            out_specs=out_specs,
            scratch_shapes=[pltpu.VMEM((HG_HEADS, HG_HEAD_DIM, HG_HEAD_DIM), f32)] * 2),
        out_shape=out_shape,
        compiler_params=pltpu.CompilerParams(
            dimension_semantics=("arbitrary", "arbitrary"), vmem_limit_bytes=VMEM_LIMIT_BYTES),
        name="hgrn2",
    )(slow, *args)


def _attn_kernel(*refs, n_ctx):
    if n_ctx:
        (qa_ref, ka_ref, vt_ref, ckt_ref, cvt_ref, x_ref, of_ref, ob_ref, sg_ref, m_ref, wo_ref, hg_ref, qg_ref,
         g_ref, b_ref, o_ref, kk_ref, vta_ref, bound_ref, fixed_ref) = refs
    else:
        (qa_ref, ka_ref, vt_ref, x_ref, of_ref, ob_ref, sg_ref, m_ref, wo_ref, hg_ref, qg_ref,
         g_ref, b_ref, o_ref, kk_ref, vta_ref, bound_ref, fixed_ref) = refs
    tq = qa_ref.shape[1]
    n_keys = kk_ref.shape[0]

    @pl.when(pl.program_id(1) == 0)
    def _():
        def fill_keys(lo, k):
            n = k.shape[0]
            low = lax.broadcasted_iota(jnp.int32, k.shape, 1) < HEAD_DIM
            k_sw = pltpu.roll(k, HEAD_DIM, 1)
            kk_ref[lo:lo + n, 0:LANES] = jnp.where(low, k, k_sw).astype(bf16)
            kk_ref[lo:lo + n, LANES:2 * LANES] = jnp.where(low, k_sw, k).astype(bf16)

        def fill_values(lo, vt):
            n = vt.shape[1]
            ones_row = jnp.where(lax.broadcasted_iota(jnp.int32, (V_ROWS - HEAD_DIM, n), 0) == 0, 1.0, 0.0)
            for kv in range(N_KV_HEADS):
                vta_ref[kv * V_ROWS:kv * V_ROWS + HEAD_DIM, lo:lo + n] = (
                    vt[kv * HEAD_DIM:(kv + 1) * HEAD_DIM].astype(bf16))
                vta_ref[kv * V_ROWS + HEAD_DIM:(kv + 1) * V_ROWS, lo:lo + n] = ones_row.astype(bf16)

        if n_ctx:
            fill_keys(0, ckt_ref[0].T)
            fill_values(0, cvt_ref[0])
        fill_keys(n_ctx, ka_ref[0])
        fill_values(n_ctx, vt_ref[0])

        gain_max = jnp.max(jnp.abs(qg_ref[...]), axis=1, keepdims=True)
        all_small = None
        for kv in range(N_KV_HEADS):
            kt = kk_ref[:, kv * LANES:(kv + 1) * LANES].astype(f32)
            k_norm2 = jnp.max(0.5 * jnp.sum(kt * kt, axis=1, keepdims=True), axis=0, keepdims=True)
            bound = (LOG2_E * SCORE_BOUND_SLACK) * gain_max * jnp.sqrt(k_norm2)
            bound_ref[kv] = bound[0, 0]
            small = jnp.where(bound <= SCORE_BOUND_LIMIT, 1, 0)
            all_small = small if all_small is None else all_small * small
        fixed_ref[0] = all_small[0, 0]

    ts = min(Q_SUB, tq)
    low = lax.broadcasted_iota(jnp.int32, (ts, LANES), 1) < HEAD_DIM
    kc = min(KEY_CHUNK, n_keys)
    pairs_per_kv = N_HEADS // N_KV_HEADS // 2

    def attend(rows, fixed_shift):
        def masked_pair(tile):
            qp = qa_ref[0, rows, tile * LANES:(tile + 1) * LANES]
            zero = jnp.zeros_like(qp)
            return jnp.concatenate([jnp.where(low, qp, zero), jnp.where(low, zero, qp)], axis=0)

        q_pairs = [masked_pair(tile) for tile in range(N_HEADS // 2)]
        work = [(tile, lo) for tile in range(N_HEADS // 2) for lo in range(0, n_keys, kc)]

        def scores(tile, lo):
            kv = tile // pairs_per_kv
            return _dot_nt(kk_ref[lo:lo + kc, kv * LANES:(kv + 1) * LANES], q_pairs[tile])

        heads_t = []
        st_next = scores(*work[0])
        m = acc = None
        for i, (tile, lo) in enumerate(work):
            st = st_next
            if i + 1 < len(work):
                st_next = scores(*work[i + 1])
            kv = tile // pairs_per_kv
            values_t = vta_ref[kv * V_ROWS:(kv + 1) * V_ROWS, lo:lo + kc]
            if fixed_shift:
                e = jnp.exp2(st - bound_ref[kv]).astype(bf16)
                pv = jnp.dot(values_t, e, preferred_element_type=f32)
                acc = pv if acc is None else acc + pv
            else:
                m_chunk = jnp.max(st, axis=0, keepdims=True)
                m_new = m_chunk if m is None else jnp.maximum(m, m_chunk)
                e = jnp.exp2(st - m_new).astype(bf16)
                pv = jnp.dot(values_t, e, preferred_element_type=f32)
                acc = pv if acc is None else acc * jnp.exp2(m - m_new) + pv
                m = m_new
            if lo + kc == n_keys:
                on = (acc[0:HEAD_DIM] * (1.0 / acc[HEAD_DIM:HEAD_DIM + 1])).astype(bf16)
                heads_t += [on[:, :ts], on[:, ts:]]
                m = acc = None
        return jnp.concatenate(heads_t, axis=0)

    def project(rows, o_att_t):
        o_sum = of_ref[0, rows] + ob_ref[0, rows]
        normed = []
        for hd in range(HG_HEADS):
            oh = o_sum[:, hd * HG_HEAD_DIM:(hd + 1) * HG_HEAD_DIM]
            ms = jnp.mean(oh * oh, axis=-1, keepdims=True)
            normed.append(oh * lax.rsqrt(ms + RMS_EPS))
        o_hg = (jnp.concatenate(normed, axis=-1) * hg_ref[...] * sg_ref[0, rows]).astype(bf16)
        y = (jnp.dot(o_hg, wo_ref[0:HG_WIDTH, :], preferred_element_type=f32)
             + _dot_tn(o_att_t, wo_ref[HG_WIDTH:HG_WIDTH + ATT_WIDTH, :]))
        r = ALPHA * x_ref[0, rows] + m_ref[0, 5:6, :] * y
        o_ref[0, rows] = _layer_norm(r, g_ref[1:2, :], b_ref[1:2, :])

    def run(fixed_shift):
        for lo in range(0, tq, ts):
            rows = slice(lo, lo + ts)
            project(rows, attend(rows, fixed_shift))

    use_bound = fixed_ref[0] != 0
    pl.when(use_bound)(functools.partial(run, True))
    pl.when(jnp.logical_not(use_bound))(functools.partial(run, False))


def _attn(qa, ka, vt, ctx_kv_t, x, o_f, o_b, sg, mod, mod_group, w_out, hg_gain, q_gain, ln_g, ln_b):
    groups, length, _ = qa.shape
    tq = min(Q_TILE, length)
    n_ctx = 0 if ctx_kv_t is None else ctx_kv_t[0].shape[2]
    n_keys = n_ctx + length
    tok = lambda width: pl.BlockSpec((1, tq, width), lambda g, t: (g, t, 0))
    whole_t = lambda n: pl.BlockSpec((1, KV_WIDTH, n), lambda g, t: (g, 0, 0))
    in_specs = [tok(ATT_WIDTH), pl.BlockSpec((1, length, KV_WIDTH), lambda g, t: (g, 0, 0)), whole_t(length)]
    args = [qa, ka, vt]
    if n_ctx:
        in_specs += [whole_t(n_ctx), whole_t(n_ctx)]
        args += list(ctx_kv_t)
    in_specs += [
        tok(D_MODEL), tok(HG_WIDTH), tok(HG_WIDTH), tok(HG_WIDTH),
        pl.BlockSpec((1, N_MOD, D_MODEL), lambda g, t: (mod_group(g), 0, 0)),
        _resident((D_MODEL, D_MODEL)),
        _resident((1, HG_WIDTH)),
        _resident((1, ATT_WIDTH)),
        _resident((3, D_MODEL)),
        _resident((3, D_MODEL)),
    ]
    args += [x, o_f, o_b, sg, mod, w_out, hg_gain, q_gain, ln_g, ln_b]
    return pl.pallas_call(
        functools.partial(_attn_kernel, n_ctx=n_ctx),
        grid=(groups, length // tq),
        in_specs=in_specs,
        out_specs=tok(D_MODEL),
        out_shape=jax.ShapeDtypeStruct(x.shape, f32),
        scratch_shapes=[pltpu.VMEM((n_keys, N_KV_HEADS * LANES), bf16),
                        pltpu.VMEM((N_KV_HEADS * V_ROWS, n_keys), bf16),
                        pltpu.SMEM((N_KV_HEADS,), f32),
                        pltpu.SMEM((1,), jnp.int32)],
        compiler_params=pltpu.CompilerParams(
            dimension_semantics=("arbitrary", "arbitrary"), vmem_limit_bytes=VMEM_LIMIT_BYTES),
        name="attn_out",
    )(*args)


def _rope_tables(n_tokens):
    half = HEAD_DIM // 2
    t = jnp.arange(n_tokens)
    inv = ROPE_THETA ** (-jnp.arange(0, half, 2, dtype=f32) / half)
    ang_row = (t // GRID_W).astype(f32)[:, None] * inv
    ang_col = (t % GRID_W).astype(f32)[:, None] * inv
    cos = jnp.concatenate([jnp.cos(ang_row)] * 2 + [jnp.cos(ang_col)] * 2, axis=-1)
    sin = jnp.concatenate([-jnp.sin(ang_row), jnp.sin(ang_row), -jnp.sin(ang_col), jnp.sin(ang_col)], axis=-1)
    return jnp.tile(cos, (1, LANES // HEAD_DIM)), jnp.tile(sin, (1, LANES // HEAD_DIM))


def kernel(x_prompt, x_sample, cache_k, cache_v, state_hgrn_fwd, state_hgrn_bwd, c, c_ctx, w_mod, b_mod,
           w_ffn1_in, w_ffn1_out, w_ffn2_in, w_ffn2_out, w_in, w_out, q_norm_g, k_norm_g, hg_norm_g,
           lb_logits_fwd, lb_logits_bwd, ln_g, ln_b):
    assert w_mod.shape[0] == DEPTH and lb_logits_fwd.shape[0] == DEPTH + 1
    batch, seq, _ = x_prompt.shape
    dec_batch, dec_seq, _ = x_sample.shape
    past = cache_k.shape[2]

    ctx_row = dec_batch
    rows = 16
    cvecs = jnp.concatenate([c, c_ctx[None, :], jnp.zeros((rows - dec_batch - 1, D_MODEL), f32)], axis=0)
    mod = _modulation(cvecs, w_mod[0], b_mod[0]).reshape(rows, N_MOD, D_MODEL)

    to_bf16 = lambda w: w.astype(bf16)
    n_chunks = D_FF // FF_CHUNK
    up_chunks = lambda w: to_bf16(
        w.reshape(D_MODEL, 2, n_chunks, FF_CHUNK).transpose(2, 0, 1, 3).reshape(n_chunks, D_MODEL, 2 * FF_CHUNK))
    w1u, w2u = up_chunks(w_ffn1_in[0]), up_chunks(w_ffn2_in[0])
    w1d, w2d = to_bf16(w_ffn1_out[0]), to_bf16(w_ffn2_out[0])
    w_in_b, w_out_b = to_bf16(w_in[0]), to_bf16(w_out[0])
    gains = ln_g[0], ln_b[0]
    q_gain = jnp.tile(q_norm_g[0], N_HEADS).reshape(1, ATT_WIDTH)
    k_gain = jnp.tile(k_norm_g[0], N_KV_HEADS).reshape(1, KV_WIDTH)
    hg_gain = hg_norm_g[0].reshape(1, HG_WIDTH)
    head_of = jnp.arange(ATT_WIDTH) // HEAD_DIM
    ones_bd = (head_of[:, None] == head_of[None, :]).astype(bf16)

    def trunk(x, mod_group, rope_tables, ctx_kv_t, init_states, hg_groups):
        shape = x.shape
        seq_len = shape[0] * shape[1] // hg_groups
        per_seq = lambda a: a.reshape(hg_groups, seq_len, a.shape[-1])
        x = _ffn(x, mod, mod_group, w1u, w1d, *gains, mod_base=0, ln_row=0)
        proj = _proj(x, mod, mod_group, w_in_b, lb_logits_fwd, lb_logits_bwd, q_gain, k_gain, ones_bd, rope_tables,
                     seq_len)
        qs, vh, f_f, f_b, sg, qa, ka = map(per_seq, proj[:7])
        vt, span_f, span_b = proj[7], proj[8].reshape(hg_groups, -1, LANES), proj[9].reshape(hg_groups, -1, LANES)
        scans = _hgrn(qs, vh, f_f, f_b, span_f, span_b, init_states)
        x = _attn(qa, ka, vt, ctx_kv_t, per_seq(x), scans[0], scans[1], sg, mod, mod_group, w_out_b, hg_gain,
                  q_gain, *gains).reshape(shape)
        x = _ffn(x, mod, mod_group, w2u, w2d, *gains, mod_base=6, ln_row=2)
        return x, proj[10:], vt, scans[2:]

    ctx_group = lambda g: ctx_row
    y_prompt, (kt_new,), vt_new, states = trunk(
        x_prompt.reshape(1, batch * seq, D_MODEL), ctx_group, None, None, None, batch)
    y_prompt = y_prompt.reshape(batch, seq, D_MODEL)
    to_cache = lambda t: t.reshape(batch, DEPTH, N_KV_HEADS, HEAD_DIM, seq).transpose(0, 1, 4, 2, 3)
    new_cache_k, new_cache_v = to_cache(kt_new), to_cache(vt_new)
    new_state_fwd = states[0].reshape(batch, DEPTH, HG_HEADS, HG_HEAD_DIM, HG_HEAD_DIM)
    new_state_bwd = states[1].reshape(batch, DEPTH, HG_HEADS, HG_HEAD_DIM, HG_HEAD_DIM)

    from_cache = lambda t: t[:, 0].transpose(0, 2, 3, 1).reshape(dec_batch, KV_WIDTH, past)
    init_states = (state_hgrn_fwd[:, 0], state_hgrn_bwd[:, 0])
    y_sample, _, _, _ = trunk(x_sample, lambda g: g, _rope_tables(dec_seq), (from_cache(cache_k), from_cache(cache_v)),
                              init_states, dec_batch)

    return (y_prompt, y_sample, new_cache_k, new_cache_v, new_state_fwd, new_state_bwd)
```

```python
import functools

import jax
import jax.numpy as jnp
from jax import lax
from jax.experimental import pallas as pl
from jax.experimental.pallas import tpu as pltpu

f32 = jnp.float32
bf16 = jnp.bfloat16

D_MODEL = 1024
N_MOD = 9
HG_WIDTH = 512
HG_HEAD_DIM = 128
HG_HEADS = 4
ATT_WIDTH = 512
HEAD_DIM = 64
N_HEADS = 8
N_KV_HEADS = 2
KV_WIDTH = 128
IN_WIDTH = 5 * HG_WIDTH + ATT_WIDTH + 2 * KV_WIDTH
D_FF = 2816
GRID_W = 64
ROPE_THETA = 10000.0
DEPTH = 1
ALPHA = (2.0 * DEPTH) ** 0.25
LOG2_E = 1.4426950408889634
LN_EPS = 1e-6
RMS_EPS = 1e-6

LANES = 128
VMEM_LIMIT_BYTES = 56 * 1024 * 1024

FF_CHUNK = 256
FFN_TILE = 1024
FFN_SUB = 512
TOKEN_TILE = 512
PROJ_SUB = 256
HG_TILE = 128
HG_STEP = 512
Q_TILE = 512
Q_SUB = 256
KEY_CHUNK = 512
SCORE_BOUND_LIMIT = 40.0
SCORE_BOUND_SLACK = 1.01
V_ROWS = 80
HG_FAST_BLOCK = 64
HG_FAST_SPAN_LOG2 = 100.0


def _silu(x):
    return x * jax.nn.sigmoid(x)


def _layer_norm(r, g, b):
    mu = jnp.mean(r, axis=-1, keepdims=True)
    c = r - mu
    var = jnp.mean(c * c, axis=-1, keepdims=True)
    return c * lax.rsqrt(var + LN_EPS) * g + b


def _split_bf16(x, parts):
    out = []
    r = x
    for _ in range(parts - 1):
        p = r.astype(bf16)
        out.append(p)
        r = r - p.astype(f32)
    out.append(r.astype(bf16))
    return out


def _dot01(mat01, x, parts, *, mat_on_left):
    pieces = _split_bf16(x, parts)
    if mat_on_left:
        return jnp.dot(jnp.concatenate([mat01] * parts, axis=1), jnp.concatenate(pieces, axis=0),
                       preferred_element_type=f32)
    return jnp.dot(jnp.concatenate(pieces, axis=1), jnp.concatenate([mat01] * parts, axis=0),
                   preferred_element_type=f32)


def _dot_nt(a, b):
    return lax.dot_general(a, b, (((1,), (1,)), ((), ())), preferred_element_type=f32)


def _dot_tn(a, b):
    return lax.dot_general(a, b, (((0,), (0,)), ((), ())), preferred_element_type=f32)


def _mod_kernel(c_ref, w_ref, b_ref, o_ref):
    a = _silu(c_ref[...]).astype(bf16)
    o_ref[...] = jnp.dot(a, w_ref[...].astype(bf16), preferred_element_type=f32) + b_ref[...]


def _modulation(cvecs, w_mod, b_mod):
    rows = cvecs.shape[0]
    n_out = w_mod.shape[1]
    tn = D_MODEL
    return pl.pallas_call(
        _mod_kernel,
        grid=(n_out // tn,),
        in_specs=[
            pl.BlockSpec((rows, D_MODEL), lambda j: (0, 0)),
            pl.BlockSpec((D_MODEL, tn), lambda j: (0, j)),
            pl.BlockSpec((1, tn), lambda j: (0, j)),
        ],
        out_specs=pl.BlockSpec((rows, tn), lambda j: (0, j)),
        out_shape=jax.ShapeDtypeStruct((rows, n_out), f32),
        compiler_params=pltpu.CompilerParams(dimension_semantics=("arbitrary",)),
        name="modulation",
    )(cvecs, w_mod, b_mod.reshape(1, n_out))


def _ffn_kernel(x_ref, m_ref, wup_ref, wd_ref, g_ref, b_ref, o_ref, act_ref, *, mod_base, ln_row):
    shift = m_ref[0, mod_base:mod_base + 1, :]
    scale = m_ref[0, mod_base + 1:mod_base + 2, :]
    gate = m_ref[0, mod_base + 2:mod_base + 3, :]
    subs = [slice(lo, lo + FFN_SUB) for lo in range(0, x_ref.shape[1], FFN_SUB)]
    xs = [x_ref[0, rows] for rows in subs]
    hs = [(x * (1.0 + scale) + shift).astype(bf16) for x in xs]
    for j in range(D_FF // FF_CHUNK):
        cols = slice(j * FF_CHUNK, (j + 1) * FF_CHUNK)
        for rows, h in zip(subs, hs):
            a = jnp.dot(h, wup_ref[:, cols], preferred_element_type=f32)
            u = jnp.dot(h, wup_ref[:, D_FF + j * FF_CHUNK:D_FF + (j + 1) * FF_CHUNK],
                        preferred_element_type=f32)
            act_ref[rows, cols] = (_silu(a) * u).astype(bf16)
    ys = [jnp.dot(act_ref[rows, :], wd_ref[...], preferred_element_type=f32) for rows in subs]
    for rows, x, y in zip(subs, xs, ys):
        r = ALPHA * x + 0.5 * gate * y
        o_ref[0, rows] = _layer_norm(r, g_ref[ln_row:ln_row + 1, :], b_ref[ln_row:ln_row + 1, :])


def _resident(shape):
    return pl.BlockSpec(shape, lambda *_: (0,) * len(shape), pipeline_mode=pl.Buffered(1))


def _ffn(x, mod, mod_group, w_up, wd, ln_g, ln_b, *, mod_base, ln_row):
    groups, length, _ = x.shape
    tm = FFN_TILE
    return pl.pallas_call(
        functools.partial(_ffn_kernel, mod_base=mod_base, ln_row=ln_row),
        grid=(groups, length // tm),
        in_specs=[
            pl.BlockSpec((1, tm, D_MODEL), lambda g, t: (g, t, 0)),
            pl.BlockSpec((1, N_MOD, D_MODEL), lambda g, t: (mod_group(g), 0, 0)),
            _resident((D_MODEL, 2 * D_FF)),
            _resident((D_FF, D_MODEL)),
            _resident((3, D_MODEL)),
            _resident((3, D_MODEL)),
        ],
        out_specs=pl.BlockSpec((1, tm, D_MODEL), lambda g, t: (g, t, 0)),
        out_shape=jax.ShapeDtypeStruct(x.shape, f32),
        scratch_shapes=[pltpu.VMEM((tm, D_FF), bf16)],
        compiler_params=pltpu.CompilerParams(
            dimension_semantics=("arbitrary", "arbitrary"), vmem_limit_bytes=VMEM_LIMIT_BYTES),
        name="ffn",
    )(x, mod, w_up, wd, ln_g, ln_b)


def _head_rms_norm(x, ones_bd, gain):
    ss = _dot01(ones_bd, x * x, 2, mat_on_left=False)
    return x * lax.rsqrt(ss * (1.0 / HEAD_DIM) + RMS_EPS) * gain


def _rope(x, cos, sin_signed):
    width = x.shape[-1]
    lane = lax.broadcasted_iota(jnp.int32, x.shape, 1)
    from_right = pltpu.roll(x, width - 16, 1)
    from_left = pltpu.roll(x, 16, 1)
    partner = jnp.where((lane & 31) < 16, from_right, from_left)
    return x * cos + partner * sin_signed


def _proj_kernel(*refs, rope):
    if rope:
        (x_ref, m_ref, w_ref, lbf_ref, lbb_ref, qg_ref, kg_ref, bd_ref, cos_ref, sin_ref,
         qs_ref, vh_ref, ff_ref, fb_ref, sg_ref, qa_ref, ka_ref, vt_ref,
         spf_ref, spb_ref) = refs
    else:
        (x_ref, m_ref, w_ref, lbf_ref, lbb_ref, qg_ref, kg_ref, bd_ref,
         qs_ref, vh_ref, ff_ref, fb_ref, sg_ref, qa_ref, ka_ref, vt_ref,
         spf_ref, spb_ref, kt_ref) = refs
    shift = m_ref[0, 3:4, :]
    scale = m_ref[0, 4:5, :]
    subs = [slice(lo, lo + PROJ_SUB) for lo in range(0, x_ref.shape[1], PROJ_SUB)]
    hs = [(x_ref[0, rows] * (1.0 + scale) + shift).astype(bf16) for rows in subs]

    def cols(h, lo, width):
        return jnp.dot(h, w_ref[:, lo:lo + width], preferred_element_type=f32)

    def lower_bound(lb_ref):
        l0 = lb_ref[0:1, :]
        l1 = lb_ref[1:2, :]
        m = jnp.maximum(l0, l1)
        e0 = jnp.exp(l0 - m)
        e1 = jnp.exp(l1 - m)
        return e0 / (e0 + e1)

    half = HG_FAST_BLOCK // 2

    def forget_gate(lb_ref, lo, f_ref, span_ref):
        lb = lower_bound(lb_ref)
        for rows, h in zip(subs, hs):
            f = lb + (1.0 - lb) * jax.nn.sigmoid(cols(h, lo, HG_WIDTH))
            f_ref[0, rows] = f
            sums = jnp.sum(jnp.log2(f).reshape(PROJ_SUB // half, half, HG_WIDTH), axis=1)
            span_ref[0, rows.start // half:rows.stop // half] = jnp.broadcast_to(
                jnp.max(jnp.abs(sums), axis=-1, keepdims=True), (PROJ_SUB // half, LANES))

    for rows, h in zip(subs, hs):
        qs_ref[0, rows] = _silu(cols(h, 0, HG_WIDTH))
    for rows, h in zip(subs, hs):
        vh_ref[0, rows] = cols(h, HG_WIDTH, HG_WIDTH).astype(bf16)
    forget_gate(lbf_ref, 2 * HG_WIDTH, ff_ref, spf_ref)
    forget_gate(lbb_ref, 3 * HG_WIDTH, fb_ref, spb_ref)
    for rows, h in zip(subs, hs):
        sg_ref[0, rows] = _silu(cols(h, 4 * HG_WIDTH, HG_WIDTH))

    base = 5 * HG_WIDTH
    piece = vt_ref.shape[2]
    for rows, h in zip(subs, hs):
        q = _head_rms_norm(cols(h, base, ATT_WIDTH), bd_ref[...], qg_ref[...])
        k = _head_rms_norm(cols(h, base + ATT_WIDTH, KV_WIDTH), bd_ref[0:KV_WIDTH, 0:KV_WIDTH], kg_ref[...])
        if rope:
            cos, sin = cos_ref[rows, :], sin_ref[rows, :]
            q = _rope(q, jnp.concatenate([cos] * (ATT_WIDTH // LANES), axis=1),
                      jnp.concatenate([sin] * (ATT_WIDTH // LANES), axis=1))
            k = _rope(k, cos, sin)
        qa_ref[0, rows] = (q * (HEAD_DIM ** -0.5 * LOG2_E)).astype(bf16)
        ka_ref[0, rows] = k
        v = cols(h, base + ATT_WIDTH + KV_WIDTH, KV_WIDTH)
        for lo in range(rows.start, rows.stop, min(piece, PROJ_SUB)):
            n = min(piece, PROJ_SUB)
            dst = (lo // piece, slice(None), slice(lo % piece, lo % piece + n))
            vt_ref[dst] = v[lo - rows.start:lo - rows.start + n].T
            if not rope:
                kt_ref[dst] = k[lo - rows.start:lo - rows.start + n].T


def _proj(x, mod, mod_group, w_in, lb_f, lb_b, q_gain, k_gain, ones_bd, rope_tables, seq_len):
    groups, length, _ = x.shape
    tm = TOKEN_TILE
    rope = rope_tables is not None
    n_seq = groups * length // seq_len
    if seq_len >= tm:
        tiles_per_seq = seq_len // tm
        t_spec = pl.BlockSpec((1, KV_WIDTH, tm), lambda g, t: (g * (length // seq_len) + t // tiles_per_seq, 0,
                                                               t % tiles_per_seq))
    else:
        t_spec = pl.BlockSpec((tm // seq_len, KV_WIDTH, seq_len), lambda g, t: (g * (length // tm) + t, 0, 0))
    t_shape = jax.ShapeDtypeStruct((n_seq, KV_WIDTH, seq_len), f32)
    tok = lambda width: pl.BlockSpec((1, tm, width), lambda g, t: (g, t, 0))
    in_specs = [
        tok(D_MODEL),
        pl.BlockSpec((1, N_MOD, D_MODEL), lambda g, t: (mod_group(g), 0, 0)),
        _resident((D_MODEL, IN_WIDTH)),
        _resident((2, HG_WIDTH)),
        _resident((2, HG_WIDTH)),
        _resident((1, ATT_WIDTH)),
        _resident((1, KV_WIDTH)),
        _resident((ATT_WIDTH, ATT_WIDTH)),
    ]
    args = [x, mod, w_in, lb_f, lb_b, q_gain, k_gain, ones_bd]
    if rope:
        in_specs += [pl.BlockSpec((tm, LANES), lambda g, t: (t, 0))] * 2
        args += list(rope_tables)
    shape = lambda width, dt: jax.ShapeDtypeStruct((groups, length, width), dt)
    half = HG_FAST_BLOCK // 2
    span_spec = pl.BlockSpec((1, tm // half, LANES), lambda g, t: (g, t, 0))
    span_shape = jax.ShapeDtypeStruct((groups, length // half, LANES), f32)
    return pl.pallas_call(
        functools.partial(_proj_kernel, rope=rope),
        grid=(groups, length // tm),
        in_specs=in_specs,
        out_specs=([tok(HG_WIDTH)] * 5 + [tok(ATT_WIDTH), tok(KV_WIDTH), t_spec, span_spec, span_spec]
                   + ([] if rope else [t_spec])),
        out_shape=([shape(HG_WIDTH, f32), shape(HG_WIDTH, bf16)] + [shape(HG_WIDTH, f32)] * 3
                   + [shape(ATT_WIDTH, bf16), shape(KV_WIDTH, f32), t_shape, span_shape, span_shape]
                   + ([] if rope else [t_shape])),
        compiler_params=pltpu.CompilerParams(
            dimension_semantics=("arbitrary", "arbitrary"), vmem_limit_bytes=VMEM_LIMIT_BYTES),
        name="mixer_proj",
    )(*args)


def _hgrn_direction(qs, vb, f, st_ref, o_ref, rows, *, rev):
    tc = qs.shape[0]
    lf = jnp.log2(f)
    kk = 1.0 - f
    row = lax.broadcasted_iota(jnp.int32, (tc, tc), 0)
    col = lax.broadcasted_iota(jnp.int32, (tc, tc), 1)
    ordered = (row < col) if rev else (row > col)
    differ = row ^ col
    tri = jnp.where((col >= row) if rev else (col <= row), 1.0, 0.0).astype(bf16)
    cum = _dot01(tri, lf, 2, mat_on_left=True)
    total = cum[0:1] if rev else cum[tc - 1:tc]
    carry = jnp.exp2(total)
    query_half = 0 if rev else 1
    heads = [slice(hd * HG_HEAD_DIM, (hd + 1) * HG_HEAD_DIM) for hd in range(HG_HEADS)]

    def level_exponent(s, ridx):
        if s == 1:
            return jnp.where((ridx & 1) == query_half, lf, 0.0)
        if s == 2:
            nxt = pltpu.roll(lf, tc - 1, 0)
            prv = pltpu.roll(lf, 1, 0)
            m4 = ridx & 3
            if rev:
                return jnp.where(m4 == 0, lf + nxt, jnp.where(m4 == 1, lf, jnp.where(m4 == 2, 0.0, prv)))
            return jnp.where(m4 == 0, nxt, jnp.where(m4 == 1, 0.0, jnp.where(m4 == 2, lf, lf + prv)))
        blocks = []
        for lo in range(0, tc, 2 * s):
            anchor = lo + (s if rev else s - 1)
            blocks.append(cum[lo:lo + 2 * s] - cum[anchor:anchor + 1])
        d = jnp.concatenate(blocks, axis=0)
        is_query = ((ridx >> (s.bit_length() - 1)) & 1) == query_half
        return jnp.where(is_query, d, -d)

    def add_levels(intra, first):
        ridx = lax.broadcasted_iota(jnp.int32, qs.shape, 0)
        s = first
        while s < tc:
            shift = s.bit_length() - 1
            is_query = ((ridx >> shift) & 1) == query_half
            mixed = (jnp.where(is_query, qs, kk) * jnp.exp2(level_exponent(s, ridx))).astype(bf16)
            pair = jnp.logical_and((differ >> shift) == 1, ordered)
            for hd, hs in enumerate(heads):
                intra[hd] = jnp.where(pair, _dot_nt(mixed[:, hs], mixed[:, hs]), intra[hd])
            s *= 2
        return intra

    def finish(intra, q_dec, k_end, extra):
        for hd, hs in enumerate(heads):
            st = st_ref[hd]
            o = jnp.dot(jnp.concatenate([intra[hd].astype(bf16), q_dec[:, hs]], axis=1),
                        jnp.concatenate([vb[:, hs], st.T.astype(bf16)], axis=0), preferred_element_type=f32)
            o_ref[0, rows, hs] = o if extra is None else o + extra[:, hs]
            st_ref[hd] = st * carry[:, hs] + _dot_tn(vb[:, hs], k_end[:, hs])

    blk = HG_FAST_BLOCK
    anchors = [lo + (blk // 2 if rev else blk // 2 - 1) for lo in range(0, tc, blk)]

    def mid_split_blocks():
        q_mid, k_mid, q_dec, k_end = [], [], [], []
        for lo, a in zip(range(0, tc, blk), anchors):
            rel = cum[lo:lo + blk] - cum[a:a + 1]
            q_mid.append(qs[lo:lo + blk] * jnp.exp2(rel))
            k_mid.append(kk[lo:lo + blk] * jnp.exp2(-rel))
            q_dec.append(q_mid[-1] * jnp.exp2(cum[a:a + 1]))
            k_end.append(k_mid[-1] * jnp.exp2(total - cum[a:a + 1]))
        q_mid, k_mid, q_dec, k_end = (jnp.concatenate(p, axis=0).astype(bf16) for p in (q_mid, k_mid, q_dec, k_end))
        same_block = (differ >> (blk.bit_length() - 1)) == 0
        keep = jnp.logical_and(same_block, (row <= col) if rev else (row >= col))
        intra = [jnp.where(keep, _dot_nt(q_mid[:, hs], k_mid[:, hs]), 0.0) for hs in heads]
        finish(add_levels(intra, blk), q_dec, k_end, None)

    def all_levels():
        intra = add_levels([jnp.zeros((tc, tc), f32)] * HG_HEADS, 1)
        q_dec = (qs * jnp.exp2(cum)).astype(bf16)
        k_end = (kk * jnp.exp2(total - cum)).astype(bf16)
        qk = qs * kk
        own = jnp.concatenate(
            [jnp.sum(qk[:, hs], axis=-1, keepdims=True) * vb[:, hs].astype(f32) for hs in heads], axis=-1)
        finish(intra, q_dec, k_end, own)

    return mid_split_blocks, all_levels


def _hgrn_kernel(slow_ref, *refs, has_init):
    if has_init:
        (qsf_ref, vf_ref, ff_ref, qsb_ref, vb_ref, fb_ref, s0f_ref, s0b_ref,
         of_ref, ob_ref, stf_ref, stb_ref) = refs
    else:
        (qsf_ref, vf_ref, ff_ref, qsb_ref, vb_ref, fb_ref,
         of_ref, ob_ref, sf_out_ref, sb_out_ref, stf_ref, stb_ref) = refs
    t = pl.program_id(1)

    @pl.when(t == 0)
    def _():
        for hd in range(HG_HEADS):
            if has_init:
                stf_ref[hd] = s0f_ref[0, hd].T
                stb_ref[hd] = s0b_ref[0, hd].T
            else:
                stf_ref[hd] = jnp.zeros((HG_HEAD_DIM, HG_HEAD_DIM), f32)
                stb_ref[hd] = jnp.zeros((HG_HEAD_DIM, HG_HEAD_DIM), f32)

    chunks = [slice(lo, lo + HG_TILE) for lo in range(0, qsf_ref.shape[1], HG_TILE)]
    scans = []
    for r_f, r_b in zip(chunks, reversed(chunks)):
        scans.append(_hgrn_direction(qsf_ref[0, r_f], vf_ref[0, r_f], ff_ref[0, r_f], stf_ref, of_ref, r_f, rev=False))
        scans.append(_hgrn_direction(qsb_ref[0, r_b], vb_ref[0, r_b], fb_ref[0, r_b], stb_ref, ob_ref, r_b, rev=True))
    slow = slow_ref[pl.program_id(0) * pl.num_programs(1) + t] != 0

    @pl.when(jnp.logical_not(slow))
    def _():
        for mid_split_blocks, _ in scans:
            mid_split_blocks()

    @pl.when(slow)
    def _():
        for _, all_levels in scans:
            all_levels()

    if not has_init:
        @pl.when(t == pl.num_programs(1) - 1)
        def _():
            for hd in range(HG_HEADS):
                sf_out_ref[0, hd] = stf_ref[hd].T
                sb_out_ref[0, hd] = stb_ref[hd].T


def _hgrn(qs, vh, f_fwd, f_bwd, span_fwd, span_bwd, init_states):
    groups, length, _ = qs.shape
    step = min(HG_STEP, length)
    nt = length // step
    has_init = init_states is not None
    per_step = lambda span: jnp.max(span[:, :, 0].reshape(groups, nt, -1), axis=-1) > HG_FAST_SPAN_LOG2
    slow = jnp.logical_or(per_step(span_fwd), per_step(span_bwd)[:, ::-1]).astype(jnp.int32).reshape(-1)
    fwd = pl.BlockSpec((1, step, HG_WIDTH), lambda g, t, _: (g, t, 0))
    bwd = pl.BlockSpec((1, step, HG_WIDTH), lambda g, t, _: (g, nt - 1 - t, 0))
    state = pl.BlockSpec((1, HG_HEADS, HG_HEAD_DIM, HG_HEAD_DIM), lambda g, t, _: (g, 0, 0, 0))
    in_specs = [fwd, fwd, fwd, bwd, bwd, bwd]
    args = [qs, vh, f_fwd, qs, vh, f_bwd]
    out_specs = [fwd, bwd]
    out_shape = [jax.ShapeDtypeStruct(qs.shape, f32)] * 2
    if has_init:
        in_specs += [state, state]
        args += list(init_states)
    else:
        out_specs += [state, state]
        out_shape += [jax.ShapeDtypeStruct((groups, HG_HEADS, HG_HEAD_DIM, HG_HEAD_DIM), f32)] * 2
    return pl.pallas_call(
        functools.partial(_hgrn_kernel, has_init=has_init),
        grid_spec=pltpu.PrefetchScalarGridSpec(
            num_scalar_prefetch=1,
            grid=(groups, nt),
            in_specs=in_specs,
            out_specs=out_specs,
            scratch_shapes=[pltpu.VMEM((HG_HEADS, HG_HEAD_DIM, HG_HEAD_DIM), f32)] * 2),
        out_shape=out_shape,
        compiler_params=pltpu.CompilerParams(
            dimension_semantics=("arbitrary", "arbitrary"), vmem_limit_bytes=VMEM_LIMIT_BYTES),
        name="hgrn2",
    )(slow, *args)


def _attn_kernel(*refs, n_ctx):
    if n_ctx:
        (qa_ref, ka_ref, vt_ref, ckt_ref, cvt_ref, x_ref, of_ref, ob_ref, sg_ref, m_ref, wo_ref, hg_ref, qg_ref,
         g_ref, b_ref, o_ref, kk_ref, vta_ref, bound_ref, fixed_ref) = refs
    else:
        (qa_ref, ka_ref, vt_ref, x_ref, of_ref, ob_ref, sg_ref, m_ref, wo_ref, hg_ref, qg_ref,
         g_ref, b_ref, o_ref, kk_ref, vta_ref, bound_ref, fixed_ref) = refs
    tq = qa_ref.shape[1]
    n_keys = kk_ref.shape[0]

    @pl.when(pl.program_id(1) == 0)
    def _():
        def fill_keys(lo, k):
            n = k.shape[0]
            low = lax.broadcasted_iota(jnp.int32, k.shape, 1) < HEAD_DIM
            k_sw = pltpu.roll(k, HEAD_DIM, 1)
            kk_ref[lo:lo + n, 0:LANES] = jnp.where(low, k, k_sw).astype(bf16)
            kk_ref[lo:lo + n, LANES:2 * LANES] = jnp.where(low, k_sw, k).astype(bf16)

        def fill_values(lo, vt):
            n = vt.shape[1]
            ones_row = jnp.where(lax.broadcasted_iota(jnp.int32, (V_ROWS - HEAD_DIM, n), 0) == 0, 1.0, 0.0)
            for kv in range(N_KV_HEADS):
                vta_ref[kv * V_ROWS:kv * V_ROWS + HEAD_DIM, lo:lo + n] = (
                    vt[kv * HEAD_DIM:(kv + 1) * HEAD_DIM].astype(bf16))
                vta_ref[kv * V_ROWS + HEAD_DIM:(kv + 1) * V_ROWS, lo:lo + n] = ones_row.astype(bf16)

        if n_ctx:
            fill_keys(0, ckt_ref[0].T)
            fill_values(0, cvt_ref[0])
        fill_keys(n_ctx, ka_ref[0])
        fill_values(n_ctx, vt_ref[0])

        gain_max = jnp.max(jnp.abs(qg_ref[...]), axis=1, keepdims=True)
        all_small = None
        for kv in range(N_KV_HEADS):
            kt = kk_ref[:, kv * LANES:(kv + 1) * LANES].astype(f32)
            k_norm2 = jnp.max(0.5 * jnp.sum(kt * kt, axis=1, keepdims=True), axis=0, keepdims=True)
            bound = (LOG2_E * SCORE_BOUND_SLACK) * gain_max * jnp.sqrt(k_norm2)
            bound_ref[kv] = bound[0, 0]
            small = jnp.where(bound <= SCORE_BOUND_LIMIT, 1, 0)
            all_small = small if all_small is None else all_small * small
        fixed_ref[0] = all_small[0, 0]

    ts = min(Q_SUB, tq)
    low = lax.broadcasted_iota(jnp.int32, (ts, LANES), 1) < HEAD_DIM
    kc = min(KEY_CHUNK, n_keys)
    pairs_per_kv = N_HEADS // N_KV_HEADS // 2

    def attend(rows, fixed_shift):
        def masked_pair(tile):
            qp = qa_ref[0, rows, tile * LANES:(tile + 1) * LANES]
            zero = jnp.zeros_like(qp)
            return jnp.concatenate([jnp.where(low, qp, zero), jnp.where(low, zero, qp)], axis=0)

        q_pairs = [masked_pair(tile) for tile in range(N_HEADS // 2)]
        work = [(tile, lo) for tile in range(N_HEADS // 2) for lo in range(0, n_keys, kc)]

        def scores(tile, lo):
            kv = tile // pairs_per_kv
            return _dot_nt(kk_ref[lo:lo + kc, kv * LANES:(kv + 1) * LANES], q_pairs[tile])

        heads_t = []
        st_next = scores(*work[0])
        m = acc = None
        for i, (tile, lo) in enumerate(work):
            st = st_next
            if i + 1 < len(work):
                st_next = scores(*work[i + 1])
            kv = tile // pairs_per_kv
            values_t = vta_ref[kv * V_ROWS:(kv + 1) * V_ROWS, lo:lo + kc]
            if fixed_shift:
                e = jnp.exp2(st - bound_ref[kv]).astype(bf16)
                pv = jnp.dot(values_t, e, preferred_element_type=f32)
                acc = pv if acc is None else acc + pv
            else:
                m_chunk = jnp.max(st, axis=0, keepdims=True)
                m_new = m_chunk if m is None else jnp.maximum(m, m_chunk)
                e = jnp.exp2(st - m_new).astype(bf16)
                pv = jnp.dot(values_t, e, preferred_element_type=f32)
                acc = pv if acc is None else acc * jnp.exp2(m - m_new) + pv
                m = m_new
            if lo + kc == n_keys:
                on = (acc[0:HEAD_DIM] * (1.0 / acc[HEAD_DIM:HEAD_DIM + 1])).astype(bf16)
                heads_t += [on[:, :ts], on[:, ts:]]
                m = acc = None
        return jnp.concatenate(heads_t, axis=0)

    def project(rows, o_att_t):
        o_sum = of_ref[0, rows] + ob_ref[0, rows]
        normed = []
        for hd in range(HG_HEADS):
            oh = o_sum[:, hd * HG_HEAD_DIM:(hd + 1) * HG_HEAD_DIM]
            ms = jnp.mean(oh * oh, axis=-1, keepdims=True)
            normed.append(oh * lax.rsqrt(ms + RMS_EPS))
        o_hg = (jnp.concatenate(normed, axis=-1) * hg_ref[...] * sg_ref[0, rows]).astype(bf16)
        y = (jnp.dot(o_hg, wo_ref[0:HG_WIDTH, :], preferred_element_type=f32)
             + _dot_tn(o_att_t, wo_ref[HG_WIDTH:HG_WIDTH + ATT_WIDTH, :]))
        r = ALPHA * x_ref[0, rows] + m_ref[0, 5:6, :] * y
        o_ref[0, rows] = _layer_norm(r, g_ref[1:2, :], b_ref[1:2, :])

    def run(fixed_shift):
        for lo in range(0, tq, ts):
            rows = slice(lo, lo + ts)
            project(rows, attend(rows, fixed_shift))

    use_bound = fixed_ref[0] != 0
    pl.when(use_bound)(functools.partial(run, True))
    pl.when(jnp.logical_not(use_bound))(functools.partial(run, False))


def _attn(qa, ka, vt, ctx_kv_t, x, o_f, o_b, sg, mod, mod_group, w_out, hg_gain, q_gain, ln_g, ln_b):
    groups, length, _ = qa.shape
    tq = min(Q_TILE, length)
    n_ctx = 0 if ctx_kv_t is None else ctx_kv_t[0].shape[2]
    n_keys = n_ctx + length
    tok = lambda width: pl.BlockSpec((1, tq, width), lambda g, t: (g, t, 0))
    whole_t = lambda n: pl.BlockSpec((1, KV_WIDTH, n), lambda g, t: (g, 0, 0))
    in_specs = [tok(ATT_WIDTH), pl.BlockSpec((1, length, KV_WIDTH), lambda g, t: (g, 0, 0)), whole_t(length)]
    args = [qa, ka, vt]
    if n_ctx:
        in_specs += [whole_t(n_ctx), whole_t(n_ctx)]
        args += list(ctx_kv_t)
    in_specs += [
        tok(D_MODEL), tok(HG_WIDTH), tok(HG_WIDTH), tok(HG_WIDTH),
        pl.BlockSpec((1, N_MOD, D_MODEL), lambda g, t: (mod_group(g), 0, 0)),
        _resident((D_MODEL, D_MODEL)),
        _resident((1, HG_WIDTH)),
        _resident((1, ATT_WIDTH)),
        _resident((3, D_MODEL)),
        _resident((3, D_MODEL)),
    ]
    args += [x, o_f, o_b, sg, mod, w_out, hg_gain, q_gain, ln_g, ln_b]
    return pl.pallas_call(
        functools.partial(_attn_kernel, n_ctx=n_ctx),
        grid=(groups, length // tq),
        in_specs=in_specs,
        out_specs=tok(D_MODEL),
        out_shape=jax.ShapeDtypeStruct(x.shape, f32),
        scratch_shapes=[pltpu.VMEM((n_keys, N_KV_HEADS * LANES), bf16),
                        pltpu.VMEM((N_KV_HEADS * V_ROWS, n_keys), bf16),
                        pltpu.SMEM((N_KV_HEADS,), f32),
                        pltpu.SMEM((1,), jnp.int32)],
        compiler_params=pltpu.CompilerParams(
            dimension_semantics=("arbitrary", "arbitrary"), vmem_limit_bytes=VMEM_LIMIT_BYTES),
        name="attn_out",
    )(*args)


def _rope_tables(n_tokens):
    half = HEAD_DIM // 2
    t = jnp.arange(n_tokens)
    inv = ROPE_THETA ** (-jnp.arange(0, half, 2, dtype=f32) / half)
    ang_row = (t // GRID_W).astype(f32)[:, None] * inv
    ang_col = (t % GRID_W).astype(f32)[:, None] * inv
    cos = jnp.concatenate([jnp.cos(ang_row)] * 2 + [jnp.cos(ang_col)] * 2, axis=-1)
    sin = jnp.concatenate([-jnp.sin(ang_row), jnp.sin(ang_row), -jnp.sin(ang_col), jnp.sin(ang_col)], axis=-1)
    return jnp.tile(cos, (1, LANES // HEAD_DIM)), jnp.tile(sin, (1, LANES // HEAD_DIM))


def kernel(x_prompt, x_sample, cache_k, cache_v, state_hgrn_fwd, state_hgrn_bwd, c, c_ctx, w_mod, b_mod,
           w_ffn1_in, w_ffn1_out, w_ffn2_in, w_ffn2_out, w_in, w_out, q_norm_g, k_norm_g, hg_norm_g,
           lb_logits_fwd, lb_logits_bwd, ln_g, ln_b):
    assert w_mod.shape[0] == DEPTH and lb_logits_fwd.shape[0] == DEPTH + 1
    batch, seq, _ = x_prompt.shape
    dec_batch, dec_seq, _ = x_sample.shape
    past = cache_k.shape[2]

    ctx_row = dec_batch
    rows = 16
    cvecs = jnp.concatenate([c, c_ctx[None, :], jnp.zeros((rows - dec_batch - 1, D_MODEL), f32)], axis=0)
    mod = _modulation(cvecs, w_mod[0], b_mod[0]).reshape(rows, N_MOD, D_MODEL)

    to_bf16 = lambda w: w.astype(bf16)
    w1u, w2u = to_bf16(w_ffn1_in[0]), to_bf16(w_ffn2_in[0])
    w1d, w2d = to_bf16(w_ffn1_out[0]), to_bf16(w_ffn2_out[0])
    w_in_b, w_out_b = to_bf16(w_in[0]), to_bf16(w_out[0])
    gains = ln_g[0], ln_b[0]
    q_gain = jnp.tile(q_norm_g[0], N_HEADS).reshape(1, ATT_WIDTH)
    k_gain = jnp.tile(k_norm_g[0], N_KV_HEADS).reshape(1, KV_WIDTH)
    hg_gain = hg_norm_g[0].reshape(1, HG_WIDTH)
    head_of = jnp.arange(ATT_WIDTH) // HEAD_DIM
    ones_bd = (head_of[:, None] == head_of[None, :]).astype(bf16)

    def trunk(x, mod_group, rope_tables, ctx_kv_t, init_states, hg_groups):
        shape = x.shape
        seq_len = shape[0] * shape[1] // hg_groups
        per_seq = lambda a: a.reshape(hg_groups, seq_len, a.shape[-1])
        x = _ffn(x, mod, mod_group, w1u, w1d, *gains, mod_base=0, ln_row=0)
        proj = _proj(x, mod, mod_group, w_in_b, lb_logits_fwd, lb_logits_bwd, q_gain, k_gain, ones_bd, rope_tables,
                     seq_len)
        qs, vh, f_f, f_b, sg, qa, ka = map(per_seq, proj[:7])
        vt, span_f, span_b = proj[7], proj[8].reshape(hg_groups, -1, LANES), proj[9].reshape(hg_groups, -1, LANES)
        scans = _hgrn(qs, vh, f_f, f_b, span_f, span_b, init_states)
        x = _attn(qa, ka, vt, ctx_kv_t, per_seq(x), scans[0], scans[1], sg, mod, mod_group, w_out_b, hg_gain,
                  q_gain, *gains).reshape(shape)
        x = _ffn(x, mod, mod_group, w2u, w2d, *gains, mod_base=6, ln_row=2)
        return x, proj[10:], vt, scans[2:]

    ctx_group = lambda g: ctx_row
    y_prompt, (kt_new,), vt_new, states = trunk(
        x_prompt.reshape(1, batch * seq, D_MODEL), ctx_group, None, None, None, batch)
    y_prompt = y_prompt.reshape(batch, seq, D_MODEL)
    to_cache = lambda t: t.reshape(batch, DEPTH, N_KV_HEADS, HEAD_DIM, seq).transpose(0, 1, 4, 2, 3)
    new_cache_k, new_cache_v = to_cache(kt_new), to_cache(vt_new)
    new_state_fwd = states[0].reshape(batch, DEPTH, HG_HEADS, HG_HEAD_DIM, HG_HEAD_DIM)
    new_state_bwd = states[1].reshape(batch, DEPTH, HG_HEADS, HG_HEAD_DIM, HG_HEAD_DIM)

    from_cache = lambda t: t[:, 0].transpose(0, 2, 3, 1).reshape(dec_batch, KV_WIDTH, past)
    init_states = (state_hgrn_fwd[:, 0], state_hgrn_bwd[:, 0])
    y_sample, _, _, _ = trunk(x_sample, lambda g: g, _rope_tables(dec_seq), (from_cache(cache_k), from_cache(cache_v)),
                              init_states, dec_batch)

    return (y_prompt, y_sample, new_cache_k, new_cache_v, new_state_fwd, new_state_bwd)
```

```python
import functools

import jax
import jax.numpy as jnp
from jax import lax
from jax.experimental import pallas as pl
from jax.experimental.pallas import tpu as pltpu

f32 = jnp.float32
bf16 = jnp.bfloat16

D_MODEL = 1024
N_MOD = 9
HG_WIDTH = 512
HG_HEAD_DIM = 128
HG_HEADS = 4
ATT_WIDTH = 512
HEAD_DIM = 64
N_HEADS = 8
N_KV_HEADS = 2
KV_WIDTH = 128
IN_WIDTH = 5 * HG_WIDTH + ATT_WIDTH + 2 * KV_WIDTH
D_FF = 2816
GRID_W = 64
ROPE_THETA = 10000.0
DEPTH = 1
ALPHA = (2.0 * DEPTH) ** 0.25
LOG2_E = 1.4426950408889634
LN_EPS = 1e-6
RMS_EPS = 1e-6

LANES = 128
BF16_SUBLANES = 16
VMEM_LIMIT_BYTES = 56 * 1024 * 1024

FF_CHUNK = 256
FFN_TILE = 1024
FFN_SUB = 512
TOKEN_TILE = 512
PROJ_SUB = 256
HG_TILE = 128
HG_STEP = 512
Q_TILE = 512
Q_SUB = 256
KEY_CHUNK = 512
SCORE_BOUND_LIMIT = 40.0
SCORE_BOUND_SLACK = 1.01
V_ROWS = 80
HG_FAST_BLOCK = 64
HG_FAST_SPAN_LOG2 = 100.0


def _silu(x):
    return x * jax.nn.sigmoid(x)


def _layer_norm(r, g, b):
    mu = jnp.mean(r, axis=-1, keepdims=True)
    c = r - mu
    var = jnp.mean(c * c, axis=-1, keepdims=True)
    return c * lax.rsqrt(var + LN_EPS) * g + b


def _split_bf16(x, parts):
    out = []
    r = x
    for _ in range(parts - 1):
        p = r.astype(bf16)
        out.append(p)
        r = r - p.astype(f32)
    out.append(r.astype(bf16))
    return out


def _dot01(mat01, x, parts, *, mat_on_left):
    pieces = _split_bf16(x, parts)
    if mat_on_left:
        return jnp.dot(jnp.concatenate([mat01] * parts, axis=1), jnp.concatenate(pieces, axis=0),
                       preferred_element_type=f32)
    return jnp.dot(jnp.concatenate(pieces, axis=1), jnp.concatenate([mat01] * parts, axis=0),
                   preferred_element_type=f32)


def _dot_nt(a, b):
    return lax.dot_general(a, b, (((1,), (1,)), ((), ())), preferred_element_type=f32)


def _dot_tn(a, b):
    return lax.dot_general(a, b, (((0,), (0,)), ((), ())), preferred_element_type=f32)


def _mod_kernel(c_ref, w_ref, b_ref, o_ref):
    a = _silu(c_ref[...]).astype(bf16)
    o_ref[...] = jnp.dot(a, w_ref[...].astype(bf16), preferred_element_type=f32) + b_ref[...]


def _modulation(cvecs, w_mod, b_mod):
    rows = cvecs.shape[0]
    n_out = w_mod.shape[1]
    tn = D_MODEL
    return pl.pallas_call(
        _mod_kernel,
        grid=(n_out // tn,),
        in_specs=[
            pl.BlockSpec((rows, D_MODEL), lambda j: (0, 0)),
            pl.BlockSpec((D_MODEL, tn), lambda j: (0, j)),
            pl.BlockSpec((1, tn), lambda j: (0, j)),
        ],
        out_specs=pl.BlockSpec((rows, tn), lambda j: (0, j)),
        out_shape=jax.ShapeDtypeStruct((rows, n_out), f32),
        compiler_params=pltpu.CompilerParams(dimension_semantics=("arbitrary",)),
        name="modulation",
    )(cvecs, w_mod, b_mod.reshape(1, n_out))


def _ffn_kernel(x_ref, m_ref, wup_ref, wd_ref, g_ref, b_ref, o_ref, act_ref, *, mod_base, ln_row):
    shift = m_ref[0, mod_base:mod_base + 1, :]
    scale = m_ref[0, mod_base + 1:mod_base + 2, :]
    gate = m_ref[0, mod_base + 2:mod_base + 3, :]
    subs = [slice(lo, lo + FFN_SUB) for lo in range(0, x_ref.shape[1], FFN_SUB)]
    xs = [x_ref[0, rows] for rows in subs]
    hs = [(x * (1.0 + scale) + shift).astype(bf16) for x in xs]
    for j in range(D_FF // FF_CHUNK):
        cols = slice(j * FF_CHUNK, (j + 1) * FF_CHUNK)
        for rows, h in zip(subs, hs):
            a = jnp.dot(h, wup_ref[:, cols], preferred_element_type=f32)
            u = jnp.dot(h, wup_ref[:, D_FF + j * FF_CHUNK:D_FF + (j + 1) * FF_CHUNK],
                        preferred_element_type=f32)
            act_ref[rows, cols] = (_silu(a) * u).astype(bf16)
    ys = [jnp.dot(act_ref[rows, :], wd_ref[...], preferred_element_type=f32) for rows in subs]
    for rows, x, y in zip(subs, xs, ys):
        r = ALPHA * x + 0.5 * gate * y
        o_ref[0, rows] = _layer_norm(r, g_ref[ln_row:ln_row + 1, :], b_ref[ln_row:ln_row + 1, :])


def _resident(shape):
    return pl.BlockSpec(shape, lambda *_: (0,) * len(shape), pipeline_mode=pl.Buffered(1))


def _ffn(x, mod, mod_group, w_up, wd, ln_g, ln_b, *, mod_base, ln_row):
    groups, length, _ = x.shape
    tm = FFN_TILE
    return pl.pallas_call(
        functools.partial(_ffn_kernel, mod_base=mod_base, ln_row=ln_row),
        grid=(groups, length // tm),
        in_specs=[
            pl.BlockSpec((1, tm, D_MODEL), lambda g, t: (g, t, 0)),
            pl.BlockSpec((1, N_MOD, D_MODEL), lambda g, t: (mod_group(g), 0, 0)),
            _resident((D_MODEL, 2 * D_FF)),
            _resident((D_FF, D_MODEL)),
            _resident((3, D_MODEL)),
            _resident((3, D_MODEL)),
        ],
        out_specs=pl.BlockSpec((1, tm, D_MODEL), lambda g, t: (g, t, 0)),
        out_shape=jax.ShapeDtypeStruct(x.shape, f32),
        scratch_shapes=[pltpu.VMEM((tm, D_FF), bf16)],
        compiler_params=pltpu.CompilerParams(
            dimension_semantics=("arbitrary", "arbitrary"), vmem_limit_bytes=VMEM_LIMIT_BYTES),
        name="ffn",
    )(x, mod, w_up, wd, ln_g, ln_b)


def _head_rms_norm(x, ones_bd, gain):
    ss = _dot01(ones_bd, x * x, 2, mat_on_left=False)
    return x * lax.rsqrt(ss * (1.0 / HEAD_DIM) + RMS_EPS) * gain


def _rope(x, cos, sin_signed):
    width = x.shape[-1]
    lane = lax.broadcasted_iota(jnp.int32, x.shape, 1)
    from_right = pltpu.roll(x, width - 16, 1)
    from_left = pltpu.roll(x, 16, 1)
    partner = jnp.where((lane & 31) < 16, from_right, from_left)
    return x * cos + partner * sin_signed


def _proj_kernel(*refs, rope):
    if rope:
        (x_ref, m_ref, w_ref, lbf_ref, lbb_ref, qg_ref, kg_ref, bd_ref, cos_ref, sin_ref,
         qs_ref, vh_ref, ff_ref, fb_ref, sg_ref, qa_ref, ka_ref, vt_ref,
         spf_ref, spb_ref) = refs
    else:
        (x_ref, m_ref, w_ref, lbf_ref, lbb_ref, qg_ref, kg_ref, bd_ref,
         qs_ref, vh_ref, ff_ref, fb_ref, sg_ref, qa_ref, ka_ref, vt_ref,
         spf_ref, spb_ref, kt_ref) = refs
    shift = m_ref[0, 3:4, :]
    scale = m_ref[0, 4:5, :]
    subs = [slice(lo, lo + PROJ_SUB) for lo in range(0, x_ref.shape[1], PROJ_SUB)]
    hs = [(x_ref[0, rows] * (1.0 + scale) + shift).astype(bf16) for rows in subs]

    def cols(h, lo, width):
        return jnp.dot(h, w_ref[:, lo:lo + width], preferred_element_type=f32)

    def lower_bound(lb_ref):
        l0 = lb_ref[0:1, :]
        l1 = lb_ref[1:2, :]
        m = jnp.maximum(l0, l1)
        e0 = jnp.exp(l0 - m)
        e1 = jnp.exp(l1 - m)
        return e0 / (e0 + e1)

    half = HG_FAST_BLOCK // 2

    def forget_gate(lb_ref, lo, f_ref, span_ref):
        lb = lower_bound(lb_ref)
        for rows, h in zip(subs, hs):
            f = lb + (1.0 - lb) * jax.nn.sigmoid(cols(h, lo, HG_WIDTH))
            f_ref[0, rows] = f
            sums = jnp.sum(jnp.log2(f).reshape(PROJ_SUB // half, half, HG_WIDTH), axis=1)
            span_ref[0, rows.start // half:rows.stop // half] = jnp.broadcast_to(
                jnp.max(jnp.abs(sums), axis=-1, keepdims=True), (PROJ_SUB // half, LANES))

    for rows, h in zip(subs, hs):
        qs_ref[0, rows] = _silu(cols(h, 0, HG_WIDTH))
    for rows, h in zip(subs, hs):
        vh_ref[0, rows] = cols(h, HG_WIDTH, HG_WIDTH).astype(bf16)
    forget_gate(lbf_ref, 2 * HG_WIDTH, ff_ref, spf_ref)
    forget_gate(lbb_ref, 3 * HG_WIDTH, fb_ref, spb_ref)
    for rows, h in zip(subs, hs):
        sg_ref[0, rows] = _silu(cols(h, 4 * HG_WIDTH, HG_WIDTH))

    base = 5 * HG_WIDTH
    piece = vt_ref.shape[2]
    for rows, h in zip(subs, hs):
        q = _head_rms_norm(cols(h, base, ATT_WIDTH), bd_ref[...], qg_ref[...])
        k = _head_rms_norm(cols(h, base + ATT_WIDTH, KV_WIDTH), bd_ref[0:KV_WIDTH, 0:KV_WIDTH], kg_ref[...])
        if rope:
            cos, sin = cos_ref[rows, :], sin_ref[rows, :]
            q = _rope(q, jnp.concatenate([cos] * (ATT_WIDTH // LANES), axis=1),
                      jnp.concatenate([sin] * (ATT_WIDTH // LANES), axis=1))
            k = _rope(k, cos, sin)
        qa_ref[0, rows] = (q * (HEAD_DIM ** -0.5 * LOG2_E)).astype(bf16)
        ka_ref[0, rows] = k
        v = cols(h, base + ATT_WIDTH + KV_WIDTH, KV_WIDTH)
        for lo in range(rows.start, rows.stop, min(piece, PROJ_SUB)):
            n = min(piece, PROJ_SUB)
            dst = (lo // piece, slice(None), slice(lo % piece, lo % piece + n))
            vt_ref[dst] = v[lo - rows.start:lo - rows.start + n].T
            if not rope:
                kt_ref[dst] = k[lo - rows.start:lo - rows.start + n].T


def _proj(x, mod, mod_group, w_in, lb_f, lb_b, q_gain, k_gain, ones_bd, rope_tables, seq_len):
    groups, length, _ = x.shape
    tm = TOKEN_TILE
    rope = rope_tables is not None
    n_seq = groups * length // seq_len
    if seq_len >= tm:
        tiles_per_seq = seq_len // tm
        t_spec = pl.BlockSpec((1, KV_WIDTH, tm), lambda g, t: (g * (length // seq_len) + t // tiles_per_seq, 0,
                                                               t % tiles_per_seq))
    else:
        t_spec = pl.BlockSpec((tm // seq_len, KV_WIDTH, seq_len), lambda g, t: (g * (length // tm) + t, 0, 0))
    t_shape = jax.ShapeDtypeStruct((n_seq, KV_WIDTH, seq_len), f32)
    tok = lambda width: pl.BlockSpec((1, tm, width), lambda g, t: (g, t, 0))
    in_specs = [
        tok(D_MODEL),
        pl.BlockSpec((1, N_MOD, D_MODEL), lambda g, t: (mod_group(g), 0, 0)),
        _resident((D_MODEL, IN_WIDTH)),
        _resident((2, HG_WIDTH)),
        _resident((2, HG_WIDTH)),
        _resident((1, ATT_WIDTH)),
        _resident((1, KV_WIDTH)),
        _resident((ATT_WIDTH, ATT_WIDTH)),
    ]
    args = [x, mod, w_in, lb_f, lb_b, q_gain, k_gain, ones_bd]
    if rope:
        in_specs += [pl.BlockSpec((tm, LANES), lambda g, t: (t, 0))] * 2
        args += list(rope_tables)
    shape = lambda width, dt: jax.ShapeDtypeStruct((groups, length, width), dt)
    half = HG_FAST_BLOCK // 2
    span_spec = pl.BlockSpec((1, tm // half, LANES), lambda g, t: (g, t, 0))
    span_shape = jax.ShapeDtypeStruct((groups, length // half, LANES), f32)
    return pl.pallas_call(
        functools.partial(_proj_kernel, rope=rope),
        grid=(groups, length // tm),
        in_specs=in_specs,
        out_specs=([tok(HG_WIDTH)] * 5 + [tok(ATT_WIDTH), tok(KV_WIDTH), t_spec, span_spec, span_spec]
                   + ([] if rope else [t_spec])),
        out_shape=([shape(HG_WIDTH, f32), shape(HG_WIDTH, bf16)] + [shape(HG_WIDTH, f32)] * 3
                   + [shape(ATT_WIDTH, bf16), shape(KV_WIDTH, f32), t_shape, span_shape, span_shape]
                   + ([] if rope else [t_shape])),
        compiler_params=pltpu.CompilerParams(
            dimension_semantics=("arbitrary", "arbitrary"), vmem_limit_bytes=VMEM_LIMIT_BYTES),
        name="mixer_proj",
    )(*args)


def _hgrn_direction(qs, vb, f, st_ref, o_ref, rows, *, rev):
    tc = qs.shape[0]
    lf = jnp.log2(f)
    kk = 1.0 - f
    row = lax.broadcasted_iota(jnp.int32, (tc, tc), 0)
    col = lax.broadcasted_iota(jnp.int32, (tc, tc), 1)
    ordered = (row < col) if rev else (row > col)
    differ = row ^ col
    tri = jnp.where((col >= row) if rev else (col <= row), 1.0, 0.0).astype(bf16)
    cum = _dot01(tri, lf, 2, mat_on_left=True)
    total = cum[0:1] if rev else cum[tc - 1:tc]
    carry = jnp.exp2(total)
    query_half = 0 if rev else 1
    heads = [slice(hd * HG_HEAD_DIM, (hd + 1) * HG_HEAD_DIM) for hd in range(HG_HEADS)]

    def level_exponent(s, ridx):
        if s == 1:
            return jnp.where((ridx & 1) == query_half, lf, 0.0)
        if s == 2:
            nxt = pltpu.roll(lf, tc - 1, 0)
            prv = pltpu.roll(lf, 1, 0)
            m4 = ridx & 3
            if rev:
                return jnp.where(m4 == 0, lf + nxt, jnp.where(m4 == 1, lf, jnp.where(m4 == 2, 0.0, prv)))
            return jnp.where(m4 == 0, nxt, jnp.where(m4 == 1, 0.0, jnp.where(m4 == 2, lf, lf + prv)))
        blocks = []
        for lo in range(0, tc, 2 * s):
            anchor = lo + (s if rev else s - 1)
            blocks.append(cum[lo:lo + 2 * s] - cum[anchor:anchor + 1])
        d = jnp.concatenate(blocks, axis=0)
        is_query = ((ridx >> (s.bit_length() - 1)) & 1) == query_half
        return jnp.where(is_query, d, -d)

    def add_levels(intra, first):
        ridx = lax.broadcasted_iota(jnp.int32, qs.shape, 0)
        s = first
        while s < tc:
            shift = s.bit_length() - 1
            is_query = ((ridx >> shift) & 1) == query_half
            mixed = (jnp.where(is_query, qs, kk) * jnp.exp2(level_exponent(s, ridx))).astype(bf16)
            pair = jnp.logical_and((differ >> shift) == 1, ordered)
            for hd, hs in enumerate(heads):
                intra[hd] = jnp.where(pair, _dot_nt(mixed[:, hs], mixed[:, hs]), intra[hd])
            s *= 2
        return intra

    def finish(intra, q_dec, k_end, extra):
        for hd, hs in enumerate(heads):
            st = st_ref[hd]
            o = jnp.dot(jnp.concatenate([intra[hd].astype(bf16), q_dec[:, hs]], axis=1),
                        jnp.concatenate([vb[:, hs], st.T.astype(bf16)], axis=0), preferred_element_type=f32)
            o_ref[0, rows, hs] = o if extra is None else o + extra[:, hs]
            st_ref[hd] = st * carry[:, hs] + _dot_tn(vb[:, hs], k_end[:, hs])

    blk = HG_FAST_BLOCK
    anchors = [lo + (blk // 2 if rev else blk // 2 - 1) for lo in range(0, tc, blk)]

    def mid_split_blocks():
        q_mid, k_mid, q_dec, k_end = [], [], [], []
        for lo, a in zip(range(0, tc, blk), anchors):
            rel = cum[lo:lo + blk] - cum[a:a + 1]
            q_mid.append(qs[lo:lo + blk] * jnp.exp2(rel))
            k_mid.append(kk[lo:lo + blk] * jnp.exp2(-rel))
            q_dec.append(q_mid[-1] * jnp.exp2(cum[a:a + 1]))
            k_end.append(k_mid[-1] * jnp.exp2(total - cum[a:a + 1]))
        q_mid, k_mid, q_dec, k_end = (jnp.concatenate(p, axis=0).astype(bf16) for p in (q_mid, k_mid, q_dec, k_end))
        same_block = (differ >> (blk.bit_length() - 1)) == 0
        keep = jnp.logical_and(same_block, (row <= col) if rev else (row >= col))
        intra = [jnp.where(keep, _dot_nt(q_mid[:, hs], k_mid[:, hs]), 0.0) for hs in heads]
        finish(add_levels(intra, blk), q_dec, k_end, None)

    def all_levels():
        intra = add_levels([jnp.zeros((tc, tc), f32)] * HG_HEADS, 1)
        q_dec = (qs * jnp.exp2(cum)).astype(bf16)
        k_end = (kk * jnp.exp2(total - cum)).astype(bf16)
        qk = qs * kk
        own = jnp.concatenate(
            [jnp.sum(qk[:, hs], axis=-1, keepdims=True) * vb[:, hs].astype(f32) for hs in heads], axis=-1)
        finish(intra, q_dec, k_end, own)

    return mid_split_blocks, all_levels


def _hgrn_kernel(slow_ref, *refs, has_init):
    if has_init:
        (qsf_ref, vf_ref, ff_ref, qsb_ref, vb_ref, fb_ref, s0f_ref, s0b_ref,
         of_ref, ob_ref, stf_ref, stb_ref) = refs
    else:
        (qsf_ref, vf_ref, ff_ref, qsb_ref, vb_ref, fb_ref,
         of_ref, ob_ref, sf_out_ref, sb_out_ref, stf_ref, stb_ref) = refs
    t = pl.program_id(1)

    @pl.when(t == 0)
    def _():
        for hd in range(HG_HEADS):
            if has_init:
                stf_ref[hd] = s0f_ref[0, hd].T
                stb_ref[hd] = s0b_ref[0, hd].T
            else:
                stf_ref[hd] = jnp.zeros((HG_HEAD_DIM, HG_HEAD_DIM), f32)
                stb_ref[hd] = jnp.zeros((HG_HEAD_DIM, HG_HEAD_DIM), f32)

    chunks = [slice(lo, lo + HG_TILE) for lo in range(0, qsf_ref.shape[1], HG_TILE)]
    scans = []
    for r_f, r_b in zip(chunks, reversed(chunks)):
        scans.append(_hgrn_direction(qsf_ref[0, r_f], vf_ref[0, r_f], ff_ref[0, r_f], stf_ref, of_ref, r_f, rev=False))
        scans.append(_hgrn_direction(qsb_ref[0, r_b], vb_ref[0, r_b], fb_ref[0, r_b], stb_ref, ob_ref, r_b, rev=True))
    slow = slow_ref[pl.program_id(0) * pl.num_programs(1) + t] != 0

    @pl.when(jnp.logical_not(slow))
    def _():
        for mid_split_blocks, _ in scans:
            mid_split_blocks()

    @pl.when(slow)
    def _():
        for _, all_levels in scans:
            all_levels()

    if not has_init:
        @pl.when(t == pl.num_programs(1) - 1)
        def _():
            for hd in range(HG_HEADS):
                sf_out_ref[0, hd] = stf_ref[hd].T
                sb_out_ref[0, hd] = stb_ref[hd].T


def _hgrn(qs, vh, f_fwd, f_bwd, span_fwd, span_bwd, init_states):
    groups, length, _ = qs.shape
    step = min(HG_STEP, length)
    nt = length // step
    has_init = init_states is not None
    per_step = lambda span: jnp.max(span[:, :, 0].reshape(groups, nt, -1), axis=-1) > HG_FAST_SPAN_LOG2
    slow = jnp.logical_or(per_step(span_fwd), per_step(span_bwd)[:, ::-1]).astype(jnp.int32).reshape(-1)
    fwd = pl.BlockSpec((1, step, HG_WIDTH), lambda g, t, _: (g, t, 0))
    bwd = pl.BlockSpec((1, step, HG_WIDTH), lambda g, t, _: (g, nt - 1 - t, 0))
    state = pl.BlockSpec((1, HG_HEADS, HG_HEAD_DIM, HG_HEAD_DIM), lambda g, t, _: (g, 0, 0, 0))
    in_specs = [fwd, fwd, fwd, bwd, bwd, bwd]
    args = [qs, vh, f_fwd, qs, vh, f_bwd]
    out_specs = [fwd, bwd]
    out_shape = [jax.ShapeDtypeStruct(qs.shape, f32)] * 2
    if has_init:
        in_specs += [state, state]
        args += list(init_states)
    else:
        out_specs += [state, state]
        out_shape += [jax.ShapeDtypeStruct((groups, HG_HEADS, HG_HEAD_DIM, HG_HEAD_DIM), f32)] * 2
    return pl.pallas_call(
        functools.partial(_hgrn_kernel, has_init=has_init),
        grid_spec=pltpu.PrefetchScalarGridSpec(
            num_scalar_prefetch=1,
            grid=(groups, nt),
            in_specs=in_specs,
            out_specs=out_specs,
            scratch_shapes=[pltpu.VMEM((HG_HEADS, HG_HEAD_DIM, HG_HEAD_DIM), f32)] * 2),
        out_shape=out_shape,
        compiler_params=pltpu.CompilerParams(
            dimension_semantics=("arbitrary", "arbitrary"), vmem_limit_bytes=VMEM_LIMIT_BYTES),
        name="hgrn2",
    )(slow, *args)


def _attn_kernel(*refs, n_ctx, n_cast):
    n_in = (15 if n_ctx else 13) + n_cast
    cast_in, cast_out = refs[n_in - n_cast:n_in], refs[n_in + 1:n_in + 1 + n_cast]
    kk_ref, vta_ref, bound_ref, fixed_ref = refs[n_in + 1 + n_cast:]
    o_ref = refs[n_in]
    if n_ctx:
        (qa_ref, ka_ref, vt_ref, ckt_ref, cvt_ref, x_ref, of_ref, ob_ref, sg_ref, m_ref, wo_ref, hg_ref, qg_ref,
         g_ref, b_ref) = refs[:n_in - n_cast]
    else:
        (qa_ref, ka_ref, vt_ref, x_ref, of_ref, ob_ref, sg_ref, m_ref, wo_ref, hg_ref, qg_ref,
         g_ref, b_ref) = refs[:n_in - n_cast]
    for src_ref, dst_ref in zip(cast_in, cast_out):
        dst_ref[...] = src_ref[...].astype(bf16)
    tq = qa_ref.shape[1]
    n_keys = kk_ref.shape[0]

    @pl.when(pl.program_id(1) == 0)
    def _():
        def fill_keys(lo, k):
            n = k.shape[0]
            low = lax.broadcasted_iota(jnp.int32, k.shape, 1) < HEAD_DIM
            k_sw = pltpu.roll(k, HEAD_DIM, 1)
            kk_ref[lo:lo + n, 0:LANES] = jnp.where(low, k, k_sw).astype(bf16)
            kk_ref[lo:lo + n, LANES:2 * LANES] = jnp.where(low, k_sw, k).astype(bf16)

        def fill_values(lo, vt):
            n = vt.shape[1]
            ones_row = jnp.where(lax.broadcasted_iota(jnp.int32, (V_ROWS - HEAD_DIM, n), 0) == 0, 1.0, 0.0)
            for kv in range(N_KV_HEADS):
                vta_ref[kv * V_ROWS:kv * V_ROWS + HEAD_DIM, lo:lo + n] = (
                    vt[kv * HEAD_DIM:(kv + 1) * HEAD_DIM].astype(bf16))
                vta_ref[kv * V_ROWS + HEAD_DIM:(kv + 1) * V_ROWS, lo:lo + n] = ones_row.astype(bf16)

        if n_ctx:
            fill_keys(0, ckt_ref[0].T)
            fill_values(0, cvt_ref[0])
        fill_keys(n_ctx, ka_ref[0])
        fill_values(n_ctx, vt_ref[0])

        gain_max = jnp.max(jnp.abs(qg_ref[...]), axis=1, keepdims=True)
        all_small = None
        for kv in range(N_KV_HEADS):
            kt = kk_ref[:, kv * LANES:(kv + 1) * LANES].astype(f32)
            k_norm2 = jnp.max(0.5 * jnp.sum(kt * kt, axis=1, keepdims=True), axis=0, keepdims=True)
            bound = (LOG2_E * SCORE_BOUND_SLACK) * gain_max * jnp.sqrt(k_norm2)
            bound_ref[kv] = bound[0, 0]
            small = jnp.where(bound <= SCORE_BOUND_LIMIT, 1, 0)
            all_small = small if all_small is None else all_small * small
        fixed_ref[0] = all_small[0, 0]

    ts = min(Q_SUB, tq)
    low = lax.broadcasted_iota(jnp.int32, (ts, LANES), 1) < HEAD_DIM
    kc = min(KEY_CHUNK, n_keys)
    pairs_per_kv = N_HEADS // N_KV_HEADS // 2

    def attend(rows, fixed_shift):
        def masked_pair(tile):
            qp = qa_ref[0, rows, tile * LANES:(tile + 1) * LANES]
            zero = jnp.zeros_like(qp)
            return jnp.concatenate([jnp.where(low, qp, zero), jnp.where(low, zero, qp)], axis=0)

        q_pairs = [masked_pair(tile) for tile in range(N_HEADS // 2)]
        work = [(tile, lo) for tile in range(N_HEADS // 2) for lo in range(0, n_keys, kc)]

        def scores(tile, lo):
            kv = tile // pairs_per_kv
            return _dot_nt(kk_ref[lo:lo + kc, kv * LANES:(kv + 1) * LANES], q_pairs[tile])

        heads_t = []
        st_next = scores(*work[0])
        m = acc = None
        for i, (tile, lo) in enumerate(work):
            st = st_next
            if i + 1 < len(work):
                st_next = scores(*work[i + 1])
            kv = tile // pairs_per_kv
            values_t = vta_ref[kv * V_ROWS:(kv + 1) * V_ROWS, lo:lo + kc]
            if fixed_shift:
                e = jnp.exp2(st - bound_ref[kv]).astype(bf16)
                pv = jnp.dot(values_t, e, preferred_element_type=f32)
                acc = pv if acc is None else acc + pv
            else:
                m_chunk = jnp.max(st, axis=0, keepdims=True)
                m_new = m_chunk if m is None else jnp.maximum(m, m_chunk)
                e = jnp.exp2(st - m_new).astype(bf16)
                pv = jnp.dot(values_t, e, preferred_element_type=f32)
                acc = pv if acc is None else acc * jnp.exp2(m - m_new) + pv
                m = m_new
            if lo + kc == n_keys:
                on = (acc[0:HEAD_DIM] * (1.0 / acc[HEAD_DIM:HEAD_DIM + 1])).astype(bf16)
                heads_t += [on[:, :ts], on[:, ts:]]
                m = acc = None
        return jnp.concatenate(heads_t, axis=0)

    def project(rows, o_att_t):
        o_sum = of_ref[0, rows] + ob_ref[0, rows]
        normed = []
        for hd in range(HG_HEADS):
            oh = o_sum[:, hd * HG_HEAD_DIM:(hd + 1) * HG_HEAD_DIM]
            ms = jnp.mean(oh * oh, axis=-1, keepdims=True)
            normed.append(oh * lax.rsqrt(ms + RMS_EPS))
        o_hg = (jnp.concatenate(normed, axis=-1) * hg_ref[...] * sg_ref[0, rows]).astype(bf16)
        y = (jnp.dot(o_hg, wo_ref[0:HG_WIDTH, :], preferred_element_type=f32)
             + _dot_tn(o_att_t, wo_ref[HG_WIDTH:HG_WIDTH + ATT_WIDTH, :]))
        r = ALPHA * x_ref[0, rows] + m_ref[0, 5:6, :] * y
        o_ref[0, rows] = _layer_norm(r, g_ref[1:2, :], b_ref[1:2, :])

    def run(fixed_shift):
        for lo in range(0, tq, ts):
            rows = slice(lo, lo + ts)
            project(rows, attend(rows, fixed_shift))

    use_bound = fixed_ref[0] != 0
    pl.when(use_bound)(functools.partial(run, True))
    pl.when(jnp.logical_not(use_bound))(functools.partial(run, False))


def _attn(qa, ka, vt, ctx_kv_t, x, o_f, o_b, sg, mod, mod_group, w_out, hg_gain, q_gain, ln_g, ln_b, to_cast=()):
    groups, length, _ = qa.shape
    tq = min(Q_TILE, length)
    nt = length // tq
    n_ctx = 0 if ctx_kv_t is None else ctx_kv_t[0].shape[2]
    n_keys = n_ctx + length
    tok = lambda width: pl.BlockSpec((1, tq, width), lambda g, t: (g, t, 0))
    whole_t = lambda n: pl.BlockSpec((1, KV_WIDTH, n), lambda g, t: (g, 0, 0))
    in_specs = [tok(ATT_WIDTH), pl.BlockSpec((1, length, KV_WIDTH), lambda g, t: (g, 0, 0)), whole_t(length)]
    args = [qa, ka, vt]
    if n_ctx:
        in_specs += [whole_t(n_ctx), whole_t(n_ctx)]
        args += list(ctx_kv_t)
    in_specs += [
        tok(D_MODEL), tok(HG_WIDTH), tok(HG_WIDTH), tok(HG_WIDTH),
        pl.BlockSpec((1, N_MOD, D_MODEL), lambda g, t: (mod_group(g), 0, 0)),
        _resident((D_MODEL, D_MODEL)),
        _resident((1, HG_WIDTH)),
        _resident((1, ATT_WIDTH)),
        _resident((3, D_MODEL)),
        _resident((3, D_MODEL)),
    ]
    args += [x, o_f, o_b, sg, mod, w_out, hg_gain, q_gain, ln_g, ln_b]
    out_specs, out_shape = [tok(D_MODEL)], [jax.ShapeDtypeStruct(x.shape, f32)]
    for w in to_cast:
        rows = next(r for r in range(BF16_SUBLANES, w.shape[0] + 1, BF16_SUBLANES)
                    if w.shape[0] % r == 0 and w.shape[0] // r <= groups * nt)
        spec = pl.BlockSpec((rows, w.shape[1]),
                            lambda g, t, last=w.shape[0] // rows - 1: (jnp.minimum(g * nt + t, last), 0))
        in_specs.append(spec)
        args.append(w)
        out_specs.append(spec)
        out_shape.append(jax.ShapeDtypeStruct(w.shape, bf16))
    return pl.pallas_call(
        functools.partial(_attn_kernel, n_ctx=n_ctx, n_cast=len(to_cast)),
        grid=(groups, nt),
        in_specs=in_specs,
        out_specs=out_specs,
        out_shape=out_shape,
        scratch_shapes=[pltpu.VMEM((n_keys, N_KV_HEADS * LANES), bf16),
                        pltpu.VMEM((N_KV_HEADS * V_ROWS, n_keys), bf16),
                        pltpu.SMEM((N_KV_HEADS,), f32),
                        pltpu.SMEM((1,), jnp.int32)],
        compiler_params=pltpu.CompilerParams(
            dimension_semantics=("arbitrary", "arbitrary"), vmem_limit_bytes=VMEM_LIMIT_BYTES),
        name="attn_out",
    )(*args)


def _rope_tables(n_tokens):
    half = HEAD_DIM // 2
    t = jnp.arange(n_tokens)
    inv = ROPE_THETA ** (-jnp.arange(0, half, 2, dtype=f32) / half)
    ang_row = (t // GRID_W).astype(f32)[:, None] * inv
    ang_col = (t % GRID_W).astype(f32)[:, None] * inv
    cos = jnp.concatenate([jnp.cos(ang_row)] * 2 + [jnp.cos(ang_col)] * 2, axis=-1)
    sin = jnp.concatenate([-jnp.sin(ang_row), jnp.sin(ang_row), -jnp.sin(ang_col), jnp.sin(ang_col)], axis=-1)
    return jnp.tile(cos, (1, LANES // HEAD_DIM)), jnp.tile(sin, (1, LANES // HEAD_DIM))


def kernel(x_prompt, x_sample, cache_k, cache_v, state_hgrn_fwd, state_hgrn_bwd, c, c_ctx, w_mod, b_mod,
           w_ffn1_in, w_ffn1_out, w_ffn2_in, w_ffn2_out, w_in, w_out, q_norm_g, k_norm_g, hg_norm_g,
           lb_logits_fwd, lb_logits_bwd, ln_g, ln_b):
    assert w_mod.shape[0] == DEPTH and lb_logits_fwd.shape[0] == DEPTH + 1
    batch, seq, _ = x_prompt.shape
    dec_batch, dec_seq, _ = x_sample.shape
    past = cache_k.shape[2]

    ctx_row = dec_batch
    rows = 16
    cvecs = jnp.concatenate([c, c_ctx[None, :], jnp.zeros((rows - dec_batch - 1, D_MODEL), f32)], axis=0)
    mod = _modulation(cvecs, w_mod[0], b_mod[0]).reshape(rows, N_MOD, D_MODEL)

    to_bf16 = lambda w: w.astype(bf16)
    w1u, w1d = to_bf16(w_ffn1_in[0]), to_bf16(w_ffn1_out[0])
    w_in_b, w_out_b = to_bf16(w_in[0]), to_bf16(w_out[0])
    gains = ln_g[0], ln_b[0]
    q_gain = jnp.tile(q_norm_g[0], N_HEADS).reshape(1, ATT_WIDTH)
    k_gain = jnp.tile(k_norm_g[0], N_KV_HEADS).reshape(1, KV_WIDTH)
    hg_gain = hg_norm_g[0].reshape(1, HG_WIDTH)
    head_of = jnp.arange(ATT_WIDTH) // HEAD_DIM
    ones_bd = (head_of[:, None] == head_of[None, :]).astype(bf16)

    def first_two_sublayers(x, mod_group, rope_tables, ctx_kv_t, init_states, hg_groups, to_cast=()):
        shape = x.shape
        seq_len = shape[0] * shape[1] // hg_groups
        per_seq = lambda a: a.reshape(hg_groups, seq_len, a.shape[-1])
        x = _ffn(x, mod, mod_group, w1u, w1d, *gains, mod_base=0, ln_row=0)
        proj = _proj(x, mod, mod_group, w_in_b, lb_logits_fwd, lb_logits_bwd, q_gain, k_gain, ones_bd, rope_tables,
                     seq_len)
        qs, vh, f_f, f_b, sg, qa, ka = map(per_seq, proj[:7])
        vt, span_f, span_b = proj[7], proj[8].reshape(hg_groups, -1, LANES), proj[9].reshape(hg_groups, -1, LANES)
        scans = _hgrn(qs, vh, f_f, f_b, span_f, span_b, init_states)
        x, *cast = _attn(qa, ka, vt, ctx_kv_t, per_seq(x), scans[0], scans[1], sg, mod, mod_group, w_out_b, hg_gain,
                         q_gain, *gains, to_cast=to_cast)
        return x.reshape(shape), proj[10:], vt, scans[2:], cast

    from_cache = lambda t: t[:, 0].transpose(0, 2, 3, 1).reshape(dec_batch, KV_WIDTH, past)
    init_states = (state_hgrn_fwd[:, 0], state_hgrn_bwd[:, 0])
    latent_group = lambda g: g
    x_latent, _, _, _, (w2u, w2d) = first_two_sublayers(
        x_sample, latent_group, _rope_tables(dec_seq), (from_cache(cache_k), from_cache(cache_v)), init_states,
        dec_batch, to_cast=(w_ffn2_in[0], w_ffn2_out[0]))

    ctx_group = lambda g: ctx_row
    x_ctx, (kt_new,), vt_new, states, _ = first_two_sublayers(
        x_prompt.reshape(1, batch * seq, D_MODEL), ctx_group, None, None, None, batch)
    y_sample = _ffn(x_latent, mod, latent_group, w2u, w2d, *gains, mod_base=6, ln_row=2)
    y_prompt = _ffn(x_ctx, mod, ctx_group, w2u, w2d, *gains, mod_base=6, ln_row=2)
    y_prompt = y_prompt.reshape(batch, seq, D_MODEL)
    to_cache = lambda t: t.reshape(batch, DEPTH, N_KV_HEADS, HEAD_DIM, seq).transpose(0, 1, 4, 2, 3)
    new_cache_k, new_cache_v = to_cache(kt_new), to_cache(vt_new)
    new_state_fwd = states[0].reshape(batch, DEPTH, HG_HEADS, HG_HEAD_DIM, HG_HEAD_DIM)
    new_state_bwd = states[1].reshape(batch, DEPTH, HG_HEADS, HG_HEAD_DIM, HG_HEAD_DIM)

    return (y_prompt, y_sample, new_cache_k, new_cache_v, new_state_fwd, new_state_bwd)
```

```python
import functools

import jax
import jax.numpy as jnp
from jax import lax
from jax.experimental import pallas as pl
from jax.experimental.pallas import tpu as pltpu

f32 = jnp.float32
bf16 = jnp.bfloat16

D_MODEL = 1024
N_MOD = 9
HG_WIDTH = 512
HG_HEAD_DIM = 128
HG_HEADS = 4
ATT_WIDTH = 512
HEAD_DIM = 64
N_HEADS = 8
N_KV_HEADS = 2
KV_WIDTH = 128
IN_WIDTH = 5 * HG_WIDTH + ATT_WIDTH + 2 * KV_WIDTH
D_FF = 2816
GRID_W = 64
ROPE_THETA = 10000.0
DEPTH = 1
ALPHA = (2.0 * DEPTH) ** 0.25
LOG2_E = 1.4426950408889634
LN_EPS = 1e-6
RMS_EPS = 1e-6

LANES = 128
BF16_SUBLANES = 16
VMEM_LIMIT_BYTES = 56 * 1024 * 1024

FF_CHUNK = 256
FFN_TILE = 1024
FFN_SUB = 512
TOKEN_TILE = 512
PROJ_SUB = 256
HG_TILE = 128
HG_STEP = 512
Q_TILE = 512
Q_SUB = 256
KEY_CHUNK = 512
SCORE_BOUND_LIMIT = 40.0
SCORE_BOUND_SLACK = 1.01
V_ROWS = 80
HG_FAST_BLOCK = 64
HG_FAST_SPAN_LOG2 = 100.0


def _silu(x):
    return x * jax.nn.sigmoid(x)


def _layer_norm(r, g, b):
    mu = jnp.mean(r, axis=-1, keepdims=True)
    c = r - mu
    var = jnp.mean(c * c, axis=-1, keepdims=True)
    return c * lax.rsqrt(var + LN_EPS) * g + b


def _split_bf16(x, parts):
    out = []
    r = x
    for _ in range(parts - 1):
        p = r.astype(bf16)
        out.append(p)
        r = r - p.astype(f32)
    out.append(r.astype(bf16))
    return out


def _dot01(mat01, x, parts, *, mat_on_left):
    pieces = _split_bf16(x, parts)
    if mat_on_left:
        return jnp.dot(jnp.concatenate([mat01] * parts, axis=1), jnp.concatenate(pieces, axis=0),
                       preferred_element_type=f32)
    return jnp.dot(jnp.concatenate(pieces, axis=1), jnp.concatenate([mat01] * parts, axis=0),
                   preferred_element_type=f32)


def _dot_nt(a, b):
    return lax.dot_general(a, b, (((1,), (1,)), ((), ())), preferred_element_type=f32)


def _dot_tn(a, b):
    return lax.dot_general(a, b, (((0,), (0,)), ((), ())), preferred_element_type=f32)


def _mod_kernel(c_ref, w_ref, b_ref, o_ref):
    a = _silu(c_ref[...]).astype(bf16)
    o_ref[...] = jnp.dot(a, w_ref[...].astype(bf16), preferred_element_type=f32) + b_ref[...]


def _modulation(cvecs, w_mod, b_mod):
    rows = cvecs.shape[0]
    n_out = w_mod.shape[1]
    tn = D_MODEL
    return pl.pallas_call(
        _mod_kernel,
        grid=(n_out // tn,),
        in_specs=[
            pl.BlockSpec((rows, D_MODEL), lambda j: (0, 0)),
            pl.BlockSpec((D_MODEL, tn), lambda j: (0, j)),
            pl.BlockSpec((1, tn), lambda j: (0, j)),
        ],
        out_specs=pl.BlockSpec((rows, tn), lambda j: (0, j)),
        out_shape=jax.ShapeDtypeStruct((rows, n_out), f32),
        compiler_params=pltpu.CompilerParams(dimension_semantics=("arbitrary",)),
        name="modulation",
    )(cvecs, w_mod, b_mod.reshape(1, n_out))


def _ffn_kernel(x_ref, m_ref, wup_ref, wd_ref, g_ref, b_ref, *refs, mod_base, ln_row):
    n_cast = (len(refs) - 2) // 2
    cast_in, o_ref, cast_out, act_ref = refs[:n_cast], refs[n_cast], refs[n_cast + 1:-1], refs[-1]
    for src_ref, dst_ref in zip(cast_in, cast_out):
        dst_ref[...] = src_ref[...].astype(bf16)
    shift = m_ref[0, mod_base:mod_base + 1, :]
    scale = m_ref[0, mod_base + 1:mod_base + 2, :]
    gate = m_ref[0, mod_base + 2:mod_base + 3, :]
    subs = [slice(lo, lo + FFN_SUB) for lo in range(0, x_ref.shape[1], FFN_SUB)]
    xs = [x_ref[0, rows] for rows in subs]
    hs = [(x * (1.0 + scale) + shift).astype(bf16) for x in xs]
    for j in range(D_FF // FF_CHUNK):
        cols = slice(j * FF_CHUNK, (j + 1) * FF_CHUNK)
        for rows, h in zip(subs, hs):
            a = jnp.dot(h, wup_ref[:, cols], preferred_element_type=f32)
            u = jnp.dot(h, wup_ref[:, D_FF + j * FF_CHUNK:D_FF + (j + 1) * FF_CHUNK],
                        preferred_element_type=f32)
            act_ref[rows, cols] = (_silu(a) * u).astype(bf16)
    ys = [jnp.dot(act_ref[rows, :], wd_ref[...], preferred_element_type=f32) for rows in subs]
    for rows, x, y in zip(subs, xs, ys):
        r = ALPHA * x + 0.5 * gate * y
        o_ref[0, rows] = _layer_norm(r, g_ref[ln_row:ln_row + 1, :], b_ref[ln_row:ln_row + 1, :])


def _resident(shape):
    return pl.BlockSpec(shape, lambda *_: (0,) * len(shape), pipeline_mode=pl.Buffered(1))


def _cast_specs(to_cast, nt, steps):
    specs, shapes = [], []
    for w in to_cast:
        rows = next(r for r in range(BF16_SUBLANES, w.shape[0] + 1, BF16_SUBLANES)
                    if w.shape[0] % r == 0 and w.shape[0] // r <= steps)
        specs.append(pl.BlockSpec((rows, w.shape[1]),
                                  lambda g, t, last=w.shape[0] // rows - 1: (jnp.minimum(g * nt + t, last), 0)))
        shapes.append(jax.ShapeDtypeStruct(w.shape, bf16))
    return specs, shapes


def _ffn(x, mod, mod_group, w_up, wd, ln_g, ln_b, *, mod_base, ln_row, to_cast=()):
    groups, length, _ = x.shape
    tm = FFN_TILE
    nt = length // tm
    cast_specs, cast_shapes = _cast_specs(to_cast, nt, groups * nt)
    out = pl.pallas_call(
        functools.partial(_ffn_kernel, mod_base=mod_base, ln_row=ln_row),
        grid=(groups, nt),
        in_specs=[
            pl.BlockSpec((1, tm, D_MODEL), lambda g, t: (g, t, 0)),
            pl.BlockSpec((1, N_MOD, D_MODEL), lambda g, t: (mod_group(g), 0, 0)),
            _resident((D_MODEL, 2 * D_FF)),
            _resident((D_FF, D_MODEL)),
            _resident((3, D_MODEL)),
            _resident((3, D_MODEL)),
        ] + cast_specs,
        out_specs=[pl.BlockSpec((1, tm, D_MODEL), lambda g, t: (g, t, 0))] + cast_specs,
        out_shape=[jax.ShapeDtypeStruct(x.shape, f32)] + cast_shapes,
        scratch_shapes=[pltpu.VMEM((tm, D_FF), bf16)],
        compiler_params=pltpu.CompilerParams(
            dimension_semantics=("arbitrary", "arbitrary"), vmem_limit_bytes=VMEM_LIMIT_BYTES),
        name="ffn",
    )(x, mod, w_up, wd, ln_g, ln_b, *to_cast)
    return out if to_cast else out[0]


def _head_rms_norm(x, ones_bd, gain):
    ss = _dot01(ones_bd, x * x, 2, mat_on_left=False)
    return x * lax.rsqrt(ss * (1.0 / HEAD_DIM) + RMS_EPS) * gain


def _rope(x, cos, sin_signed):
    width = x.shape[-1]
    lane = lax.broadcasted_iota(jnp.int32, x.shape, 1)
    from_right = pltpu.roll(x, width - 16, 1)
    from_left = pltpu.roll(x, 16, 1)
    partner = jnp.where((lane & 31) < 16, from_right, from_left)
    return x * cos + partner * sin_signed


def _proj_kernel(*refs, rope):
    if rope:
        (x_ref, m_ref, w_ref, lbf_ref, lbb_ref, qg_ref, kg_ref, bd_ref, cos_ref, sin_ref,
         qs_ref, vh_ref, ff_ref, fb_ref, sg_ref, qa_ref, ka_ref, vt_ref,
         spf_ref, spb_ref) = refs
    else:
        (x_ref, m_ref, w_ref, lbf_ref, lbb_ref, qg_ref, kg_ref, bd_ref,
         qs_ref, vh_ref, ff_ref, fb_ref, sg_ref, qa_ref, ka_ref, vt_ref,
         spf_ref, spb_ref, kt_ref) = refs
    shift = m_ref[0, 3:4, :]
    scale = m_ref[0, 4:5, :]
    subs = [slice(lo, lo + PROJ_SUB) for lo in range(0, x_ref.shape[1], PROJ_SUB)]
    hs = [(x_ref[0, rows] * (1.0 + scale) + shift).astype(bf16) for rows in subs]

    def cols(h, lo, width):
        return jnp.dot(h, w_ref[:, lo:lo + width], preferred_element_type=f32)

    def lower_bound(lb_ref):
        l0 = lb_ref[0:1, :]
        l1 = lb_ref[1:2, :]
        m = jnp.maximum(l0, l1)
        e0 = jnp.exp(l0 - m)
        e1 = jnp.exp(l1 - m)
        return e0 / (e0 + e1)

    half = HG_FAST_BLOCK // 2

    def forget_gate(lb_ref, lo, f_ref, span_ref):
        lb = lower_bound(lb_ref)
        for rows, h in zip(subs, hs):
            f = lb + (1.0 - lb) * jax.nn.sigmoid(cols(h, lo, HG_WIDTH))
            f_ref[0, rows] = f
            sums = jnp.sum(jnp.log2(f).reshape(PROJ_SUB // half, half, HG_WIDTH), axis=1)
            span_ref[0, rows.start // half:rows.stop // half] = jnp.broadcast_to(
                jnp.max(jnp.abs(sums), axis=-1, keepdims=True), (PROJ_SUB // half, LANES))

    for rows, h in zip(subs, hs):
        qs_ref[0, rows] = _silu(cols(h, 0, HG_WIDTH))
    for rows, h in zip(subs, hs):
        vh_ref[0, rows] = cols(h, HG_WIDTH, HG_WIDTH).astype(bf16)
    forget_gate(lbf_ref, 2 * HG_WIDTH, ff_ref, spf_ref)
    forget_gate(lbb_ref, 3 * HG_WIDTH, fb_ref, spb_ref)
    for rows, h in zip(subs, hs):
        sg_ref[0, rows] = _silu(cols(h, 4 * HG_WIDTH, HG_WIDTH))

    base = 5 * HG_WIDTH
    piece = vt_ref.shape[2]
    for rows, h in zip(subs, hs):
        q = _head_rms_norm(cols(h, base, ATT_WIDTH), bd_ref[...], qg_ref[...])
        k = _head_rms_norm(cols(h, base + ATT_WIDTH, KV_WIDTH), bd_ref[0:KV_WIDTH, 0:KV_WIDTH], kg_ref[...])
        if rope:
            cos, sin = cos_ref[rows, :], sin_ref[rows, :]
            q = _rope(q, jnp.concatenate([cos] * (ATT_WIDTH // LANES), axis=1),
                      jnp.concatenate([sin] * (ATT_WIDTH // LANES), axis=1))
            k = _rope(k, cos, sin)
        qa_ref[0, rows] = (q * (HEAD_DIM ** -0.5 * LOG2_E)).astype(bf16)
        ka_ref[0, rows] = k
        v = cols(h, base + ATT_WIDTH + KV_WIDTH, KV_WIDTH)
        for lo in range(rows.start, rows.stop, min(piece, PROJ_SUB)):
            n = min(piece, PROJ_SUB)
            dst = (lo // piece, slice(None), slice(lo % piece, lo % piece + n))
            vt_ref[dst] = v[lo - rows.start:lo - rows.start + n].T
            if not rope:
                kt_ref[dst] = k[lo - rows.start:lo - rows.start + n].T


def _proj(x, mod, mod_group, w_in, lb_f, lb_b, q_gain, k_gain, ones_bd, rope_tables, seq_len):
    groups, length, _ = x.shape
    tm = TOKEN_TILE
    rope = rope_tables is not None
    n_seq = groups * length // seq_len
    if seq_len >= tm:
        tiles_per_seq = seq_len // tm
        t_spec = pl.BlockSpec((1, KV_WIDTH, tm), lambda g, t: (g * (length // seq_len) + t // tiles_per_seq, 0,
                                                               t % tiles_per_seq))
    else:
        t_spec = pl.BlockSpec((tm // seq_len, KV_WIDTH, seq_len), lambda g, t: (g * (length // tm) + t, 0, 0))
    t_shape = jax.ShapeDtypeStruct((n_seq, KV_WIDTH, seq_len), f32)
    tok = lambda width: pl.BlockSpec((1, tm, width), lambda g, t: (g, t, 0))
    in_specs = [
        tok(D_MODEL),
        pl.BlockSpec((1, N_MOD, D_MODEL), lambda g, t: (mod_group(g), 0, 0)),
        _resident((D_MODEL, IN_WIDTH)),
        _resident((2, HG_WIDTH)),
        _resident((2, HG_WIDTH)),
        _resident((1, ATT_WIDTH)),
        _resident((1, KV_WIDTH)),
        _resident((ATT_WIDTH, ATT_WIDTH)),
    ]
    args = [x, mod, w_in, lb_f, lb_b, q_gain, k_gain, ones_bd]
    if rope:
        in_specs += [pl.BlockSpec((tm, LANES), lambda g, t: (t, 0))] * 2
        args += list(rope_tables)
    shape = lambda width, dt: jax.ShapeDtypeStruct((groups, length, width), dt)
    half = HG_FAST_BLOCK // 2
    span_spec = pl.BlockSpec((1, tm // half, LANES), lambda g, t: (g, t, 0))
    span_shape = jax.ShapeDtypeStruct((groups, length // half, LANES), f32)
    return pl.pallas_call(
        functools.partial(_proj_kernel, rope=rope),
        grid=(groups, length // tm),
        in_specs=in_specs,
        out_specs=([tok(HG_WIDTH)] * 5 + [tok(ATT_WIDTH), tok(KV_WIDTH), t_spec, span_spec, span_spec]
                   + ([] if rope else [t_spec])),
        out_shape=([shape(HG_WIDTH, f32), shape(HG_WIDTH, bf16)] + [shape(HG_WIDTH, f32)] * 3
                   + [shape(ATT_WIDTH, bf16), shape(KV_WIDTH, f32), t_shape, span_shape, span_shape]
                   + ([] if rope else [t_shape])),
        compiler_params=pltpu.CompilerParams(
            dimension_semantics=("arbitrary", "arbitrary"), vmem_limit_bytes=VMEM_LIMIT_BYTES),
        name="mixer_proj",
    )(*args)


def _hgrn_direction(qs, vb, f, st_ref, o_ref, rows, *, rev):
    tc = qs.shape[0]
    lf = jnp.log2(f)
    kk = 1.0 - f
    row = lax.broadcasted_iota(jnp.int32, (tc, tc), 0)
    col = lax.broadcasted_iota(jnp.int32, (tc, tc), 1)
    ordered = (row < col) if rev else (row > col)
    differ = row ^ col
    tri = jnp.where((col >= row) if rev else (col <= row), 1.0, 0.0).astype(bf16)
    cum = _dot01(tri, lf, 2, mat_on_left=True)
    total = cum[0:1] if rev else cum[tc - 1:tc]
    carry = jnp.exp2(total)
    query_half = 0 if rev else 1
    heads = [slice(hd * HG_HEAD_DIM, (hd + 1) * HG_HEAD_DIM) for hd in range(HG_HEADS)]

    def level_exponent(s, ridx):
        if s == 1:
            return jnp.where((ridx & 1) == query_half, lf, 0.0)
        if s == 2:
            nxt = pltpu.roll(lf, tc - 1, 0)
            prv = pltpu.roll(lf, 1, 0)
            m4 = ridx & 3
            if rev:
                return jnp.where(m4 == 0, lf + nxt, jnp.where(m4 == 1, lf, jnp.where(m4 == 2, 0.0, prv)))
            return jnp.where(m4 == 0, nxt, jnp.where(m4 == 1, 0.0, jnp.where(m4 == 2, lf, lf + prv)))
        blocks = []
        for lo in range(0, tc, 2 * s):
            anchor = lo + (s if rev else s - 1)
            blocks.append(cum[lo:lo + 2 * s] - cum[anchor:anchor + 1])
        d = jnp.concatenate(blocks, axis=0)
        is_query = ((ridx >> (s.bit_length() - 1)) & 1) == query_half
        return jnp.where(is_query, d, -d)

    def add_levels(intra, first):
        ridx = lax.broadcasted_iota(jnp.int32, qs.shape, 0)
        s = first
        while s < tc:
            shift = s.bit_length() - 1
            is_query = ((ridx >> shift) & 1) == query_half
            mixed = (jnp.where(is_query, qs, kk) * jnp.exp2(level_exponent(s, ridx))).astype(bf16)
            pair = jnp.logical_and((differ >> shift) == 1, ordered)
            for hd, hs in enumerate(heads):
                intra[hd] = jnp.where(pair, _dot_nt(mixed[:, hs], mixed[:, hs]), intra[hd])
            s *= 2
        return intra

    def finish(intra, q_dec, k_end, extra):
        for hd, hs in enumerate(heads):
            st = st_ref[hd]
            o = jnp.dot(jnp.concatenate([intra[hd].astype(bf16), q_dec[:, hs]], axis=1),
                        jnp.concatenate([vb[:, hs], st.T.astype(bf16)], axis=0), preferred_element_type=f32)
            o_ref[0, rows, hs] = o if extra is None else o + extra[:, hs]
            st_ref[hd] = st * carry[:, hs] + _dot_tn(vb[:, hs], k_end[:, hs])

    blk = HG_FAST_BLOCK
    anchors = [lo + (blk // 2 if rev else blk // 2 - 1) for lo in range(0, tc, blk)]

    def mid_split_blocks():
        q_mid, k_mid, q_dec, k_end = [], [], [], []
        for lo, a in zip(range(0, tc, blk), anchors):
            rel = cum[lo:lo + blk] - cum[a:a + 1]
            q_mid.append(qs[lo:lo + blk] * jnp.exp2(rel))
            k_mid.append(kk[lo:lo + blk] * jnp.exp2(-rel))
            q_dec.append(q_mid[-1] * jnp.exp2(cum[a:a + 1]))
            k_end.append(k_mid[-1] * jnp.exp2(total - cum[a:a + 1]))
        q_mid, k_mid, q_dec, k_end = (jnp.concatenate(p, axis=0).astype(bf16) for p in (q_mid, k_mid, q_dec, k_end))
        same_block = (differ >> (blk.bit_length() - 1)) == 0
        keep = jnp.logical_and(same_block, (row <= col) if rev else (row >= col))
        intra = [jnp.where(keep, _dot_nt(q_mid[:, hs], k_mid[:, hs]), 0.0) for hs in heads]
        finish(add_levels(intra, blk), q_dec, k_end, None)

    def all_levels():
        intra = add_levels([jnp.zeros((tc, tc), f32)] * HG_HEADS, 1)
        q_dec = (qs * jnp.exp2(cum)).astype(bf16)
        k_end = (kk * jnp.exp2(total - cum)).astype(bf16)
        qk = qs * kk
        own = jnp.concatenate(
            [jnp.sum(qk[:, hs], axis=-1, keepdims=True) * vb[:, hs].astype(f32) for hs in heads], axis=-1)
        finish(intra, q_dec, k_end, own)

    return mid_split_blocks, all_levels


def _hgrn_kernel(slow_ref, *refs, has_init):
    if has_init:
        (qsf_ref, vf_ref, ff_ref, qsb_ref, vb_ref, fb_ref, s0f_ref, s0b_ref,
         of_ref, ob_ref, stf_ref, stb_ref) = refs
    else:
        (qsf_ref, vf_ref, ff_ref, qsb_ref, vb_ref, fb_ref,
         of_ref, ob_ref, sf_out_ref, sb_out_ref, stf_ref, stb_ref) = refs
    t = pl.program_id(1)

    @pl.when(t == 0)
    def _():
        for hd in range(HG_HEADS):
            if has_init:
                stf_ref[hd] = s0f_ref[0, hd].T
                stb_ref[hd] = s0b_ref[0, hd].T
            else:
                stf_ref[hd] = jnp.zeros((HG_HEAD_DIM, HG_HEAD_DIM), f32)
                stb_ref[hd] = jnp.zeros((HG_HEAD_DIM, HG_HEAD_DIM), f32)

    chunks = [slice(lo, lo + HG_TILE) for lo in range(0, qsf_ref.shape[1], HG_TILE)]
    scans = []
    for r_f, r_b in zip(chunks, reversed(chunks)):
        scans.append(_hgrn_direction(qsf_ref[0, r_f], vf_ref[0, r_f], ff_ref[0, r_f], stf_ref, of_ref, r_f, rev=False))
        scans.append(_hgrn_direction(qsb_ref[0, r_b], vb_ref[0, r_b], fb_ref[0, r_b], stb_ref, ob_ref, r_b, rev=True))
    slow = slow_ref[pl.program_id(0) * pl.num_programs(1) + t] != 0

    @pl.when(jnp.logical_not(slow))
    def _():
        for mid_split_blocks, _ in scans:
            mid_split_blocks()

    @pl.when(slow)
    def _():
        for _, all_levels in scans:
            all_levels()

    if not has_init:
        @pl.when(t == pl.num_programs(1) - 1)
        def _():
            for hd in range(HG_HEADS):
                sf_out_ref[0, hd] = stf_ref[hd].T
                sb_out_ref[0, hd] = stb_ref[hd].T


def _hgrn(qs, vh, f_fwd, f_bwd, span_fwd, span_bwd, init_states):
    groups, length, _ = qs.shape
    step = min(HG_STEP, length)
    nt = length // step
    has_init = init_states is not None
    per_step = lambda span: jnp.max(span[:, :, 0].reshape(groups, nt, -1), axis=-1) > HG_FAST_SPAN_LOG2
    slow = jnp.logical_or(per_step(span_fwd), per_step(span_bwd)[:, ::-1]).astype(jnp.int32).reshape(-1)
    fwd = pl.BlockSpec((1, step, HG_WIDTH), lambda g, t, _: (g, t, 0))
    bwd = pl.BlockSpec((1, step, HG_WIDTH), lambda g, t, _: (g, nt - 1 - t, 0))
    state = pl.BlockSpec((1, HG_HEADS, HG_HEAD_DIM, HG_HEAD_DIM), lambda g, t, _: (g, 0, 0, 0))
    in_specs = [fwd, fwd, fwd, bwd, bwd, bwd]
    args = [qs, vh, f_fwd, qs, vh, f_bwd]
    out_specs = [fwd, bwd]
    out_shape = [jax.ShapeDtypeStruct(qs.shape, f32)] * 2
    if has_init:
        in_specs += [state, state]
        args += list(init_states)
    else:
        out_specs += [state, state]
        out_shape += [jax.ShapeDtypeStruct((groups, HG_HEADS, HG_HEAD_DIM, HG_HEAD_DIM), f32)] * 2
    return pl.pallas_call(
        functools.partial(_hgrn_kernel, has_init=has_init),
        grid_spec=pltpu.PrefetchScalarGridSpec(
            num_scalar_prefetch=1,
            grid=(groups, nt),
            in_specs=in_specs,
            out_specs=out_specs,
            scratch_shapes=[pltpu.VMEM((HG_HEADS, HG_HEAD_DIM, HG_HEAD_DIM), f32)] * 2),
        out_shape=out_shape,
        compiler_params=pltpu.CompilerParams(
            dimension_semantics=("arbitrary", "arbitrary"), vmem_limit_bytes=VMEM_LIMIT_BYTES),
        name="hgrn2",
    )(slow, *args)


def _attn_kernel(*refs, n_ctx, n_cast):
    n_in = (15 if n_ctx else 13) + n_cast
    cast_in, cast_out = refs[n_in - n_cast:n_in], refs[n_in + 1:n_in + 1 + n_cast]
    kk_ref, vta_ref, bound_ref, fixed_ref = refs[n_in + 1 + n_cast:]
    o_ref = refs[n_in]
    if n_ctx:
        (qa_ref, ka_ref, vt_ref, ckt_ref, cvt_ref, x_ref, of_ref, ob_ref, sg_ref, m_ref, wo_ref, hg_ref, qg_ref,
         g_ref, b_ref) = refs[:n_in - n_cast]
    else:
        (qa_ref, ka_ref, vt_ref, x_ref, of_ref, ob_ref, sg_ref, m_ref, wo_ref, hg_ref, qg_ref,
         g_ref, b_ref) = refs[:n_in - n_cast]
    for src_ref, dst_ref in zip(cast_in, cast_out):
        dst_ref[...] = src_ref[...].astype(bf16)
    tq = qa_ref.shape[1]
    n_keys = kk_ref.shape[0]

    @pl.when(pl.program_id(1) == 0)
    def _():
        def fill_keys(lo, k):
            n = k.shape[0]
            low = lax.broadcasted_iota(jnp.int32, k.shape, 1) < HEAD_DIM
            k_sw = pltpu.roll(k, HEAD_DIM, 1)
            kk_ref[lo:lo + n, 0:LANES] = jnp.where(low, k, k_sw).astype(bf16)
            kk_ref[lo:lo + n, LANES:2 * LANES] = jnp.where(low, k_sw, k).astype(bf16)

        def fill_values(lo, vt):
            n = vt.shape[1]
            ones_row = jnp.where(lax.broadcasted_iota(jnp.int32, (V_ROWS - HEAD_DIM, n), 0) == 0, 1.0, 0.0)
            for kv in range(N_KV_HEADS):
                vta_ref[kv * V_ROWS:kv * V_ROWS + HEAD_DIM, lo:lo + n] = (
                    vt[kv * HEAD_DIM:(kv + 1) * HEAD_DIM].astype(bf16))
                vta_ref[kv * V_ROWS + HEAD_DIM:(kv + 1) * V_ROWS, lo:lo + n] = ones_row.astype(bf16)

        if n_ctx:
            fill_keys(0, ckt_ref[0].T)
            fill_values(0, cvt_ref[0])
        fill_keys(n_ctx, ka_ref[0])
        fill_values(n_ctx, vt_ref[0])

        gain_max = jnp.max(jnp.abs(qg_ref[...]), axis=1, keepdims=True)
        all_small = None
        for kv in range(N_KV_HEADS):
            kt = kk_ref[:, kv * LANES:(kv + 1) * LANES].astype(f32)
            k_norm2 = jnp.max(0.5 * jnp.sum(kt * kt, axis=1, keepdims=True), axis=0, keepdims=True)
            bound = (LOG2_E * SCORE_BOUND_SLACK) * gain_max * jnp.sqrt(k_norm2)
            bound_ref[kv] = bound[0, 0]
            small = jnp.where(bound <= SCORE_BOUND_LIMIT, 1, 0)
            all_small = small if all_small is None else all_small * small
        fixed_ref[0] = all_small[0, 0]

    ts = min(Q_SUB, tq)
    low = lax.broadcasted_iota(jnp.int32, (ts, LANES), 1) < HEAD_DIM
    kc = min(KEY_CHUNK, n_keys)
    pairs_per_kv = N_HEADS // N_KV_HEADS // 2

    def attend(rows, fixed_shift):
        def masked_pair(tile):
            qp = qa_ref[0, rows, tile * LANES:(tile + 1) * LANES]
            zero = jnp.zeros_like(qp)
            return jnp.concatenate([jnp.where(low, qp, zero), jnp.where(low, zero, qp)], axis=0)

        q_pairs = [masked_pair(tile) for tile in range(N_HEADS // 2)]
        work = [(tile, lo) for tile in range(N_HEADS // 2) for lo in range(0, n_keys, kc)]

        def scores(tile, lo):
            kv = tile // pairs_per_kv
            return _dot_nt(kk_ref[lo:lo + kc, kv * LANES:(kv + 1) * LANES], q_pairs[tile])

        heads_t = []
        st_next = scores(*work[0])
        m = acc = None
        for i, (tile, lo) in enumerate(work):
            st = st_next
            if i + 1 < len(work):
                st_next = scores(*work[i + 1])
            kv = tile // pairs_per_kv
            values_t = vta_ref[kv * V_ROWS:(kv + 1) * V_ROWS, lo:lo + kc]
            if fixed_shift:
                e = jnp.exp2(st - bound_ref[kv]).astype(bf16)
                pv = jnp.dot(values_t, e, preferred_element_type=f32)
                acc = pv if acc is None else acc + pv
            else:
                m_chunk = jnp.max(st, axis=0, keepdims=True)
                m_new = m_chunk if m is None else jnp.maximum(m, m_chunk)
                e = jnp.exp2(st - m_new).astype(bf16)
                pv = jnp.dot(values_t, e, preferred_element_type=f32)
                acc = pv if acc is None else acc * jnp.exp2(m - m_new) + pv
                m = m_new
            if lo + kc == n_keys:
                on = (acc[0:HEAD_DIM] * (1.0 / acc[HEAD_DIM:HEAD_DIM + 1])).astype(bf16)
                heads_t += [on[:, :ts], on[:, ts:]]
                m = acc = None
        return jnp.concatenate(heads_t, axis=0)

    def project(rows, o_att_t):
        o_sum = of_ref[0, rows] + ob_ref[0, rows]
        normed = []
        for hd in range(HG_HEADS):
            oh = o_sum[:, hd * HG_HEAD_DIM:(hd + 1) * HG_HEAD_DIM]
            ms = jnp.mean(oh * oh, axis=-1, keepdims=True)
            normed.append(oh * lax.rsqrt(ms + RMS_EPS))
        o_hg = (jnp.concatenate(normed, axis=-1) * hg_ref[...] * sg_ref[0, rows]).astype(bf16)
        y = (jnp.dot(o_hg, wo_ref[0:HG_WIDTH, :], preferred_element_type=f32)
             + _dot_tn(o_att_t, wo_ref[HG_WIDTH:HG_WIDTH + ATT_WIDTH, :]))
        r = ALPHA * x_ref[0, rows] + m_ref[0, 5:6, :] * y
        o_ref[0, rows] = _layer_norm(r, g_ref[1:2, :], b_ref[1:2, :])

    def run(fixed_shift):
        for lo in range(0, tq, ts):
            rows = slice(lo, lo + ts)
            project(rows, attend(rows, fixed_shift))

    use_bound = fixed_ref[0] != 0
    pl.when(use_bound)(functools.partial(run, True))
    pl.when(jnp.logical_not(use_bound))(functools.partial(run, False))


def _attn(qa, ka, vt, ctx_kv_t, x, o_f, o_b, sg, mod, mod_group, w_out, hg_gain, q_gain, ln_g, ln_b, to_cast=()):
    groups, length, _ = qa.shape
    tq = min(Q_TILE, length)
    nt = length // tq
    n_ctx = 0 if ctx_kv_t is None else ctx_kv_t[0].shape[2]
    n_keys = n_ctx + length
    tok = lambda width: pl.BlockSpec((1, tq, width), lambda g, t: (g, t, 0))
    whole_t = lambda n: pl.BlockSpec((1, KV_WIDTH, n), lambda g, t: (g, 0, 0))
    in_specs = [tok(ATT_WIDTH), pl.BlockSpec((1, length, KV_WIDTH), lambda g, t: (g, 0, 0)), whole_t(length)]
    args = [qa, ka, vt]
    if n_ctx:
        in_specs += [whole_t(n_ctx), whole_t(n_ctx)]
        args += list(ctx_kv_t)
    in_specs += [
        tok(D_MODEL), tok(HG_WIDTH), tok(HG_WIDTH), tok(HG_WIDTH),
        pl.BlockSpec((1, N_MOD, D_MODEL), lambda g, t: (mod_group(g), 0, 0)),
        _resident((D_MODEL, D_MODEL)),
        _resident((1, HG_WIDTH)),
        _resident((1, ATT_WIDTH)),
        _resident((3, D_MODEL)),
        _resident((3, D_MODEL)),
    ]
    args += [x, o_f, o_b, sg, mod, w_out, hg_gain, q_gain, ln_g, ln_b]
    cast_specs, cast_shapes = _cast_specs(to_cast, nt, groups * nt)
    in_specs += cast_specs
    args += list(to_cast)
    out_specs = [tok(D_MODEL)] + cast_specs
    out_shape = [jax.ShapeDtypeStruct(x.shape, f32)] + cast_shapes
    return pl.pallas_call(
        functools.partial(_attn_kernel, n_ctx=n_ctx, n_cast=len(to_cast)),
        grid=(groups, nt),
        in_specs=in_specs,
        out_specs=out_specs,
        out_shape=out_shape,
        scratch_shapes=[pltpu.VMEM((n_keys, N_KV_HEADS * LANES), bf16),
                        pltpu.VMEM((N_KV_HEADS * V_ROWS, n_keys), bf16),
                        pltpu.SMEM((N_KV_HEADS,), f32),
                        pltpu.SMEM((1,), jnp.int32)],
        compiler_params=pltpu.CompilerParams(
            dimension_semantics=("arbitrary", "arbitrary"), vmem_limit_bytes=VMEM_LIMIT_BYTES),
        name="attn_out",
    )(*args)


def _rope_tables(n_tokens):
    half = HEAD_DIM // 2
    t = jnp.arange(n_tokens)
    inv = ROPE_THETA ** (-jnp.arange(0, half, 2, dtype=f32) / half)
    ang_row = (t // GRID_W).astype(f32)[:, None] * inv
    ang_col = (t % GRID_W).astype(f32)[:, None] * inv
    cos = jnp.concatenate([jnp.cos(ang_row)] * 2 + [jnp.cos(ang_col)] * 2, axis=-1)
    sin = jnp.concatenate([-jnp.sin(ang_row), jnp.sin(ang_row), -jnp.sin(ang_col), jnp.sin(ang_col)], axis=-1)
    return jnp.tile(cos, (1, LANES // HEAD_DIM)), jnp.tile(sin, (1, LANES // HEAD_DIM))


def kernel(x_prompt, x_sample, cache_k, cache_v, state_hgrn_fwd, state_hgrn_bwd, c, c_ctx, w_mod, b_mod,
           w_ffn1_in, w_ffn1_out, w_ffn2_in, w_ffn2_out, w_in, w_out, q_norm_g, k_norm_g, hg_norm_g,
           lb_logits_fwd, lb_logits_bwd, ln_g, ln_b):
    assert w_mod.shape[0] == DEPTH and lb_logits_fwd.shape[0] == DEPTH + 1
    batch, seq, _ = x_prompt.shape
    dec_batch, dec_seq, _ = x_sample.shape
    past = cache_k.shape[2]

    ctx_row = dec_batch
    rows = 16
    cvecs = jnp.concatenate([c, c_ctx[None, :], jnp.zeros((rows - dec_batch - 1, D_MODEL), f32)], axis=0)
    mod = _modulation(cvecs, w_mod[0], b_mod[0]).reshape(rows, N_MOD, D_MODEL)

    to_bf16 = lambda w: w.astype(bf16)
    w1u, w1d = to_bf16(w_ffn1_in[0]), to_bf16(w_ffn1_out[0])
    gains = ln_g[0], ln_b[0]
    q_gain = jnp.tile(q_norm_g[0], N_HEADS).reshape(1, ATT_WIDTH)
    k_gain = jnp.tile(k_norm_g[0], N_KV_HEADS).reshape(1, KV_WIDTH)
    hg_gain = hg_norm_g[0].reshape(1, HG_WIDTH)
    head_of = jnp.arange(ATT_WIDTH) // HEAD_DIM
    ones_bd = (head_of[:, None] == head_of[None, :]).astype(bf16)

    def mixer_sublayer(x, mod_group, rope_tables, ctx_kv_t, init_states, hg_groups, to_cast=()):
        shape = x.shape
        seq_len = shape[0] * shape[1] // hg_groups
        per_seq = lambda a: a.reshape(hg_groups, seq_len, a.shape[-1])
        proj = _proj(x, mod, mod_group, w_in_b, lb_logits_fwd, lb_logits_bwd, q_gain, k_gain, ones_bd, rope_tables,
                     seq_len)
        qs, vh, f_f, f_b, sg, qa, ka = map(per_seq, proj[:7])
        vt, span_f, span_b = proj[7], proj[8].reshape(hg_groups, -1, LANES), proj[9].reshape(hg_groups, -1, LANES)
        scans = _hgrn(qs, vh, f_f, f_b, span_f, span_b, init_states)
        x, *cast = _attn(qa, ka, vt, ctx_kv_t, per_seq(x), scans[0], scans[1], sg, mod, mod_group, w_out_b, hg_gain,
                         q_gain, *gains, to_cast=to_cast)
        return x.reshape(shape), proj[10:], vt, scans[2:], cast

    from_cache = lambda t: t[:, 0].transpose(0, 2, 3, 1).reshape(dec_batch, KV_WIDTH, past)
    init_states = (state_hgrn_fwd[:, 0], state_hgrn_bwd[:, 0])
    latent_group = lambda g: g
    x_latent, w_in_b, w_out_b = _ffn(x_sample, mod, latent_group, w1u, w1d, *gains, mod_base=0, ln_row=0,
                                     to_cast=(w_in[0], w_out[0]))
    x_latent, _, _, _, (w2u, w2d) = mixer_sublayer(
        x_latent, latent_group, _rope_tables(dec_seq), (from_cache(cache_k), from_cache(cache_v)), init_states,
        dec_batch, to_cast=(w_ffn2_in[0], w_ffn2_out[0]))

    ctx_group = lambda g: ctx_row
    x_ctx = _ffn(x_prompt.reshape(1, batch * seq, D_MODEL), mod, ctx_group, w1u, w1d, *gains, mod_base=0, ln_row=0)
    x_ctx, (kt_new,), vt_new, states, _ = mixer_sublayer(x_ctx, ctx_group, None, None, None, batch)
    y_sample = _ffn(x_latent, mod, latent_group, w2u, w2d, *gains, mod_base=6, ln_row=2)
    y_prompt = _ffn(x_ctx, mod, ctx_group, w2u, w2d, *gains, mod_base=6, ln_row=2)
    y_prompt = y_prompt.reshape(batch, seq, D_MODEL)
    to_cache = lambda t: t.reshape(batch, DEPTH, N_KV_HEADS, HEAD_DIM, seq).transpose(0, 1, 4, 2, 3)
    new_cache_k, new_cache_v = to_cache(kt_new), to_cache(vt_new)
    new_state_fwd = states[0].reshape(batch, DEPTH, HG_HEADS, HG_HEAD_DIM, HG_HEAD_DIM)
    new_state_bwd = states[1].reshape(batch, DEPTH, HG_HEADS, HG_HEAD_DIM, HG_HEAD_DIM)

    return (y_prompt, y_sample, new_cache_k, new_cache_v, new_state_fwd, new_state_bwd)
```

```python
import functools

import jax
import jax.numpy as jnp
from jax import lax
from jax.experimental import pallas as pl
from jax.experimental.pallas import tpu as pltpu

f32 = jnp.float32
bf16 = jnp.bfloat16

D_MODEL = 1024
N_MOD = 9
HG_WIDTH = 512
HG_HEAD_DIM = 128
HG_HEADS = 4
ATT_WIDTH = 512
HEAD_DIM = 64
N_HEADS = 8
N_KV_HEADS = 2
KV_WIDTH = 128
IN_WIDTH = 5 * HG_WIDTH + ATT_WIDTH + 2 * KV_WIDTH
D_FF = 2816
GRID_W = 64
ROPE_PAIR = HEAD_DIM // 4
ROPE_THETA = 10000.0
DEPTH = 1
ALPHA = (2.0 * DEPTH) ** 0.25
LOG2_E = 1.4426950408889634
LN_EPS = 1e-6
RMS_EPS = 1e-6

LANES = 128
BF16_SUBLANES = 16
VMEM_LIMIT_BYTES = 56 * 1024 * 1024

FF_CHUNK = 256
FFN_TILE = 1024
FFN_SUB = 512
TOKEN_TILE = 512
PROJ_SUB = 256
HG_TILE = 128
HG_STEP = 512
Q_TILE = 512
Q_SUB = 256
KEY_CHUNK = 512
SCORE_BOUND_LIMIT = 40.0
SCORE_BOUND_SLACK = 1.01
V_ROWS = 80
HG_FAST_BLOCK = 64
HG_FAST_SPAN_LOG2 = 100.0


def _silu(x):
    return x * jax.nn.sigmoid(x)


def _layer_norm(r, g, b):
    mu = jnp.mean(r, axis=-1, keepdims=True)
    c = r - mu
    var = jnp.mean(c * c, axis=-1, keepdims=True)
    return c * lax.rsqrt(var + LN_EPS) * g + b


def _split_bf16(x, parts):
    out = []
    r = x
    for _ in range(parts - 1):
        p = r.astype(bf16)
        out.append(p)
        r = r - p.astype(f32)
    out.append(r.astype(bf16))
    return out


def _dot01(mat01, x, parts, *, mat_on_left):
    pieces = _split_bf16(x, parts)
    if mat_on_left:
        return jnp.dot(jnp.concatenate([mat01] * parts, axis=1), jnp.concatenate(pieces, axis=0),
                       preferred_element_type=f32)
    return jnp.dot(jnp.concatenate(pieces, axis=1), jnp.concatenate([mat01] * parts, axis=0),
                   preferred_element_type=f32)


def _dot_nt(a, b):
    return lax.dot_general(a, b, (((1,), (1,)), ((), ())), preferred_element_type=f32)


def _dot_tn(a, b):
    return lax.dot_general(a, b, (((0,), (0,)), ((), ())), preferred_element_type=f32)


def _mod_kernel(c_ref, w_ref, b_ref, o_ref):
    a = _silu(c_ref[...]).astype(bf16)
    o_ref[...] = jnp.dot(a, w_ref[...].astype(bf16), preferred_element_type=f32) + b_ref[...]


def _modulation(cvecs, w_mod, b_mod):
    rows = cvecs.shape[0]
    n_out = w_mod.shape[1]
    tn = D_MODEL
    return pl.pallas_call(
        _mod_kernel,
        grid=(n_out // tn,),
        in_specs=[
            pl.BlockSpec((rows, D_MODEL), lambda j: (0, 0)),
            pl.BlockSpec((D_MODEL, tn), lambda j: (0, j)),
            pl.BlockSpec((1, tn), lambda j: (0, j)),
        ],
        out_specs=pl.BlockSpec((rows, tn), lambda j: (0, j)),
        out_shape=jax.ShapeDtypeStruct((rows, n_out), f32),
        compiler_params=pltpu.CompilerParams(dimension_semantics=("arbitrary",)),
        name="modulation",
    )(cvecs, w_mod, b_mod.reshape(1, n_out))


def _ffn_kernel(x_ref, m_ref, wup_ref, wd_ref, g_ref, b_ref, *refs, mod_base, ln_row):
    n_cast = (len(refs) - 2) // 2
    cast_in, o_ref, cast_out, act_ref = refs[:n_cast], refs[n_cast], refs[n_cast + 1:-1], refs[-1]
    for src_ref, dst_ref in zip(cast_in, cast_out):
        dst_ref[...] = src_ref[...].astype(bf16)
    shift = m_ref[0, mod_base:mod_base + 1, :]
    scale = m_ref[0, mod_base + 1:mod_base + 2, :]
    gate = m_ref[0, mod_base + 2:mod_base + 3, :]
    subs = [slice(lo, lo + FFN_SUB) for lo in range(0, x_ref.shape[1], FFN_SUB)]
    xs = [x_ref[0, rows] for rows in subs]
    hs = [(x * (1.0 + scale) + shift).astype(bf16) for x in xs]
    for j in range(D_FF // FF_CHUNK):
        cols = slice(j * FF_CHUNK, (j + 1) * FF_CHUNK)
        for rows, h in zip(subs, hs):
            a = jnp.dot(h, wup_ref[:, cols], preferred_element_type=f32)
            u = jnp.dot(h, wup_ref[:, D_FF + j * FF_CHUNK:D_FF + (j + 1) * FF_CHUNK],
                        preferred_element_type=f32)
            act_ref[rows, cols] = (_silu(a) * u).astype(bf16)
    ys = [jnp.dot(act_ref[rows, :], wd_ref[...], preferred_element_type=f32) for rows in subs]
    for rows, x, y in zip(subs, xs, ys):
        r = ALPHA * x + 0.5 * gate * y
        o_ref[0, rows] = _layer_norm(r, g_ref[ln_row:ln_row + 1, :], b_ref[ln_row:ln_row + 1, :])


def _resident(shape):
    return pl.BlockSpec(shape, lambda *_: (0,) * len(shape), pipeline_mode=pl.Buffered(1))


def _cast_specs(to_cast, nt, steps):
    specs, shapes = [], []
    for w in to_cast:
        rows = next(r for r in range(BF16_SUBLANES, w.shape[0] + 1, BF16_SUBLANES)
                    if w.shape[0] % r == 0 and w.shape[0] // r <= steps)
        specs.append(pl.BlockSpec((rows, w.shape[1]),
                                  lambda g, t, last=w.shape[0] // rows - 1: (jnp.minimum(g * nt + t, last), 0)))
        shapes.append(jax.ShapeDtypeStruct(w.shape, bf16))
    return specs, shapes


def _ffn(x, mod, mod_group, w_up, wd, ln_g, ln_b, *, mod_base, ln_row, to_cast=()):
    groups, length, _ = x.shape
    tm = FFN_TILE
    nt = length // tm
    cast_specs, cast_shapes = _cast_specs(to_cast, nt, groups * nt)
    out = pl.pallas_call(
        functools.partial(_ffn_kernel, mod_base=mod_base, ln_row=ln_row),
        grid=(groups, nt),
        in_specs=[
            pl.BlockSpec((1, tm, D_MODEL), lambda g, t: (g, t, 0)),
            pl.BlockSpec((1, N_MOD, D_MODEL), lambda g, t: (mod_group(g), 0, 0)),
            _resident((D_MODEL, 2 * D_FF)),
            _resident((D_FF, D_MODEL)),
            _resident((3, D_MODEL)),
            _resident((3, D_MODEL)),
        ] + cast_specs,
        out_specs=[pl.BlockSpec((1, tm, D_MODEL), lambda g, t: (g, t, 0))] + cast_specs,
        out_shape=[jax.ShapeDtypeStruct(x.shape, f32)] + cast_shapes,
        scratch_shapes=[pltpu.VMEM((tm, D_FF), bf16)],
        compiler_params=pltpu.CompilerParams(
            dimension_semantics=("arbitrary", "arbitrary"), vmem_limit_bytes=VMEM_LIMIT_BYTES),
        name="ffn",
    )(x, mod, w_up, wd, ln_g, ln_b, *to_cast)
    return out if to_cast else out[0]


def _head_rms_norm(x, ones_bd, gain):
    ss = _dot01(ones_bd, x * x, 2, mat_on_left=False)
    return x * lax.rsqrt(ss * (1.0 / HEAD_DIM) + RMS_EPS) * gain


def _rope(x, cos, sin_signed):
    width = x.shape[-1]
    lane = lax.broadcasted_iota(jnp.int32, x.shape, 1)
    from_right = pltpu.roll(x, width - ROPE_PAIR, 1)
    from_left = pltpu.roll(x, ROPE_PAIR, 1)
    partner = jnp.where((lane & (2 * ROPE_PAIR - 1)) < ROPE_PAIR, from_right, from_left)
    return x * cos + partner * sin_signed


def _proj_kernel(*refs, rope):
    if rope:
        (x_ref, m_ref, w_ref, lbf_ref, lbb_ref, qg_ref, kg_ref, bd_ref, cos_ref, sin_ref,
         qs_ref, vh_ref, ff_ref, fb_ref, sg_ref, qa_ref, ka_ref, vt_ref,
         spf_ref, spb_ref) = refs
    else:
        (x_ref, m_ref, w_ref, lbf_ref, lbb_ref, qg_ref, kg_ref, bd_ref,
         qs_ref, vh_ref, ff_ref, fb_ref, sg_ref, qa_ref, ka_ref, vt_ref,
         spf_ref, spb_ref, kt_ref) = refs
    shift = m_ref[0, 3:4, :]
    scale = m_ref[0, 4:5, :]
    subs = [slice(lo, lo + PROJ_SUB) for lo in range(0, x_ref.shape[1], PROJ_SUB)]
    hs = [(x_ref[0, rows] * (1.0 + scale) + shift).astype(bf16) for rows in subs]

    def cols(h, lo, width):
        return jnp.dot(h, w_ref[:, lo:lo + width], preferred_element_type=f32)

    def lower_bound(lb_ref):
        l0 = lb_ref[0:1, :]
        l1 = lb_ref[1:2, :]
        m = jnp.maximum(l0, l1)
        e0 = jnp.exp(l0 - m)
        e1 = jnp.exp(l1 - m)
        return e0 / (e0 + e1)

    half = HG_FAST_BLOCK // 2

    def forget_gate(lb_ref, lo, f_ref, span_ref):
        lb = lower_bound(lb_ref)
        for rows, h in zip(subs, hs):
            f = lb + (1.0 - lb) * jax.nn.sigmoid(cols(h, lo, HG_WIDTH))
            f_ref[0, rows] = f
            sums = jnp.sum(jnp.log2(f).reshape(PROJ_SUB // half, half, HG_WIDTH), axis=1)
            span_ref[0, rows.start // half:rows.stop // half] = jnp.broadcast_to(
                jnp.max(jnp.abs(sums), axis=-1, keepdims=True), (PROJ_SUB // half, LANES))

    for rows, h in zip(subs, hs):
        qs_ref[0, rows] = _silu(cols(h, 0, HG_WIDTH))
    for rows, h in zip(subs, hs):
        vh_ref[0, rows] = cols(h, HG_WIDTH, HG_WIDTH).astype(bf16)
    forget_gate(lbf_ref, 2 * HG_WIDTH, ff_ref, spf_ref)
    forget_gate(lbb_ref, 3 * HG_WIDTH, fb_ref, spb_ref)
    for rows, h in zip(subs, hs):
        sg_ref[0, rows] = _silu(cols(h, 4 * HG_WIDTH, HG_WIDTH))

    base = 5 * HG_WIDTH
    piece = vt_ref.shape[2]
    for rows, h in zip(subs, hs):
        q = _head_rms_norm(cols(h, base, ATT_WIDTH), bd_ref[...], qg_ref[...])
        k = _head_rms_norm(cols(h, base + ATT_WIDTH, KV_WIDTH), bd_ref[0:KV_WIDTH, 0:KV_WIDTH], kg_ref[...])
        if rope:
            cos, sin = cos_ref[rows, :], sin_ref[rows, :]
            q = _rope(q, jnp.concatenate([cos] * (ATT_WIDTH // LANES), axis=1),
                      jnp.concatenate([sin] * (ATT_WIDTH // LANES), axis=1))
            k = _rope(k, cos, sin)
        qa_ref[0, rows] = (q * (HEAD_DIM ** -0.5 * LOG2_E)).astype(bf16)
        ka_ref[0, rows] = k
        v = cols(h, base + ATT_WIDTH + KV_WIDTH, KV_WIDTH)
        for lo in range(rows.start, rows.stop, min(piece, PROJ_SUB)):
            n = min(piece, PROJ_SUB)
            dst = (lo // piece, slice(None), slice(lo % piece, lo % piece + n))
            vt_ref[dst] = v[lo - rows.start:lo - rows.start + n].T
            if not rope:
                kt_ref[dst] = k[lo - rows.start:lo - rows.start + n].T


def _proj(x, mod, mod_group, w_in, lb_f, lb_b, q_gain, k_gain, ones_bd, rope_tables, seq_len):
    groups, length, _ = x.shape
    tm = TOKEN_TILE
    rope = rope_tables is not None
    n_seq = groups * length // seq_len
    if seq_len >= tm:
        tiles_per_seq = seq_len // tm
        t_spec = pl.BlockSpec((1, KV_WIDTH, tm), lambda g, t: (g * (length // seq_len) + t // tiles_per_seq, 0,
                                                               t % tiles_per_seq))
    else:
        t_spec = pl.BlockSpec((tm // seq_len, KV_WIDTH, seq_len), lambda g, t: (g * (length // tm) + t, 0, 0))
    t_shape = jax.ShapeDtypeStruct((n_seq, KV_WIDTH, seq_len), f32)
    tok = lambda width: pl.BlockSpec((1, tm, width), lambda g, t: (g, t, 0))
    in_specs = [
        tok(D_MODEL),
        pl.BlockSpec((1, N_MOD, D_MODEL), lambda g, t: (mod_group(g), 0, 0)),
        _resident((D_MODEL, IN_WIDTH)),
        _resident((2, HG_WIDTH)),
        _resident((2, HG_WIDTH)),
        _resident((1, ATT_WIDTH)),
        _resident((1, KV_WIDTH)),
        _resident((ATT_WIDTH, ATT_WIDTH)),
    ]
    args = [x, mod, w_in, lb_f, lb_b, q_gain, k_gain, ones_bd]
    if rope:
        in_specs += [pl.BlockSpec((tm, LANES), lambda g, t: (t, 0))] * 2
        args += list(rope_tables)
    shape = lambda width, dt: jax.ShapeDtypeStruct((groups, length, width), dt)
    half = HG_FAST_BLOCK // 2
    span_spec = pl.BlockSpec((1, tm // half, LANES), lambda g, t: (g, t, 0))
    span_shape = jax.ShapeDtypeStruct((groups, length // half, LANES), f32)
    return pl.pallas_call(
        functools.partial(_proj_kernel, rope=rope),
        grid=(groups, length // tm),
        in_specs=in_specs,
        out_specs=([tok(HG_WIDTH)] * 5 + [tok(ATT_WIDTH), tok(KV_WIDTH), t_spec, span_spec, span_spec]
                   + ([] if rope else [t_spec])),
        out_shape=([shape(HG_WIDTH, f32), shape(HG_WIDTH, bf16)] + [shape(HG_WIDTH, f32)] * 3
                   + [shape(ATT_WIDTH, bf16), shape(KV_WIDTH, f32), t_shape, span_shape, span_shape]
                   + ([] if rope else [t_shape])),
        compiler_params=pltpu.CompilerParams(
            dimension_semantics=("arbitrary", "arbitrary"), vmem_limit_bytes=VMEM_LIMIT_BYTES),
        name="mixer_proj",
    )(*args)


def _hgrn_direction(qs, vb, f, st_ref, o_ref, rows, *, rev):
    tc = qs.shape[0]
    lf = jnp.log2(f)
    kk = 1.0 - f
    row = lax.broadcasted_iota(jnp.int32, (tc, tc), 0)
    col = lax.broadcasted_iota(jnp.int32, (tc, tc), 1)
    ordered = (row < col) if rev else (row > col)
    differ = row ^ col
    tri = jnp.where((col >= row) if rev else (col <= row), 1.0, 0.0).astype(bf16)
    cum = _dot01(tri, lf, 2, mat_on_left=True)
    total = cum[0:1] if rev else cum[tc - 1:tc]
    carry = jnp.exp2(total)
    query_half = 0 if rev else 1
    heads = [slice(hd * HG_HEAD_DIM, (hd + 1) * HG_HEAD_DIM) for hd in range(HG_HEADS)]

    def level_exponent(s, ridx):
        if s == 1:
            return jnp.where((ridx & 1) == query_half, lf, 0.0)
        if s == 2:
            nxt = pltpu.roll(lf, tc - 1, 0)
            prv = pltpu.roll(lf, 1, 0)
            m4 = ridx & 3
            if rev:
                return jnp.where(m4 == 0, lf + nxt, jnp.where(m4 == 1, lf, jnp.where(m4 == 2, 0.0, prv)))
            return jnp.where(m4 == 0, nxt, jnp.where(m4 == 1, 0.0, jnp.where(m4 == 2, lf, lf + prv)))
        blocks = []
        for lo in range(0, tc, 2 * s):
            anchor = lo + (s if rev else s - 1)
            blocks.append(cum[lo:lo + 2 * s] - cum[anchor:anchor + 1])
        d = jnp.concatenate(blocks, axis=0)
        is_query = ((ridx >> (s.bit_length() - 1)) & 1) == query_half
        return jnp.where(is_query, d, -d)

    def add_levels(intra, first):
        ridx = lax.broadcasted_iota(jnp.int32, qs.shape, 0)
        s = first
        while s < tc:
            shift = s.bit_length() - 1
            is_query = ((ridx >> shift) & 1) == query_half
            mixed = (jnp.where(is_query, qs, kk) * jnp.exp2(level_exponent(s, ridx))).astype(bf16)
            pair = jnp.logical_and((differ >> shift) == 1, ordered)
            for hd, hs in enumerate(heads):
                intra[hd] = jnp.where(pair, _dot_nt(mixed[:, hs], mixed[:, hs]), intra[hd])
            s *= 2
        return intra

    def finish(intra, q_dec, k_end, extra):
        for hd, hs in enumerate(heads):
            st = st_ref[hd]
            o = jnp.dot(jnp.concatenate([intra[hd].astype(bf16), q_dec[:, hs]], axis=1),
                        jnp.concatenate([vb[:, hs], st.T.astype(bf16)], axis=0), preferred_element_type=f32)
            o_ref[0, rows, hs] = o if extra is None else o + extra[:, hs]
            st_ref[hd] = st * carry[:, hs] + _dot_tn(vb[:, hs], k_end[:, hs])

    blk = HG_FAST_BLOCK
    anchors = [lo + (blk // 2 if rev else blk // 2 - 1) for lo in range(0, tc, blk)]

    def mid_split_blocks():
        q_mid, k_mid, q_dec, k_end = [], [], [], []
        for lo, a in zip(range(0, tc, blk), anchors):
            rel = cum[lo:lo + blk] - cum[a:a + 1]
            q_mid.append(qs[lo:lo + blk] * jnp.exp2(rel))
            k_mid.append(kk[lo:lo + blk] * jnp.exp2(-rel))
            q_dec.append(q_mid[-1] * jnp.exp2(cum[a:a + 1]))
            k_end.append(k_mid[-1] * jnp.exp2(total - cum[a:a + 1]))
        q_mid, k_mid, q_dec, k_end = (jnp.concatenate(p, axis=0).astype(bf16) for p in (q_mid, k_mid, q_dec, k_end))
        same_block = (differ >> (blk.bit_length() - 1)) == 0
        keep = jnp.logical_and(same_block, (row <= col) if rev else (row >= col))
        intra = [jnp.where(keep, _dot_nt(q_mid[:, hs], k_mid[:, hs]), 0.0) for hs in heads]
        finish(add_levels(intra, blk), q_dec, k_end, None)

    def all_levels():
        intra = add_levels([jnp.zeros((tc, tc), f32)] * HG_HEADS, 1)
        q_dec = (qs * jnp.exp2(cum)).astype(bf16)
        k_end = (kk * jnp.exp2(total - cum)).astype(bf16)
        qk = qs * kk
        own = jnp.concatenate(
            [jnp.sum(qk[:, hs], axis=-1, keepdims=True) * vb[:, hs].astype(f32) for hs in heads], axis=-1)
        finish(intra, q_dec, k_end, own)

    return mid_split_blocks, all_levels


def _hgrn_kernel(slow_ref, *refs, has_init):
    if has_init:
        (qsf_ref, vf_ref, ff_ref, qsb_ref, vb_ref, fb_ref, s0f_ref, s0b_ref,
         of_ref, ob_ref, stf_ref, stb_ref) = refs
    else:
        (qsf_ref, vf_ref, ff_ref, qsb_ref, vb_ref, fb_ref,
         of_ref, ob_ref, sf_out_ref, sb_out_ref, stf_ref, stb_ref) = refs
    t = pl.program_id(1)

    @pl.when(t == 0)
    def _():
        for hd in range(HG_HEADS):
            if has_init:
                stf_ref[hd] = s0f_ref[0, hd].T
                stb_ref[hd] = s0b_ref[0, hd].T
            else:
                stf_ref[hd] = jnp.zeros((HG_HEAD_DIM, HG_HEAD_DIM), f32)
                stb_ref[hd] = jnp.zeros((HG_HEAD_DIM, HG_HEAD_DIM), f32)

    chunks = [slice(lo, lo + HG_TILE) for lo in range(0, qsf_ref.shape[1], HG_TILE)]
    scans = []
    for r_f, r_b in zip(chunks, reversed(chunks)):
        scans.append(_hgrn_direction(qsf_ref[0, r_f], vf_ref[0, r_f], ff_ref[0, r_f], stf_ref, of_ref, r_f, rev=False))
        scans.append(_hgrn_direction(qsb_ref[0, r_b], vb_ref[0, r_b], fb_ref[0, r_b], stb_ref, ob_ref, r_b, rev=True))
    slow = slow_ref[pl.program_id(0) * pl.num_programs(1) + t] != 0

    @pl.when(jnp.logical_not(slow))
    def _():
        for mid_split_blocks, _ in scans:
            mid_split_blocks()

    @pl.when(slow)
    def _():
        for _, all_levels in scans:
            all_levels()

    if not has_init:
        @pl.when(t == pl.num_programs(1) - 1)
        def _():
            for hd in range(HG_HEADS):
                sf_out_ref[0, hd] = stf_ref[hd].T
                sb_out_ref[0, hd] = stb_ref[hd].T


def _hgrn(qs, vh, f_fwd, f_bwd, span_fwd, span_bwd, init_states):
    groups, length, _ = qs.shape
    step = min(HG_STEP, length)
    nt = length // step
    has_init = init_states is not None
    per_step = lambda span: jnp.max(span[:, :, 0].reshape(groups, nt, -1), axis=-1) > HG_FAST_SPAN_LOG2
    slow = jnp.logical_or(per_step(span_fwd), per_step(span_bwd)[:, ::-1]).astype(jnp.int32).reshape(-1)
    fwd = pl.BlockSpec((1, step, HG_WIDTH), lambda g, t, _: (g, t, 0))
    bwd = pl.BlockSpec((1, step, HG_WIDTH), lambda g, t, _: (g, nt - 1 - t, 0))
    state = pl.BlockSpec((1, HG_HEADS, HG_HEAD_DIM, HG_HEAD_DIM), lambda g, t, _: (g, 0, 0, 0))
    in_specs = [fwd, fwd, fwd, bwd, bwd, bwd]
    args = [qs, vh, f_fwd, qs, vh, f_bwd]
    out_specs = [fwd, bwd]
    out_shape = [jax.ShapeDtypeStruct(qs.shape, f32)] * 2
    if has_init:
        in_specs += [state, state]
        args += list(init_states)
    else:
        out_specs += [state, state]
        out_shape += [jax.ShapeDtypeStruct((groups, HG_HEADS, HG_HEAD_DIM, HG_HEAD_DIM), f32)] * 2
    return pl.pallas_call(
        functools.partial(_hgrn_kernel, has_init=has_init),
        grid_spec=pltpu.PrefetchScalarGridSpec(
            num_scalar_prefetch=1,
            grid=(groups, nt),
            in_specs=in_specs,
            out_specs=out_specs,
            scratch_shapes=[pltpu.VMEM((HG_HEADS, HG_HEAD_DIM, HG_HEAD_DIM), f32)] * 2),
        out_shape=out_shape,
        compiler_params=pltpu.CompilerParams(
            dimension_semantics=("arbitrary", "arbitrary"), vmem_limit_bytes=VMEM_LIMIT_BYTES),
        name="hgrn2",
    )(slow, *args)


def _attn_kernel(*refs, n_ctx, n_cast):
    kk_ref, vta_ref, bound_ref, fixed_ref = refs[-4:]
    o_ref, *cast_out = refs[len(refs) - 5 - n_cast:-4]
    inputs = refs[:len(refs) - 5 - n_cast]
    cast_in = inputs[len(inputs) - n_cast:]
    if n_ctx:
        (qa_ref, ka_ref, vt_ref, ckt_ref, cvt_ref, x_ref, of_ref, ob_ref, sg_ref, m_ref, wo_ref, hg_ref, qg_ref,
         g_ref, b_ref) = inputs[:len(inputs) - n_cast]
    else:
        (qa_ref, ka_ref, vt_ref, x_ref, of_ref, ob_ref, sg_ref, m_ref, wo_ref, hg_ref, qg_ref,
         g_ref, b_ref) = inputs[:len(inputs) - n_cast]
    for src_ref, dst_ref in zip(cast_in, cast_out):
        dst_ref[...] = src_ref[...].astype(bf16)
    tq = qa_ref.shape[1]
    n_keys = kk_ref.shape[0]

    @pl.when(pl.program_id(1) == 0)
    def _():
        def fill_keys(lo, k):
            n = k.shape[0]
            low = lax.broadcasted_iota(jnp.int32, k.shape, 1) < HEAD_DIM
            k_sw = pltpu.roll(k, HEAD_DIM, 1)
            kk_ref[lo:lo + n, 0:LANES] = jnp.where(low, k, k_sw).astype(bf16)
            kk_ref[lo:lo + n, LANES:2 * LANES] = jnp.where(low, k_sw, k).astype(bf16)

        def fill_values(lo, vt):
            n = vt.shape[1]
            ones_row = jnp.where(lax.broadcasted_iota(jnp.int32, (V_ROWS - HEAD_DIM, n), 0) == 0, 1.0, 0.0)
            for kv in range(N_KV_HEADS):
                vta_ref[kv * V_ROWS:kv * V_ROWS + HEAD_DIM, lo:lo + n] = (
                    vt[kv * HEAD_DIM:(kv + 1) * HEAD_DIM].astype(bf16))
                vta_ref[kv * V_ROWS + HEAD_DIM:(kv + 1) * V_ROWS, lo:lo + n] = ones_row.astype(bf16)

        if n_ctx:
            fill_keys(0, ckt_ref[0].T)
            fill_values(0, cvt_ref[0])
        fill_keys(n_ctx, ka_ref[0])
        fill_values(n_ctx, vt_ref[0])

        gain_max = jnp.max(jnp.abs(qg_ref[...]), axis=1, keepdims=True)
        all_small = None
        for kv in range(N_KV_HEADS):
            kt = kk_ref[:, kv * LANES:(kv + 1) * LANES].astype(f32)
            k_norm2 = jnp.max(0.5 * jnp.sum(kt * kt, axis=1, keepdims=True), axis=0, keepdims=True)
            bound = (LOG2_E * SCORE_BOUND_SLACK) * gain_max * jnp.sqrt(k_norm2)
            bound_ref[kv] = bound[0, 0]
            small = jnp.where(bound <= SCORE_BOUND_LIMIT, 1, 0)
            all_small = small if all_small is None else all_small * small
        fixed_ref[0] = all_small[0, 0]

    ts = min(Q_SUB, tq)
    low = lax.broadcasted_iota(jnp.int32, (ts, LANES), 1) < HEAD_DIM
    kc = min(KEY_CHUNK, n_keys)
    pairs_per_kv = N_HEADS // N_KV_HEADS // 2

    def attend(rows, fixed_shift):
        def masked_pair(tile):
            qp = qa_ref[0, rows, tile * LANES:(tile + 1) * LANES]
            zero = jnp.zeros_like(qp)
            return jnp.concatenate([jnp.where(low, qp, zero), jnp.where(low, zero, qp)], axis=0)

        q_pairs = [masked_pair(tile) for tile in range(N_HEADS // 2)]
        work = [(tile, lo) for tile in range(N_HEADS // 2) for lo in range(0, n_keys, kc)]

        def scores(tile, lo):
            kv = tile // pairs_per_kv
            return _dot_nt(kk_ref[lo:lo + kc, kv * LANES:(kv + 1) * LANES], q_pairs[tile])

        heads_t = []
        st_next = scores(*work[0])
        m = acc = None
        for i, (tile, lo) in enumerate(work):
            st = st_next
            if i + 1 < len(work):
                st_next = scores(*work[i + 1])
            kv = tile // pairs_per_kv
            values_t = vta_ref[kv * V_ROWS:(kv + 1) * V_ROWS, lo:lo + kc]
            if fixed_shift:
                e = jnp.exp2(st - bound_ref[kv]).astype(bf16)
                pv = jnp.dot(values_t, e, preferred_element_type=f32)
                acc = pv if acc is None else acc + pv
            else:
                m_chunk = jnp.max(st, axis=0, keepdims=True)
                m_new = m_chunk if m is None else jnp.maximum(m, m_chunk)
                e = jnp.exp2(st - m_new).astype(bf16)
                pv = jnp.dot(values_t, e, preferred_element_type=f32)
                acc = pv if acc is None else acc * jnp.exp2(m - m_new) + pv
                m = m_new
            if lo + kc == n_keys:
                on = (acc[0:HEAD_DIM] * (1.0 / acc[HEAD_DIM:HEAD_DIM + 1])).astype(bf16)
                heads_t += [on[:, :ts], on[:, ts:]]
                m = acc = None
        return jnp.concatenate(heads_t, axis=0)

    def project(rows, o_att_t):
        o_sum = of_ref[0, rows] + ob_ref[0, rows]
        normed = []
        for hd in range(HG_HEADS):
            oh = o_sum[:, hd * HG_HEAD_DIM:(hd + 1) * HG_HEAD_DIM]
            ms = jnp.mean(oh * oh, axis=-1, keepdims=True)
            normed.append(oh * lax.rsqrt(ms + RMS_EPS))
        o_hg = (jnp.concatenate(normed, axis=-1) * hg_ref[...] * sg_ref[0, rows]).astype(bf16)
        y = (jnp.dot(o_hg, wo_ref[0:HG_WIDTH, :], preferred_element_type=f32)
             + _dot_tn(o_att_t, wo_ref[HG_WIDTH:HG_WIDTH + ATT_WIDTH, :]))
        r = ALPHA * x_ref[0, rows] + m_ref[0, 5:6, :] * y
        o_ref[0, rows] = _layer_norm(r, g_ref[1:2, :], b_ref[1:2, :])

    def run(fixed_shift):
        for lo in range(0, tq, ts):
            rows = slice(lo, lo + ts)
            project(rows, attend(rows, fixed_shift))

    use_bound = fixed_ref[0] != 0
    pl.when(use_bound)(functools.partial(run, True))
    pl.when(jnp.logical_not(use_bound))(functools.partial(run, False))


def _attn(qa, ka, vt, ctx_kv_t, x, o_f, o_b, sg, mod, mod_group, w_out, hg_gain, q_gain, ln_g, ln_b, to_cast=()):
    groups, length, _ = qa.shape
    tq = min(Q_TILE, length)
    nt = length // tq
    n_ctx = 0 if ctx_kv_t is None else ctx_kv_t[0].shape[2]
    n_keys = n_ctx + length
    tok = lambda width: pl.BlockSpec((1, tq, width), lambda g, t: (g, t, 0))
    whole_t = lambda n: pl.BlockSpec((1, KV_WIDTH, n), lambda g, t: (g, 0, 0))
    in_specs = [tok(ATT_WIDTH), pl.BlockSpec((1, length, KV_WIDTH), lambda g, t: (g, 0, 0)), whole_t(length)]
    args = [qa, ka, vt]
    if n_ctx:
        in_specs += [whole_t(n_ctx), whole_t(n_ctx)]
        args += list(ctx_kv_t)
    in_specs += [
        tok(D_MODEL), tok(HG_WIDTH), tok(HG_WIDTH), tok(HG_WIDTH),
        pl.BlockSpec((1, N_MOD, D_MODEL), lambda g, t: (mod_group(g), 0, 0)),
        _resident((D_MODEL, D_MODEL)),
        _resident((1, HG_WIDTH)),
        _resident((1, ATT_WIDTH)),
        _resident((3, D_MODEL)),
        _resident((3, D_MODEL)),
    ]
    args += [x, o_f, o_b, sg, mod, w_out, hg_gain, q_gain, ln_g, ln_b]
    cast_specs, cast_shapes = _cast_specs(to_cast, nt, groups * nt)
    in_specs += cast_specs
    args += list(to_cast)
    out_specs = [tok(D_MODEL)] + cast_specs
    out_shape = [jax.ShapeDtypeStruct(x.shape, f32)] + cast_shapes
    return pl.pallas_call(
        functools.partial(_attn_kernel, n_ctx=n_ctx, n_cast=len(to_cast)),
        grid=(groups, nt),
        in_specs=in_specs,
        out_specs=out_specs,
        out_shape=out_shape,
        scratch_shapes=[pltpu.VMEM((n_keys, N_KV_HEADS * LANES), bf16),
                        pltpu.VMEM((N_KV_HEADS * V_ROWS, n_keys), bf16),
                        pltpu.SMEM((N_KV_HEADS,), f32),
                        pltpu.SMEM((1,), jnp.int32)],
        compiler_params=pltpu.CompilerParams(
            dimension_semantics=("arbitrary", "arbitrary"), vmem_limit_bytes=VMEM_LIMIT_BYTES),
        name="attn_out",
    )(*args)


def _rope_tables(n_tokens):
    half = HEAD_DIM // 2
    t = jnp.arange(n_tokens)
    inv = ROPE_THETA ** (-jnp.arange(0, half, 2, dtype=f32) / half)
    ang_row = (t // GRID_W).astype(f32)[:, None] * inv
    ang_col = (t % GRID_W).astype(f32)[:, None] * inv
    cos = jnp.concatenate([jnp.cos(ang_row)] * 2 + [jnp.cos(ang_col)] * 2, axis=-1)
    sin = jnp.concatenate([-jnp.sin(ang_row), jnp.sin(ang_row), -jnp.sin(ang_col), jnp.sin(ang_col)], axis=-1)
    return jnp.tile(cos, (1, LANES // HEAD_DIM)), jnp.tile(sin, (1, LANES // HEAD_DIM))


def kernel(x_prompt, x_sample, cache_k, cache_v, state_hgrn_fwd, state_hgrn_bwd, c, c_ctx, w_mod, b_mod,
           w_ffn1_in, w_ffn1_out, w_ffn2_in, w_ffn2_out, w_in, w_out, q_norm_g, k_norm_g, hg_norm_g,
           lb_logits_fwd, lb_logits_bwd, ln_g, ln_b):
    assert w_mod.shape[0] == DEPTH and lb_logits_fwd.shape[0] == DEPTH + 1
    batch, seq, _ = x_prompt.shape
    dec_batch, dec_seq, _ = x_sample.shape
    past = cache_k.shape[2]

    ctx_row = dec_batch
    rows = -(-(dec_batch + 1) // BF16_SUBLANES) * BF16_SUBLANES
    cvecs = jnp.concatenate([c, c_ctx[None, :], jnp.zeros((rows - dec_batch - 1, D_MODEL), f32)], axis=0)
    mod = _modulation(cvecs, w_mod[0], b_mod[0]).reshape(rows, N_MOD, D_MODEL)

    to_bf16 = lambda w: w.astype(bf16)
    w1u, w1d = to_bf16(w_ffn1_in[0]), to_bf16(w_ffn1_out[0])
    gains = ln_g[0], ln_b[0]
    q_gain = jnp.tile(q_norm_g[0], N_HEADS).reshape(1, ATT_WIDTH)
    k_gain = jnp.tile(k_norm_g[0], N_KV_HEADS).reshape(1, KV_WIDTH)
    hg_gain = hg_norm_g[0].reshape(1, HG_WIDTH)
    head_of = jnp.arange(ATT_WIDTH) // HEAD_DIM
    ones_bd = (head_of[:, None] == head_of[None, :]).astype(bf16)

    def mixer_sublayer(x, mod_group, rope_tables, ctx_kv_t, init_states, hg_groups, to_cast=()):
        shape = x.shape
        seq_len = shape[0] * shape[1] // hg_groups
        per_seq = lambda a: a.reshape(hg_groups, seq_len, a.shape[-1])
        proj = _proj(x, mod, mod_group, w_in_b, lb_logits_fwd, lb_logits_bwd, q_gain, k_gain, ones_bd, rope_tables,
                     seq_len)
        qs, vh, f_f, f_b, sg, qa, ka = map(per_seq, proj[:7])
        vt, span_f, span_b = proj[7], proj[8].reshape(hg_groups, -1, LANES), proj[9].reshape(hg_groups, -1, LANES)
        scans = _hgrn(qs, vh, f_f, f_b, span_f, span_b, init_states)
        x, *cast = _attn(qa, ka, vt, ctx_kv_t, per_seq(x), scans[0], scans[1], sg, mod, mod_group, w_out_b, hg_gain,
                         q_gain, *gains, to_cast=to_cast)
        return x.reshape(shape), proj[10:], vt, scans[2:], cast

    from_cache = lambda t: t[:, 0].transpose(0, 2, 3, 1).reshape(dec_batch, KV_WIDTH, past)
    init_states = (state_hgrn_fwd[:, 0], state_hgrn_bwd[:, 0])
    latent_group = lambda g: g
    x_latent, w_in_b, w_out_b = _ffn(x_sample, mod, latent_group, w1u, w1d, *gains, mod_base=0, ln_row=0,
                                     to_cast=(w_in[0], w_out[0]))
    x_latent, _, _, _, (w2u, w2d) = mixer_sublayer(
        x_latent, latent_group, _rope_tables(dec_seq), (from_cache(cache_k), from_cache(cache_v)), init_states,
        dec_batch, to_cast=(w_ffn2_in[0], w_ffn2_out[0]))

    ctx_group = lambda g: ctx_row
    x_ctx = _ffn(x_prompt.reshape(1, batch * seq, D_MODEL), mod, ctx_group, w1u, w1d, *gains, mod_base=0, ln_row=0)
    x_ctx, (kt_new,), vt_new, states, _ = mixer_sublayer(x_ctx, ctx_group, None, None, None, batch)
    y_sample = _ffn(x_latent, mod, latent_group, w2u, w2d, *gains, mod_base=6, ln_row=2)
    y_prompt = _ffn(x_ctx, mod, ctx_group, w2u, w2d, *gains, mod_base=6, ln_row=2)
    y_prompt = y_prompt.reshape(batch, seq, D_MODEL)
    to_cache = lambda t: t.reshape(batch, DEPTH, N_KV_HEADS, HEAD_DIM, seq).transpose(0, 1, 4, 2, 3)
    new_cache_k, new_cache_v = to_cache(kt_new), to_cache(vt_new)
    new_state_fwd = states[0].reshape(batch, DEPTH, HG_HEADS, HG_HEAD_DIM, HG_HEAD_DIM)
    new_state_bwd = states[1].reshape(batch, DEPTH, HG_HEADS, HG_HEAD_DIM, HG_HEAD_DIM)

    return (y_prompt, y_sample, new_cache_k, new_cache_v, new_state_fwd, new_state_bwd)
```

```python
import functools

import jax
import jax.numpy as jnp
from jax import lax
from jax.experimental import pallas as pl
from jax.experimental.pallas import tpu as pltpu

f32 = jnp.float32
bf16 = jnp.bfloat16

D_MODEL = 1024
N_MOD = 9
HG_WIDTH = 512
HG_HEAD_DIM = 128
HG_HEADS = 4
ATT_WIDTH = 512
HEAD_DIM = 64
N_HEADS = 8
N_KV_HEADS = 2
KV_WIDTH = 128
IN_WIDTH = 5 * HG_WIDTH + ATT_WIDTH + 2 * KV_WIDTH
D_FF = 2816
GRID_W = 64
ROPE_PAIR = HEAD_DIM // 4
ROPE_THETA = 10000.0
DEPTH = 1
ALPHA = (2.0 * DEPTH) ** 0.25
LOG2_E = 1.4426950408889634
LN_EPS = 1e-6
RMS_EPS = 1e-6

LANES = 128
BF16_SUBLANES = 16
VMEM_LIMIT_BYTES = 56 * 1024 * 1024

FF_CHUNK = 256
FFN_TILE = 1024
FFN_SUB = 512
TOKEN_TILE = 512
PROJ_SUB = 256
HG_TILE = 128
HG_STEP = 512
Q_TILE = 512
Q_SUB = 256
KEY_CHUNK = 512
SCORE_BOUND_LIMIT = 40.0
SCORE_BOUND_SLACK = 1.01
V_ROWS = 80
HG_FAST_BLOCK = 64
HG_FAST_SPAN_LOG2 = 100.0


def _silu(x):
    return x * jax.nn.sigmoid(x)


def _layer_norm(r, g, b):
    mu = jnp.mean(r, axis=-1, keepdims=True)
    c = r - mu
    var = jnp.mean(c * c, axis=-1, keepdims=True)
    return c * lax.rsqrt(var + LN_EPS) * g + b


def _split_bf16(x, parts):
    out = []
    r = x
    for _ in range(parts - 1):
        p = r.astype(bf16)
        out.append(p)
        r = r - p.astype(f32)
    out.append(r.astype(bf16))
    return out


def _dot01(mat01, x, parts, *, mat_on_left):
    pieces = _split_bf16(x, parts)
    if mat_on_left:
        return jnp.dot(jnp.concatenate([mat01] * parts, axis=1), jnp.concatenate(pieces, axis=0),
                       preferred_element_type=f32)
    return jnp.dot(jnp.concatenate(pieces, axis=1), jnp.concatenate([mat01] * parts, axis=0),
                   preferred_element_type=f32)


def _dot_nt(a, b):
    return lax.dot_general(a, b, (((1,), (1,)), ((), ())), preferred_element_type=f32)


def _dot_tn(a, b):
    return lax.dot_general(a, b, (((0,), (0,)), ((), ())), preferred_element_type=f32)


def _mod_kernel(c_ref, w_ref, b_ref, o_ref):
    a = _silu(c_ref[...]).astype(bf16)
    o_ref[...] = jnp.dot(a, w_ref[...].astype(bf16), preferred_element_type=f32) + b_ref[...]


def _modulation(cvecs, w_mod, b_mod):
    rows = cvecs.shape[0]
    n_out = w_mod.shape[1]
    tn = D_MODEL
    return pl.pallas_call(
        _mod_kernel,
        grid=(n_out // tn,),
        in_specs=[
            pl.BlockSpec((rows, D_MODEL), lambda j: (0, 0)),
            pl.BlockSpec((D_MODEL, tn), lambda j: (0, j)),
            pl.BlockSpec((1, tn), lambda j: (0, j)),
        ],
        out_specs=pl.BlockSpec((rows, tn), lambda j: (0, j)),
        out_shape=jax.ShapeDtypeStruct((rows, n_out), f32),
        compiler_params=pltpu.CompilerParams(dimension_semantics=("arbitrary",)),
        name="modulation",
    )(cvecs, w_mod, b_mod.reshape(1, n_out))


def _ffn_kernel(x_ref, m_ref, wup_ref, wd_ref, g_ref, b_ref, *refs, mod_base, ln_row):
    n_cast = (len(refs) - 2) // 2
    cast_in, o_ref, cast_out, act_ref = refs[:n_cast], refs[n_cast], refs[n_cast + 1:-1], refs[-1]
    for src_ref, dst_ref in zip(cast_in, cast_out):
        dst_ref[...] = src_ref[...].astype(bf16)
    shift = m_ref[0, mod_base:mod_base + 1, :]
    scale = m_ref[0, mod_base + 1:mod_base + 2, :]
    gate = m_ref[0, mod_base + 2:mod_base + 3, :]
    subs = [slice(lo, lo + FFN_SUB) for lo in range(0, x_ref.shape[1], FFN_SUB)]
    xs = [x_ref[0, rows] for rows in subs]
    hs = [(x * (1.0 + scale) + shift).astype(bf16) for x in xs]
    for j in range(D_FF // FF_CHUNK):
        cols = slice(j * FF_CHUNK, (j + 1) * FF_CHUNK)
        for rows, h in zip(subs, hs):
            a = jnp.dot(h, wup_ref[:, cols], preferred_element_type=f32)
            u = jnp.dot(h, wup_ref[:, D_FF + j * FF_CHUNK:D_FF + (j + 1) * FF_CHUNK],
                        preferred_element_type=f32)
            act_ref[rows, cols] = (_silu(a) * u).astype(bf16)
    ys = [jnp.dot(act_ref[rows, :], wd_ref[...], preferred_element_type=f32) for rows in subs]
    for rows, x, y in zip(subs, xs, ys):
        r = ALPHA * x + 0.5 * gate * y
        o_ref[0, rows] = _layer_norm(r, g_ref[ln_row:ln_row + 1, :], b_ref[ln_row:ln_row + 1, :])


def _resident(shape):
    return pl.BlockSpec(shape, lambda *_: (0,) * len(shape), pipeline_mode=pl.Buffered(1))


def _cast_specs(to_cast, nt, steps):
    specs, shapes = [], []
    for w in to_cast:
        rows = next(r for r in range(BF16_SUBLANES, w.shape[0] + 1, BF16_SUBLANES)
                    if w.shape[0] % r == 0 and w.shape[0] // r <= steps)
        specs.append(pl.BlockSpec((rows, w.shape[1]),
                                  lambda g, t, last=w.shape[0] // rows - 1: (jnp.minimum(g * nt + t, last), 0)))
        shapes.append(jax.ShapeDtypeStruct(w.shape, bf16))
    return specs, shapes


def _ffn(x, mod, mod_group, w_up, wd, ln_g, ln_b, *, mod_base, ln_row, to_cast=()):
    groups, length, _ = x.shape
    tm = FFN_TILE
    nt = length // tm
    cast_specs, cast_shapes = _cast_specs(to_cast, nt, groups * nt)
    out = pl.pallas_call(
        functools.partial(_ffn_kernel, mod_base=mod_base, ln_row=ln_row),
        grid=(groups, nt),
        in_specs=[
            pl.BlockSpec((1, tm, D_MODEL), lambda g, t: (g, t, 0)),
            pl.BlockSpec((1, N_MOD, D_MODEL), lambda g, t: (mod_group(g), 0, 0)),
            _resident((D_MODEL, 2 * D_FF)),
            _resident((D_FF, D_MODEL)),
            _resident((3, D_MODEL)),
            _resident((3, D_MODEL)),
        ] + cast_specs,
        out_specs=[pl.BlockSpec((1, tm, D_MODEL), lambda g, t: (g, t, 0))] + cast_specs,
        out_shape=[jax.ShapeDtypeStruct(x.shape, f32)] + cast_shapes,
        scratch_shapes=[pltpu.VMEM((tm, D_FF), bf16)],
        compiler_params=pltpu.CompilerParams(
            dimension_semantics=("arbitrary", "arbitrary"), vmem_limit_bytes=VMEM_LIMIT_BYTES),
        name="ffn",
    )(x, mod, w_up, wd, ln_g, ln_b, *to_cast)
    return out if to_cast else out[0]


def _head_rms_norm(x, ones_bd, gain, parts):
    ss = _dot01(ones_bd, x * x, parts, mat_on_left=False)
    return x * lax.rsqrt(ss * (1.0 / HEAD_DIM) + RMS_EPS) * gain


def _rope(x, cos, sin_signed):
    width = x.shape[-1]
    lane = lax.broadcasted_iota(jnp.int32, x.shape, 1)
    from_right = pltpu.roll(x, width - ROPE_PAIR, 1)
    from_left = pltpu.roll(x, ROPE_PAIR, 1)
    partner = jnp.where((lane & (2 * ROPE_PAIR - 1)) < ROPE_PAIR, from_right, from_left)
    return x * cos + partner * sin_signed


def _proj_kernel(*refs, rope):
    if rope:
        (x_ref, m_ref, w_ref, lbf_ref, lbb_ref, qg_ref, kg_ref, bd_ref, cos_ref, sin_ref,
         qs_ref, vh_ref, ff_ref, fb_ref, sg_ref, qa_ref, ka_ref, vt_ref,
         spf_ref, spb_ref) = refs
    else:
        (x_ref, m_ref, w_ref, lbf_ref, lbb_ref, qg_ref, kg_ref, bd_ref,
         qs_ref, vh_ref, ff_ref, fb_ref, sg_ref, qa_ref, ka_ref, vt_ref,
         spf_ref, spb_ref, kt_ref) = refs
    shift = m_ref[0, 3:4, :]
    scale = m_ref[0, 4:5, :]
    subs = [slice(lo, lo + PROJ_SUB) for lo in range(0, x_ref.shape[1], PROJ_SUB)]
    hs = [(x_ref[0, rows] * (1.0 + scale) + shift).astype(bf16) for rows in subs]

    def cols(h, lo, width):
        return jnp.dot(h, w_ref[:, lo:lo + width], preferred_element_type=f32)

    def lower_bound(lb_ref):
        l0 = lb_ref[0:1, :]
        l1 = lb_ref[1:2, :]
        m = jnp.maximum(l0, l1)
        e0 = jnp.exp(l0 - m)
        e1 = jnp.exp(l1 - m)
        return e0 / (e0 + e1)

    half = HG_FAST_BLOCK // 2

    def forget_gate(lb_ref, lo, f_ref, span_ref):
        lb = lower_bound(lb_ref)
        for rows, h in zip(subs, hs):
            f = lb + (1.0 - lb) * jax.nn.sigmoid(cols(h, lo, HG_WIDTH))
            f_ref[0, rows] = f
            sums = jnp.sum(jnp.log2(f).reshape(PROJ_SUB // half, half, HG_WIDTH), axis=1)
            span_ref[0, rows.start // half:rows.stop // half] = jnp.broadcast_to(
                jnp.max(jnp.abs(sums), axis=-1, keepdims=True), (PROJ_SUB // half, LANES))

    for rows, h in zip(subs, hs):
        qs_ref[0, rows] = _silu(cols(h, 0, HG_WIDTH))
    for rows, h in zip(subs, hs):
        vh_ref[0, rows] = cols(h, HG_WIDTH, HG_WIDTH).astype(bf16)
    forget_gate(lbf_ref, 2 * HG_WIDTH, ff_ref, spf_ref)
    forget_gate(lbb_ref, 3 * HG_WIDTH, fb_ref, spb_ref)
    for rows, h in zip(subs, hs):
        sg_ref[0, rows] = _silu(cols(h, 4 * HG_WIDTH, HG_WIDTH))

    base = 5 * HG_WIDTH
    piece = vt_ref.shape[2]
    for rows, h in zip(subs, hs):
        q = _head_rms_norm(cols(h, base, ATT_WIDTH), bd_ref[...], qg_ref[...], 1)
        k = _head_rms_norm(cols(h, base + ATT_WIDTH, KV_WIDTH), bd_ref[0:KV_WIDTH, 0:KV_WIDTH], kg_ref[...], 2)
        if rope:
            cos, sin = cos_ref[rows, :], sin_ref[rows, :]
            q = _rope(q, jnp.concatenate([cos] * (ATT_WIDTH // LANES), axis=1),
                      jnp.concatenate([sin] * (ATT_WIDTH // LANES), axis=1))
            k = _rope(k, cos, sin)
        qa_ref[0, rows] = (q * (HEAD_DIM ** -0.5 * LOG2_E)).astype(bf16)
        ka_ref[0, rows] = k
        v = cols(h, base + ATT_WIDTH + KV_WIDTH, KV_WIDTH)
        for lo in range(rows.start, rows.stop, min(piece, PROJ_SUB)):
            n = min(piece, PROJ_SUB)
            dst = (lo // piece, slice(None), slice(lo % piece, lo % piece + n))
            vt_ref[dst] = v[lo - rows.start:lo - rows.start + n].T
            if not rope:
                kt_ref[dst] = k[lo - rows.start:lo - rows.start + n].T


def _proj(x, mod, mod_group, w_in, lb_f, lb_b, q_gain, k_gain, ones_bd, rope_tables, seq_len):
    groups, length, _ = x.shape
    tm = TOKEN_TILE
    rope = rope_tables is not None
    n_seq = groups * length // seq_len
    if seq_len >= tm:
        tiles_per_seq = seq_len // tm
        t_spec = pl.BlockSpec((1, KV_WIDTH, tm), lambda g, t: (g * (length // seq_len) + t // tiles_per_seq, 0,
                                                               t % tiles_per_seq))
    else:
        t_spec = pl.BlockSpec((tm // seq_len, KV_WIDTH, seq_len), lambda g, t: (g * (length // tm) + t, 0, 0))
    t_shape = jax.ShapeDtypeStruct((n_seq, KV_WIDTH, seq_len), f32)
    tok = lambda width: pl.BlockSpec((1, tm, width), lambda g, t: (g, t, 0))
    in_specs = [
        tok(D_MODEL),
        pl.BlockSpec((1, N_MOD, D_MODEL), lambda g, t: (mod_group(g), 0, 0)),
        _resident((D_MODEL, IN_WIDTH)),
        _resident((2, HG_WIDTH)),
        _resident((2, HG_WIDTH)),
        _resident((1, ATT_WIDTH)),
        _resident((1, KV_WIDTH)),
        _resident((ATT_WIDTH, ATT_WIDTH)),
    ]
    args = [x, mod, w_in, lb_f, lb_b, q_gain, k_gain, ones_bd]
    if rope:
        in_specs += [pl.BlockSpec((tm, LANES), lambda g, t: (t, 0))] * 2
        args += list(rope_tables)
    shape = lambda width, dt: jax.ShapeDtypeStruct((groups, length, width), dt)
    half = HG_FAST_BLOCK // 2
    span_spec = pl.BlockSpec((1, tm // half, LANES), lambda g, t: (g, t, 0))
    span_shape = jax.ShapeDtypeStruct((groups, length // half, LANES), f32)
    return pl.pallas_call(
        functools.partial(_proj_kernel, rope=rope),
        grid=(groups, length // tm),
        in_specs=in_specs,
        out_specs=([tok(HG_WIDTH)] * 5 + [tok(ATT_WIDTH), tok(KV_WIDTH), t_spec, span_spec, span_spec]
                   + ([] if rope else [t_spec])),
        out_shape=([shape(HG_WIDTH, f32), shape(HG_WIDTH, bf16)] + [shape(HG_WIDTH, f32)] * 3
                   + [shape(ATT_WIDTH, bf16), shape(KV_WIDTH, f32), t_shape, span_shape, span_shape]
                   + ([] if rope else [t_shape])),
        compiler_params=pltpu.CompilerParams(
            dimension_semantics=("arbitrary", "arbitrary"), vmem_limit_bytes=VMEM_LIMIT_BYTES),
        name="mixer_proj",
    )(*args)


def _hgrn_direction(qs, vb, f, st_ref, o_ref, rows, *, rev):
    tc = qs.shape[0]
    lf = jnp.log2(f)
    kk = 1.0 - f
    row = lax.broadcasted_iota(jnp.int32, (tc, tc), 0)
    col = lax.broadcasted_iota(jnp.int32, (tc, tc), 1)
    ordered = (row < col) if rev else (row > col)
    differ = row ^ col
    tri = jnp.where((col >= row) if rev else (col <= row), 1.0, 0.0).astype(bf16)
    cum = _dot01(tri, lf, 2, mat_on_left=True)
    total = cum[0:1] if rev else cum[tc - 1:tc]
    carry = jnp.exp2(total)
    query_half = 0 if rev else 1
    heads = [slice(hd * HG_HEAD_DIM, (hd + 1) * HG_HEAD_DIM) for hd in range(HG_HEADS)]

    def level_exponent(s, ridx):
        if s == 1:
            return jnp.where((ridx & 1) == query_half, lf, 0.0)
        if s == 2:
            nxt = pltpu.roll(lf, tc - 1, 0)
            prv = pltpu.roll(lf, 1, 0)
            m4 = ridx & 3
            if rev:
                return jnp.where(m4 == 0, lf + nxt, jnp.where(m4 == 1, lf, jnp.where(m4 == 2, 0.0, prv)))
            return jnp.where(m4 == 0, nxt, jnp.where(m4 == 1, 0.0, jnp.where(m4 == 2, lf, lf + prv)))
        blocks = []
        for lo in range(0, tc, 2 * s):
            anchor = lo + (s if rev else s - 1)
            blocks.append(cum[lo:lo + 2 * s] - cum[anchor:anchor + 1])
        d = jnp.concatenate(blocks, axis=0)
        is_query = ((ridx >> (s.bit_length() - 1)) & 1) == query_half
        return jnp.where(is_query, d, -d)

    def add_levels(intra, first):
        ridx = lax.broadcasted_iota(jnp.int32, qs.shape, 0)
        s = first
        while s < tc:
            shift = s.bit_length() - 1
            is_query = ((ridx >> shift) & 1) == query_half
            mixed = (jnp.where(is_query, qs, kk) * jnp.exp2(level_exponent(s, ridx))).astype(bf16)
            pair = jnp.logical_and((differ >> shift) == 1, ordered)
            for hd, hs in enumerate(heads):
                intra[hd] = jnp.where(pair, _dot_nt(mixed[:, hs], mixed[:, hs]), intra[hd])
            s *= 2
        return intra

    def finish(intra, q_dec, k_end, extra):
        for hd, hs in enumerate(heads):
            st = st_ref[hd]
            o = jnp.dot(jnp.concatenate([intra[hd].astype(bf16), q_dec[:, hs]], axis=1),
                        jnp.concatenate([vb[:, hs], st.T.astype(bf16)], axis=0), preferred_element_type=f32)
            o_ref[0, rows, hs] = o if extra is None else o + extra[:, hs]
            st_ref[hd] = st * carry[:, hs] + _dot_tn(vb[:, hs], k_end[:, hs])

    blk = HG_FAST_BLOCK
    anchors = [lo + (blk // 2 if rev else blk // 2 - 1) for lo in range(0, tc, blk)]

    def mid_split_blocks():
        q_mid, k_mid, q_dec, k_end = [], [], [], []
        for lo, a in zip(range(0, tc, blk), anchors):
            rel = cum[lo:lo + blk] - cum[a:a + 1]
            q_mid.append(qs[lo:lo + blk] * jnp.exp2(rel))
            k_mid.append(kk[lo:lo + blk] * jnp.exp2(-rel))
            q_dec.append(q_mid[-1] * jnp.exp2(cum[a:a + 1]))
            k_end.append(k_mid[-1] * jnp.exp2(total - cum[a:a + 1]))
        q_mid, k_mid, q_dec, k_end = (jnp.concatenate(p, axis=0).astype(bf16) for p in (q_mid, k_mid, q_dec, k_end))
        same_block = (differ >> (blk.bit_length() - 1)) == 0
        keep = jnp.logical_and(same_block, (row <= col) if rev else (row >= col))
        intra = [jnp.where(keep, _dot_nt(q_mid[:, hs], k_mid[:, hs]), 0.0) for hs in heads]
        finish(add_levels(intra, blk), q_dec, k_end, None)

    def all_levels():
        intra = add_levels([jnp.zeros((tc, tc), f32)] * HG_HEADS, 1)
        q_dec = (qs * jnp.exp2(cum)).astype(bf16)
        k_end = (kk * jnp.exp2(total - cum)).astype(bf16)
        qk = qs * kk
        own = jnp.concatenate(
            [jnp.sum(qk[:, hs], axis=-1, keepdims=True) * vb[:, hs].astype(f32) for hs in heads], axis=-1)
        finish(intra, q_dec, k_end, own)

    return mid_split_blocks, all_levels


def _hgrn_kernel(slow_ref, *refs, has_init):
    if has_init:
        (qsf_ref, vf_ref, ff_ref, qsb_ref, vb_ref, fb_ref, s0f_ref, s0b_ref,
         of_ref, ob_ref, stf_ref, stb_ref) = refs
    else:
        (qsf_ref, vf_ref, ff_ref, qsb_ref, vb_ref, fb_ref,
         of_ref, ob_ref, sf_out_ref, sb_out_ref, stf_ref, stb_ref) = refs
    t = pl.program_id(1)

    @pl.when(t == 0)
    def _():
        for hd in range(HG_HEADS):
            if has_init:
                stf_ref[hd] = s0f_ref[0, hd].T
                stb_ref[hd] = s0b_ref[0, hd].T
            else:
                stf_ref[hd] = jnp.zeros((HG_HEAD_DIM, HG_HEAD_DIM), f32)
                stb_ref[hd] = jnp.zeros((HG_HEAD_DIM, HG_HEAD_DIM), f32)

    chunks = [slice(lo, lo + HG_TILE) for lo in range(0, qsf_ref.shape[1], HG_TILE)]
    scans = []
    for r_f, r_b in zip(chunks, reversed(chunks)):
        scans.append(_hgrn_direction(qsf_ref[0, r_f], vf_ref[0, r_f], ff_ref[0, r_f], stf_ref, of_ref, r_f, rev=False))
        scans.append(_hgrn_direction(qsb_ref[0, r_b], vb_ref[0, r_b], fb_ref[0, r_b], stb_ref, ob_ref, r_b, rev=True))
    slow = slow_ref[pl.program_id(0) * pl.num_programs(1) + t] != 0

    @pl.when(jnp.logical_not(slow))
    def _():
        for mid_split_blocks, _ in scans:
            mid_split_blocks()

    @pl.when(slow)
    def _():
        for _, all_levels in scans:
            all_levels()

    if not has_init:
        @pl.when(t == pl.num_programs(1) - 1)
        def _():
            for hd in range(HG_HEADS):
                sf_out_ref[0, hd] = stf_ref[hd].T
                sb_out_ref[0, hd] = stb_ref[hd].T


def _hgrn(qs, vh, f_fwd, f_bwd, span_fwd, span_bwd, init_states):
    groups, length, _ = qs.shape
    step = min(HG_STEP, length)
    nt = length // step
    has_init = init_states is not None
    per_step = lambda span: jnp.max(span[:, :, 0].reshape(groups, nt, -1), axis=-1) > HG_FAST_SPAN_LOG2
    slow = jnp.logical_or(per_step(span_fwd), per_step(span_bwd)[:, ::-1]).astype(jnp.int32).reshape(-1)
    fwd = pl.BlockSpec((1, step, HG_WIDTH), lambda g, t, _: (g, t, 0))
    bwd = pl.BlockSpec((1, step, HG_WIDTH), lambda g, t, _: (g, nt - 1 - t, 0))
    state = pl.BlockSpec((1, HG_HEADS, HG_HEAD_DIM, HG_HEAD_DIM), lambda g, t, _: (g, 0, 0, 0))
    in_specs = [fwd, fwd, fwd, bwd, bwd, bwd]
    args = [qs, vh, f_fwd, qs, vh, f_bwd]
    out_specs = [fwd, bwd]
    out_shape = [jax.ShapeDtypeStruct(qs.shape, f32)] * 2
    if has_init:
        in_specs += [state, state]
        args += list(init_states)
    else:
        out_specs += [state, state]
        out_shape += [jax.ShapeDtypeStruct((groups, HG_HEADS, HG_HEAD_DIM, HG_HEAD_DIM), f32)] * 2
    return pl.pallas_call(
        functools.partial(_hgrn_kernel, has_init=has_init),
        grid_spec=pltpu.PrefetchScalarGridSpec(
            num_scalar_prefetch=1,
            grid=(groups, nt),
            in_specs=in_specs,
            out_specs=out_specs,
            scratch_shapes=[pltpu.VMEM((HG_HEADS, HG_HEAD_DIM, HG_HEAD_DIM), f32)] * 2),
        out_shape=out_shape,
        compiler_params=pltpu.CompilerParams(
            dimension_semantics=("arbitrary", "arbitrary"), vmem_limit_bytes=VMEM_LIMIT_BYTES),
        name="hgrn2",
    )(slow, *args)


def _attn_kernel(*refs, n_ctx, n_cast):
    kk_ref, vta_ref, bound_ref, fixed_ref = refs[-4:]
    o_ref, *cast_out = refs[len(refs) - 5 - n_cast:-4]
    inputs = refs[:len(refs) - 5 - n_cast]
    cast_in = inputs[len(inputs) - n_cast:]
    if n_ctx:
        (qa_ref, ka_ref, vt_ref, ckt_ref, cvt_ref, x_ref, of_ref, ob_ref, sg_ref, m_ref, wo_ref, hg_ref, qg_ref,
         g_ref, b_ref) = inputs[:len(inputs) - n_cast]
    else:
        (qa_ref, ka_ref, vt_ref, x_ref, of_ref, ob_ref, sg_ref, m_ref, wo_ref, hg_ref, qg_ref,
         g_ref, b_ref) = inputs[:len(inputs) - n_cast]
    for src_ref, dst_ref in zip(cast_in, cast_out):
        dst_ref[...] = src_ref[...].astype(bf16)
    tq = qa_ref.shape[1]
    n_keys = kk_ref.shape[0]

    @pl.when(pl.program_id(1) == 0)
    def _():
        def fill_keys(lo, k):
            n = k.shape[0]
            low = lax.broadcasted_iota(jnp.int32, k.shape, 1) < HEAD_DIM
            k_sw = pltpu.roll(k, HEAD_DIM, 1)
            kk_ref[lo:lo + n, 0:LANES] = jnp.where(low, k, k_sw).astype(bf16)
            kk_ref[lo:lo + n, LANES:2 * LANES] = jnp.where(low, k_sw, k).astype(bf16)

        def fill_values(lo, vt):
            n = vt.shape[1]
            ones_row = jnp.where(lax.broadcasted_iota(jnp.int32, (V_ROWS - HEAD_DIM, n), 0) == 0, 1.0, 0.0)
            for kv in range(N_KV_HEADS):
                vta_ref[kv * V_ROWS:kv * V_ROWS + HEAD_DIM, lo:lo + n] = (
                    vt[kv * HEAD_DIM:(kv + 1) * HEAD_DIM].astype(bf16))
                vta_ref[kv * V_ROWS + HEAD_DIM:(kv + 1) * V_ROWS, lo:lo + n] = ones_row.astype(bf16)

        if n_ctx:
            fill_keys(0, ckt_ref[0].T)
            fill_values(0, cvt_ref[0])
        fill_keys(n_ctx, ka_ref[0])
        fill_values(n_ctx, vt_ref[0])

        gain_max = jnp.max(jnp.abs(qg_ref[...]), axis=1, keepdims=True)
        all_small = None
        for kv in range(N_KV_HEADS):
            kt = kk_ref[:, kv * LANES:(kv + 1) * LANES].astype(f32)
            k_norm2 = jnp.max(0.5 * jnp.sum(kt * kt, axis=1, keepdims=True), axis=0, keepdims=True)
            bound = (LOG2_E * SCORE_BOUND_SLACK) * gain_max * jnp.sqrt(k_norm2)
            bound_ref[kv] = bound[0, 0]
            small = jnp.where(bound <= SCORE_BOUND_LIMIT, 1, 0)
            all_small = small if all_small is None else all_small * small
        fixed_ref[0] = all_small[0, 0]

    ts = min(Q_SUB, tq)
    low = lax.broadcasted_iota(jnp.int32, (ts, LANES), 1) < HEAD_DIM
    kc = min(KEY_CHUNK, n_keys)
    pairs_per_kv = N_HEADS // N_KV_HEADS // 2

    def attend(rows, fixed_shift):
        def masked_pair(tile):
            qp = qa_ref[0, rows, tile * LANES:(tile + 1) * LANES]
            zero = jnp.zeros_like(qp)
            return jnp.concatenate([jnp.where(low, qp, zero), jnp.where(low, zero, qp)], axis=0)

        q_pairs = [masked_pair(tile) for tile in range(N_HEADS // 2)]
        work = [(tile, lo) for tile in range(N_HEADS // 2) for lo in range(0, n_keys, kc)]

        def scores(tile, lo):
            kv = tile // pairs_per_kv
            return _dot_nt(kk_ref[lo:lo + kc, kv * LANES:(kv + 1) * LANES], q_pairs[tile])

        heads_t = []
        st_next = scores(*work[0])
        m = acc = None
        for i, (tile, lo) in enumerate(work):
            st = st_next
            if i + 1 < len(work):
                st_next = scores(*work[i + 1])
            kv = tile // pairs_per_kv
            values_t = vta_ref[kv * V_ROWS:(kv + 1) * V_ROWS, lo:lo + kc]
            if fixed_shift:
                e = jnp.exp2(st - bound_ref[kv]).astype(bf16)
                pv = jnp.dot(values_t, e, preferred_element_type=f32)
                acc = pv if acc is None else acc + pv
            else:
                m_chunk = jnp.max(st, axis=0, keepdims=True)
                m_new = m_chunk if m is None else jnp.maximum(m, m_chunk)
                e = jnp.exp2(st - m_new).astype(bf16)
                pv = jnp.dot(values_t, e, preferred_element_type=f32)
                acc = pv if acc is None else acc * jnp.exp2(m - m_new) + pv
                m = m_new
            if lo + kc == n_keys:
                on = (acc[0:HEAD_DIM] * (1.0 / acc[HEAD_DIM:HEAD_DIM + 1])).astype(bf16)
                heads_t += [on[:, :ts], on[:, ts:]]
                m = acc = None
        return jnp.concatenate(heads_t, axis=0)

    def project(rows, o_att_t):
        o_sum = of_ref[0, rows] + ob_ref[0, rows]
        normed = []
        for hd in range(HG_HEADS):
            oh = o_sum[:, hd * HG_HEAD_DIM:(hd + 1) * HG_HEAD_DIM]
            ms = jnp.mean(oh * oh, axis=-1, keepdims=True)
            normed.append(oh * lax.rsqrt(ms + RMS_EPS))
        o_hg = (jnp.concatenate(normed, axis=-1) * hg_ref[...] * sg_ref[0, rows]).astype(bf16)
        y = (jnp.dot(o_hg, wo_ref[0:HG_WIDTH, :], preferred_element_type=f32)
             + _dot_tn(o_att_t, wo_ref[HG_WIDTH:HG_WIDTH + ATT_WIDTH, :]))
        r = ALPHA * x_ref[0, rows] + m_ref[0, 5:6, :] * y
        o_ref[0, rows] = _layer_norm(r, g_ref[1:2, :], b_ref[1:2, :])

    def run(fixed_shift):
        for lo in range(0, tq, ts):
            rows = slice(lo, lo + ts)
            project(rows, attend(rows, fixed_shift))

    use_bound = fixed_ref[0] != 0
    pl.when(use_bound)(functools.partial(run, True))
    pl.when(jnp.logical_not(use_bound))(functools.partial(run, False))


def _attn(qa, ka, vt, ctx_kv_t, x, o_f, o_b, sg, mod, mod_group, w_out, hg_gain, q_gain, ln_g, ln_b, to_cast=()):
    groups, length, _ = qa.shape
    tq = min(Q_TILE, length)
    nt = length // tq
    n_ctx = 0 if ctx_kv_t is None else ctx_kv_t[0].shape[2]
    n_keys = n_ctx + length
    tok = lambda width: pl.BlockSpec((1, tq, width), lambda g, t: (g, t, 0))
    whole_t = lambda n: pl.BlockSpec((1, KV_WIDTH, n), lambda g, t: (g, 0, 0))
    in_specs = [tok(ATT_WIDTH), pl.BlockSpec((1, length, KV_WIDTH), lambda g, t: (g, 0, 0)), whole_t(length)]
    args = [qa, ka, vt]
    if n_ctx:
        in_specs += [whole_t(n_ctx), whole_t(n_ctx)]
        args += list(ctx_kv_t)
    in_specs += [
        tok(D_MODEL), tok(HG_WIDTH), tok(HG_WIDTH), tok(HG_WIDTH),
        pl.BlockSpec((1, N_MOD, D_MODEL), lambda g, t: (mod_group(g), 0, 0)),
        _resident((D_MODEL, D_MODEL)),
        _resident((1, HG_WIDTH)),
        _resident((1, ATT_WIDTH)),
        _resident((3, D_MODEL)),
        _resident((3, D_MODEL)),
    ]
    args += [x, o_f, o_b, sg, mod, w_out, hg_gain, q_gain, ln_g, ln_b]
    cast_specs, cast_shapes = _cast_specs(to_cast, nt, groups * nt)
    in_specs += cast_specs
    args += list(to_cast)
    out_specs = [tok(D_MODEL)] + cast_specs
    out_shape = [jax.ShapeDtypeStruct(x.shape, f32)] + cast_shapes
    return pl.pallas_call(
        functools.partial(_attn_kernel, n_ctx=n_ctx, n_cast=len(to_cast)),
        grid=(groups, nt),
        in_specs=in_specs,
        out_specs=out_specs,
        out_shape=out_shape,
        scratch_shapes=[pltpu.VMEM((n_keys, N_KV_HEADS * LANES), bf16),
                        pltpu.VMEM((N_KV_HEADS * V_ROWS, n_keys), bf16),
                        pltpu.SMEM((N_KV_HEADS,), f32),
                        pltpu.SMEM((1,), jnp.int32)],
        compiler_params=pltpu.CompilerParams(
            dimension_semantics=("arbitrary", "arbitrary"), vmem_limit_bytes=VMEM_LIMIT_BYTES),
        name="attn_out",
    )(*args)


def _rope_tables(n_tokens):
    half = HEAD_DIM // 2
    t = jnp.arange(n_tokens)
    inv = ROPE_THETA ** (-jnp.arange(0, half, 2, dtype=f32) / half)
    ang_row = (t // GRID_W).astype(f32)[:, None] * inv
    ang_col = (t % GRID_W).astype(f32)[:, None] * inv
    cos = jnp.concatenate([jnp.cos(ang_row)] * 2 + [jnp.cos(ang_col)] * 2, axis=-1)
    sin = jnp.concatenate([-jnp.sin(ang_row), jnp.sin(ang_row), -jnp.sin(ang_col), jnp.sin(ang_col)], axis=-1)
    return jnp.tile(cos, (1, LANES // HEAD_DIM)), jnp.tile(sin, (1, LANES // HEAD_DIM))


def kernel(x_prompt, x_sample, cache_k, cache_v, state_hgrn_fwd, state_hgrn_bwd, c, c_ctx, w_mod, b_mod,
           w_ffn1_in, w_ffn1_out, w_ffn2_in, w_ffn2_out, w_in, w_out, q_norm_g, k_norm_g, hg_norm_g,
           lb_logits_fwd, lb_logits_bwd, ln_g, ln_b):
    assert w_mod.shape[0] == DEPTH and lb_logits_fwd.shape[0] == DEPTH + 1
    batch, seq, _ = x_prompt.shape
    dec_batch, dec_seq, _ = x_sample.shape
    past = cache_k.shape[2]

    ctx_row = dec_batch
    rows = -(-(dec_batch + 1) // BF16_SUBLANES) * BF16_SUBLANES
    cvecs = jnp.concatenate([c, c_ctx[None, :], jnp.zeros((rows - dec_batch - 1, D_MODEL), f32)], axis=0)
    mod = _modulation(cvecs, w_mod[0], b_mod[0]).reshape(rows, N_MOD, D_MODEL)

    to_bf16 = lambda w: w.astype(bf16)
    w1u, w1d = to_bf16(w_ffn1_in[0]), to_bf16(w_ffn1_out[0])
    gains = ln_g[0], ln_b[0]
    q_gain = jnp.tile(q_norm_g[0], N_HEADS).reshape(1, ATT_WIDTH)
    k_gain = jnp.tile(k_norm_g[0], N_KV_HEADS).reshape(1, KV_WIDTH)
    hg_gain = hg_norm_g[0].reshape(1, HG_WIDTH)
    head_of = jnp.arange(ATT_WIDTH) // HEAD_DIM
    ones_bd = (head_of[:, None] == head_of[None, :]).astype(bf16)

    def mixer_sublayer(x, mod_group, rope_tables, ctx_kv_t, init_states, hg_groups, to_cast=()):
        shape = x.shape
        seq_len = shape[0] * shape[1] // hg_groups
        per_seq = lambda a: a.reshape(hg_groups, seq_len, a.shape[-1])
        proj = _proj(x, mod, mod_group, w_in_b, lb_logits_fwd, lb_logits_bwd, q_gain, k_gain, ones_bd, rope_tables,
                     seq_len)
        qs, vh, f_f, f_b, sg, qa, ka = map(per_seq, proj[:7])
        vt, span_f, span_b = proj[7], proj[8].reshape(hg_groups, -1, LANES), proj[9].reshape(hg_groups, -1, LANES)
        scans = _hgrn(qs, vh, f_f, f_b, span_f, span_b, init_states)
        x, *cast = _attn(qa, ka, vt, ctx_kv_t, per_seq(x), scans[0], scans[1], sg, mod, mod_group, w_out_b, hg_gain,
                         q_gain, *gains, to_cast=to_cast)
        return x.reshape(shape), proj[10:], vt, scans[2:], cast

    from_cache = lambda t: t[:, 0].transpose(0, 2, 3, 1).reshape(dec_batch, KV_WIDTH, past)
    init_states = (state_hgrn_fwd[:, 0], state_hgrn_bwd[:, 0])
    latent_group = lambda g: g
    x_latent, w_in_b, w_out_b = _ffn(x_sample, mod, latent_group, w1u, w1d, *gains, mod_base=0, ln_row=0,
                                     to_cast=(w_in[0], w_out[0]))
    x_latent, _, _, _, (w2u, w2d) = mixer_sublayer(
        x_latent, latent_group, _rope_tables(dec_seq), (from_cache(cache_k), from_cache(cache_v)), init_states,
        dec_batch, to_cast=(w_ffn2_in[0], w_ffn2_out[0]))

    ctx_group = lambda g: ctx_row
    x_ctx = _ffn(x_prompt.reshape(1, batch * seq, D_MODEL), mod, ctx_group, w1u, w1d, *gains, mod_base=0, ln_row=0)
    x_ctx, (kt_new,), vt_new, states, _ = mixer_sublayer(x_ctx, ctx_group, None, None, None, batch)
    y_sample = _ffn(x_latent, mod, latent_group, w2u, w2d, *gains, mod_base=6, ln_row=2)
    y_prompt = _ffn(x_ctx, mod, ctx_group, w2u, w2d, *gains, mod_base=6, ln_row=2)
    y_prompt = y_prompt.reshape(batch, seq, D_MODEL)
    to_cache = lambda t: t.reshape(batch, DEPTH, N_KV_HEADS, HEAD_DIM, seq).transpose(0, 1, 4, 2, 3)
    new_cache_k, new_cache_v = to_cache(kt_new), to_cache(vt_new)
    new_state_fwd = states[0].reshape(batch, DEPTH, HG_HEADS, HG_HEAD_DIM, HG_HEAD_DIM)
    new_state_bwd = states[1].reshape(batch, DEPTH, HG_HEADS, HG_HEAD_DIM, HG_HEAD_DIM)

    return (y_prompt, y_sample, new_cache_k, new_cache_v, new_state_fwd, new_state_bwd)
```

```python
import functools

import jax
import jax.numpy as jnp
from jax import lax
from jax.experimental import pallas as pl
from jax.experimental.pallas import tpu as pltpu

f32 = jnp.float32
bf16 = jnp.bfloat16

D_MODEL = 1024
N_MOD = 9
HG_WIDTH = 512
HG_HEAD_DIM = 128
HG_HEADS = 4
ATT_WIDTH = 512
HEAD_DIM = 64
N_HEADS = 8
N_KV_HEADS = 2
KV_WIDTH = 128
IN_WIDTH = 5 * HG_WIDTH + ATT_WIDTH + 2 * KV_WIDTH
D_FF = 2816
GRID_W = 64
ROPE_PAIR = HEAD_DIM // 4
ROPE_THETA = 10000.0
DEPTH = 1
ALPHA = (2.0 * DEPTH) ** 0.25
LOG2_E = 1.4426950408889634
LN_EPS = 1e-6
RMS_EPS = 1e-6

LANES = 128
BF16_SUBLANES = 16
VMEM_LIMIT_BYTES = 56 * 1024 * 1024

FF_CHUNK = 256
FFN_TILE = 1024
FFN_SUB = 512
TOKEN_TILE = 512
PROJ_SUB = 256
HG_TILE = 128
HG_STEP = 512
HG_SEQS = 2
Q_TILE = 512
Q_SUB = 256
KEY_CHUNK = 512
SCORE_BOUND_LIMIT = 40.0
SCORE_BOUND_SLACK = 1.01
V_ROWS = 80
HG_FAST_BLOCK = 64
HG_FAST_SPAN_LOG2 = 100.0


def _silu(x):
    return x * jax.nn.sigmoid(x)


def _layer_norm(r, g, b):
    mu = jnp.mean(r, axis=-1, keepdims=True)
    c = r - mu
    var = jnp.mean(c * c, axis=-1, keepdims=True)
    return c * lax.rsqrt(var + LN_EPS) * g + b


def _split_bf16(x, parts):
    out = []
    r = x
    for _ in range(parts - 1):
        p = r.astype(bf16)
        out.append(p)
        r = r - p.astype(f32)
    out.append(r.astype(bf16))
    return out


def _dot01(mat01, x, parts, *, mat_on_left):
    pieces = _split_bf16(x, parts)
    if mat_on_left:
        return jnp.dot(jnp.concatenate([mat01] * parts, axis=1), jnp.concatenate(pieces, axis=0),
                       preferred_element_type=f32)
    return jnp.dot(jnp.concatenate(pieces, axis=1), jnp.concatenate([mat01] * parts, axis=0),
                   preferred_element_type=f32)


def _dot_nt(a, b):
    return lax.dot_general(a, b, (((1,), (1,)), ((), ())), preferred_element_type=f32)


def _dot_tn(a, b):
    return lax.dot_general(a, b, (((0,), (0,)), ((), ())), preferred_element_type=f32)


def _mod_kernel(c_ref, w_ref, b_ref, o_ref):
    a = _silu(c_ref[...]).astype(bf16)
    o_ref[...] = jnp.dot(a, w_ref[...].astype(bf16), preferred_element_type=f32) + b_ref[...]


def _modulation(cvecs, w_mod, b_mod):
    rows = cvecs.shape[0]
    n_out = w_mod.shape[1]
    tn = D_MODEL
    return pl.pallas_call(
        _mod_kernel,
        grid=(n_out // tn,),
        in_specs=[
            pl.BlockSpec((rows, D_MODEL), lambda j: (0, 0)),
            pl.BlockSpec((D_MODEL, tn), lambda j: (0, j)),
            pl.BlockSpec((1, tn), lambda j: (0, j)),
        ],
        out_specs=pl.BlockSpec((rows, tn), lambda j: (0, j)),
        out_shape=jax.ShapeDtypeStruct((rows, n_out), f32),
        compiler_params=pltpu.CompilerParams(dimension_semantics=("arbitrary",)),
        name="modulation",
    )(cvecs, w_mod, b_mod.reshape(1, n_out))


def _ffn_kernel(x_ref, m_ref, wup_ref, wd_ref, g_ref, b_ref, *refs, mod_base, ln_row):
    n_cast = (len(refs) - 2) // 2
    cast_in, o_ref, cast_out, act_ref = refs[:n_cast], refs[n_cast], refs[n_cast + 1:-1], refs[-1]
    for src_ref, dst_ref in zip(cast_in, cast_out):
        dst_ref[...] = src_ref[...].astype(bf16)
    shift = m_ref[0, mod_base:mod_base + 1, :]
    scale = m_ref[0, mod_base + 1:mod_base + 2, :]
    gate = m_ref[0, mod_base + 2:mod_base + 3, :]
    subs = [slice(lo, lo + FFN_SUB) for lo in range(0, x_ref.shape[1], FFN_SUB)]
    xs = [x_ref[0, rows] for rows in subs]
    hs = [(x * (1.0 + scale) + shift).astype(bf16) for x in xs]
    for j in range(D_FF // FF_CHUNK):
        cols = slice(j * FF_CHUNK, (j + 1) * FF_CHUNK)
        for rows, h in zip(subs, hs):
            a = jnp.dot(h, wup_ref[:, cols], preferred_element_type=f32)
            u = jnp.dot(h, wup_ref[:, D_FF + j * FF_CHUNK:D_FF + (j + 1) * FF_CHUNK],
                        preferred_element_type=f32)
            act_ref[rows, cols] = (_silu(a) * u).astype(bf16)
    ys = [jnp.dot(act_ref[rows, :], wd_ref[...], preferred_element_type=f32) for rows in subs]
    for rows, x, y in zip(subs, xs, ys):
        r = ALPHA * x + 0.5 * gate * y
        o_ref[0, rows] = _layer_norm(r, g_ref[ln_row:ln_row + 1, :], b_ref[ln_row:ln_row + 1, :])


def _resident(shape):
    return pl.BlockSpec(shape, lambda *_: (0,) * len(shape), pipeline_mode=pl.Buffered(1))


def _cast_specs(to_cast, nt, steps):
    specs, shapes = [], []
    for w in to_cast:
        rows = next(r for r in range(BF16_SUBLANES, w.shape[0] + 1, BF16_SUBLANES)
                    if w.shape[0] % r == 0 and w.shape[0] // r <= steps)
        specs.append(pl.BlockSpec((rows, w.shape[1]),
                                  lambda g, t, last=w.shape[0] // rows - 1: (jnp.minimum(g * nt + t, last), 0)))
        shapes.append(jax.ShapeDtypeStruct(w.shape, bf16))
    return specs, shapes


def _ffn(x, mod, mod_group, w_up, wd, ln_g, ln_b, *, mod_base, ln_row, to_cast=()):
    groups, length, _ = x.shape
    tm = FFN_TILE
    nt = length // tm
    cast_specs, cast_shapes = _cast_specs(to_cast, nt, groups * nt)
    out = pl.pallas_call(
        functools.partial(_ffn_kernel, mod_base=mod_base, ln_row=ln_row),
        grid=(groups, nt),
        in_specs=[
            pl.BlockSpec((1, tm, D_MODEL), lambda g, t: (g, t, 0)),
            pl.BlockSpec((1, N_MOD, D_MODEL), lambda g, t: (mod_group(g), 0, 0)),
            _resident((D_MODEL, 2 * D_FF)),
            _resident((D_FF, D_MODEL)),
            _resident((3, D_MODEL)),
            _resident((3, D_MODEL)),
        ] + cast_specs,
        out_specs=[pl.BlockSpec((1, tm, D_MODEL), lambda g, t: (g, t, 0))] + cast_specs,
        out_shape=[jax.ShapeDtypeStruct(x.shape, f32)] + cast_shapes,
        scratch_shapes=[pltpu.VMEM((tm, D_FF), bf16)],
        compiler_params=pltpu.CompilerParams(
            dimension_semantics=("arbitrary", "arbitrary"), vmem_limit_bytes=VMEM_LIMIT_BYTES),
        name="ffn",
    )(x, mod, w_up, wd, ln_g, ln_b, *to_cast)
    return out if to_cast else out[0]


def _head_rms_norm(x, ones_bd, gain, parts):
    ss = _dot01(ones_bd, x * x, parts, mat_on_left=False)
    return x * lax.rsqrt(ss * (1.0 / HEAD_DIM) + RMS_EPS) * gain


def _rope(x, cos, sin_signed):
    width = x.shape[-1]
    lane = lax.broadcasted_iota(jnp.int32, x.shape, 1)
    from_right = pltpu.roll(x, width - ROPE_PAIR, 1)
    from_left = pltpu.roll(x, ROPE_PAIR, 1)
    partner = jnp.where((lane & (2 * ROPE_PAIR - 1)) < ROPE_PAIR, from_right, from_left)
    return x * cos + partner * sin_signed


def _proj_kernel(*refs, rope):
    if rope:
        (x_ref, m_ref, w_ref, lbf_ref, lbb_ref, qg_ref, kg_ref, bd_ref, cos_ref, sin_ref,
         qs_ref, vh_ref, ff_ref, fb_ref, sg_ref, qa_ref, ka_ref, vt_ref,
         spf_ref, spb_ref) = refs
    else:
        (x_ref, m_ref, w_ref, lbf_ref, lbb_ref, qg_ref, kg_ref, bd_ref,
         qs_ref, vh_ref, ff_ref, fb_ref, sg_ref, qa_ref, ka_ref, vt_ref,
         spf_ref, spb_ref, kt_ref) = refs
    shift = m_ref[0, 3:4, :]
    scale = m_ref[0, 4:5, :]
    subs = [slice(lo, lo + PROJ_SUB) for lo in range(0, x_ref.shape[1], PROJ_SUB)]
    hs = [(x_ref[0, rows] * (1.0 + scale) + shift).astype(bf16) for rows in subs]

    def cols(h, lo, width):
        return jnp.dot(h, w_ref[:, lo:lo + width], preferred_element_type=f32)

    def lower_bound(lb_ref):
        l0 = lb_ref[0:1, :]
        l1 = lb_ref[1:2, :]
        m = jnp.maximum(l0, l1)
        e0 = jnp.exp(l0 - m)
        e1 = jnp.exp(l1 - m)
        return e0 / (e0 + e1)

    half = HG_FAST_BLOCK // 2

    def forget_gate(lb_ref, lo, f_ref, span_ref):
        lb = lower_bound(lb_ref)
        for rows, h in zip(subs, hs):
            f = lb + (1.0 - lb) * jax.nn.sigmoid(cols(h, lo, HG_WIDTH))
            f_ref[0, rows] = f
            sums = jnp.sum(jnp.log2(f).reshape(PROJ_SUB // half, half, HG_WIDTH), axis=1)
            span_ref[0, rows.start // half:rows.stop // half] = jnp.broadcast_to(
                jnp.max(jnp.abs(sums), axis=-1, keepdims=True), (PROJ_SUB // half, LANES))

    for rows, h in zip(subs, hs):
        qs_ref[0, rows] = _silu(cols(h, 0, HG_WIDTH))
    for rows, h in zip(subs, hs):
        vh_ref[0, rows] = cols(h, HG_WIDTH, HG_WIDTH).astype(bf16)
    forget_gate(lbf_ref, 2 * HG_WIDTH, ff_ref, spf_ref)
    forget_gate(lbb_ref, 3 * HG_WIDTH, fb_ref, spb_ref)
    for rows, h in zip(subs, hs):
        sg_ref[0, rows] = _silu(cols(h, 4 * HG_WIDTH, HG_WIDTH))

    base = 5 * HG_WIDTH
    piece = vt_ref.shape[2]
    for rows, h in zip(subs, hs):
        q = _head_rms_norm(cols(h, base, ATT_WIDTH), bd_ref[...], qg_ref[...], 1)
        k = _head_rms_norm(cols(h, base + ATT_WIDTH, KV_WIDTH), bd_ref[0:KV_WIDTH, 0:KV_WIDTH], kg_ref[...], 2)
        if rope:
            cos, sin = cos_ref[rows, :], sin_ref[rows, :]
            q = _rope(q, jnp.concatenate([cos] * (ATT_WIDTH // LANES), axis=1),
                      jnp.concatenate([sin] * (ATT_WIDTH // LANES), axis=1))
            k = _rope(k, cos, sin)
        qa_ref[0, rows] = (q * (HEAD_DIM ** -0.5 * LOG2_E)).astype(bf16)
        ka_ref[0, rows] = k
        v = cols(h, base + ATT_WIDTH + KV_WIDTH, KV_WIDTH)
        for lo in range(rows.start, rows.stop, min(piece, PROJ_SUB)):
            n = min(piece, PROJ_SUB)
            dst = (lo // piece, slice(None), slice(lo % piece, lo % piece + n))
            vt_ref[dst] = v[lo - rows.start:lo - rows.start + n].T
            if not rope:
                kt_ref[dst] = k[lo - rows.start:lo - rows.start + n].T


def _proj(x, mod, mod_group, w_in, lb_f, lb_b, q_gain, k_gain, ones_bd, rope_tables, seq_len):
    groups, length, _ = x.shape
    tm = TOKEN_TILE
    rope = rope_tables is not None
    n_seq = groups * length // seq_len
    if seq_len >= tm:
        tiles_per_seq = seq_len // tm
        t_spec = pl.BlockSpec((1, KV_WIDTH, tm), lambda g, t: (g * (length // seq_len) + t // tiles_per_seq, 0,
                                                               t % tiles_per_seq))
    else:
        t_spec = pl.BlockSpec((tm // seq_len, KV_WIDTH, seq_len), lambda g, t: (g * (length // tm) + t, 0, 0))
    t_shape = jax.ShapeDtypeStruct((n_seq, KV_WIDTH, seq_len), f32)
    tok = lambda width: pl.BlockSpec((1, tm, width), lambda g, t: (g, t, 0))
    in_specs = [
        tok(D_MODEL),
        pl.BlockSpec((1, N_MOD, D_MODEL), lambda g, t: (mod_group(g), 0, 0)),
        _resident((D_MODEL, IN_WIDTH)),
        _resident((2, HG_WIDTH)),
        _resident((2, HG_WIDTH)),
        _resident((1, ATT_WIDTH)),
        _resident((1, KV_WIDTH)),
        _resident((ATT_WIDTH, ATT_WIDTH)),
    ]
    args = [x, mod, w_in, lb_f, lb_b, q_gain, k_gain, ones_bd]
    if rope:
        in_specs += [pl.BlockSpec((tm, LANES), lambda g, t: (t, 0))] * 2
        args += list(rope_tables)
    shape = lambda width, dt: jax.ShapeDtypeStruct((groups, length, width), dt)
    half = HG_FAST_BLOCK // 2
    span_spec = pl.BlockSpec((1, tm // half, LANES), lambda g, t: (g, t, 0))
    span_shape = jax.ShapeDtypeStruct((groups, length // half, LANES), f32)
    return pl.pallas_call(
        functools.partial(_proj_kernel, rope=rope),
        grid=(groups, length // tm),
        in_specs=in_specs,
        out_specs=([tok(HG_WIDTH)] * 5 + [tok(ATT_WIDTH), tok(KV_WIDTH), t_spec, span_spec, span_spec]
                   + ([] if rope else [t_spec])),
        out_shape=([shape(HG_WIDTH, f32), shape(HG_WIDTH, bf16)] + [shape(HG_WIDTH, f32)] * 3
                   + [shape(ATT_WIDTH, bf16), shape(KV_WIDTH, f32), t_shape, span_shape, span_shape]
                   + ([] if rope else [t_shape])),
        compiler_params=pltpu.CompilerParams(
            dimension_semantics=("arbitrary", "arbitrary"), vmem_limit_bytes=VMEM_LIMIT_BYTES),
        name="mixer_proj",
    )(*args)


def _hgrn_direction(qs, vb, f, st_ref, o_ref, rows, *, rev):
    tc = qs.shape[0]
    lf = jnp.log2(f)
    kk = 1.0 - f
    row = lax.broadcasted_iota(jnp.int32, (tc, tc), 0)
    col = lax.broadcasted_iota(jnp.int32, (tc, tc), 1)
    ordered = (row < col) if rev else (row > col)
    differ = row ^ col
    tri = jnp.where((col >= row) if rev else (col <= row), 1.0, 0.0).astype(bf16)
    cum = _dot01(tri, lf, 2, mat_on_left=True)
    total = cum[0:1] if rev else cum[tc - 1:tc]
    carry = jnp.exp2(total)
    query_half = 0 if rev else 1
    heads = [slice(hd * HG_HEAD_DIM, (hd + 1) * HG_HEAD_DIM) for hd in range(HG_HEADS)]

    def level_exponent(s, ridx):
        if s == 1:
            return jnp.where((ridx & 1) == query_half, lf, 0.0)
        if s == 2:
            nxt = pltpu.roll(lf, tc - 1, 0)
            prv = pltpu.roll(lf, 1, 0)
            m4 = ridx & 3
            if rev:
                return jnp.where(m4 == 0, lf + nxt, jnp.where(m4 == 1, lf, jnp.where(m4 == 2, 0.0, prv)))
            return jnp.where(m4 == 0, nxt, jnp.where(m4 == 1, 0.0, jnp.where(m4 == 2, lf, lf + prv)))
        blocks = []
        for lo in range(0, tc, 2 * s):
            anchor = lo + (s if rev else s - 1)
            blocks.append(cum[lo:lo + 2 * s] - cum[anchor:anchor + 1])
        d = jnp.concatenate(blocks, axis=0)
        is_query = ((ridx >> (s.bit_length() - 1)) & 1) == query_half
        return jnp.where(is_query, d, -d)

    def add_levels(intra, first):
        ridx = lax.broadcasted_iota(jnp.int32, qs.shape, 0)
        s = first
        while s < tc:
            shift = s.bit_length() - 1
            is_query = ((ridx >> shift) & 1) == query_half
            mixed = (jnp.where(is_query, qs, kk) * jnp.exp2(level_exponent(s, ridx))).astype(bf16)
            pair = jnp.logical_and((differ >> shift) == 1, ordered)
            for hd, hs in enumerate(heads):
                intra[hd] = jnp.where(pair, _dot_nt(mixed[:, hs], mixed[:, hs]), intra[hd])
            s *= 2
        return intra

    def finish(intra, q_dec, k_end, extra):
        for hd, hs in enumerate(heads):
            st = st_ref[hd]
            o = jnp.dot(jnp.concatenate([intra[hd].astype(bf16), q_dec[:, hs]], axis=1),
                        jnp.concatenate([vb[:, hs], st.T.astype(bf16)], axis=0), preferred_element_type=f32)
            o_ref[rows, hs] = o if extra is None else o + extra[:, hs]
            st_ref[hd] = st * carry[:, hs] + _dot_tn(vb[:, hs], k_end[:, hs])

    blk = HG_FAST_BLOCK
    anchors = [lo + (blk // 2 if rev else blk // 2 - 1) for lo in range(0, tc, blk)]

    def mid_split_blocks():
        q_mid, k_mid, q_dec, k_end = [], [], [], []
        for lo, a in zip(range(0, tc, blk), anchors):
            rel = cum[lo:lo + blk] - cum[a:a + 1]
            q_mid.append(qs[lo:lo + blk] * jnp.exp2(rel))
            k_mid.append(kk[lo:lo + blk] * jnp.exp2(-rel))
            q_dec.append(q_mid[-1] * jnp.exp2(cum[a:a + 1]))
            k_end.append(k_mid[-1] * jnp.exp2(total - cum[a:a + 1]))
        q_mid, k_mid, q_dec, k_end = (jnp.concatenate(p, axis=0).astype(bf16) for p in (q_mid, k_mid, q_dec, k_end))
        same_block = (differ >> (blk.bit_length() - 1)) == 0
        keep = jnp.logical_and(same_block, (row <= col) if rev else (row >= col))
        intra = [jnp.where(keep, _dot_nt(q_mid[:, hs], k_mid[:, hs]), 0.0) for hs in heads]
        finish(add_levels(intra, blk), q_dec, k_end, None)

    def all_levels():
        intra = add_levels([jnp.zeros((tc, tc), f32)] * HG_HEADS, 1)
        q_dec = (qs * jnp.exp2(cum)).astype(bf16)
        k_end = (kk * jnp.exp2(total - cum)).astype(bf16)
        qk = qs * kk
        own = jnp.concatenate(
            [jnp.sum(qk[:, hs], axis=-1, keepdims=True) * vb[:, hs].astype(f32) for hs in heads], axis=-1)
        finish(intra, q_dec, k_end, own)

    return mid_split_blocks, all_levels


def _hgrn_kernel(slow_ref, *refs, has_init):
    if has_init:
        (qsf_ref, vf_ref, ff_ref, qsb_ref, vb_ref, fb_ref, s0f_ref, s0b_ref,
         of_ref, ob_ref, stf_ref, stb_ref) = refs
    else:
        (qsf_ref, vf_ref, ff_ref, qsb_ref, vb_ref, fb_ref,
         of_ref, ob_ref, sf_out_ref, sb_out_ref, stf_ref, stb_ref) = refs
    t = pl.program_id(1)
    n_seq = qsf_ref.shape[0]

    @pl.when(t == 0)
    def _():
        for b in range(n_seq):
            for hd in range(HG_HEADS):
                if has_init:
                    stf_ref[b, hd] = s0f_ref[b, hd].T
                    stb_ref[b, hd] = s0b_ref[b, hd].T
                else:
                    stf_ref[b, hd] = jnp.zeros((HG_HEAD_DIM, HG_HEAD_DIM), f32)
                    stb_ref[b, hd] = jnp.zeros((HG_HEAD_DIM, HG_HEAD_DIM), f32)

    chunks = [slice(lo, lo + HG_TILE) for lo in range(0, qsf_ref.shape[1], HG_TILE)]
    scans = []
    for r_f, r_b in zip(chunks, reversed(chunks)):
        for b in range(n_seq):
            scans.append(_hgrn_direction(qsf_ref[b, r_f], vf_ref[b, r_f], ff_ref[b, r_f], stf_ref.at[b], of_ref.at[b],
                                         r_f, rev=False))
            scans.append(_hgrn_direction(qsb_ref[b, r_b], vb_ref[b, r_b], fb_ref[b, r_b], stb_ref.at[b], ob_ref.at[b],
                                         r_b, rev=True))
    slow = slow_ref[pl.program_id(0) * pl.num_programs(1) + t] != 0

    @pl.when(jnp.logical_not(slow))
    def _():
        for mid_split_blocks, _ in scans:
            mid_split_blocks()

    @pl.when(slow)
    def _():
        for _, all_levels in scans:
            all_levels()

    if not has_init:
        @pl.when(t == pl.num_programs(1) - 1)
        def _():
            for b in range(n_seq):
                for hd in range(HG_HEADS):
                    sf_out_ref[b, hd] = stf_ref[b, hd].T
                    sb_out_ref[b, hd] = stb_ref[b, hd].T


def _hgrn(qs, vh, f_fwd, f_bwd, span_fwd, span_bwd, init_states):
    n_seqs, length, _ = qs.shape
    step = min(HG_STEP, length)
    nt = length // step
    groups = n_seqs // HG_SEQS
    has_init = init_states is not None
    per_step = lambda span: jnp.max(span[:, :, 0].reshape(n_seqs, nt, -1), axis=-1) > HG_FAST_SPAN_LOG2
    slow = jnp.logical_or(per_step(span_fwd), per_step(span_bwd)[:, ::-1])
    slow = jnp.any(slow.reshape(groups, HG_SEQS, nt), axis=1).astype(jnp.int32).reshape(-1)
    fwd = pl.BlockSpec((HG_SEQS, step, HG_WIDTH), lambda g, t, _: (g, t, 0))
    bwd = pl.BlockSpec((HG_SEQS, step, HG_WIDTH), lambda g, t, _: (g, nt - 1 - t, 0))
    state = pl.BlockSpec((HG_SEQS, HG_HEADS, HG_HEAD_DIM, HG_HEAD_DIM), lambda g, t, _: (g, 0, 0, 0))
    in_specs = [fwd, fwd, fwd, bwd, bwd, bwd]
    args = [qs, vh, f_fwd, qs, vh, f_bwd]
    out_specs = [fwd, bwd]
    out_shape = [jax.ShapeDtypeStruct(qs.shape, f32)] * 2
    if has_init:
        in_specs += [state, state]
        args += list(init_states)
    else:
        out_specs += [state, state]
        out_shape += [jax.ShapeDtypeStruct((n_seqs, HG_HEADS, HG_HEAD_DIM, HG_HEAD_DIM), f32)] * 2
    return pl.pallas_call(
        functools.partial(_hgrn_kernel, has_init=has_init),
        grid_spec=pltpu.PrefetchScalarGridSpec(
            num_scalar_prefetch=1,
            grid=(groups, nt),
            in_specs=in_specs,
            out_specs=out_specs,
            scratch_shapes=[pltpu.VMEM((HG_SEQS, HG_HEADS, HG_HEAD_DIM, HG_HEAD_DIM), f32)] * 2),
        out_shape=out_shape,
        compiler_params=pltpu.CompilerParams(
            dimension_semantics=("arbitrary", "arbitrary"), vmem_limit_bytes=VMEM_LIMIT_BYTES),
        name="hgrn2",
    )(slow, *args)


def _attn_kernel(*refs, n_ctx, n_cast):
    kk_ref, vta_ref, bound_ref, fixed_ref = refs[-4:]
    o_ref, *cast_out = refs[len(refs) - 5 - n_cast:-4]
    inputs = refs[:len(refs) - 5 - n_cast]
    cast_in = inputs[len(inputs) - n_cast:]
    if n_ctx:
        (qa_ref, ka_ref, vt_ref, ckt_ref, cvt_ref, x_ref, of_ref, ob_ref, sg_ref, m_ref, wo_ref, hg_ref, qg_ref,
         g_ref, b_ref) = inputs[:len(inputs) - n_cast]
    else:
        (qa_ref, ka_ref, vt_ref, x_ref, of_ref, ob_ref, sg_ref, m_ref, wo_ref, hg_ref, qg_ref,
         g_ref, b_ref) = inputs[:len(inputs) - n_cast]
    for src_ref, dst_ref in zip(cast_in, cast_out):
        dst_ref[...] = src_ref[...].astype(bf16)
    tq = qa_ref.shape[1]
    n_keys = kk_ref.shape[0]

    @pl.when(pl.program_id(1) == 0)
    def _():
        def fill_keys(lo, k):
            n = k.shape[0]
            low = lax.broadcasted_iota(jnp.int32, k.shape, 1) < HEAD_DIM
            k_sw = pltpu.roll(k, HEAD_DIM, 1)
            kk_ref[lo:lo + n, 0:LANES] = jnp.where(low, k, k_sw).astype(bf16)
            kk_ref[lo:lo + n, LANES:2 * LANES] = jnp.where(low, k_sw, k).astype(bf16)

        def fill_values(lo, vt):
            n = vt.shape[1]
            ones_row = jnp.where(lax.broadcasted_iota(jnp.int32, (V_ROWS - HEAD_DIM, n), 0) == 0, 1.0, 0.0)
            for kv in range(N_KV_HEADS):
                vta_ref[kv * V_ROWS:kv * V_ROWS + HEAD_DIM, lo:lo + n] = (
                    vt[kv * HEAD_DIM:(kv + 1) * HEAD_DIM].astype(bf16))
                vta_ref[kv * V_ROWS + HEAD_DIM:(kv + 1) * V_ROWS, lo:lo + n] = ones_row.astype(bf16)

        if n_ctx:
            fill_keys(0, ckt_ref[0].T)
            fill_values(0, cvt_ref[0])
        fill_keys(n_ctx, ka_ref[0])
        fill_values(n_ctx, vt_ref[0])

        gain_max = jnp.max(jnp.abs(qg_ref[...]), axis=1, keepdims=True)
        all_small = None
        for kv in range(N_KV_HEADS):
            kt = kk_ref[:, kv * LANES:(kv + 1) * LANES].astype(f32)
            k_norm2 = jnp.max(0.5 * jnp.sum(kt * kt, axis=1, keepdims=True), axis=0, keepdims=True)
            bound = (LOG2_E * SCORE_BOUND_SLACK) * gain_max * jnp.sqrt(k_norm2)
            bound_ref[kv] = bound[0, 0]
            small = jnp.where(bound <= SCORE_BOUND_LIMIT, 1, 0)
            all_small = small if all_small is None else all_small * small
        fixed_ref[0] = all_small[0, 0]

    ts = min(Q_SUB, tq)
    low = lax.broadcasted_iota(jnp.int32, (ts, LANES), 1) < HEAD_DIM
    kc = min(KEY_CHUNK, n_keys)
    pairs_per_kv = N_HEADS // N_KV_HEADS // 2

    def attend(rows, fixed_shift):
        def masked_pair(tile):
            qp = qa_ref[0, rows, tile * LANES:(tile + 1) * LANES]
            zero = jnp.zeros_like(qp)
            return jnp.concatenate([jnp.where(low, qp, zero), jnp.where(low, zero, qp)], axis=0)

        q_pairs = [masked_pair(tile) for tile in range(N_HEADS // 2)]
        work = [(tile, lo) for tile in range(N_HEADS // 2) for lo in range(0, n_keys, kc)]

        def scores(tile, lo):
            kv = tile // pairs_per_kv
            return _dot_nt(kk_ref[lo:lo + kc, kv * LANES:(kv + 1) * LANES], q_pairs[tile])

        heads_t = []
        st_next = scores(*work[0])
        m = acc = None
        for i, (tile, lo) in enumerate(work):
            st = st_next
            if i + 1 < len(work):
                st_next = scores(*work[i + 1])
            kv = tile // pairs_per_kv
            values_t = vta_ref[kv * V_ROWS:(kv + 1) * V_ROWS, lo:lo + kc]
            if fixed_shift:
                e = jnp.exp2(st - bound_ref[kv]).astype(bf16)
                pv = jnp.dot(values_t, e, preferred_element_type=f32)
                acc = pv if acc is None else acc + pv
            else:
                m_chunk = jnp.max(st, axis=0, keepdims=True)
                m_new = m_chunk if m is None else jnp.maximum(m, m_chunk)
                e = jnp.exp2(st - m_new).astype(bf16)
                pv = jnp.dot(values_t, e, preferred_element_type=f32)
                acc = pv if acc is None else acc * jnp.exp2(m - m_new) + pv
                m = m_new
            if lo + kc == n_keys:
                on = (acc[0:HEAD_DIM] * (1.0 / acc[HEAD_DIM:HEAD_DIM + 1])).astype(bf16)
                heads_t += [on[:, :ts], on[:, ts:]]
                m = acc = None
        return jnp.concatenate(heads_t, axis=0)

    def project(rows, o_att_t):
        o_sum = of_ref[0, rows] + ob_ref[0, rows]
        normed = []
        for hd in range(HG_HEADS):
            oh = o_sum[:, hd * HG_HEAD_DIM:(hd + 1) * HG_HEAD_DIM]
            ms = jnp.mean(oh * oh, axis=-1, keepdims=True)
            normed.append(oh * lax.rsqrt(ms + RMS_EPS))
        o_hg = (jnp.concatenate(normed, axis=-1) * hg_ref[...] * sg_ref[0, rows]).astype(bf16)
        y = (jnp.dot(o_hg, wo_ref[0:HG_WIDTH, :], preferred_element_type=f32)
             + _dot_tn(o_att_t, wo_ref[HG_WIDTH:HG_WIDTH + ATT_WIDTH, :]))
        r = ALPHA * x_ref[0, rows] + m_ref[0, 5:6, :] * y
        o_ref[0, rows] = _layer_norm(r, g_ref[1:2, :], b_ref[1:2, :])

    def run(fixed_shift):
        for lo in range(0, tq, ts):
            rows = slice(lo, lo + ts)
            project(rows, attend(rows, fixed_shift))

    use_bound = fixed_ref[0] != 0
    pl.when(use_bound)(functools.partial(run, True))
    pl.when(jnp.logical_not(use_bound))(functools.partial(run, False))


def _attn(qa, ka, vt, ctx_kv_t, x, o_f, o_b, sg, mod, mod_group, w_out, hg_gain, q_gain, ln_g, ln_b, to_cast=()):
    groups, length, _ = qa.shape
    tq = min(Q_TILE, length)
    nt = length // tq
    n_ctx = 0 if ctx_kv_t is None else ctx_kv_t[0].shape[2]
    n_keys = n_ctx + length
    tok = lambda width: pl.BlockSpec((1, tq, width), lambda g, t: (g, t, 0))
    whole_t = lambda n: pl.BlockSpec((1, KV_WIDTH, n), lambda g, t: (g, 0, 0))
    in_specs = [tok(ATT_WIDTH), pl.BlockSpec((1, length, KV_WIDTH), lambda g, t: (g, 0, 0)), whole_t(length)]
    args = [qa, ka, vt]
    if n_ctx:
        in_specs += [whole_t(n_ctx), whole_t(n_ctx)]
        args += list(ctx_kv_t)
    in_specs += [
        tok(D_MODEL), tok(HG_WIDTH), tok(HG_WIDTH), tok(HG_WIDTH),
        pl.BlockSpec((1, N_MOD, D_MODEL), lambda g, t: (mod_group(g), 0, 0)),
        _resident((D_MODEL, D_MODEL)),
        _resident((1, HG_WIDTH)),
        _resident((1, ATT_WIDTH)),
        _resident((3, D_MODEL)),
        _resident((3, D_MODEL)),
    ]
    args += [x, o_f, o_b, sg, mod, w_out, hg_gain, q_gain, ln_g, ln_b]
    cast_specs, cast_shapes = _cast_specs(to_cast, nt, groups * nt)
    in_specs += cast_specs
    args += list(to_cast)
    out_specs = [tok(D_MODEL)] + cast_specs
    out_shape = [jax.ShapeDtypeStruct(x.shape, f32)] + cast_shapes
    return pl.pallas_call(
        functools.partial(_attn_kernel, n_ctx=n_ctx, n_cast=len(to_cast)),
        grid=(groups, nt),
        in_specs=in_specs,
        out_specs=out_specs,
        out_shape=out_shape,
        scratch_shapes=[pltpu.VMEM((n_keys, N_KV_HEADS * LANES), bf16),
                        pltpu.VMEM((N_KV_HEADS * V_ROWS, n_keys), bf16),
                        pltpu.SMEM((N_KV_HEADS,), f32),
                        pltpu.SMEM((1,), jnp.int32)],
        compiler_params=pltpu.CompilerParams(
            dimension_semantics=("arbitrary", "arbitrary"), vmem_limit_bytes=VMEM_LIMIT_BYTES),
        name="attn_out",
    )(*args)


def _rope_tables(n_tokens):
    half = HEAD_DIM // 2
    t = jnp.arange(n_tokens)
    inv = ROPE_THETA ** (-jnp.arange(0, half, 2, dtype=f32) / half)
    ang_row = (t // GRID_W).astype(f32)[:, None] * inv
    ang_col = (t % GRID_W).astype(f32)[:, None] * inv
    cos = jnp.concatenate([jnp.cos(ang_row)] * 2 + [jnp.cos(ang_col)] * 2, axis=-1)
    sin = jnp.concatenate([-jnp.sin(ang_row), jnp.sin(ang_row), -jnp.sin(ang_col), jnp.sin(ang_col)], axis=-1)
    return jnp.tile(cos, (1, LANES // HEAD_DIM)), jnp.tile(sin, (1, LANES // HEAD_DIM))


def kernel(x_prompt, x_sample, cache_k, cache_v, state_hgrn_fwd, state_hgrn_bwd, c, c_ctx, w_mod, b_mod,
           w_ffn1_in, w_ffn1_out, w_ffn2_in, w_ffn2_out, w_in, w_out, q_norm_g, k_norm_g, hg_norm_g,
           lb_logits_fwd, lb_logits_bwd, ln_g, ln_b):
    assert w_mod.shape[0] == DEPTH and lb_logits_fwd.shape[0] == DEPTH + 1
    batch, seq, _ = x_prompt.shape
    dec_batch, dec_seq, _ = x_sample.shape
    past = cache_k.shape[2]

    ctx_row = dec_batch
    rows = -(-(dec_batch + 1) // BF16_SUBLANES) * BF16_SUBLANES
    cvecs = jnp.concatenate([c, c_ctx[None, :], jnp.zeros((rows - dec_batch - 1, D_MODEL), f32)], axis=0)
    mod = _modulation(cvecs, w_mod[0], b_mod[0]).reshape(rows, N_MOD, D_MODEL)

    to_bf16 = lambda w: w.astype(bf16)
    w1u, w1d = to_bf16(w_ffn1_in[0]), to_bf16(w_ffn1_out[0])
    gains = ln_g[0], ln_b[0]
    q_gain = jnp.tile(q_norm_g[0], N_HEADS).reshape(1, ATT_WIDTH)
    k_gain = jnp.tile(k_norm_g[0], N_KV_HEADS).reshape(1, KV_WIDTH)
    hg_gain = hg_norm_g[0].reshape(1, HG_WIDTH)
    head_of = jnp.arange(ATT_WIDTH) // HEAD_DIM
    ones_bd = (head_of[:, None] == head_of[None, :]).astype(bf16)

    def mixer_sublayer(x, mod_group, rope_tables, ctx_kv_t, init_states, hg_groups, to_cast=()):
        shape = x.shape
        seq_len = shape[0] * shape[1] // hg_groups
        per_seq = lambda a: a.reshape(hg_groups, seq_len, a.shape[-1])
        proj = _proj(x, mod, mod_group, w_in_b, lb_logits_fwd, lb_logits_bwd, q_gain, k_gain, ones_bd, rope_tables,
                     seq_len)
        qs, vh, f_f, f_b, sg, qa, ka = map(per_seq, proj[:7])
        vt, span_f, span_b = proj[7], proj[8].reshape(hg_groups, -1, LANES), proj[9].reshape(hg_groups, -1, LANES)
        scans = _hgrn(qs, vh, f_f, f_b, span_f, span_b, init_states)
        x, *cast = _attn(qa, ka, vt, ctx_kv_t, per_seq(x), scans[0], scans[1], sg, mod, mod_group, w_out_b, hg_gain,
                         q_gain, *gains, to_cast=to_cast)
        return x.reshape(shape), proj[10:], vt, scans[2:], cast

    from_cache = lambda t: t[:, 0].transpose(0, 2, 3, 1).reshape(dec_batch, KV_WIDTH, past)
    init_states = (state_hgrn_fwd[:, 0], state_hgrn_bwd[:, 0])
    latent_group = lambda g: g
    x_latent, w_in_b, w_out_b = _ffn(x_sample, mod, latent_group, w1u, w1d, *gains, mod_base=0, ln_row=0,
                                     to_cast=(w_in[0], w_out[0]))
    x_latent, _, _, _, (w2u, w2d) = mixer_sublayer(
        x_latent, latent_group, _rope_tables(dec_seq), (from_cache(cache_k), from_cache(cache_v)), init_states,
        dec_batch, to_cast=(w_ffn2_in[0], w_ffn2_out[0]))

    ctx_group = lambda g: ctx_row
    x_ctx = _ffn(x_prompt.reshape(1, batch * seq, D_MODEL), mod, ctx_group, w1u, w1d, *gains, mod_base=0, ln_row=0)
    x_ctx, (kt_new,), vt_new, states, _ = mixer_sublayer(x_ctx, ctx_group, None, None, None, batch)
    y_sample = _ffn(x_latent, mod, latent_group, w2u, w2d, *gains, mod_base=6, ln_row=2)
    y_prompt = _ffn(x_ctx, mod, ctx_group, w2u, w2d, *gains, mod_base=6, ln_row=2)
    y_prompt = y_prompt.reshape(batch, seq, D_MODEL)
    to_cache = lambda t: t.reshape(batch, DEPTH, N_KV_HEADS, HEAD_DIM, seq).transpose(0, 1, 4, 2, 3)
    new_cache_k, new_cache_v = to_cache(kt_new), to_cache(vt_new)
    new_state_fwd = states[0].reshape(batch, DEPTH, HG_HEADS, HG_HEAD_DIM, HG_HEAD_DIM)
    new_state_bwd = states[1].reshape(batch, DEPTH, HG_HEADS, HG_HEAD_DIM, HG_HEAD_DIM)

    return (y_prompt, y_sample, new_cache_k, new_cache_v, new_state_fwd, new_state_bwd)
```

```python
import functools

import jax
import jax.numpy as jnp
from jax import lax
from jax.experimental import pallas as pl
from jax.experimental.pallas import tpu as pltpu

f32 = jnp.float32
bf16 = jnp.bfloat16

D_MODEL = 1024
N_MOD = 9
HG_WIDTH = 512
HG_HEAD_DIM = 128
HG_HEADS = 4
ATT_WIDTH = 512
HEAD_DIM = 64
N_HEADS = 8
N_KV_HEADS = 2
KV_WIDTH = 128
IN_WIDTH = 5 * HG_WIDTH + ATT_WIDTH + 2 * KV_WIDTH
D_FF = 2816
GRID_W = 64
ROPE_PAIR = HEAD_DIM // 4
ROPE_THETA = 10000.0
DEPTH = 1
ALPHA = (2.0 * DEPTH) ** 0.25
LOG2_E = 1.4426950408889634
LN_EPS = 1e-6
RMS_EPS = 1e-6

LANES = 128
BF16_SUBLANES = 16
VMEM_LIMIT_BYTES = 56 * 1024 * 1024

FF_CHUNK = 256
FFN_TILE = 1024
FFN_SUB = 512
TOKEN_TILE = 512
PROJ_SUB = 256
HG_TILE = 128
HG_STEP = 512
HG_SEQS = 2
Q_TILE = 512
Q_SUB = 256
KEY_CHUNK = 512
SCORE_BOUND_LIMIT = 40.0
SCORE_BOUND_SLACK = 1.01
V_ROWS = 80
HG_FAST_BLOCK = 64
HG_FAST_SPAN_LOG2 = 100.0


def _silu(x):
    return x * jax.nn.sigmoid(x)


def _layer_norm(r, g, b):
    mu = jnp.mean(r, axis=-1, keepdims=True)
    c = r - mu
    var = jnp.mean(c * c, axis=-1, keepdims=True)
    return c * lax.rsqrt(var + LN_EPS) * g + b


def _split_bf16(x, parts):
    out = []
    r = x
    for _ in range(parts - 1):
        p = r.astype(bf16)
        out.append(p)
        r = r - p.astype(f32)
    out.append(r.astype(bf16))
    return out


def _dot01(mat01, x, parts, *, mat_on_left):
    pieces = _split_bf16(x, parts)
    if mat_on_left:
        return jnp.dot(jnp.concatenate([mat01] * parts, axis=1), jnp.concatenate(pieces, axis=0),
                       preferred_element_type=f32)
    return jnp.dot(jnp.concatenate(pieces, axis=1), jnp.concatenate([mat01] * parts, axis=0),
                   preferred_element_type=f32)


def _dot_nt(a, b):
    return lax.dot_general(a, b, (((1,), (1,)), ((), ())), preferred_element_type=f32)


def _dot_tn(a, b):
    return lax.dot_general(a, b, (((0,), (0,)), ((), ())), preferred_element_type=f32)


def _mod_kernel(c_ref, w_ref, b_ref, o_ref):
    a = _silu(c_ref[...]).astype(bf16)
    o_ref[...] = jnp.dot(a, w_ref[...].astype(bf16), preferred_element_type=f32) + b_ref[...]


def _modulation(cvecs, w_mod, b_mod):
    rows = cvecs.shape[0]
    n_out = w_mod.shape[1]
    tn = D_MODEL
    return pl.pallas_call(
        _mod_kernel,
        grid=(n_out // tn,),
        in_specs=[
            pl.BlockSpec((rows, D_MODEL), lambda j: (0, 0)),
            pl.BlockSpec((D_MODEL, tn), lambda j: (0, j)),
            pl.BlockSpec((1, tn), lambda j: (0, j)),
        ],
        out_specs=pl.BlockSpec((rows, tn), lambda j: (0, j)),
        out_shape=jax.ShapeDtypeStruct((rows, n_out), f32),
        compiler_params=pltpu.CompilerParams(dimension_semantics=("arbitrary",)),
        name="modulation",
    )(cvecs, w_mod, b_mod.reshape(1, n_out))


def _ffn_kernel(x_ref, m_ref, wup_ref, wd_ref, g_ref, b_ref, *refs, mod_base, ln_row):
    n_cast = (len(refs) - 2) // 2
    cast_in, o_ref, cast_out, act_ref = refs[:n_cast], refs[n_cast], refs[n_cast + 1:-1], refs[-1]
    for src_ref, dst_ref in zip(cast_in, cast_out):
        dst_ref[...] = src_ref[...].astype(bf16)
    shift = m_ref[0, mod_base:mod_base + 1, :]
    scale = m_ref[0, mod_base + 1:mod_base + 2, :]
    gate = m_ref[0, mod_base + 2:mod_base + 3, :]
    subs = [slice(lo, lo + FFN_SUB) for lo in range(0, x_ref.shape[1], FFN_SUB)]
    xs = [x_ref[0, rows] for rows in subs]
    hs = [(x * (1.0 + scale) + shift).astype(bf16) for x in xs]
    for j in range(D_FF // FF_CHUNK):
        cols = slice(j * FF_CHUNK, (j + 1) * FF_CHUNK)
        for rows, h in zip(subs, hs):
            a = jnp.dot(h, wup_ref[:, cols], preferred_element_type=f32)
            u = jnp.dot(h, wup_ref[:, D_FF + j * FF_CHUNK:D_FF + (j + 1) * FF_CHUNK],
                        preferred_element_type=f32)
            act_ref[rows, cols] = (_silu(a) * u).astype(bf16)
    ys = [jnp.dot(act_ref[rows, :], wd_ref[...], preferred_element_type=f32) for rows in subs]
    for rows, x, y in zip(subs, xs, ys):
        r = ALPHA * x + 0.5 * gate * y
        o_ref[0, rows] = _layer_norm(r, g_ref[ln_row:ln_row + 1, :], b_ref[ln_row:ln_row + 1, :])


def _resident(shape):
    return pl.BlockSpec(shape, lambda *_: (0,) * len(shape), pipeline_mode=pl.Buffered(1))


def _cast_specs(to_cast, nt, steps):
    specs, shapes = [], []
    for w in to_cast:
        rows = next(r for r in range(BF16_SUBLANES, w.shape[0] + 1, BF16_SUBLANES)
                    if w.shape[0] % r == 0 and w.shape[0] // r <= steps)
        specs.append(pl.BlockSpec((rows, w.shape[1]),
                                  lambda g, t, last=w.shape[0] // rows - 1: (jnp.minimum(g * nt + t, last), 0)))
        shapes.append(jax.ShapeDtypeStruct(w.shape, bf16))
    return specs, shapes


def _ffn(x, mod, mod_group, w_up, wd, ln_g, ln_b, *, mod_base, ln_row, to_cast=()):
    groups, length, _ = x.shape
    tm = FFN_TILE
    nt = length // tm
    cast_specs, cast_shapes = _cast_specs(to_cast, nt, groups * nt)
    out = pl.pallas_call(
        functools.partial(_ffn_kernel, mod_base=mod_base, ln_row=ln_row),
        grid=(groups, nt),
        in_specs=[
            pl.BlockSpec((1, tm, D_MODEL), lambda g, t: (g, t, 0)),
            pl.BlockSpec((1, N_MOD, D_MODEL), lambda g, t: (mod_group(g), 0, 0)),
            _resident((D_MODEL, 2 * D_FF)),
            _resident((D_FF, D_MODEL)),
            _resident((3, D_MODEL)),
            _resident((3, D_MODEL)),
        ] + cast_specs,
        out_specs=[pl.BlockSpec((1, tm, D_MODEL), lambda g, t: (g, t, 0))] + cast_specs,
        out_shape=[jax.ShapeDtypeStruct(x.shape, f32)] + cast_shapes,
        scratch_shapes=[pltpu.VMEM((tm, D_FF), bf16)],
        compiler_params=pltpu.CompilerParams(
            dimension_semantics=("arbitrary", "arbitrary"), vmem_limit_bytes=VMEM_LIMIT_BYTES),
        name="ffn",
    )(x, mod, w_up, wd, ln_g, ln_b, *to_cast)
    return out if to_cast else out[0]


def _head_rms_norm(x, ones_bd, gain, parts):
    ss = _dot01(ones_bd, x * x, parts, mat_on_left=False)
    return x * lax.rsqrt(ss * (1.0 / HEAD_DIM) + RMS_EPS) * gain


def _rope(x, cos, sin_signed):
    width = x.shape[-1]
    lane = lax.broadcasted_iota(jnp.int32, x.shape, 1)
    from_right = pltpu.roll(x, width - ROPE_PAIR, 1)
    from_left = pltpu.roll(x, ROPE_PAIR, 1)
    partner = jnp.where((lane & (2 * ROPE_PAIR - 1)) < ROPE_PAIR, from_right, from_left)
    return x * cos + partner * sin_signed


def _proj_kernel(*refs, rope):
    if rope:
        (x_ref, m_ref, w_ref, lbf_ref, lbb_ref, qg_ref, kg_ref, bd_ref, cos_ref, sin_ref,
         qs_ref, vh_ref, ff_ref, fb_ref, sg_ref, qa_ref, ka_ref, vt_ref,
         spf_ref, spb_ref) = refs
    else:
        (x_ref, m_ref, w_ref, lbf_ref, lbb_ref, qg_ref, kg_ref, bd_ref,
         qs_ref, vh_ref, ff_ref, fb_ref, sg_ref, qa_ref, ka_ref, vt_ref,
         spf_ref, spb_ref, kt_ref) = refs
    shift = m_ref[0, 3:4, :]
    scale = m_ref[0, 4:5, :]
    subs = [slice(lo, lo + PROJ_SUB) for lo in range(0, x_ref.shape[1], PROJ_SUB)]
    hs = [(x_ref[0, rows] * (1.0 + scale) + shift).astype(bf16) for rows in subs]

    def cols(h, lo, width):
        return jnp.dot(h, w_ref[:, lo:lo + width], preferred_element_type=f32)

    def lower_bound(lb_ref):
        l0 = lb_ref[0:1, :]
        l1 = lb_ref[1:2, :]
        m = jnp.maximum(l0, l1)
        e0 = jnp.exp(l0 - m)
        e1 = jnp.exp(l1 - m)
        return e0 / (e0 + e1)

    half = HG_FAST_BLOCK // 2

    def forget_gate(lb_ref, lo, f_ref, span_ref):
        lb = lower_bound(lb_ref)
        for rows, h in zip(subs, hs):
            f = lb + (1.0 - lb) * jax.nn.sigmoid(cols(h, lo, HG_WIDTH))
            f_ref[0, rows] = f
            sums = jnp.sum(jnp.log2(f).reshape(PROJ_SUB // half, half, HG_WIDTH), axis=1)
            span_ref[0, rows.start // half:rows.stop // half] = jnp.broadcast_to(
                jnp.max(jnp.abs(sums), axis=-1, keepdims=True), (PROJ_SUB // half, LANES))

    base = 5 * HG_WIDTH
    piece = vt_ref.shape[2]
    for rows, h in zip(subs, hs):
        q = _head_rms_norm(cols(h, base, ATT_WIDTH), bd_ref[...], qg_ref[...], 1)
        k = _head_rms_norm(cols(h, base + ATT_WIDTH, KV_WIDTH), bd_ref[0:KV_WIDTH, 0:KV_WIDTH], kg_ref[...], 2)
        if rope:
            cos, sin = cos_ref[rows, :], sin_ref[rows, :]
            q = _rope(q, jnp.concatenate([cos] * (ATT_WIDTH // LANES), axis=1),
                      jnp.concatenate([sin] * (ATT_WIDTH // LANES), axis=1))
            k = _rope(k, cos, sin)
        qa_ref[0, rows] = (q * (HEAD_DIM ** -0.5 * LOG2_E)).astype(bf16)
        ka_ref[0, rows] = k
        v = cols(h, base + ATT_WIDTH + KV_WIDTH, KV_WIDTH)
        for lo in range(rows.start, rows.stop, min(piece, PROJ_SUB)):
            n = min(piece, PROJ_SUB)
            dst = (lo // piece, slice(None), slice(lo % piece, lo % piece + n))
            vt_ref[dst] = v[lo - rows.start:lo - rows.start + n].T
            if not rope:
                kt_ref[dst] = k[lo - rows.start:lo - rows.start + n].T

    for rows, h in zip(subs, hs):
        qs_ref[0, rows] = _silu(cols(h, 0, HG_WIDTH))
    for rows, h in zip(subs, hs):
        vh_ref[0, rows] = cols(h, HG_WIDTH, HG_WIDTH).astype(bf16)
    forget_gate(lbf_ref, 2 * HG_WIDTH, ff_ref, spf_ref)
    forget_gate(lbb_ref, 3 * HG_WIDTH, fb_ref, spb_ref)
    for rows, h in zip(subs, hs):
        sg_ref[0, rows] = _silu(cols(h, 4 * HG_WIDTH, HG_WIDTH))


def _proj(x, mod, mod_group, w_in, lb_f, lb_b, q_gain, k_gain, ones_bd, rope_tables, seq_len):
    groups, length, _ = x.shape
    tm = TOKEN_TILE
    rope = rope_tables is not None
    n_seq = groups * length // seq_len
    if seq_len >= tm:
        tiles_per_seq = seq_len // tm
        t_spec = pl.BlockSpec((1, KV_WIDTH, tm), lambda g, t: (g * (length // seq_len) + t // tiles_per_seq, 0,
                                                               t % tiles_per_seq))
    else:
        t_spec = pl.BlockSpec((tm // seq_len, KV_WIDTH, seq_len), lambda g, t: (g * (length // tm) + t, 0, 0))
    t_shape = jax.ShapeDtypeStruct((n_seq, KV_WIDTH, seq_len), f32)
    tok = lambda width: pl.BlockSpec((1, tm, width), lambda g, t: (g, t, 0))
    in_specs = [
        tok(D_MODEL),
        pl.BlockSpec((1, N_MOD, D_MODEL), lambda g, t: (mod_group(g), 0, 0)),
        _resident((D_MODEL, IN_WIDTH)),
        _resident((2, HG_WIDTH)),
        _resident((2, HG_WIDTH)),
        _resident((1, ATT_WIDTH)),
        _resident((1, KV_WIDTH)),
        _resident((ATT_WIDTH, ATT_WIDTH)),
    ]
    args = [x, mod, w_in, lb_f, lb_b, q_gain, k_gain, ones_bd]
    if rope:
        in_specs += [pl.BlockSpec((tm, LANES), lambda g, t: (t, 0))] * 2
        args += list(rope_tables)
    shape = lambda width, dt: jax.ShapeDtypeStruct((groups, length, width), dt)
    half = HG_FAST_BLOCK // 2
    span_spec = pl.BlockSpec((1, tm // half, LANES), lambda g, t: (g, t, 0))
    span_shape = jax.ShapeDtypeStruct((groups, length // half, LANES), f32)
    return pl.pallas_call(
        functools.partial(_proj_kernel, rope=rope),
        grid=(groups, length // tm),
        in_specs=in_specs,
        out_specs=([tok(HG_WIDTH)] * 5 + [tok(ATT_WIDTH), tok(KV_WIDTH), t_spec, span_spec, span_spec]
                   + ([] if rope else [t_spec])),
        out_shape=([shape(HG_WIDTH, f32), shape(HG_WIDTH, bf16)] + [shape(HG_WIDTH, f32)] * 3
                   + [shape(ATT_WIDTH, bf16), shape(KV_WIDTH, f32), t_shape, span_shape, span_shape]
                   + ([] if rope else [t_shape])),
        compiler_params=pltpu.CompilerParams(
            dimension_semantics=("arbitrary", "arbitrary"), vmem_limit_bytes=VMEM_LIMIT_BYTES),
        name="mixer_proj",
    )(*args)


def _hgrn_direction(qs, vb, f, st_ref, o_ref, rows, *, rev):
    tc = qs.shape[0]
    lf = jnp.log2(f)
    kk = 1.0 - f
    row = lax.broadcasted_iota(jnp.int32, (tc, tc), 0)
    col = lax.broadcasted_iota(jnp.int32, (tc, tc), 1)
    ordered = (row < col) if rev else (row > col)
    differ = row ^ col
    tri = jnp.where((col >= row) if rev else (col <= row), 1.0, 0.0).astype(bf16)
    cum = _dot01(tri, lf, 2, mat_on_left=True)
    total = cum[0:1] if rev else cum[tc - 1:tc]
    carry = jnp.exp2(total)
    query_half = 0 if rev else 1
    heads = [slice(hd * HG_HEAD_DIM, (hd + 1) * HG_HEAD_DIM) for hd in range(HG_HEADS)]

    def level_exponent(s, ridx):
        if s == 1:
            return jnp.where((ridx & 1) == query_half, lf, 0.0)
        if s == 2:
            nxt = pltpu.roll(lf, tc - 1, 0)
            prv = pltpu.roll(lf, 1, 0)
            m4 = ridx & 3
            if rev:
                return jnp.where(m4 == 0, lf + nxt, jnp.where(m4 == 1, lf, jnp.where(m4 == 2, 0.0, prv)))
            return jnp.where(m4 == 0, nxt, jnp.where(m4 == 1, 0.0, jnp.where(m4 == 2, lf, lf + prv)))
        blocks = []
        for lo in range(0, tc, 2 * s):
            anchor = lo + (s if rev else s - 1)
            blocks.append(cum[lo:lo + 2 * s] - cum[anchor:anchor + 1])
        d = jnp.concatenate(blocks, axis=0)
        is_query = ((ridx >> (s.bit_length() - 1)) & 1) == query_half
        return jnp.where(is_query, d, -d)

    def add_levels(intra, first):
        ridx = lax.broadcasted_iota(jnp.int32, qs.shape, 0)
        s = first
        while s < tc:
            shift = s.bit_length() - 1
            is_query = ((ridx >> shift) & 1) == query_half
            mixed = (jnp.where(is_query, qs, kk) * jnp.exp2(level_exponent(s, ridx))).astype(bf16)
            pair = jnp.logical_and((differ >> shift) == 1, ordered)
            for hd, hs in enumerate(heads):
                intra[hd] = jnp.where(pair, _dot_nt(mixed[:, hs], mixed[:, hs]), intra[hd])
            s *= 2
        return intra

    def finish(intra, q_dec, k_end, extra):
        for hd, hs in enumerate(heads):
            st = st_ref[hd]
            o = jnp.dot(jnp.concatenate([intra[hd].astype(bf16), q_dec[:, hs]], axis=1),
                        jnp.concatenate([vb[:, hs], st.T.astype(bf16)], axis=0), preferred_element_type=f32)
            o_ref[rows, hs] = o if extra is None else o + extra[:, hs]
            st_ref[hd] = st * carry[:, hs] + _dot_tn(vb[:, hs], k_end[:, hs])

    blk = HG_FAST_BLOCK
    anchors = [lo + (blk // 2 if rev else blk // 2 - 1) for lo in range(0, tc, blk)]

    def mid_split_blocks():
        q_mid, k_mid, q_dec, k_end = [], [], [], []
        for lo, a in zip(range(0, tc, blk), anchors):
            rel = cum[lo:lo + blk] - cum[a:a + 1]
            q_mid.append(qs[lo:lo + blk] * jnp.exp2(rel))
            k_mid.append(kk[lo:lo + blk] * jnp.exp2(-rel))
            q_dec.append(q_mid[-1] * jnp.exp2(cum[a:a + 1]))
            k_end.append(k_mid[-1] * jnp.exp2(total - cum[a:a + 1]))
        q_mid, k_mid, q_dec, k_end = (jnp.concatenate(p, axis=0).astype(bf16) for p in (q_mid, k_mid, q_dec, k_end))
        same_block = (differ >> (blk.bit_length() - 1)) == 0
        keep = jnp.logical_and(same_block, (row <= col) if rev else (row >= col))
        intra = [jnp.where(keep, _dot_nt(q_mid[:, hs], k_mid[:, hs]), 0.0) for hs in heads]
        finish(add_levels(intra, blk), q_dec, k_end, None)

    def all_levels():
        intra = add_levels([jnp.zeros((tc, tc), f32)] * HG_HEADS, 1)
        q_dec = (qs * jnp.exp2(cum)).astype(bf16)
        k_end = (kk * jnp.exp2(total - cum)).astype(bf16)
        qk = qs * kk
        own = jnp.concatenate(
            [jnp.sum(qk[:, hs], axis=-1, keepdims=True) * vb[:, hs].astype(f32) for hs in heads], axis=-1)
        finish(intra, q_dec, k_end, own)

    return mid_split_blocks, all_levels


def _hgrn_kernel(slow_ref, *refs, has_init):
    if has_init:
        (qsf_ref, vf_ref, ff_ref, qsb_ref, vb_ref, fb_ref, s0f_ref, s0b_ref,
         of_ref, ob_ref, stf_ref, stb_ref) = refs
    else:
        (qsf_ref, vf_ref, ff_ref, qsb_ref, vb_ref, fb_ref,
         of_ref, ob_ref, sf_out_ref, sb_out_ref, stf_ref, stb_ref) = refs
    t = pl.program_id(1)
    n_seq = qsf_ref.shape[0]

    @pl.when(t == 0)
    def _():
        for b in range(n_seq):
            for hd in range(HG_HEADS):
                if has_init:
                    stf_ref[b, hd] = s0f_ref[b, hd].T
                    stb_ref[b, hd] = s0b_ref[b, hd].T
                else:
                    stf_ref[b, hd] = jnp.zeros((HG_HEAD_DIM, HG_HEAD_DIM), f32)
                    stb_ref[b, hd] = jnp.zeros((HG_HEAD_DIM, HG_HEAD_DIM), f32)

    chunks = [slice(lo, lo + HG_TILE) for lo in range(0, qsf_ref.shape[1], HG_TILE)]
    scans = []
    for r_f, r_b in zip(chunks, reversed(chunks)):
        for b in range(n_seq):
            scans.append(_hgrn_direction(qsf_ref[b, r_f], vf_ref[b, r_f], ff_ref[b, r_f], stf_ref.at[b], of_ref.at[b],
                                         r_f, rev=False))
            scans.append(_hgrn_direction(qsb_ref[b, r_b], vb_ref[b, r_b], fb_ref[b, r_b], stb_ref.at[b], ob_ref.at[b],
                                         r_b, rev=True))
    slow = slow_ref[pl.program_id(0) * pl.num_programs(1) + t] != 0

    @pl.when(jnp.logical_not(slow))
    def _():
        for mid_split_blocks, _ in scans:
            mid_split_blocks()

    @pl.when(slow)
    def _():
        for _, all_levels in scans:
            all_levels()

    if not has_init:
        @pl.when(t == pl.num_programs(1) - 1)
        def _():
            for b in range(n_seq):
                for hd in range(HG_HEADS):
                    sf_out_ref[b, hd] = stf_ref[b, hd].T
                    sb_out_ref[b, hd] = stb_ref[b, hd].T


def _hgrn(qs, vh, f_fwd, f_bwd, span_fwd, span_bwd, init_states):
    n_seqs, length, _ = qs.shape
    step = min(HG_STEP, length)
    nt = length // step
    groups = n_seqs // HG_SEQS
    has_init = init_states is not None
    per_step = lambda span: jnp.max(span[:, :, 0].reshape(n_seqs, nt, -1), axis=-1) > HG_FAST_SPAN_LOG2
    slow = jnp.logical_or(per_step(span_fwd), per_step(span_bwd)[:, ::-1])
    slow = jnp.any(slow.reshape(groups, HG_SEQS, nt), axis=1).astype(jnp.int32).reshape(-1)
    fwd = pl.BlockSpec((HG_SEQS, step, HG_WIDTH), lambda g, t, _: (g, t, 0))
    bwd = pl.BlockSpec((HG_SEQS, step, HG_WIDTH), lambda g, t, _: (g, nt - 1 - t, 0))
    state = pl.BlockSpec((HG_SEQS, HG_HEADS, HG_HEAD_DIM, HG_HEAD_DIM), lambda g, t, _: (g, 0, 0, 0))
    in_specs = [fwd, fwd, fwd, bwd, bwd, bwd]
    args = [qs, vh, f_fwd, qs, vh, f_bwd]
    out_specs = [fwd, bwd]
    out_shape = [jax.ShapeDtypeStruct(qs.shape, f32)] * 2
    if has_init:
        in_specs += [state, state]
        args += list(init_states)
    else:
        out_specs += [state, state]
        out_shape += [jax.ShapeDtypeStruct((n_seqs, HG_HEADS, HG_HEAD_DIM, HG_HEAD_DIM), f32)] * 2
    return pl.pallas_call(
        functools.partial(_hgrn_kernel, has_init=has_init),
        grid_spec=pltpu.PrefetchScalarGridSpec(
            num_scalar_prefetch=1,
            grid=(groups, nt),
            in_specs=in_specs,
            out_specs=out_specs,
            scratch_shapes=[pltpu.VMEM((HG_SEQS, HG_HEADS, HG_HEAD_DIM, HG_HEAD_DIM), f32)] * 2),
        out_shape=out_shape,
        compiler_params=pltpu.CompilerParams(
            dimension_semantics=("arbitrary", "arbitrary"), vmem_limit_bytes=VMEM_LIMIT_BYTES),
        name="hgrn2",
    )(slow, *args)


def _attn_kernel(*refs, n_ctx, n_cast):
    kk_ref, vta_ref, bound_ref, fixed_ref = refs[-4:]
    o_ref, *cast_out = refs[len(refs) - 5 - n_cast:-4]
    inputs = refs[:len(refs) - 5 - n_cast]
    cast_in = inputs[len(inputs) - n_cast:]
    if n_ctx:
        (qa_ref, ka_ref, vt_ref, ckt_ref, cvt_ref, x_ref, of_ref, ob_ref, sg_ref, m_ref, wo_ref, hg_ref, qg_ref,
         g_ref, b_ref) = inputs[:len(inputs) - n_cast]
    else:
        (qa_ref, ka_ref, vt_ref, x_ref, of_ref, ob_ref, sg_ref, m_ref, wo_ref, hg_ref, qg_ref,
         g_ref, b_ref) = inputs[:len(inputs) - n_cast]
    for src_ref, dst_ref in zip(cast_in, cast_out):
        dst_ref[...] = src_ref[...].astype(bf16)
    tq = qa_ref.shape[1]
    n_keys = kk_ref.shape[0]

    @pl.when(pl.program_id(1) == 0)
    def _():
        def fill_keys(lo, k):
            n = k.shape[0]
            low = lax.broadcasted_iota(jnp.int32, k.shape, 1) < HEAD_DIM
            k_sw = pltpu.roll(k, HEAD_DIM, 1)
            kk_ref[lo:lo + n, 0:LANES] = jnp.where(low, k, k_sw).astype(bf16)
            kk_ref[lo:lo + n, LANES:2 * LANES] = jnp.where(low, k_sw, k).astype(bf16)

        def fill_values(lo, vt):
            n = vt.shape[1]
            ones_row = jnp.where(lax.broadcasted_iota(jnp.int32, (V_ROWS - HEAD_DIM, n), 0) == 0, 1.0, 0.0)
            for kv in range(N_KV_HEADS):
                vta_ref[kv * V_ROWS:kv * V_ROWS + HEAD_DIM, lo:lo + n] = (
                    vt[kv * HEAD_DIM:(kv + 1) * HEAD_DIM].astype(bf16))
                vta_ref[kv * V_ROWS + HEAD_DIM:(kv + 1) * V_ROWS, lo:lo + n] = ones_row.astype(bf16)

        if n_ctx:
            fill_keys(0, ckt_ref[0].T)
            fill_values(0, cvt_ref[0])
        fill_keys(n_ctx, ka_ref[0])
        fill_values(n_ctx, vt_ref[0])

        gain_max = jnp.max(jnp.abs(qg_ref[...]), axis=1, keepdims=True)
        all_small = None
        for kv in range(N_KV_HEADS):
            kt = kk_ref[:, kv * LANES:(kv + 1) * LANES].astype(f32)
            k_norm2 = jnp.max(0.5 * jnp.sum(kt * kt, axis=1, keepdims=True), axis=0, keepdims=True)
            bound = (LOG2_E * SCORE_BOUND_SLACK) * gain_max * jnp.sqrt(k_norm2)
            bound_ref[kv] = bound[0, 0]
            small = jnp.where(bound <= SCORE_BOUND_LIMIT, 1, 0)
            all_small = small if all_small is None else all_small * small
        fixed_ref[0] = all_small[0, 0]

    ts = min(Q_SUB, tq)
    low = lax.broadcasted_iota(jnp.int32, (ts, LANES), 1) < HEAD_DIM
    kc = min(KEY_CHUNK, n_keys)
    pairs_per_kv = N_HEADS // N_KV_HEADS // 2

    def attend(rows, fixed_shift):
        def masked_pair(tile):
            qp = qa_ref[0, rows, tile * LANES:(tile + 1) * LANES]
            zero = jnp.zeros_like(qp)
            return jnp.concatenate([jnp.where(low, qp, zero), jnp.where(low, zero, qp)], axis=0)

        q_pairs = [masked_pair(tile) for tile in range(N_HEADS // 2)]
        work = [(tile, lo) for tile in range(N_HEADS // 2) for lo in range(0, n_keys, kc)]

        def scores(tile, lo):
            kv = tile // pairs_per_kv
            return _dot_nt(kk_ref[lo:lo + kc, kv * LANES:(kv + 1) * LANES], q_pairs[tile])

        heads_t = []
        st_next = scores(*work[0])
        m = acc = None
        for i, (tile, lo) in enumerate(work):
            st = st_next
            if i + 1 < len(work):
                st_next = scores(*work[i + 1])
            kv = tile // pairs_per_kv
            values_t = vta_ref[kv * V_ROWS:(kv + 1) * V_ROWS, lo:lo + kc]
            if fixed_shift:
                e = jnp.exp2(st - bound_ref[kv]).astype(bf16)
                pv = jnp.dot(values_t, e, preferred_element_type=f32)
                acc = pv if acc is None else acc + pv
            else:
                m_chunk = jnp.max(st, axis=0, keepdims=True)
                m_new = m_chunk if m is None else jnp.maximum(m, m_chunk)
                e = jnp.exp2(st - m_new).astype(bf16)
                pv = jnp.dot(values_t, e, preferred_element_type=f32)
                acc = pv if acc is None else acc * jnp.exp2(m - m_new) + pv
                m = m_new
            if lo + kc == n_keys:
                on = (acc[0:HEAD_DIM] * (1.0 / acc[HEAD_DIM:HEAD_DIM + 1])).astype(bf16)
                heads_t += [on[:, :ts], on[:, ts:]]
                m = acc = None
        return jnp.concatenate(heads_t, axis=0)

    def project(rows, o_att_t):
        o_sum = of_ref[0, rows] + ob_ref[0, rows]
        normed = []
        for hd in range(HG_HEADS):
            oh = o_sum[:, hd * HG_HEAD_DIM:(hd + 1) * HG_HEAD_DIM]
            ms = jnp.mean(oh * oh, axis=-1, keepdims=True)
            normed.append(oh * lax.rsqrt(ms + RMS_EPS))
        o_hg = (jnp.concatenate(normed, axis=-1) * hg_ref[...] * sg_ref[0, rows]).astype(bf16)
        y = (jnp.dot(o_hg, wo_ref[0:HG_WIDTH, :], preferred_element_type=f32)
             + _dot_tn(o_att_t, wo_ref[HG_WIDTH:HG_WIDTH + ATT_WIDTH, :]))
        r = ALPHA * x_ref[0, rows] + m_ref[0, 5:6, :] * y
        o_ref[0, rows] = _layer_norm(r, g_ref[1:2, :], b_ref[1:2, :])

    def run(fixed_shift):
        for lo in range(0, tq, ts):
            rows = slice(lo, lo + ts)
            project(rows, attend(rows, fixed_shift))

    use_bound = fixed_ref[0] != 0
    pl.when(use_bound)(functools.partial(run, True))
    pl.when(jnp.logical_not(use_bound))(functools.partial(run, False))


def _attn(qa, ka, vt, ctx_kv_t, x, o_f, o_b, sg, mod, mod_group, w_out, hg_gain, q_gain, ln_g, ln_b, to_cast=()):
    groups, length, _ = qa.shape
    tq = min(Q_TILE, length)
    nt = length // tq
    n_ctx = 0 if ctx_kv_t is None else ctx_kv_t[0].shape[2]
    n_keys = n_ctx + length
    tok = lambda width: pl.BlockSpec((1, tq, width), lambda g, t: (g, t, 0))
    whole_t = lambda n: pl.BlockSpec((1, KV_WIDTH, n), lambda g, t: (g, 0, 0))
    in_specs = [tok(ATT_WIDTH), pl.BlockSpec((1, length, KV_WIDTH), lambda g, t: (g, 0, 0)), whole_t(length)]
    args = [qa, ka, vt]
    if n_ctx:
        in_specs += [whole_t(n_ctx), whole_t(n_ctx)]
        args += list(ctx_kv_t)
    in_specs += [
        tok(D_MODEL), tok(HG_WIDTH), tok(HG_WIDTH), tok(HG_WIDTH),
        pl.BlockSpec((1, N_MOD, D_MODEL), lambda g, t: (mod_group(g), 0, 0)),
        _resident((D_MODEL, D_MODEL)),
        _resident((1, HG_WIDTH)),
        _resident((1, ATT_WIDTH)),
        _resident((3, D_MODEL)),
        _resident((3, D_MODEL)),
    ]
    args += [x, o_f, o_b, sg, mod, w_out, hg_gain, q_gain, ln_g, ln_b]
    cast_specs, cast_shapes = _cast_specs(to_cast, nt, groups * nt)
    in_specs += cast_specs
    args += list(to_cast)
    out_specs = [tok(D_MODEL)] + cast_specs
    out_shape = [jax.ShapeDtypeStruct(x.shape, f32)] + cast_shapes
    return pl.pallas_call(
        functools.partial(_attn_kernel, n_ctx=n_ctx, n_cast=len(to_cast)),
        grid=(groups, nt),
        in_specs=in_specs,
        out_specs=out_specs,
        out_shape=out_shape,
        scratch_shapes=[pltpu.VMEM((n_keys, N_KV_HEADS * LANES), bf16),
                        pltpu.VMEM((N_KV_HEADS * V_ROWS, n_keys), bf16),
                        pltpu.SMEM((N_KV_HEADS,), f32),
                        pltpu.SMEM((1,), jnp.int32)],
        compiler_params=pltpu.CompilerParams(
            dimension_semantics=("arbitrary", "arbitrary"), vmem_limit_bytes=VMEM_LIMIT_BYTES),
        name="attn_out",
    )(*args)


def _rope_tables(n_tokens):
    half = HEAD_DIM // 2
    t = jnp.arange(n_tokens)
    inv = ROPE_THETA ** (-jnp.arange(0, half, 2, dtype=f32) / half)
    ang_row = (t // GRID_W).astype(f32)[:, None] * inv
    ang_col = (t % GRID_W).astype(f32)[:, None] * inv
    cos = jnp.concatenate([jnp.cos(ang_row)] * 2 + [jnp.cos(ang_col)] * 2, axis=-1)
    sin = jnp.concatenate([-jnp.sin(ang_row), jnp.sin(ang_row), -jnp.sin(ang_col), jnp.sin(ang_col)], axis=-1)
    return jnp.tile(cos, (1, LANES // HEAD_DIM)), jnp.tile(sin, (1, LANES // HEAD_DIM))


def kernel(x_prompt, x_sample, cache_k, cache_v, state_hgrn_fwd, state_hgrn_bwd, c, c_ctx, w_mod, b_mod,
           w_ffn1_in, w_ffn1_out, w_ffn2_in, w_ffn2_out, w_in, w_out, q_norm_g, k_norm_g, hg_norm_g,
           lb_logits_fwd, lb_logits_bwd, ln_g, ln_b):
    assert w_mod.shape[0] == DEPTH and lb_logits_fwd.shape[0] == DEPTH + 1
    batch, seq, _ = x_prompt.shape
    dec_batch, dec_seq, _ = x_sample.shape
    past = cache_k.shape[2]

    ctx_row = dec_batch
    rows = -(-(dec_batch + 1) // BF16_SUBLANES) * BF16_SUBLANES
    cvecs = jnp.concatenate([c, c_ctx[None, :], jnp.zeros((rows - dec_batch - 1, D_MODEL), f32)], axis=0)
    mod = _modulation(cvecs, w_mod[0], b_mod[0]).reshape(rows, N_MOD, D_MODEL)

    to_bf16 = lambda w: w.astype(bf16)
    w1u, w1d = to_bf16(w_ffn1_in[0]), to_bf16(w_ffn1_out[0])
    gains = ln_g[0], ln_b[0]
    q_gain = jnp.tile(q_norm_g[0], N_HEADS).reshape(1, ATT_WIDTH)
    k_gain = jnp.tile(k_norm_g[0], N_KV_HEADS).reshape(1, KV_WIDTH)
    hg_gain = hg_norm_g[0].reshape(1, HG_WIDTH)
    head_of = jnp.arange(ATT_WIDTH) // HEAD_DIM
    ones_bd = (head_of[:, None] == head_of[None, :]).astype(bf16)

    def mixer_sublayer(x, mod_group, rope_tables, ctx_kv_t, init_states, hg_groups, to_cast=()):
        shape = x.shape
        seq_len = shape[0] * shape[1] // hg_groups
        per_seq = lambda a: a.reshape(hg_groups, seq_len, a.shape[-1])
        proj = _proj(x, mod, mod_group, w_in_b, lb_logits_fwd, lb_logits_bwd, q_gain, k_gain, ones_bd, rope_tables,
                     seq_len)
        qs, vh, f_f, f_b, sg, qa, ka = map(per_seq, proj[:7])
        vt, span_f, span_b = proj[7], proj[8].reshape(hg_groups, -1, LANES), proj[9].reshape(hg_groups, -1, LANES)
        scans = _hgrn(qs, vh, f_f, f_b, span_f, span_b, init_states)
        x, *cast = _attn(qa, ka, vt, ctx_kv_t, per_seq(x), scans[0], scans[1], sg, mod, mod_group, w_out_b, hg_gain,
                         q_gain, *gains, to_cast=to_cast)
        return x.reshape(shape), proj[10:], vt, scans[2:], cast

    from_cache = lambda t: t[:, 0].transpose(0, 2, 3, 1).reshape(dec_batch, KV_WIDTH, past)
    init_states = (state_hgrn_fwd[:, 0], state_hgrn_bwd[:, 0])
    latent_group = lambda g: g
    x_latent, w_in_b, w_out_b = _ffn(x_sample, mod, latent_group, w1u, w1d, *gains, mod_base=0, ln_row=0,
                                     to_cast=(w_in[0], w_out[0]))
    x_latent, _, _, _, (w2u, w2d) = mixer_sublayer(
        x_latent, latent_group, _rope_tables(dec_seq), (from_cache(cache_k), from_cache(cache_v)), init_states,
        dec_batch, to_cast=(w_ffn2_in[0], w_ffn2_out[0]))

    ctx_group = lambda g: ctx_row
    x_ctx = _ffn(x_prompt.reshape(1, batch * seq, D_MODEL), mod, ctx_group, w1u, w1d, *gains, mod_base=0, ln_row=0)
    x_ctx, (kt_new,), vt_new, states, _ = mixer_sublayer(x_ctx, ctx_group, None, None, None, batch)
    y_sample = _ffn(x_latent, mod, latent_group, w2u, w2d, *gains, mod_base=6, ln_row=2)
    y_prompt = _ffn(x_ctx, mod, ctx_group, w2u, w2d, *gains, mod_base=6, ln_row=2)
    y_prompt = y_prompt.reshape(batch, seq, D_MODEL)
    to_cache = lambda t: t.reshape(batch, DEPTH, N_KV_HEADS, HEAD_DIM, seq).transpose(0, 1, 4, 2, 3)
    new_cache_k, new_cache_v = to_cache(kt_new), to_cache(vt_new)
    new_state_fwd = states[0].reshape(batch, DEPTH, HG_HEADS, HG_HEAD_DIM, HG_HEAD_DIM)
    new_state_bwd = states[1].reshape(batch, DEPTH, HG_HEADS, HG_HEAD_DIM, HG_HEAD_DIM)

    return (y_prompt, y_sample, new_cache_k, new_cache_v, new_state_fwd, new_state_bwd)
```

```python
import functools

import jax
import jax.numpy as jnp
from jax import lax
from jax.experimental import pallas as pl
from jax.experimental.pallas import tpu as pltpu

f32 = jnp.float32
bf16 = jnp.bfloat16

D_MODEL = 1024
N_MOD = 9
HG_WIDTH = 512
HG_HEAD_DIM = 128
HG_HEADS = 4
ATT_WIDTH = 512
HEAD_DIM = 64
N_HEADS = 8
N_KV_HEADS = 2
KV_WIDTH = 128
IN_WIDTH = 5 * HG_WIDTH + ATT_WIDTH + 2 * KV_WIDTH
D_FF = 2816
GRID_W = 64
ROPE_PAIR = HEAD_DIM // 4
ROPE_THETA = 10000.0
DEPTH = 1
ALPHA = (2.0 * DEPTH) ** 0.25
LOG2_E = 1.4426950408889634
LN_EPS = 1e-6
RMS_EPS = 1e-6

LANES = 128
BF16_SUBLANES = 16
VMEM_LIMIT_BYTES = 56 * 1024 * 1024

FF_CHUNK = 256
FFN_TILE = 1024
FFN_SUB = 512
TOKEN_TILE = 512
PROJ_SUB = 256
HG_TILE = 128
HG_STEP = 512
HG_SEQS = 2
Q_TILE = 512
Q_SUB = 256
KEY_CHUNK = 512
SCORE_BOUND_LIMIT = 40.0
SCORE_BOUND_SLACK = 1.01
V_ROWS = 80
HG_FAST_BLOCK = 64
HG_FAST_SPAN_LOG2 = 100.0


def _silu(x):
    return x * jax.nn.sigmoid(x)


def _layer_norm(r, g, b):
    mu = jnp.mean(r, axis=-1, keepdims=True)
    c = r - mu
    var = jnp.mean(c * c, axis=-1, keepdims=True)
    return c * lax.rsqrt(var + LN_EPS) * g + b


def _split_bf16(x, parts):
    out = []
    r = x
    for _ in range(parts - 1):
        p = r.astype(bf16)
        out.append(p)
        r = r - p.astype(f32)
    out.append(r.astype(bf16))
    return out


def _dot01(mat01, x, parts, *, mat_on_left):
    pieces = _split_bf16(x, parts)
    if mat_on_left:
        return jnp.dot(jnp.concatenate([mat01] * parts, axis=1), jnp.concatenate(pieces, axis=0),
                       preferred_element_type=f32)
    return jnp.dot(jnp.concatenate(pieces, axis=1), jnp.concatenate([mat01] * parts, axis=0),
                   preferred_element_type=f32)


def _dot_nt(a, b):
    return lax.dot_general(a, b, (((1,), (1,)), ((), ())), preferred_element_type=f32)


def _dot_tn(a, b):
    return lax.dot_general(a, b, (((0,), (0,)), ((), ())), preferred_element_type=f32)


def _mod_kernel(c_ref, w_ref, b_ref, o_ref):
    a = _silu(c_ref[...]).astype(bf16)
    o_ref[...] = jnp.dot(a, w_ref[...].astype(bf16), preferred_element_type=f32) + b_ref[...]


def _modulation(cvecs, w_mod, b_mod):
    rows = cvecs.shape[0]
    n_out = w_mod.shape[1]
    tn = D_MODEL
    return pl.pallas_call(
        _mod_kernel,
        grid=(n_out // tn,),
        in_specs=[
            pl.BlockSpec((rows, D_MODEL), lambda j: (0, 0)),
            pl.BlockSpec((D_MODEL, tn), lambda j: (0, j)),
            pl.BlockSpec((1, tn), lambda j: (0, j)),
        ],
        out_specs=pl.BlockSpec((rows, tn), lambda j: (0, j)),
        out_shape=jax.ShapeDtypeStruct((rows, n_out), f32),
        compiler_params=pltpu.CompilerParams(dimension_semantics=("arbitrary",)),
        name="modulation",
    )(cvecs, w_mod, b_mod.reshape(1, n_out))


def _ffn_kernel(x_ref, m_ref, wup_ref, wd_ref, g_ref, b_ref, *refs, mod_base, ln_row):
    n_cast = (len(refs) - 2) // 2
    cast_in, o_ref, cast_out, act_ref = refs[:n_cast], refs[n_cast], refs[n_cast + 1:-1], refs[-1]
    for src_ref, dst_ref in zip(cast_in, cast_out):
        dst_ref[...] = src_ref[...].astype(bf16)
    shift = m_ref[0, mod_base:mod_base + 1, :]
    scale = m_ref[0, mod_base + 1:mod_base + 2, :]
    gate = m_ref[0, mod_base + 2:mod_base + 3, :]
    subs = [slice(lo, lo + FFN_SUB) for lo in range(0, x_ref.shape[1], FFN_SUB)]
    xs = [x_ref[0, rows] for rows in subs]
    hs = [(x * (1.0 + scale) + shift).astype(bf16) for x in xs]
    for j in range(D_FF // FF_CHUNK):
        cols = slice(j * FF_CHUNK, (j + 1) * FF_CHUNK)
        for rows, h in zip(subs, hs):
            a = jnp.dot(h, wup_ref[:, cols], preferred_element_type=f32)
            u = jnp.dot(h, wup_ref[:, D_FF + j * FF_CHUNK:D_FF + (j + 1) * FF_CHUNK],
                        preferred_element_type=f32)
            act_ref[rows, cols] = (_silu(a) * u).astype(bf16)
    ys = [jnp.dot(act_ref[rows, :], wd_ref[...], preferred_element_type=f32) for rows in subs]
    for rows, x, y in zip(subs, xs, ys):
        r = ALPHA * x + 0.5 * gate * y
        o_ref[0, rows] = _layer_norm(r, g_ref[ln_row:ln_row + 1, :], b_ref[ln_row:ln_row + 1, :])


def _resident(shape):
    return pl.BlockSpec(shape, lambda *_: (0,) * len(shape), pipeline_mode=pl.Buffered(1))


def _cast_specs(to_cast, nt, steps):
    specs, shapes = [], []
    for w in to_cast:
        rows = next(r for r in range(BF16_SUBLANES, w.shape[0] + 1, BF16_SUBLANES)
                    if w.shape[0] % r == 0 and w.shape[0] // r <= steps)
        specs.append(pl.BlockSpec((rows, w.shape[1]),
                                  lambda g, t, last=w.shape[0] // rows - 1: (jnp.minimum(g * nt + t, last), 0)))
        shapes.append(jax.ShapeDtypeStruct(w.shape, bf16))
    return specs, shapes


def _ffn(x, mod, mod_group, w_up, wd, ln_g, ln_b, *, mod_base, ln_row, to_cast=()):
    groups, length, _ = x.shape
    tm = FFN_TILE
    nt = length // tm
    cast_specs, cast_shapes = _cast_specs(to_cast, nt, groups * nt)
    out = pl.pallas_call(
        functools.partial(_ffn_kernel, mod_base=mod_base, ln_row=ln_row),
        grid=(groups, nt),
        in_specs=[
            pl.BlockSpec((1, tm, D_MODEL), lambda g, t: (g, t, 0)),
            pl.BlockSpec((1, N_MOD, D_MODEL), lambda g, t: (mod_group(g), 0, 0)),
            _resident((D_MODEL, 2 * D_FF)),
            _resident((D_FF, D_MODEL)),
            _resident((3, D_MODEL)),
            _resident((3, D_MODEL)),
        ] + cast_specs,
        out_specs=[pl.BlockSpec((1, tm, D_MODEL), lambda g, t: (g, t, 0))] + cast_specs,
        out_shape=[jax.ShapeDtypeStruct(x.shape, f32)] + cast_shapes,
        scratch_shapes=[pltpu.VMEM((tm, D_FF), bf16)],
        compiler_params=pltpu.CompilerParams(
            dimension_semantics=("arbitrary", "arbitrary"), vmem_limit_bytes=VMEM_LIMIT_BYTES),
        name="ffn",
    )(x, mod, w_up, wd, ln_g, ln_b, *to_cast)
    return out if to_cast else out[0]


def _head_rms_norm(x, ones_bd, gain, parts):
    ss = _dot01(ones_bd, x * x, parts, mat_on_left=False)
    return x * lax.rsqrt(ss * (1.0 / HEAD_DIM) + RMS_EPS) * gain


def _rope(x, cos, sin_signed):
    width = x.shape[-1]
    lane = lax.broadcasted_iota(jnp.int32, x.shape, 1)
    from_right = pltpu.roll(x, width - ROPE_PAIR, 1)
    from_left = pltpu.roll(x, ROPE_PAIR, 1)
    partner = jnp.where((lane & (2 * ROPE_PAIR - 1)) < ROPE_PAIR, from_right, from_left)
    return x * cos + partner * sin_signed


def _proj_kernel(*refs, rope):
    if rope:
        (x_ref, m_ref, w_ref, lbf_ref, lbb_ref, qg_ref, kg_ref, bd_ref, cos_ref, sin_ref,
         qs_ref, vh_ref, ff_ref, fb_ref, sg_ref, qa_ref, ka_ref, vt_ref,
         spf_ref, spb_ref) = refs
    else:
        (x_ref, m_ref, w_ref, lbf_ref, lbb_ref, qg_ref, kg_ref, bd_ref,
         qs_ref, vh_ref, ff_ref, fb_ref, sg_ref, qa_ref, ka_ref, vt_ref,
         spf_ref, spb_ref, kt_ref) = refs
    shift = m_ref[0, 3:4, :]
    scale = m_ref[0, 4:5, :]
    subs = [slice(lo, lo + PROJ_SUB) for lo in range(0, x_ref.shape[1], PROJ_SUB)]
    hs = [(x_ref[0, rows] * (1.0 + scale) + shift).astype(bf16) for rows in subs]

    def cols(h, lo, width):
        return jnp.dot(h, w_ref[:, lo:lo + width], preferred_element_type=f32)

    def lower_bound(lb_ref):
        l0 = lb_ref[0:1, :]
        l1 = lb_ref[1:2, :]
        m = jnp.maximum(l0, l1)
        e0 = jnp.exp(l0 - m)
        e1 = jnp.exp(l1 - m)
        return e0 / (e0 + e1)

    half = HG_FAST_BLOCK // 2

    def forget_gate(lb_ref, lo, f_ref, span_ref):
        lb = lower_bound(lb_ref)
        for rows, h in zip(subs, hs):
            f = lb + (1.0 - lb) * jax.nn.sigmoid(cols(h, lo, HG_WIDTH))
            f_ref[0, rows] = f
            sums = jnp.sum(jnp.log2(f).reshape(PROJ_SUB // half, half, HG_WIDTH), axis=1)
            span_ref[0, rows.start // half:rows.stop // half] = jnp.broadcast_to(
                jnp.max(jnp.abs(sums), axis=-1, keepdims=True), (PROJ_SUB // half, LANES))

    base = 5 * HG_WIDTH
    piece = vt_ref.shape[2]
    for rows, h in zip(subs, hs):
        q = _head_rms_norm(cols(h, base, ATT_WIDTH), bd_ref[...], qg_ref[...], 1)
        k = _head_rms_norm(cols(h, base + ATT_WIDTH, KV_WIDTH), bd_ref[0:KV_WIDTH, 0:KV_WIDTH], kg_ref[...], 2)
        if rope:
            cos, sin = cos_ref[rows, :], sin_ref[rows, :]
            q = _rope(q, jnp.concatenate([cos] * (ATT_WIDTH // LANES), axis=1),
                      jnp.concatenate([sin] * (ATT_WIDTH // LANES), axis=1))
            k = _rope(k, cos, sin)
        qa_ref[0, rows] = (q * (HEAD_DIM ** -0.5 * LOG2_E)).astype(bf16)
        ka_ref[0, rows] = k
        v = cols(h, base + ATT_WIDTH + KV_WIDTH, KV_WIDTH)
        for lo in range(rows.start, rows.stop, min(piece, PROJ_SUB)):
            n = min(piece, PROJ_SUB)
            dst = (lo // piece, slice(None), slice(lo % piece, lo % piece + n))
            vt_ref[dst] = v[lo - rows.start:lo - rows.start + n].T
            if not rope:
                kt_ref[dst] = k[lo - rows.start:lo - rows.start + n].T

    for rows, h in zip(subs, hs):
        qs_ref[0, rows] = _silu(cols(h, 0, HG_WIDTH))
    for rows, h in zip(subs, hs):
        vh_ref[0, rows] = cols(h, HG_WIDTH, HG_WIDTH).astype(bf16)
    forget_gate(lbf_ref, 2 * HG_WIDTH, ff_ref, spf_ref)
    forget_gate(lbb_ref, 3 * HG_WIDTH, fb_ref, spb_ref)
    for rows, h in zip(subs, hs):
        sg_ref[0, rows] = _silu(cols(h, 4 * HG_WIDTH, HG_WIDTH))


def _proj(x, mod, mod_group, w_in, lb_f, lb_b, q_gain, k_gain, ones_bd, rope_tables, seq_len):
    groups, length, _ = x.shape
    tm = TOKEN_TILE
    rope = rope_tables is not None
    n_seq = groups * length // seq_len
    if seq_len >= tm:
        tiles_per_seq = seq_len // tm
        t_spec = pl.BlockSpec((1, KV_WIDTH, tm), lambda g, t: (g * (length // seq_len) + t // tiles_per_seq, 0,
                                                               t % tiles_per_seq))
    else:
        t_spec = pl.BlockSpec((tm // seq_len, KV_WIDTH, seq_len), lambda g, t: (g * (length // tm) + t, 0, 0))
    t_shape = jax.ShapeDtypeStruct((n_seq, KV_WIDTH, seq_len), f32)
    tok = lambda width: pl.BlockSpec((1, tm, width), lambda g, t: (g, t, 0))
    in_specs = [
        tok(D_MODEL),
        pl.BlockSpec((1, N_MOD, D_MODEL), lambda g, t: (mod_group(g), 0, 0)),
        _resident((D_MODEL, IN_WIDTH)),
        _resident((2, HG_WIDTH)),
        _resident((2, HG_WIDTH)),
        _resident((1, ATT_WIDTH)),
        _resident((1, KV_WIDTH)),
        _resident((ATT_WIDTH, ATT_WIDTH)),
    ]
    args = [x, mod, w_in, lb_f, lb_b, q_gain, k_gain, ones_bd]
    if rope:
        in_specs += [pl.BlockSpec((tm, LANES), lambda g, t: (t, 0))] * 2
        args += list(rope_tables)
    shape = lambda width, dt: jax.ShapeDtypeStruct((groups, length, width), dt)
    half = HG_FAST_BLOCK // 2
    span_spec = pl.BlockSpec((1, tm // half, LANES), lambda g, t: (g, t, 0))
    span_shape = jax.ShapeDtypeStruct((groups, length // half, LANES), f32)
    return pl.pallas_call(
        functools.partial(_proj_kernel, rope=rope),
        grid=(groups, length // tm),
        in_specs=in_specs,
        out_specs=([tok(HG_WIDTH)] * 5 + [tok(ATT_WIDTH), tok(KV_WIDTH), t_spec, span_spec, span_spec]
                   + ([] if rope else [t_spec])),
        out_shape=([shape(HG_WIDTH, f32), shape(HG_WIDTH, bf16)] + [shape(HG_WIDTH, f32)] * 3
                   + [shape(ATT_WIDTH, bf16), shape(KV_WIDTH, f32), t_shape, span_shape, span_shape]
                   + ([] if rope else [t_shape])),
        compiler_params=pltpu.CompilerParams(
            dimension_semantics=("arbitrary", "arbitrary"), vmem_limit_bytes=VMEM_LIMIT_BYTES),
        name="mixer_proj",
    )(*args)


def _hgrn_direction(qs, vb, f, st_ref, o_ref, rows, *, rev):
    tc = qs.shape[0]
    lf = jnp.log2(f)
    kk = 1.0 - f
    row = lax.broadcasted_iota(jnp.int32, (tc, tc), 0)
    col = lax.broadcasted_iota(jnp.int32, (tc, tc), 1)
    ordered = (row < col) if rev else (row > col)
    differ = row ^ col
    tri = jnp.where((col >= row) if rev else (col <= row), 1.0, 0.0).astype(bf16)
    cum = _dot01(tri, lf, 2, mat_on_left=True)
    total = cum[0:1] if rev else cum[tc - 1:tc]
    carry = jnp.exp2(total)
    query_half = 0 if rev else 1
    heads = [slice(hd * HG_HEAD_DIM, (hd + 1) * HG_HEAD_DIM) for hd in range(HG_HEADS)]

    def level_exponent(s, ridx):
        if s == 1:
            return jnp.where((ridx & 1) == query_half, lf, 0.0)
        if s == 2:
            nxt = pltpu.roll(lf, tc - 1, 0)
            prv = pltpu.roll(lf, 1, 0)
            m4 = ridx & 3
            if rev:
                return jnp.where(m4 == 0, lf + nxt, jnp.where(m4 == 1, lf, jnp.where(m4 == 2, 0.0, prv)))
            return jnp.where(m4 == 0, nxt, jnp.where(m4 == 1, 0.0, jnp.where(m4 == 2, lf, lf + prv)))
        blocks = []
        for lo in range(0, tc, 2 * s):
            anchor = lo + (s if rev else s - 1)
            blocks.append(cum[lo:lo + 2 * s] - cum[anchor:anchor + 1])
        d = jnp.concatenate(blocks, axis=0)
        is_query = ((ridx >> (s.bit_length() - 1)) & 1) == query_half
        return jnp.where(is_query, d, -d)

    def add_levels(intra, first):
        ridx = lax.broadcasted_iota(jnp.int32, qs.shape, 0)
        s = first
        while s < tc:
            shift = s.bit_length() - 1
            if 2 * s == tc:
                lower, upper = slice(0, s), slice(s, tc)
                anchor = cum[s:s + 1] if rev else cum[s - 1:s]
                q_rows, k_rows = (lower, upper) if rev else (upper, lower)
                halves = {q_rows: qs[q_rows] * jnp.exp2(cum[q_rows] - anchor),
                          k_rows: kk[k_rows] * jnp.exp2(anchor - cum[k_rows])}
                mixed = jnp.concatenate([halves[lower], halves[upper]], axis=0).astype(bf16)
            else:
                is_query = ((ridx >> shift) & 1) == query_half
                mixed = (jnp.where(is_query, qs, kk) * jnp.exp2(level_exponent(s, ridx))).astype(bf16)
            pair = jnp.logical_and((differ >> shift) == 1, ordered)
            for hd, hs in enumerate(heads):
                intra[hd] = jnp.where(pair, _dot_nt(mixed[:, hs], mixed[:, hs]), intra[hd])
            s *= 2
        return intra

    def finish(intra, q_dec, k_end, extra):
        for hd, hs in enumerate(heads):
            st = st_ref[hd]
            o = jnp.dot(jnp.concatenate([intra[hd].astype(bf16), q_dec[:, hs]], axis=1),
                        jnp.concatenate([vb[:, hs], st.T.astype(bf16)], axis=0), preferred_element_type=f32)
            o_ref[rows, hs] = o if extra is None else o + extra[:, hs]
            st_ref[hd] = st * carry[:, hs] + _dot_tn(vb[:, hs], k_end[:, hs])

    blk = HG_FAST_BLOCK
    anchors = [lo + (blk // 2 if rev else blk // 2 - 1) for lo in range(0, tc, blk)]

    def mid_split_blocks():
        q_mid, k_mid, q_dec, k_end = [], [], [], []
        for lo, a in zip(range(0, tc, blk), anchors):
            rel = cum[lo:lo + blk] - cum[a:a + 1]
            q_mid.append(qs[lo:lo + blk] * jnp.exp2(rel))
            k_mid.append(kk[lo:lo + blk] * jnp.exp2(-rel))
            q_dec.append(q_mid[-1] * jnp.exp2(cum[a:a + 1]))
            k_end.append(k_mid[-1] * jnp.exp2(total - cum[a:a + 1]))
        q_mid, k_mid, q_dec, k_end = (jnp.concatenate(p, axis=0).astype(bf16) for p in (q_mid, k_mid, q_dec, k_end))
        same_block = (differ >> (blk.bit_length() - 1)) == 0
        keep = jnp.logical_and(same_block, (row <= col) if rev else (row >= col))
        intra = [jnp.where(keep, _dot_nt(q_mid[:, hs], k_mid[:, hs]), 0.0) for hs in heads]
        finish(add_levels(intra, blk), q_dec, k_end, None)

    def all_levels():
        intra = add_levels([jnp.zeros((tc, tc), f32)] * HG_HEADS, 1)
        q_dec = (qs * jnp.exp2(cum)).astype(bf16)
        k_end = (kk * jnp.exp2(total - cum)).astype(bf16)
        qk = qs * kk
        own = jnp.concatenate(
            [jnp.sum(qk[:, hs], axis=-1, keepdims=True) * vb[:, hs].astype(f32) for hs in heads], axis=-1)
        finish(intra, q_dec, k_end, own)

    return mid_split_blocks, all_levels


def _hgrn_kernel(slow_ref, *refs, has_init):
    if has_init:
        (qsf_ref, vf_ref, ff_ref, qsb_ref, vb_ref, fb_ref, s0f_ref, s0b_ref,
         of_ref, ob_ref, stf_ref, stb_ref) = refs
    else:
        (qsf_ref, vf_ref, ff_ref, qsb_ref, vb_ref, fb_ref,
         of_ref, ob_ref, sf_out_ref, sb_out_ref, stf_ref, stb_ref) = refs
    t = pl.program_id(1)
    n_seq = qsf_ref.shape[0]

    @pl.when(t == 0)
    def _():
        for b in range(n_seq):
            for hd in range(HG_HEADS):
                if has_init:
                    stf_ref[b, hd] = s0f_ref[b, hd].T
                    stb_ref[b, hd] = s0b_ref[b, hd].T
                else:
                    stf_ref[b, hd] = jnp.zeros((HG_HEAD_DIM, HG_HEAD_DIM), f32)
                    stb_ref[b, hd] = jnp.zeros((HG_HEAD_DIM, HG_HEAD_DIM), f32)

    chunks = [slice(lo, lo + HG_TILE) for lo in range(0, qsf_ref.shape[1], HG_TILE)]
    scans = []
    for r_f, r_b in zip(chunks, reversed(chunks)):
        for b in range(n_seq):
            scans.append(_hgrn_direction(qsf_ref[b, r_f], vf_ref[b, r_f], ff_ref[b, r_f], stf_ref.at[b], of_ref.at[b],
                                         r_f, rev=False))
            scans.append(_hgrn_direction(qsb_ref[b, r_b], vb_ref[b, r_b], fb_ref[b, r_b], stb_ref.at[b], ob_ref.at[b],
                                         r_b, rev=True))
    slow = slow_ref[pl.program_id(0) * pl.num_programs(1) + t] != 0

    @pl.when(jnp.logical_not(slow))
    def _():
        for mid_split_blocks, _ in scans:
            mid_split_blocks()

    @pl.when(slow)
    def _():
        for _, all_levels in scans:
            all_levels()

    if not has_init:
        @pl.when(t == pl.num_programs(1) - 1)
        def _():
            for b in range(n_seq):
                for hd in range(HG_HEADS):
                    sf_out_ref[b, hd] = stf_ref[b, hd].T
                    sb_out_ref[b, hd] = stb_ref[b, hd].T


def _hgrn(qs, vh, f_fwd, f_bwd, span_fwd, span_bwd, init_states):
    n_seqs, length, _ = qs.shape
    step = min(HG_STEP, length)
    nt = length // step
    groups = n_seqs // HG_SEQS
    has_init = init_states is not None
    per_step = lambda span: jnp.max(span[:, :, 0].reshape(n_seqs, nt, -1), axis=-1) > HG_FAST_SPAN_LOG2
    slow = jnp.logical_or(per_step(span_fwd), per_step(span_bwd)[:, ::-1])
    slow = jnp.any(slow.reshape(groups, HG_SEQS, nt), axis=1).astype(jnp.int32).reshape(-1)
    fwd = pl.BlockSpec((HG_SEQS, step, HG_WIDTH), lambda g, t, _: (g, t, 0))
    bwd = pl.BlockSpec((HG_SEQS, step, HG_WIDTH), lambda g, t, _: (g, nt - 1 - t, 0))
    state = pl.BlockSpec((HG_SEQS, HG_HEADS, HG_HEAD_DIM, HG_HEAD_DIM), lambda g, t, _: (g, 0, 0, 0))
    in_specs = [fwd, fwd, fwd, bwd, bwd, bwd]
    args = [qs, vh, f_fwd, qs, vh, f_bwd]
    out_specs = [fwd, bwd]
    out_shape = [jax.ShapeDtypeStruct(qs.shape, f32)] * 2
    if has_init:
        in_specs += [state, state]
        args += list(init_states)
    else:
        out_specs += [state, state]
        out_shape += [jax.ShapeDtypeStruct((n_seqs, HG_HEADS, HG_HEAD_DIM, HG_HEAD_DIM), f32)] * 2
    return pl.pallas_call(
        functools.partial(_hgrn_kernel, has_init=has_init),
        grid_spec=pltpu.PrefetchScalarGridSpec(
            num_scalar_prefetch=1,
            grid=(groups, nt),
            in_specs=in_specs,
            out_specs=out_specs,
            scratch_shapes=[pltpu.VMEM((HG_SEQS, HG_HEADS, HG_HEAD_DIM, HG_HEAD_DIM), f32)] * 2),
        out_shape=out_shape,
        compiler_params=pltpu.CompilerParams(
            dimension_semantics=("arbitrary", "arbitrary"), vmem_limit_bytes=VMEM_LIMIT_BYTES),
        name="hgrn2",
    )(slow, *args)


def _attn_kernel(*refs, n_ctx, n_cast):
    kk_ref, vta_ref, bound_ref, fixed_ref = refs[-4:]
    o_ref, *cast_out = refs[len(refs) - 5 - n_cast:-4]
    inputs = refs[:len(refs) - 5 - n_cast]
    cast_in = inputs[len(inputs) - n_cast:]
    if n_ctx:
        (qa_ref, ka_ref, vt_ref, ckt_ref, cvt_ref, x_ref, of_ref, ob_ref, sg_ref, m_ref, wo_ref, hg_ref, qg_ref,
         g_ref, b_ref) = inputs[:len(inputs) - n_cast]
    else:
        (qa_ref, ka_ref, vt_ref, x_ref, of_ref, ob_ref, sg_ref, m_ref, wo_ref, hg_ref, qg_ref,
         g_ref, b_ref) = inputs[:len(inputs) - n_cast]
    for src_ref, dst_ref in zip(cast_in, cast_out):
        dst_ref[...] = src_ref[...].astype(bf16)
    tq = qa_ref.shape[1]
    n_keys = kk_ref.shape[0]

    @pl.when(pl.program_id(1) == 0)
    def _():
        def fill_keys(lo, k):
            n = k.shape[0]
            low = lax.broadcasted_iota(jnp.int32, k.shape, 1) < HEAD_DIM
            k_sw = pltpu.roll(k, HEAD_DIM, 1)
            kk_ref[lo:lo + n, 0:LANES] = jnp.where(low, k, k_sw).astype(bf16)
            kk_ref[lo:lo + n, LANES:2 * LANES] = jnp.where(low, k_sw, k).astype(bf16)

        def fill_values(lo, vt):
            n = vt.shape[1]
            ones_row = jnp.where(lax.broadcasted_iota(jnp.int32, (V_ROWS - HEAD_DIM, n), 0) == 0, 1.0, 0.0)
            for kv in range(N_KV_HEADS):
                vta_ref[kv * V_ROWS:kv * V_ROWS + HEAD_DIM, lo:lo + n] = (
                    vt[kv * HEAD_DIM:(kv + 1) * HEAD_DIM].astype(bf16))
                vta_ref[kv * V_ROWS + HEAD_DIM:(kv + 1) * V_ROWS, lo:lo + n] = ones_row.astype(bf16)

        if n_ctx:
            fill_keys(0, ckt_ref[0].T)
            fill_values(0, cvt_ref[0])
        fill_keys(n_ctx, ka_ref[0])
        fill_values(n_ctx, vt_ref[0])

        gain_max = jnp.max(jnp.abs(qg_ref[...]), axis=1, keepdims=True)
        all_small = None
        for kv in range(N_KV_HEADS):
            kt = kk_ref[:, kv * LANES:(kv + 1) * LANES].astype(f32)
            k_norm2 = jnp.max(0.5 * jnp.sum(kt * kt, axis=1, keepdims=True), axis=0, keepdims=True)
            bound = (LOG2_E * SCORE_BOUND_SLACK) * gain_max * jnp.sqrt(k_norm2)
            bound_ref[kv] = bound[0, 0]
            small = jnp.where(bound <= SCORE_BOUND_LIMIT, 1, 0)
            all_small = small if all_small is None else all_small * small
        fixed_ref[0] = all_small[0, 0]

    ts = min(Q_SUB, tq)
    low = lax.broadcasted_iota(jnp.int32, (ts, LANES), 1) < HEAD_DIM
    kc = min(KEY_CHUNK, n_keys)
    pairs_per_kv = N_HEADS // N_KV_HEADS // 2

    def attend(rows, fixed_shift):
        def masked_pair(tile):
            qp = qa_ref[0, rows, tile * LANES:(tile + 1) * LANES]
            zero = jnp.zeros_like(qp)
            return jnp.concatenate([jnp.where(low, qp, zero), jnp.where(low, zero, qp)], axis=0)

        q_pairs = [masked_pair(tile) for tile in range(N_HEADS // 2)]
        work = [(tile, lo) for tile in range(N_HEADS // 2) for lo in range(0, n_keys, kc)]

        def scores(tile, lo):
            kv = tile // pairs_per_kv
            return _dot_nt(kk_ref[lo:lo + kc, kv * LANES:(kv + 1) * LANES], q_pairs[tile])

        heads_t = []
        st_next = scores(*work[0])
        m = acc = None
        for i, (tile, lo) in enumerate(work):
            st = st_next
            if i + 1 < len(work):
                st_next = scores(*work[i + 1])
            kv = tile // pairs_per_kv
            values_t = vta_ref[kv * V_ROWS:(kv + 1) * V_ROWS, lo:lo + kc]
            if fixed_shift:
                e = jnp.exp2(st - bound_ref[kv]).astype(bf16)
                pv = jnp.dot(values_t, e, preferred_element_type=f32)
                acc = pv if acc is None else acc + pv
            else:
                m_chunk = jnp.max(st, axis=0, keepdims=True)
                m_new = m_chunk if m is None else jnp.maximum(m, m_chunk)
                e = jnp.exp2(st - m_new).astype(bf16)
                pv = jnp.dot(values_t, e, preferred_element_type=f32)
                acc = pv if acc is None else acc * jnp.exp2(m - m_new) + pv
                m = m_new
            if lo + kc == n_keys:
                on = (acc[0:HEAD_DIM] * (1.0 / acc[HEAD_DIM:HEAD_DIM + 1])).astype(bf16)
                heads_t += [on[:, :ts], on[:, ts:]]
                m = acc = None
        return jnp.concatenate(heads_t, axis=0)

    def project(rows, o_att_t):
        o_sum = of_ref[0, rows] + ob_ref[0, rows]
        normed = []
        for hd in range(HG_HEADS):
            oh = o_sum[:, hd * HG_HEAD_DIM:(hd + 1) * HG_HEAD_DIM]
            ms = jnp.mean(oh * oh, axis=-1, keepdims=True)
            normed.append(oh * lax.rsqrt(ms + RMS_EPS))
        o_hg = (jnp.concatenate(normed, axis=-1) * hg_ref[...] * sg_ref[0, rows]).astype(bf16)
        y = (jnp.dot(o_hg, wo_ref[0:HG_WIDTH, :], preferred_element_type=f32)
             + _dot_tn(o_att_t, wo_ref[HG_WIDTH:HG_WIDTH + ATT_WIDTH, :]))
        r = ALPHA * x_ref[0, rows] + m_ref[0, 5:6, :] * y
        o_ref[0, rows] = _layer_norm(r, g_ref[1:2, :], b_ref[1:2, :])

    def run(fixed_shift):
        for lo in range(0, tq, ts):
            rows = slice(lo, lo + ts)
            project(rows, attend(rows, fixed_shift))

    use_bound = fixed_ref[0] != 0
    pl.when(use_bound)(functools.partial(run, True))
    pl.when(jnp.logical_not(use_bound))(functools.partial(run, False))


def _attn(qa, ka, vt, ctx_kv_t, x, o_f, o_b, sg, mod, mod_group, w_out, hg_gain, q_gain, ln_g, ln_b, to_cast=()):
    groups, length, _ = qa.shape
    tq = min(Q_TILE, length)
    nt = length // tq
    n_ctx = 0 if ctx_kv_t is None else ctx_kv_t[0].shape[2]
    n_keys = n_ctx + length
    tok = lambda width: pl.BlockSpec((1, tq, width), lambda g, t: (g, t, 0))
    whole_t = lambda n: pl.BlockSpec((1, KV_WIDTH, n), lambda g, t: (g, 0, 0))
    in_specs = [tok(ATT_WIDTH), pl.BlockSpec((1, length, KV_WIDTH), lambda g, t: (g, 0, 0)), whole_t(length)]
    args = [qa, ka, vt]
    if n_ctx:
        in_specs += [whole_t(n_ctx), whole_t(n_ctx)]
        args += list(ctx_kv_t)
    in_specs += [
        tok(D_MODEL), tok(HG_WIDTH), tok(HG_WIDTH), tok(HG_WIDTH),
        pl.BlockSpec((1, N_MOD, D_MODEL), lambda g, t: (mod_group(g), 0, 0)),
        _resident((D_MODEL, D_MODEL)),
        _resident((1, HG_WIDTH)),
        _resident((1, ATT_WIDTH)),
        _resident((3, D_MODEL)),
        _resident((3, D_MODEL)),
    ]
    args += [x, o_f, o_b, sg, mod, w_out, hg_gain, q_gain, ln_g, ln_b]
    cast_specs, cast_shapes = _cast_specs(to_cast, nt, groups * nt)
    in_specs += cast_specs
    args += list(to_cast)
    out_specs = [tok(D_MODEL)] + cast_specs
    out_shape = [jax.ShapeDtypeStruct(x.shape, f32)] + cast_shapes
    return pl.pallas_call(
        functools.partial(_attn_kernel, n_ctx=n_ctx, n_cast=len(to_cast)),
        grid=(groups, nt),
        in_specs=in_specs,
        out_specs=out_specs,
        out_shape=out_shape,
        scratch_shapes=[pltpu.VMEM((n_keys, N_KV_HEADS * LANES), bf16),
                        pltpu.VMEM((N_KV_HEADS * V_ROWS, n_keys), bf16),
                        pltpu.SMEM((N_KV_HEADS,), f32),
                        pltpu.SMEM((1,), jnp.int32)],
        compiler_params=pltpu.CompilerParams(
            dimension_semantics=("arbitrary", "arbitrary"), vmem_limit_bytes=VMEM_LIMIT_BYTES),
        name="attn_out",
    )(*args)


def _rope_tables(n_tokens):
    half = HEAD_DIM // 2
    t = jnp.arange(n_tokens)
    inv = ROPE_THETA ** (-jnp.arange(0, half, 2, dtype=f32) / half)
    ang_row = (t // GRID_W).astype(f32)[:, None] * inv
    ang_col = (t % GRID_W).astype(f32)[:, None] * inv
    cos = jnp.concatenate([jnp.cos(ang_row)] * 2 + [jnp.cos(ang_col)] * 2, axis=-1)
    sin = jnp.concatenate([-jnp.sin(ang_row), jnp.sin(ang_row), -jnp.sin(ang_col), jnp.sin(ang_col)], axis=-1)
    return jnp.tile(cos, (1, LANES // HEAD_DIM)), jnp.tile(sin, (1, LANES // HEAD_DIM))


def kernel(x_prompt, x_sample, cache_k, cache_v, state_hgrn_fwd, state_hgrn_bwd, c, c_ctx, w_mod, b_mod,
           w_ffn1_in, w_ffn1_out, w_ffn2_in, w_ffn2_out, w_in, w_out, q_norm_g, k_norm_g, hg_norm_g,
           lb_logits_fwd, lb_logits_bwd, ln_g, ln_b):
    assert w_mod.shape[0] == DEPTH and lb_logits_fwd.shape[0] == DEPTH + 1
    batch, seq, _ = x_prompt.shape
    dec_batch, dec_seq, _ = x_sample.shape
    past = cache_k.shape[2]

    ctx_row = dec_batch
    rows = -(-(dec_batch + 1) // BF16_SUBLANES) * BF16_SUBLANES
    cvecs = jnp.concatenate([c, c_ctx[None, :], jnp.zeros((rows - dec_batch - 1, D_MODEL), f32)], axis=0)
    mod = _modulation(cvecs, w_mod[0], b_mod[0]).reshape(rows, N_MOD, D_MODEL)

    to_bf16 = lambda w: w.astype(bf16)
    w1u, w1d = to_bf16(w_ffn1_in[0]), to_bf16(w_ffn1_out[0])
    gains = ln_g[0], ln_b[0]
    q_gain = jnp.tile(q_norm_g[0], N_HEADS).reshape(1, ATT_WIDTH)
    k_gain = jnp.tile(k_norm_g[0], N_KV_HEADS).reshape(1, KV_WIDTH)
    hg_gain = hg_norm_g[0].reshape(1, HG_WIDTH)
    head_of = jnp.arange(ATT_WIDTH) // HEAD_DIM
    ones_bd = (head_of[:, None] == head_of[None, :]).astype(bf16)

    def mixer_sublayer(x, mod_group, rope_tables, ctx_kv_t, init_states, hg_groups, to_cast=()):
        shape = x.shape
        seq_len = shape[0] * shape[1] // hg_groups
        per_seq = lambda a: a.reshape(hg_groups, seq_len, a.shape[-1])
        proj = _proj(x, mod, mod_group, w_in_b, lb_logits_fwd, lb_logits_bwd, q_gain, k_gain, ones_bd, rope_tables,
                     seq_len)
        qs, vh, f_f, f_b, sg, qa, ka = map(per_seq, proj[:7])
        vt, span_f, span_b = proj[7], proj[8].reshape(hg_groups, -1, LANES), proj[9].reshape(hg_groups, -1, LANES)
        scans = _hgrn(qs, vh, f_f, f_b, span_f, span_b, init_states)
        x, *cast = _attn(qa, ka, vt, ctx_kv_t, per_seq(x), scans[0], scans[1], sg, mod, mod_group, w_out_b, hg_gain,
                         q_gain, *gains, to_cast=to_cast)
        return x.reshape(shape), proj[10:], vt, scans[2:], cast

    from_cache = lambda t: t[:, 0].transpose(0, 2, 3, 1).reshape(dec_batch, KV_WIDTH, past)
    init_states = (state_hgrn_fwd[:, 0], state_hgrn_bwd[:, 0])
    latent_group = lambda g: g
    x_latent, w_in_b, w_out_b = _ffn(x_sample, mod, latent_group, w1u, w1d, *gains, mod_base=0, ln_row=0,
                                     to_cast=(w_in[0], w_out[0]))
    x_latent, _, _, _, (w2u, w2d) = mixer_sublayer(
        x_latent, latent_group, _rope_tables(dec_seq), (from_cache(cache_k), from_cache(cache_v)), init_states,
        dec_batch, to_cast=(w_ffn2_in[0], w_ffn2_out[0]))

    ctx_group = lambda g: ctx_row
    x_ctx = _ffn(x_prompt.reshape(1, batch * seq, D_MODEL), mod, ctx_group, w1u, w1d, *gains, mod_base=0, ln_row=0)
    x_ctx, (kt_new,), vt_new, states, _ = mixer_sublayer(x_ctx, ctx_group, None, None, None, batch)
    y_sample = _ffn(x_latent, mod, latent_group, w2u, w2d, *gains, mod_base=6, ln_row=2)
    y_prompt = _ffn(x_ctx, mod, ctx_group, w2u, w2d, *gains, mod_base=6, ln_row=2)
    y_prompt = y_prompt.reshape(batch, seq, D_MODEL)
    to_cache = lambda t: t.reshape(batch, DEPTH, N_KV_HEADS, HEAD_DIM, seq).transpose(0, 1, 4, 2, 3)
    new_cache_k, new_cache_v = to_cache(kt_new), to_cache(vt_new)
    new_state_fwd = states[0].reshape(batch, DEPTH, HG_HEADS, HG_HEAD_DIM, HG_HEAD_DIM)
    new_state_bwd = states[1].reshape(batch, DEPTH, HG_HEADS, HG_HEAD_DIM, HG_HEAD_DIM)

    return (y_prompt, y_sample, new_cache_k, new_cache_v, new_state_fwd, new_state_bwd)
```

```python
import functools

import jax
import jax.numpy as jnp
from jax import lax
from jax.experimental import pallas as pl
from jax.experimental.pallas import tpu as pltpu

f32 = jnp.float32
bf16 = jnp.bfloat16

D_MODEL = 1024
N_MOD = 9
HG_WIDTH = 512
HG_HEAD_DIM = 128
HG_HEADS = 4
ATT_WIDTH = 512
HEAD_DIM = 64
N_HEADS = 8
N_KV_HEADS = 2
KV_WIDTH = 128
IN_WIDTH = 5 * HG_WIDTH + ATT_WIDTH + 2 * KV_WIDTH
D_FF = 2816
GRID_W = 64
ROPE_PAIR = HEAD_DIM // 4
ROPE_THETA = 10000.0
DEPTH = 1
ALPHA = (2.0 * DEPTH) ** 0.25
LOG2_E = 1.4426950408889634
LN_EPS = 1e-6
RMS_EPS = 1e-6

LANES = 128
BF16_SUBLANES = 16
VMEM_LIMIT_BYTES = 56 * 1024 * 1024

FF_CHUNK = 256
FFN_TILE = 1024
FFN_SUB = 512
TOKEN_TILE = 512
PROJ_SUB = 256
HG_TILE = 128
HG_STEP = 512
HG_SEQS = 2
Q_TILE = 512
Q_SUB = 256
KEY_CHUNK = 512
SCORE_BOUND_LIMIT = 40.0
SCORE_BOUND_SLACK = 1.01
V_ROWS = 80
HG_FAST_BLOCK = 64
HG_FAST_SPAN_LOG2 = 100.0


def _silu(x):
    return x * jax.nn.sigmoid(x)


def _layer_norm(r, g, b):
    mu = jnp.mean(r, axis=-1, keepdims=True)
    c = r - mu
    var = jnp.mean(c * c, axis=-1, keepdims=True)
    return c * lax.rsqrt(var + LN_EPS) * g + b


def _split_bf16(x, parts):
    out = []
    r = x
    for _ in range(parts - 1):
        p = r.astype(bf16)
        out.append(p)
        r = r - p.astype(f32)
    out.append(r.astype(bf16))
    return out


def _dot01(mat01, x, parts, *, mat_on_left):
    pieces = _split_bf16(x, parts)
    if mat_on_left:
        return jnp.dot(jnp.concatenate([mat01] * parts, axis=1), jnp.concatenate(pieces, axis=0),
                       preferred_element_type=f32)
    return jnp.dot(jnp.concatenate(pieces, axis=1), jnp.concatenate([mat01] * parts, axis=0),
                   preferred_element_type=f32)


def _dot_nt(a, b):
    return lax.dot_general(a, b, (((1,), (1,)), ((), ())), preferred_element_type=f32)


def _dot_tn(a, b):
    return lax.dot_general(a, b, (((0,), (0,)), ((), ())), preferred_element_type=f32)


def _mod_kernel(c_ref, w_ref, b_ref, o_ref):
    a = _silu(c_ref[...]).astype(bf16)
    o_ref[...] = jnp.dot(a, w_ref[...].astype(bf16), preferred_element_type=f32) + b_ref[...]


def _modulation(cvecs, w_mod, b_mod):
    rows = cvecs.shape[0]
    n_out = w_mod.shape[1]
    tn = D_MODEL
    return pl.pallas_call(
        _mod_kernel,
        grid=(n_out // tn,),
        in_specs=[
            pl.BlockSpec((rows, D_MODEL), lambda j: (0, 0)),
            pl.BlockSpec((D_MODEL, tn), lambda j: (0, j)),
            pl.BlockSpec((1, tn), lambda j: (0, j)),
        ],
        out_specs=pl.BlockSpec((rows, tn), lambda j: (0, j)),
        out_shape=jax.ShapeDtypeStruct((rows, n_out), f32),
        compiler_params=pltpu.CompilerParams(dimension_semantics=("arbitrary",)),
        name="modulation",
    )(cvecs, w_mod, b_mod.reshape(1, n_out))


def _ffn_kernel(x_ref, m_ref, wup_ref, wd_ref, g_ref, b_ref, *refs, mod_base, ln_row):
    n_cast = (len(refs) - 2) // 2
    cast_in, o_ref, cast_out, act_ref = refs[:n_cast], refs[n_cast], refs[n_cast + 1:-1], refs[-1]
    for src_ref, dst_ref in zip(cast_in, cast_out):
        dst_ref[...] = src_ref[...].astype(bf16)
    shift = m_ref[0, mod_base:mod_base + 1, :]
    scale = m_ref[0, mod_base + 1:mod_base + 2, :]
    gate = m_ref[0, mod_base + 2:mod_base + 3, :]
    subs = [slice(lo, lo + FFN_SUB) for lo in range(0, x_ref.shape[1], FFN_SUB)]
    xs = [x_ref[0, rows] for rows in subs]
    hs = [(x * (1.0 + scale) + shift).astype(bf16) for x in xs]
    for j in range(D_FF // FF_CHUNK):
        cols = slice(j * FF_CHUNK, (j + 1) * FF_CHUNK)
        for rows, h in zip(subs, hs):
            a = jnp.dot(h, wup_ref[:, cols], preferred_element_type=f32)
            u = jnp.dot(h, wup_ref[:, D_FF + j * FF_CHUNK:D_FF + (j + 1) * FF_CHUNK],
                        preferred_element_type=f32)
            act_ref[rows, cols] = (_silu(a) * u).astype(bf16)
    ys = [jnp.dot(act_ref[rows, :], wd_ref[...], preferred_element_type=f32) for rows in subs]
    for rows, x, y in zip(subs, xs, ys):
        r = ALPHA * x + 0.5 * gate * y
        o_ref[0, rows] = _layer_norm(r, g_ref[ln_row:ln_row + 1, :], b_ref[ln_row:ln_row + 1, :])


def _resident(shape):
    return pl.BlockSpec(shape, lambda *_: (0,) * len(shape), pipeline_mode=pl.Buffered(1))


def _cast_specs(to_cast, nt, steps):
    specs, shapes = [], []
    for w in to_cast:
        rows = next(r for r in range(BF16_SUBLANES, w.shape[0] + 1, BF16_SUBLANES)
                    if w.shape[0] % r == 0 and w.shape[0] // r <= steps)
        specs.append(pl.BlockSpec((rows, w.shape[1]),
                                  lambda g, t, last=w.shape[0] // rows - 1: (jnp.minimum(g * nt + t, last), 0)))
        shapes.append(jax.ShapeDtypeStruct(w.shape, bf16))
    return specs, shapes


def _ffn(x, mod, mod_group, w_up, wd, ln_g, ln_b, *, mod_base, ln_row, to_cast=()):
    groups, length, _ = x.shape
    tm = FFN_TILE
    nt = length // tm
    cast_specs, cast_shapes = _cast_specs(to_cast, nt, groups * nt)
    out = pl.pallas_call(
        functools.partial(_ffn_kernel, mod_base=mod_base, ln_row=ln_row),
        grid=(groups, nt),
        in_specs=[
            pl.BlockSpec((1, tm, D_MODEL), lambda g, t: (g, t, 0)),
            pl.BlockSpec((1, N_MOD, D_MODEL), lambda g, t: (mod_group(g), 0, 0)),
            _resident((D_MODEL, 2 * D_FF)),
            _resident((D_FF, D_MODEL)),
            _resident((3, D_MODEL)),
            _resident((3, D_MODEL)),
        ] + cast_specs,
        out_specs=[pl.BlockSpec((1, tm, D_MODEL), lambda g, t: (g, t, 0))] + cast_specs,
        out_shape=[jax.ShapeDtypeStruct(x.shape, f32)] + cast_shapes,
        scratch_shapes=[pltpu.VMEM((tm, D_FF), bf16)],
        compiler_params=pltpu.CompilerParams(
            dimension_semantics=("arbitrary", "arbitrary"), vmem_limit_bytes=VMEM_LIMIT_BYTES),
        name="ffn",
    )(x, mod, w_up, wd, ln_g, ln_b, *to_cast)
    return out if to_cast else out[0]


def _head_rms_norm(x, ones_bd, gain, parts):
    ss = _dot01(ones_bd, x * x, parts, mat_on_left=False)
    return x * lax.rsqrt(ss * (1.0 / HEAD_DIM) + RMS_EPS) * gain


def _rope(x, cos, sin_signed):
    width = x.shape[-1]
    lane = lax.broadcasted_iota(jnp.int32, x.shape, 1)
    from_right = pltpu.roll(x, width - ROPE_PAIR, 1)
    from_left = pltpu.roll(x, ROPE_PAIR, 1)
    partner = jnp.where((lane & (2 * ROPE_PAIR - 1)) < ROPE_PAIR, from_right, from_left)
    return x * cos + partner * sin_signed


def _proj_kernel(*refs, rope):
    if rope:
        (x_ref, m_ref, w_ref, lbf_ref, lbb_ref, qg_ref, kg_ref, bd_ref, cos_ref, sin_ref,
         qs_ref, vh_ref, ff_ref, fb_ref, sg_ref, qa_ref, ka_ref, vt_ref,
         spf_ref, spb_ref) = refs
    else:
        (x_ref, m_ref, w_ref, lbf_ref, lbb_ref, qg_ref, kg_ref, bd_ref,
         qs_ref, vh_ref, ff_ref, fb_ref, sg_ref, qa_ref, ka_ref, vt_ref,
         spf_ref, spb_ref, kt_ref) = refs
    shift = m_ref[0, 3:4, :]
    scale = m_ref[0, 4:5, :]
    subs = [slice(lo, lo + PROJ_SUB) for lo in range(0, x_ref.shape[1], PROJ_SUB)]
    hs = [(x_ref[0, rows] * (1.0 + scale) + shift).astype(bf16) for rows in subs]

    def cols(h, lo, width):
        return jnp.dot(h, w_ref[:, lo:lo + width], preferred_element_type=f32)

    def lower_bound(lb_ref):
        l0 = lb_ref[0:1, :]
        l1 = lb_ref[1:2, :]
        m = jnp.maximum(l0, l1)
        e0 = jnp.exp(l0 - m)
        e1 = jnp.exp(l1 - m)
        return e0 / (e0 + e1)

    half = HG_FAST_BLOCK // 2

    def forget_gate(lb_ref, lo, f_ref, span_ref):
        lb = lower_bound(lb_ref)
        for rows, h in zip(subs, hs):
            f = lb + (1.0 - lb) * jax.nn.sigmoid(cols(h, lo, HG_WIDTH))
            f_ref[0, rows] = f
            sums = jnp.sum(jnp.log2(f).reshape(PROJ_SUB // half, half, HG_WIDTH), axis=1)
            span_ref[0, rows.start // half:rows.stop // half] = jnp.broadcast_to(
                jnp.max(jnp.abs(sums), axis=-1, keepdims=True), (PROJ_SUB // half, LANES))

    base = 5 * HG_WIDTH
    piece = vt_ref.shape[2]
    for rows, h in zip(subs, hs):
        q = _head_rms_norm(cols(h, base, ATT_WIDTH), bd_ref[...], qg_ref[...], 1)
        k = _head_rms_norm(cols(h, base + ATT_WIDTH, KV_WIDTH), bd_ref[0:KV_WIDTH, 0:KV_WIDTH], kg_ref[...], 2)
        if rope:
            cos, sin = cos_ref[rows, :], sin_ref[rows, :]
            q = _rope(q, jnp.concatenate([cos] * (ATT_WIDTH // LANES), axis=1),
                      jnp.concatenate([sin] * (ATT_WIDTH // LANES), axis=1))
            k = _rope(k, cos, sin)
        qa_ref[0, rows] = (q * (HEAD_DIM ** -0.5 * LOG2_E)).astype(bf16)
        ka_ref[0, rows] = k
        v = cols(h, base + ATT_WIDTH + KV_WIDTH, KV_WIDTH)
        for lo in range(rows.start, rows.stop, min(piece, PROJ_SUB)):
            n = min(piece, PROJ_SUB)
            dst = (lo // piece, slice(None), slice(lo % piece, lo % piece + n))
            vt_ref[dst] = v[lo - rows.start:lo - rows.start + n].T
            if not rope:
                kt_ref[dst] = k[lo - rows.start:lo - rows.start + n].T

    for rows, h in zip(subs, hs):
        qs_ref[0, rows] = _silu(cols(h, 0, HG_WIDTH))
    for rows, h in zip(subs, hs):
        vh_ref[0, rows] = cols(h, HG_WIDTH, HG_WIDTH).astype(bf16)
    forget_gate(lbf_ref, 2 * HG_WIDTH, ff_ref, spf_ref)
    forget_gate(lbb_ref, 3 * HG_WIDTH, fb_ref, spb_ref)
    for rows, h in zip(subs, hs):
        sg_ref[0, rows] = _silu(cols(h, 4 * HG_WIDTH, HG_WIDTH))


def _proj(x, mod, mod_group, w_in, lb_f, lb_b, q_gain, k_gain, ones_bd, rope_tables, seq_len):
    groups, length, _ = x.shape
    tm = TOKEN_TILE
    rope = rope_tables is not None
    n_seq = groups * length // seq_len
    if seq_len >= tm:
        tiles_per_seq = seq_len // tm
        t_spec = pl.BlockSpec((1, KV_WIDTH, tm), lambda g, t: (g * (length // seq_len) + t // tiles_per_seq, 0,
                                                               t % tiles_per_seq))
    else:
        t_spec = pl.BlockSpec((tm // seq_len, KV_WIDTH, seq_len), lambda g, t: (g * (length // tm) + t, 0, 0))
    t_shape = jax.ShapeDtypeStruct((n_seq, KV_WIDTH, seq_len), f32)
    tok = lambda width: pl.BlockSpec((1, tm, width), lambda g, t: (g, t, 0))
    in_specs = [
        tok(D_MODEL),
        pl.BlockSpec((1, N_MOD, D_MODEL), lambda g, t: (mod_group(g), 0, 0)),
        _resident((D_MODEL, IN_WIDTH)),
        _resident((2, HG_WIDTH)),
        _resident((2, HG_WIDTH)),
        _resident((1, ATT_WIDTH)),
        _resident((1, KV_WIDTH)),
        _resident((ATT_WIDTH, ATT_WIDTH)),
    ]
    args = [x, mod, w_in, lb_f, lb_b, q_gain, k_gain, ones_bd]
    if rope:
        in_specs += [pl.BlockSpec((tm, LANES), lambda g, t: (t, 0))] * 2
        args += list(rope_tables)
    shape = lambda width, dt: jax.ShapeDtypeStruct((groups, length, width), dt)
    half = HG_FAST_BLOCK // 2
    span_spec = pl.BlockSpec((1, tm // half, LANES), lambda g, t: (g, t, 0))
    span_shape = jax.ShapeDtypeStruct((groups, length // half, LANES), f32)
    return pl.pallas_call(
        functools.partial(_proj_kernel, rope=rope),
        grid=(groups, length // tm),
        in_specs=in_specs,
        out_specs=([tok(HG_WIDTH)] * 5 + [tok(ATT_WIDTH), tok(KV_WIDTH), t_spec, span_spec, span_spec]
                   + ([] if rope else [t_spec])),
        out_shape=([shape(HG_WIDTH, f32), shape(HG_WIDTH, bf16)] + [shape(HG_WIDTH, f32)] * 3
                   + [shape(ATT_WIDTH, bf16), shape(KV_WIDTH, f32), t_shape, span_shape, span_shape]
                   + ([] if rope else [t_shape])),
        compiler_params=pltpu.CompilerParams(
            dimension_semantics=("arbitrary", "arbitrary"), vmem_limit_bytes=VMEM_LIMIT_BYTES),
        name="mixer_proj",
    )(*args)


def _hgrn_direction(qs, vb, f, st_ref, o_ref, rows, *, rev):
    tc = qs.shape[0]
    lf = jnp.log2(f)
    kk = 1.0 - f
    row = lax.broadcasted_iota(jnp.int32, (tc, tc), 0)
    col = lax.broadcasted_iota(jnp.int32, (tc, tc), 1)
    ordered = (row < col) if rev else (row > col)
    differ = row ^ col
    tri = jnp.where((col >= row) if rev else (col <= row), 1.0, 0.0).astype(bf16)
    cum = _dot01(tri, lf, 2, mat_on_left=True)
    total = cum[0:1] if rev else cum[tc - 1:tc]
    carry = jnp.exp2(total)
    query_half = 0 if rev else 1
    heads = [slice(hd * HG_HEAD_DIM, (hd + 1) * HG_HEAD_DIM) for hd in range(HG_HEADS)]

    def level_exponent(s, ridx):
        if s == 1:
            return jnp.where((ridx & 1) == query_half, lf, 0.0)
        if s == 2:
            nxt = pltpu.roll(lf, tc - 1, 0)
            prv = pltpu.roll(lf, 1, 0)
            m4 = ridx & 3
            if rev:
                return jnp.where(m4 == 0, lf + nxt, jnp.where(m4 == 1, lf, jnp.where(m4 == 2, 0.0, prv)))
            return jnp.where(m4 == 0, nxt, jnp.where(m4 == 1, 0.0, jnp.where(m4 == 2, lf, lf + prv)))
        blocks = []
        for lo in range(0, tc, 2 * s):
            anchor = lo + (s if rev else s - 1)
            blocks.append(cum[lo:lo + 2 * s] - cum[anchor:anchor + 1])
        d = jnp.concatenate(blocks, axis=0)
        is_query = ((ridx >> (s.bit_length() - 1)) & 1) == query_half
        return jnp.where(is_query, d, -d)

    def top_level():
        s = tc // 2
        lower, upper = slice(0, s), slice(s, tc)
        anchor = cum[s:s + 1] if rev else cum[s - 1:s]
        q_rows, k_rows = (lower, upper) if rev else (upper, lower)
        q_half = (qs[q_rows] * jnp.exp2(cum[q_rows] - anchor)).astype(bf16)
        k_half = (kk[k_rows] * jnp.exp2(anchor - cum[k_rows])).astype(bf16)
        zeros = jnp.zeros_like(q_half)
        q_only = jnp.concatenate([q_half, zeros] if rev else [zeros, q_half], axis=0)
        k_only = jnp.concatenate([zeros, k_half] if rev else [k_half, zeros], axis=0)
        return [_dot_nt(q_only[:, hs], k_only[:, hs]) for hs in heads]

    def add_levels(intra, first):
        ridx = lax.broadcasted_iota(jnp.int32, qs.shape, 0)
        s = first
        while 2 * s < tc:
            shift = s.bit_length() - 1
            is_query = ((ridx >> shift) & 1) == query_half
            mixed = (jnp.where(is_query, qs, kk) * jnp.exp2(level_exponent(s, ridx))).astype(bf16)
            pair = jnp.logical_and((differ >> shift) == 1, ordered)
            for hd, hs in enumerate(heads):
                intra[hd] = jnp.where(pair, _dot_nt(mixed[:, hs], mixed[:, hs]), intra[hd])
            s *= 2
        return [a + top for a, top in zip(intra, top_level())]

    def finish(intra, q_dec, k_end, extra):
        for hd, hs in enumerate(heads):
            st = st_ref[hd]
            o = jnp.dot(jnp.concatenate([intra[hd].astype(bf16), q_dec[:, hs]], axis=1),
                        jnp.concatenate([vb[:, hs], st.T.astype(bf16)], axis=0), preferred_element_type=f32)
            o_ref[rows, hs] = o if extra is None else o + extra[:, hs]
            st_ref[hd] = st * carry[:, hs] + _dot_tn(vb[:, hs], k_end[:, hs])

    blk = HG_FAST_BLOCK
    anchors = [lo + (blk // 2 if rev else blk // 2 - 1) for lo in range(0, tc, blk)]

    def mid_split_blocks():
        q_mid, k_mid, q_dec, k_end = [], [], [], []
        for lo, a in zip(range(0, tc, blk), anchors):
            rel = cum[lo:lo + blk] - cum[a:a + 1]
            q_mid.append(qs[lo:lo + blk] * jnp.exp2(rel))
            k_mid.append(kk[lo:lo + blk] * jnp.exp2(-rel))
            q_dec.append(q_mid[-1] * jnp.exp2(cum[a:a + 1]))
            k_end.append(k_mid[-1] * jnp.exp2(total - cum[a:a + 1]))
        q_mid, k_mid, q_dec, k_end = (jnp.concatenate(p, axis=0).astype(bf16) for p in (q_mid, k_mid, q_dec, k_end))
        same_block = (differ >> (blk.bit_length() - 1)) == 0
        keep = jnp.logical_and(same_block, (row <= col) if rev else (row >= col))
        in_block = [_dot_nt(q_mid[:, hs], k_mid[:, hs]) for hs in heads]
        if 2 * blk == tc:
            intra = [jnp.where(keep, a, top) for a, top in zip(in_block, top_level())]
        else:
            intra = add_levels([jnp.where(keep, a, 0.0) for a in in_block], blk)
        finish(intra, q_dec, k_end, None)

    def all_levels():
        intra = add_levels([jnp.zeros((tc, tc), f32)] * HG_HEADS, 1)
        q_dec = (qs * jnp.exp2(cum)).astype(bf16)
        k_end = (kk * jnp.exp2(total - cum)).astype(bf16)
        qk = qs * kk
        own = jnp.concatenate(
            [jnp.sum(qk[:, hs], axis=-1, keepdims=True) * vb[:, hs].astype(f32) for hs in heads], axis=-1)
        finish(intra, q_dec, k_end, own)

    return mid_split_blocks, all_levels


def _hgrn_kernel(slow_ref, *refs, has_init):
    if has_init:
        (qsf_ref, vf_ref, ff_ref, qsb_ref, vb_ref, fb_ref, s0f_ref, s0b_ref,
         of_ref, ob_ref, stf_ref, stb_ref) = refs
    else:
        (qsf_ref, vf_ref, ff_ref, qsb_ref, vb_ref, fb_ref,
         of_ref, ob_ref, sf_out_ref, sb_out_ref, stf_ref, stb_ref) = refs
    t = pl.program_id(1)
    n_seq = qsf_ref.shape[0]

    @pl.when(t == 0)
    def _():
        for b in range(n_seq):
            for hd in range(HG_HEADS):
                if has_init:
                    stf_ref[b, hd] = s0f_ref[b, hd].T
                    stb_ref[b, hd] = s0b_ref[b, hd].T
                else:
                    stf_ref[b, hd] = jnp.zeros((HG_HEAD_DIM, HG_HEAD_DIM), f32)
                    stb_ref[b, hd] = jnp.zeros((HG_HEAD_DIM, HG_HEAD_DIM), f32)

    chunks = [slice(lo, lo + HG_TILE) for lo in range(0, qsf_ref.shape[1], HG_TILE)]
    scans = []
    for r_f, r_b in zip(chunks, reversed(chunks)):
        for b in range(n_seq):
            scans.append(_hgrn_direction(qsf_ref[b, r_f], vf_ref[b, r_f], ff_ref[b, r_f], stf_ref.at[b], of_ref.at[b],
                                         r_f, rev=False))
            scans.append(_hgrn_direction(qsb_ref[b, r_b], vb_ref[b, r_b], fb_ref[b, r_b], stb_ref.at[b], ob_ref.at[b],
                                         r_b, rev=True))
    slow = slow_ref[pl.program_id(0) * pl.num_programs(1) + t] != 0

    @pl.when(jnp.logical_not(slow))
    def _():
        for mid_split_blocks, _ in scans:
            mid_split_blocks()

    @pl.when(slow)
    def _():
        for _, all_levels in scans:
            all_levels()

    if not has_init:
        @pl.when(t == pl.num_programs(1) - 1)
        def _():
            for b in range(n_seq):
                for hd in range(HG_HEADS):
                    sf_out_ref[b, hd] = stf_ref[b, hd].T
                    sb_out_ref[b, hd] = stb_ref[b, hd].T


def _hgrn(qs, vh, f_fwd, f_bwd, span_fwd, span_bwd, init_states):
    n_seqs, length, _ = qs.shape
    step = min(HG_STEP, length)
    nt = length // step
    groups = n_seqs // HG_SEQS
    has_init = init_states is not None
    per_step = lambda span: jnp.max(span[:, :, 0].reshape(n_seqs, nt, -1), axis=-1) > HG_FAST_SPAN_LOG2
    slow = jnp.logical_or(per_step(span_fwd), per_step(span_bwd)[:, ::-1])
    slow = jnp.any(slow.reshape(groups, HG_SEQS, nt), axis=1).astype(jnp.int32).reshape(-1)
    fwd = pl.BlockSpec((HG_SEQS, step, HG_WIDTH), lambda g, t, _: (g, t, 0))
    bwd = pl.BlockSpec((HG_SEQS, step, HG_WIDTH), lambda g, t, _: (g, nt - 1 - t, 0))
    state = pl.BlockSpec((HG_SEQS, HG_HEADS, HG_HEAD_DIM, HG_HEAD_DIM), lambda g, t, _: (g, 0, 0, 0))
    in_specs = [fwd, fwd, fwd, bwd, bwd, bwd]
    args = [qs, vh, f_fwd, qs, vh, f_bwd]
    out_specs = [fwd, bwd]
    out_shape = [jax.ShapeDtypeStruct(qs.shape, f32)] * 2
    if has_init:
        in_specs += [state, state]
        args += list(init_states)
    else:
        out_specs += [state, state]
        out_shape += [jax.ShapeDtypeStruct((n_seqs, HG_HEADS, HG_HEAD_DIM, HG_HEAD_DIM), f32)] * 2
    return pl.pallas_call(
        functools.partial(_hgrn_kernel, has_init=has_init),
        grid_spec=pltpu.PrefetchScalarGridSpec(
            num_scalar_prefetch=1,
            grid=(groups, nt),
            in_specs=in_specs,
            out_specs=out_specs,
            scratch_shapes=[pltpu.VMEM((HG_SEQS, HG_HEADS, HG_HEAD_DIM, HG_HEAD_DIM), f32)] * 2),
        out_shape=out_shape,
        compiler_params=pltpu.CompilerParams(
            dimension_semantics=("arbitrary", "arbitrary"), vmem_limit_bytes=VMEM_LIMIT_BYTES),
        name="hgrn2",
    )(slow, *args)


def _attn_kernel(*refs, n_ctx, n_cast):
    kk_ref, vta_ref, bound_ref, fixed_ref = refs[-4:]
    o_ref, *cast_out = refs[len(refs) - 5 - n_cast:-4]
    inputs = refs[:len(refs) - 5 - n_cast]
    cast_in = inputs[len(inputs) - n_cast:]
    if n_ctx:
        (qa_ref, ka_ref, vt_ref, ckt_ref, cvt_ref, x_ref, of_ref, ob_ref, sg_ref, m_ref, wo_ref, hg_ref, qg_ref,
         g_ref, b_ref) = inputs[:len(inputs) - n_cast]
    else:
        (qa_ref, ka_ref, vt_ref, x_ref, of_ref, ob_ref, sg_ref, m_ref, wo_ref, hg_ref, qg_ref,
         g_ref, b_ref) = inputs[:len(inputs) - n_cast]
    for src_ref, dst_ref in zip(cast_in, cast_out):
        dst_ref[...] = src_ref[...].astype(bf16)
    tq = qa_ref.shape[1]
    n_keys = kk_ref.shape[0]

    @pl.when(pl.program_id(1) == 0)
    def _():
        def fill_keys(lo, k):
            n = k.shape[0]
            low = lax.broadcasted_iota(jnp.int32, k.shape, 1) < HEAD_DIM
            k_sw = pltpu.roll(k, HEAD_DIM, 1)
            kk_ref[lo:lo + n, 0:LANES] = jnp.where(low, k, k_sw).astype(bf16)
            kk_ref[lo:lo + n, LANES:2 * LANES] = jnp.where(low, k_sw, k).astype(bf16)

        def fill_values(lo, vt):
            n = vt.shape[1]
            ones_row = jnp.where(lax.broadcasted_iota(jnp.int32, (V_ROWS - HEAD_DIM, n), 0) == 0, 1.0, 0.0)
            for kv in range(N_KV_HEADS):
                vta_ref[kv * V_ROWS:kv * V_ROWS + HEAD_DIM, lo:lo + n] = (
                    vt[kv * HEAD_DIM:(kv + 1) * HEAD_DIM].astype(bf16))
                vta_ref[kv * V_ROWS + HEAD_DIM:(kv + 1) * V_ROWS, lo:lo + n] = ones_row.astype(bf16)

        if n_ctx:
            fill_keys(0, ckt_ref[0].T)
            fill_values(0, cvt_ref[0])
        fill_keys(n_ctx, ka_ref[0])
        fill_values(n_ctx, vt_ref[0])

        gain_max = jnp.max(jnp.abs(qg_ref[...]), axis=1, keepdims=True)
        all_small = None
        for kv in range(N_KV_HEADS):
            kt = kk_ref[:, kv * LANES:(kv + 1) * LANES].astype(f32)
            k_norm2 = jnp.max(0.5 * jnp.sum(kt * kt, axis=1, keepdims=True), axis=0, keepdims=True)
            bound = (LOG2_E * SCORE_BOUND_SLACK) * gain_max * jnp.sqrt(k_norm2)
            bound_ref[kv] = bound[0, 0]
            small = jnp.where(bound <= SCORE_BOUND_LIMIT, 1, 0)
            all_small = small if all_small is None else all_small * small
        fixed_ref[0] = all_small[0, 0]

    ts = min(Q_SUB, tq)
    low = lax.broadcasted_iota(jnp.int32, (ts, LANES), 1) < HEAD_DIM
    kc = min(KEY_CHUNK, n_keys)
    pairs_per_kv = N_HEADS // N_KV_HEADS // 2

    def attend(rows, fixed_shift):
        def masked_pair(tile):
            qp = qa_ref[0, rows, tile * LANES:(tile + 1) * LANES]
            zero = jnp.zeros_like(qp)
            return jnp.concatenate([jnp.where(low, qp, zero), jnp.where(low, zero, qp)], axis=0)

        q_pairs = [masked_pair(tile) for tile in range(N_HEADS // 2)]
        work = [(tile, lo) for tile in range(N_HEADS // 2) for lo in range(0, n_keys, kc)]

        def scores(tile, lo):
            kv = tile // pairs_per_kv
            return _dot_nt(kk_ref[lo:lo + kc, kv * LANES:(kv + 1) * LANES], q_pairs[tile])

        heads_t = []
        st_next = scores(*work[0])
        m = acc = None
        for i, (tile, lo) in enumerate(work):
            st = st_next
            if i + 1 < len(work):
                st_next = scores(*work[i + 1])
            kv = tile // pairs_per_kv
            values_t = vta_ref[kv * V_ROWS:(kv + 1) * V_ROWS, lo:lo + kc]
            if fixed_shift:
                e = jnp.exp2(st - bound_ref[kv]).astype(bf16)
                pv = jnp.dot(values_t, e, preferred_element_type=f32)
                acc = pv if acc is None else acc + pv
            else:
                m_chunk = jnp.max(st, axis=0, keepdims=True)
                m_new = m_chunk if m is None else jnp.maximum(m, m_chunk)
                e = jnp.exp2(st - m_new).astype(bf16)
                pv = jnp.dot(values_t, e, preferred_element_type=f32)
                acc = pv if acc is None else acc * jnp.exp2(m - m_new) + pv
                m = m_new
            if lo + kc == n_keys:
                on = (acc[0:HEAD_DIM] * (1.0 / acc[HEAD_DIM:HEAD_DIM + 1])).astype(bf16)
                heads_t += [on[:, :ts], on[:, ts:]]
                m = acc = None
        return jnp.concatenate(heads_t, axis=0)

    def project(rows, o_att_t):
        o_sum = of_ref[0, rows] + ob_ref[0, rows]
        normed = []
        for hd in range(HG_HEADS):
            oh = o_sum[:, hd * HG_HEAD_DIM:(hd + 1) * HG_HEAD_DIM]
            ms = jnp.mean(oh * oh, axis=-1, keepdims=True)
            normed.append(oh * lax.rsqrt(ms + RMS_EPS))
        o_hg = (jnp.concatenate(normed, axis=-1) * hg_ref[...] * sg_ref[0, rows]).astype(bf16)
        y = (jnp.dot(o_hg, wo_ref[0:HG_WIDTH, :], preferred_element_type=f32)
             + _dot_tn(o_att_t, wo_ref[HG_WIDTH:HG_WIDTH + ATT_WIDTH, :]))
        r = ALPHA * x_ref[0, rows] + m_ref[0, 5:6, :] * y
        o_ref[0, rows] = _layer_norm(r, g_ref[1:2, :], b_ref[1:2, :])

    def run(fixed_shift):
        for lo in range(0, tq, ts):
            rows = slice(lo, lo + ts)
            project(rows, attend(rows, fixed_shift))

    use_bound = fixed_ref[0] != 0
    pl.when(use_bound)(functools.partial(run, True))
    pl.when(jnp.logical_not(use_bound))(functools.partial(run, False))


def _attn(qa, ka, vt, ctx_kv_t, x, o_f, o_b, sg, mod, mod_group, w_out, hg_gain, q_gain, ln_g, ln_b, to_cast=()):
    groups, length, _ = qa.shape
    tq = min(Q_TILE, length)
    nt = length // tq
    n_ctx = 0 if ctx_kv_t is None else ctx_kv_t[0].shape[2]
    n_keys = n_ctx + length
    tok = lambda width: pl.BlockSpec((1, tq, width), lambda g, t: (g, t, 0))
    whole_t = lambda n: pl.BlockSpec((1, KV_WIDTH, n), lambda g, t: (g, 0, 0))
    in_specs = [tok(ATT_WIDTH), pl.BlockSpec((1, length, KV_WIDTH), lambda g, t: (g, 0, 0)), whole_t(length)]
    args = [qa, ka, vt]
    if n_ctx:
        in_specs += [whole_t(n_ctx), whole_t(n_ctx)]
        args += list(ctx_kv_t)
    in_specs += [
        tok(D_MODEL), tok(HG_WIDTH), tok(HG_WIDTH), tok(HG_WIDTH),
        pl.BlockSpec((1, N_MOD, D_MODEL), lambda g, t: (mod_group(g), 0, 0)),
        _resident((D_MODEL, D_MODEL)),
        _resident((1, HG_WIDTH)),
        _resident((1, ATT_WIDTH)),
        _resident((3, D_MODEL)),
        _resident((3, D_MODEL)),
    ]
    args += [x, o_f, o_b, sg, mod, w_out, hg_gain, q_gain, ln_g, ln_b]
    cast_specs, cast_shapes = _cast_specs(to_cast, nt, groups * nt)
    in_specs += cast_specs
    args += list(to_cast)
    out_specs = [tok(D_MODEL)] + cast_specs
    out_shape = [jax.ShapeDtypeStruct(x.shape, f32)] + cast_shapes
    return pl.pallas_call(
        functools.partial(_attn_kernel, n_ctx=n_ctx, n_cast=len(to_cast)),
        grid=(groups, nt),
        in_specs=in_specs,
        out_specs=out_specs,
        out_shape=out_shape,
        scratch_shapes=[pltpu.VMEM((n_keys, N_KV_HEADS * LANES), bf16),
                        pltpu.VMEM((N_KV_HEADS * V_ROWS, n_keys), bf16),
                        pltpu.SMEM((N_KV_HEADS,), f32),
                        pltpu.SMEM((1,), jnp.int32)],
        compiler_params=pltpu.CompilerParams(
            dimension_semantics=("arbitrary", "arbitrary"), vmem_limit_bytes=VMEM_LIMIT_BYTES),
        name="attn_out",
    )(*args)


def _rope_tables(n_tokens):
    half = HEAD_DIM // 2
    t = jnp.arange(n_tokens)
    inv = ROPE_THETA ** (-jnp.arange(0, half, 2, dtype=f32) / half)
    ang_row = (t // GRID_W).astype(f32)[:, None] * inv
    ang_col = (t % GRID_W).astype(f32)[:, None] * inv
    cos = jnp.concatenate([jnp.cos(ang_row)] * 2 + [jnp.cos(ang_col)] * 2, axis=-1)
    sin = jnp.concatenate([-jnp.sin(ang_row), jnp.sin(ang_row), -jnp.sin(ang_col), jnp.sin(ang_col)], axis=-1)
    return jnp.tile(cos, (1, LANES // HEAD_DIM)), jnp.tile(sin, (1, LANES // HEAD_DIM))


def kernel(x_prompt, x_sample, cache_k, cache_v, state_hgrn_fwd, state_hgrn_bwd, c, c_ctx, w_mod, b_mod,
           w_ffn1_in, w_ffn1_out, w_ffn2_in, w_ffn2_out, w_in, w_out, q_norm_g, k_norm_g, hg_norm_g,
           lb_logits_fwd, lb_logits_bwd, ln_g, ln_b):
    assert w_mod.shape[0] == DEPTH and lb_logits_fwd.shape[0] == DEPTH + 1
    batch, seq, _ = x_prompt.shape
    dec_batch, dec_seq, _ = x_sample.shape
    past = cache_k.shape[2]

    ctx_row = dec_batch
    rows = -(-(dec_batch + 1) // BF16_SUBLANES) * BF16_SUBLANES
    cvecs = jnp.concatenate([c, c_ctx[None, :], jnp.zeros((rows - dec_batch - 1, D_MODEL), f32)], axis=0)
    mod = _modulation(cvecs, w_mod[0], b_mod[0]).reshape(rows, N_MOD, D_MODEL)

    to_bf16 = lambda w: w.astype(bf16)
    w1u, w1d = to_bf16(w_ffn1_in[0]), to_bf16(w_ffn1_out[0])
    gains = ln_g[0], ln_b[0]
    q_gain = jnp.tile(q_norm_g[0], N_HEADS).reshape(1, ATT_WIDTH)
    k_gain = jnp.tile(k_norm_g[0], N_KV_HEADS).reshape(1, KV_WIDTH)
    hg_gain = hg_norm_g[0].reshape(1, HG_WIDTH)
    head_of = jnp.arange(ATT_WIDTH) // HEAD_DIM
    ones_bd = (head_of[:, None] == head_of[None, :]).astype(bf16)

    def mixer_sublayer(x, mod_group, rope_tables, ctx_kv_t, init_states, hg_groups, to_cast=()):
        shape = x.shape
        seq_len = shape[0] * shape[1] // hg_groups
        per_seq = lambda a: a.reshape(hg_groups, seq_len, a.shape[-1])
        proj = _proj(x, mod, mod_group, w_in_b, lb_logits_fwd, lb_logits_bwd, q_gain, k_gain, ones_bd, rope_tables,
                     seq_len)
        qs, vh, f_f, f_b, sg, qa, ka = map(per_seq, proj[:7])
        vt, span_f, span_b = proj[7], proj[8].reshape(hg_groups, -1, LANES), proj[9].reshape(hg_groups, -1, LANES)
        scans = _hgrn(qs, vh, f_f, f_b, span_f, span_b, init_states)
        x, *cast = _attn(qa, ka, vt, ctx_kv_t, per_seq(x), scans[0], scans[1], sg, mod, mod_group, w_out_b, hg_gain,
                         q_gain, *gains, to_cast=to_cast)
        return x.reshape(shape), proj[10:], vt, scans[2:], cast

    from_cache = lambda t: t[:, 0].transpose(0, 2, 3, 1).reshape(dec_batch, KV_WIDTH, past)
    init_states = (state_hgrn_fwd[:, 0], state_hgrn_bwd[:, 0])
    latent_group = lambda g: g
    x_latent, w_in_b, w_out_b = _ffn(x_sample, mod, latent_group, w1u, w1d, *gains, mod_base=0, ln_row=0,
                                     to_cast=(w_in[0], w_out[0]))
    x_latent, _, _, _, (w2u, w2d) = mixer_sublayer(
        x_latent, latent_group, _rope_tables(dec_seq), (from_cache(cache_k), from_cache(cache_v)), init_states,
        dec_batch, to_cast=(w_ffn2_in[0], w_ffn2_out[0]))

    ctx_group = lambda g: ctx_row
    x_ctx = _ffn(x_prompt.reshape(1, batch * seq, D_MODEL), mod, ctx_group, w1u, w1d, *gains, mod_base=0, ln_row=0)
    x_ctx, (kt_new,), vt_new, states, _ = mixer_sublayer(x_ctx, ctx_group, None, None, None, batch)
    y_sample = _ffn(x_latent, mod, latent_group, w2u, w2d, *gains, mod_base=6, ln_row=2)
    y_prompt = _ffn(x_ctx, mod, ctx_group, w2u, w2d, *gains, mod_base=6, ln_row=2)
    y_prompt = y_prompt.reshape(batch, seq, D_MODEL)
    to_cache = lambda t: t.reshape(batch, DEPTH, N_KV_HEADS, HEAD_DIM, seq).transpose(0, 1, 4, 2, 3)
    new_cache_k, new_cache_v = to_cache(kt_new), to_cache(vt_new)
    new_state_fwd = states[0].reshape(batch, DEPTH, HG_HEADS, HG_HEAD_DIM, HG_HEAD_DIM)
    new_state_bwd = states[1].reshape(batch, DEPTH, HG_HEADS, HG_HEAD_DIM, HG_HEAD_DIM)

    return (y_prompt, y_sample, new_cache_k, new_cache_v, new_state_fwd, new_state_bwd)
```

```python
import functools

import jax
import jax.numpy as jnp
from jax import lax
from jax.experimental import pallas as pl
from jax.experimental.pallas import tpu as pltpu

f32 = jnp.float32
bf16 = jnp.bfloat16

D_MODEL = 1024
N_MOD = 9
HG_WIDTH = 512
HG_HEAD_DIM = 128
HG_HEADS = 4
ATT_WIDTH = 512
HEAD_DIM = 64
N_HEADS = 8
N_KV_HEADS = 2
KV_WIDTH = 128
IN_WIDTH = 5 * HG_WIDTH + ATT_WIDTH + 2 * KV_WIDTH
D_FF = 2816
GRID_W = 64
ROPE_PAIR = HEAD_DIM // 4
ROPE_THETA = 10000.0
DEPTH = 1
ALPHA = (2.0 * DEPTH) ** 0.25
LOG2_E = 1.4426950408889634
LN_EPS = 1e-6
RMS_EPS = 1e-6

LANES = 128
BF16_SUBLANES = 16
MIB = 1024 * 1024
FFN_VMEM_BYTES = 52 * MIB
PROJ_VMEM_BYTES = 32 * MIB
HGRN_VMEM_BYTES = 56 * MIB
ATTN_VMEM_BYTES = 40 * MIB

FF_CHUNK = 256
FFN_TILE = 1024
FFN_SUB = 512
TOKEN_TILE = 512
PROJ_SUB = 256
HG_TILE = 128
HG_STEP = 512
HG_SEQS = 2
Q_TILE = 512
Q_SUB = 256
KEY_CHUNK = 512
SCORE_BOUND_LIMIT = 40.0
SCORE_BOUND_SLACK = 1.01
V_ROWS = 80
HG_FAST_BLOCK = 64
HG_FAST_SPAN_LOG2 = 100.0


def _silu(x):
    return x * jax.nn.sigmoid(x)


def _layer_norm(r, g, b):
    mu = jnp.mean(r, axis=-1, keepdims=True)
    c = r - mu
    var = jnp.mean(c * c, axis=-1, keepdims=True)
    return c * lax.rsqrt(var + LN_EPS) * g + b


def _split_bf16(x, parts):
    out = []
    r = x
    for _ in range(parts - 1):
        p = r.astype(bf16)
        out.append(p)
        r = r - p.astype(f32)
    out.append(r.astype(bf16))
    return out


def _dot01(mat01, x, parts, *, mat_on_left):
    pieces = _split_bf16(x, parts)
    if mat_on_left:
        return jnp.dot(jnp.concatenate([mat01] * parts, axis=1), jnp.concatenate(pieces, axis=0),
                       preferred_element_type=f32)
    return jnp.dot(jnp.concatenate(pieces, axis=1), jnp.concatenate([mat01] * parts, axis=0),
                   preferred_element_type=f32)


def _dot_nt(a, b):
    return lax.dot_general(a, b, (((1,), (1,)), ((), ())), preferred_element_type=f32)


def _dot_tn(a, b):
    return lax.dot_general(a, b, (((0,), (0,)), ((), ())), preferred_element_type=f32)


def _mod_kernel(c_ref, w_ref, b_ref, o_ref):
    a = _silu(c_ref[...]).astype(bf16)
    o_ref[...] = jnp.dot(a, w_ref[...].astype(bf16), preferred_element_type=f32) + b_ref[...]


def _modulation(cvecs, w_mod, b_mod):
    rows = cvecs.shape[0]
    n_out = w_mod.shape[1]
    tn = D_MODEL
    return pl.pallas_call(
        _mod_kernel,
        grid=(n_out // tn,),
        in_specs=[
            pl.BlockSpec((rows, D_MODEL), lambda j: (0, 0)),
            pl.BlockSpec((D_MODEL, tn), lambda j: (0, j)),
            pl.BlockSpec((1, tn), lambda j: (0, j)),
        ],
        out_specs=pl.BlockSpec((rows, tn), lambda j: (0, j)),
        out_shape=jax.ShapeDtypeStruct((rows, n_out), f32),
        compiler_params=pltpu.CompilerParams(dimension_semantics=("arbitrary",)),
        name="modulation",
    )(cvecs, w_mod, b_mod.reshape(1, n_out))


def _ffn_kernel(x_ref, m_ref, wup_ref, wd_ref, g_ref, b_ref, *refs, mod_base, ln_row):
    n_cast = (len(refs) - 2) // 2
    cast_in, o_ref, cast_out, act_ref = refs[:n_cast], refs[n_cast], refs[n_cast + 1:-1], refs[-1]
    for src_ref, dst_ref in zip(cast_in, cast_out):
        dst_ref[...] = src_ref[...].astype(bf16)
    shift = m_ref[0, mod_base:mod_base + 1, :]
    scale = m_ref[0, mod_base + 1:mod_base + 2, :]
    gate = m_ref[0, mod_base + 2:mod_base + 3, :]
    subs = [slice(lo, lo + FFN_SUB) for lo in range(0, x_ref.shape[1], FFN_SUB)]
    xs = [x_ref[0, rows] for rows in subs]
    hs = [(x * (1.0 + scale) + shift).astype(bf16) for x in xs]
    for j in range(D_FF // FF_CHUNK):
        cols = slice(j * FF_CHUNK, (j + 1) * FF_CHUNK)
        for rows, h in zip(subs, hs):
            a = jnp.dot(h, wup_ref[:, cols], preferred_element_type=f32)
            u = jnp.dot(h, wup_ref[:, D_FF + j * FF_CHUNK:D_FF + (j + 1) * FF_CHUNK],
                        preferred_element_type=f32)
            act_ref[rows, cols] = (_silu(a) * u).astype(bf16)
    ys = [jnp.dot(act_ref[rows, :], wd_ref[...], preferred_element_type=f32) for rows in subs]
    for rows, x, y in zip(subs, xs, ys):
        r = ALPHA * x + 0.5 * gate * y
        o_ref[0, rows] = _layer_norm(r, g_ref[ln_row:ln_row + 1, :], b_ref[ln_row:ln_row + 1, :])


def _resident(shape):
    return pl.BlockSpec(shape, lambda *_: (0,) * len(shape), pipeline_mode=pl.Buffered(1))


def _cast_specs(to_cast, nt, steps):
    specs, shapes = [], []
    for w in to_cast:
        rows = next(r for r in range(BF16_SUBLANES, w.shape[0] + 1, BF16_SUBLANES)
                    if w.shape[0] % r == 0 and w.shape[0] // r <= steps)
        specs.append(pl.BlockSpec((rows, w.shape[1]),
                                  lambda g, t, last=w.shape[0] // rows - 1: (jnp.minimum(g * nt + t, last), 0)))
        shapes.append(jax.ShapeDtypeStruct(w.shape, bf16))
    return specs, shapes


def _ffn(x, mod, mod_group, w_up, wd, ln_g, ln_b, *, mod_base, ln_row, to_cast=()):
    groups, length, _ = x.shape
    tm = FFN_TILE
    nt = length // tm
    cast_specs, cast_shapes = _cast_specs(to_cast, nt, groups * nt)
    out = pl.pallas_call(
        functools.partial(_ffn_kernel, mod_base=mod_base, ln_row=ln_row),
        grid=(groups, nt),
        in_specs=[
            pl.BlockSpec((1, tm, D_MODEL), lambda g, t: (g, t, 0)),
            pl.BlockSpec((1, N_MOD, D_MODEL), lambda g, t: (mod_group(g), 0, 0)),
            _resident((D_MODEL, 2 * D_FF)),
            _resident((D_FF, D_MODEL)),
            _resident((3, D_MODEL)),
            _resident((3, D_MODEL)),
        ] + cast_specs,
        out_specs=[pl.BlockSpec((1, tm, D_MODEL), lambda g, t: (g, t, 0))] + cast_specs,
        out_shape=[jax.ShapeDtypeStruct(x.shape, f32)] + cast_shapes,
        scratch_shapes=[pltpu.VMEM((tm, D_FF), bf16)],
        compiler_params=pltpu.CompilerParams(
            dimension_semantics=("arbitrary", "arbitrary"), vmem_limit_bytes=FFN_VMEM_BYTES),
        name="ffn",
    )(x, mod, w_up, wd, ln_g, ln_b, *to_cast)
    return out if to_cast else out[0]


def _head_rms_norm(x, ones_bd, gain, parts):
    ss = _dot01(ones_bd, x * x, parts, mat_on_left=False)
    return x * lax.rsqrt(ss * (1.0 / HEAD_DIM) + RMS_EPS) * gain


def _rope(x, cos, sin_signed):
    width = x.shape[-1]
    lane = lax.broadcasted_iota(jnp.int32, x.shape, 1)
    from_right = pltpu.roll(x, width - ROPE_PAIR, 1)
    from_left = pltpu.roll(x, ROPE_PAIR, 1)
    partner = jnp.where((lane & (2 * ROPE_PAIR - 1)) < ROPE_PAIR, from_right, from_left)
    return x * cos + partner * sin_signed


def _proj_kernel(*refs, rope):
    if rope:
        (x_ref, m_ref, w_ref, lbf_ref, lbb_ref, qg_ref, kg_ref, bd_ref, cos_ref, sin_ref,
         qs_ref, vh_ref, ff_ref, fb_ref, sg_ref, qa_ref, ka_ref, vt_ref,
         spf_ref, spb_ref) = refs
    else:
        (x_ref, m_ref, w_ref, lbf_ref, lbb_ref, qg_ref, kg_ref, bd_ref,
         qs_ref, vh_ref, ff_ref, fb_ref, sg_ref, qa_ref, ka_ref, vt_ref,
         spf_ref, spb_ref, kt_ref) = refs
    shift = m_ref[0, 3:4, :]
    scale = m_ref[0, 4:5, :]
    subs = [slice(lo, lo + PROJ_SUB) for lo in range(0, x_ref.shape[1], PROJ_SUB)]
    hs = [(x_ref[0, rows] * (1.0 + scale) + shift).astype(bf16) for rows in subs]

    def cols(h, lo, width):
        return jnp.dot(h, w_ref[:, lo:lo + width], preferred_element_type=f32)

    def lower_bound(lb_ref):
        l0 = lb_ref[0:1, :]
        l1 = lb_ref[1:2, :]
        m = jnp.maximum(l0, l1)
        e0 = jnp.exp(l0 - m)
        e1 = jnp.exp(l1 - m)
        return e0 / (e0 + e1)

    half = HG_FAST_BLOCK // 2

    def forget_gate(lb_ref, lo, f_ref, span_ref):
        lb = lower_bound(lb_ref)
        for rows, h in zip(subs, hs):
            f = lb + (1.0 - lb) * jax.nn.sigmoid(cols(h, lo, HG_WIDTH))
            f_ref[0, rows] = f
            sums = jnp.sum(jnp.log2(f).reshape(PROJ_SUB // half, half, HG_WIDTH), axis=1)
            span_ref[0, rows.start // half:rows.stop // half] = jnp.broadcast_to(
                jnp.max(jnp.abs(sums), axis=-1, keepdims=True), (PROJ_SUB // half, LANES))

    base = 5 * HG_WIDTH
    piece = vt_ref.shape[2]
    for rows, h in zip(subs, hs):
        q = _head_rms_norm(cols(h, base, ATT_WIDTH), bd_ref[...], qg_ref[...], 1)
        k = _head_rms_norm(cols(h, base + ATT_WIDTH, KV_WIDTH), bd_ref[0:KV_WIDTH, 0:KV_WIDTH], kg_ref[...], 2)
        if rope:
            cos, sin = cos_ref[rows, :], sin_ref[rows, :]
            q = _rope(q, jnp.concatenate([cos] * (ATT_WIDTH // LANES), axis=1),
                      jnp.concatenate([sin] * (ATT_WIDTH // LANES), axis=1))
            k = _rope(k, cos, sin)
        qa_ref[0, rows] = (q * (HEAD_DIM ** -0.5 * LOG2_E)).astype(bf16)
        ka_ref[0, rows] = k
        v = cols(h, base + ATT_WIDTH + KV_WIDTH, KV_WIDTH)
        for lo in range(rows.start, rows.stop, min(piece, PROJ_SUB)):
            n = min(piece, PROJ_SUB)
            dst = (lo // piece, slice(None), slice(lo % piece, lo % piece + n))
            vt_ref[dst] = v[lo - rows.start:lo - rows.start + n].T
            if not rope:
                kt_ref[dst] = k[lo - rows.start:lo - rows.start + n].T

    for rows, h in zip(subs, hs):
        qs_ref[0, rows] = _silu(cols(h, 0, HG_WIDTH))
    for rows, h in zip(subs, hs):
        vh_ref[0, rows] = cols(h, HG_WIDTH, HG_WIDTH).astype(bf16)
    forget_gate(lbf_ref, 2 * HG_WIDTH, ff_ref, spf_ref)
    forget_gate(lbb_ref, 3 * HG_WIDTH, fb_ref, spb_ref)
    for rows, h in zip(subs, hs):
        sg_ref[0, rows] = _silu(cols(h, 4 * HG_WIDTH, HG_WIDTH))


def _proj(x, mod, mod_group, w_in, lb_f, lb_b, q_gain, k_gain, ones_bd, rope_tables, seq_len):
    groups, length, _ = x.shape
    tm = TOKEN_TILE
    rope = rope_tables is not None
    n_seq = groups * length // seq_len
    if seq_len >= tm:
        tiles_per_seq = seq_len // tm
        t_spec = pl.BlockSpec((1, KV_WIDTH, tm), lambda g, t: (g * (length // seq_len) + t // tiles_per_seq, 0,
                                                               t % tiles_per_seq))
    else:
        t_spec = pl.BlockSpec((tm // seq_len, KV_WIDTH, seq_len), lambda g, t: (g * (length // tm) + t, 0, 0))
    t_shape = jax.ShapeDtypeStruct((n_seq, KV_WIDTH, seq_len), f32)
    tok = lambda width: pl.BlockSpec((1, tm, width), lambda g, t: (g, t, 0))
    in_specs = [
        tok(D_MODEL),
        pl.BlockSpec((1, N_MOD, D_MODEL), lambda g, t: (mod_group(g), 0, 0)),
        _resident((D_MODEL, IN_WIDTH)),
        _resident((2, HG_WIDTH)),
        _resident((2, HG_WIDTH)),
        _resident((1, ATT_WIDTH)),
        _resident((1, KV_WIDTH)),
        _resident((ATT_WIDTH, ATT_WIDTH)),
    ]
    args = [x, mod, w_in, lb_f, lb_b, q_gain, k_gain, ones_bd]
    if rope:
        in_specs += [pl.BlockSpec((tm, LANES), lambda g, t: (t, 0))] * 2
        args += list(rope_tables)
    shape = lambda width, dt: jax.ShapeDtypeStruct((groups, length, width), dt)
    half = HG_FAST_BLOCK // 2
    span_spec = pl.BlockSpec((1, tm // half, LANES), lambda g, t: (g, t, 0))
    span_shape = jax.ShapeDtypeStruct((groups, length // half, LANES), f32)
    return pl.pallas_call(
        functools.partial(_proj_kernel, rope=rope),
        grid=(groups, length // tm),
        in_specs=in_specs,
        out_specs=([tok(HG_WIDTH)] * 5 + [tok(ATT_WIDTH), tok(KV_WIDTH), t_spec, span_spec, span_spec]
                   + ([] if rope else [t_spec])),
        out_shape=([shape(HG_WIDTH, f32), shape(HG_WIDTH, bf16)] + [shape(HG_WIDTH, f32)] * 3
                   + [shape(ATT_WIDTH, bf16), shape(KV_WIDTH, f32), t_shape, span_shape, span_shape]
                   + ([] if rope else [t_shape])),
        compiler_params=pltpu.CompilerParams(
            dimension_semantics=("arbitrary", "arbitrary"), vmem_limit_bytes=PROJ_VMEM_BYTES),
        name="mixer_proj",
    )(*args)


def _hgrn_direction(qs, vb, f, st_ref, o_ref, rows, *, rev):
    tc = qs.shape[0]
    lf = jnp.log2(f)
    kk = 1.0 - f
    row = lax.broadcasted_iota(jnp.int32, (tc, tc), 0)
    col = lax.broadcasted_iota(jnp.int32, (tc, tc), 1)
    ordered = (row < col) if rev else (row > col)
    differ = row ^ col
    tri = jnp.where((col >= row) if rev else (col <= row), 1.0, 0.0).astype(bf16)
    cum = _dot01(tri, lf, 2, mat_on_left=True)
    total = cum[0:1] if rev else cum[tc - 1:tc]
    carry = jnp.exp2(total)
    query_half = 0 if rev else 1
    heads = [slice(hd * HG_HEAD_DIM, (hd + 1) * HG_HEAD_DIM) for hd in range(HG_HEADS)]

    def level_exponent(s, ridx):
        if s == 1:
            return jnp.where((ridx & 1) == query_half, lf, 0.0)
        if s == 2:
            nxt = pltpu.roll(lf, tc - 1, 0)
            prv = pltpu.roll(lf, 1, 0)
            m4 = ridx & 3
            if rev:
                return jnp.where(m4 == 0, lf + nxt, jnp.where(m4 == 1, lf, jnp.where(m4 == 2, 0.0, prv)))
            return jnp.where(m4 == 0, nxt, jnp.where(m4 == 1, 0.0, jnp.where(m4 == 2, lf, lf + prv)))
        blocks = []
        for lo in range(0, tc, 2 * s):
            anchor = lo + (s if rev else s - 1)
            blocks.append(cum[lo:lo + 2 * s] - cum[anchor:anchor + 1])
        d = jnp.concatenate(blocks, axis=0)
        is_query = ((ridx >> (s.bit_length() - 1)) & 1) == query_half
        return jnp.where(is_query, d, -d)

    def add_levels(intra, first):
        ridx = lax.broadcasted_iota(jnp.int32, qs.shape, 0)
        s = first
        while s < tc:
            shift = s.bit_length() - 1
            if 2 * s == tc:
                lower, upper = slice(0, s), slice(s, tc)
                anchor = cum[s:s + 1] if rev else cum[s - 1:s]
                q_rows, k_rows = (lower, upper) if rev else (upper, lower)
                halves = {q_rows: qs[q_rows] * jnp.exp2(cum[q_rows] - anchor),
                          k_rows: kk[k_rows] * jnp.exp2(anchor - cum[k_rows])}
                mixed = jnp.concatenate([halves[lower], halves[upper]], axis=0).astype(bf16)
            else:
                is_query = ((ridx >> shift) & 1) == query_half
                mixed = (jnp.where(is_query, qs, kk) * jnp.exp2(level_exponent(s, ridx))).astype(bf16)
            pair = jnp.logical_and((differ >> shift) == 1, ordered)
            for hd, hs in enumerate(heads):
                intra[hd] = jnp.where(pair, _dot_nt(mixed[:, hs], mixed[:, hs]), intra[hd])
            s *= 2
        return intra

    def finish(intra, q_dec, k_end, extra):
        for hd, hs in enumerate(heads):
            st = st_ref[hd]
            o = jnp.dot(jnp.concatenate([intra[hd].astype(bf16), q_dec[:, hs]], axis=1),
                        jnp.concatenate([vb[:, hs], st.T.astype(bf16)], axis=0), preferred_element_type=f32)
            o_ref[rows, hs] = o if extra is None else o + extra[:, hs]
            st_ref[hd] = st * carry[:, hs] + _dot_tn(vb[:, hs], k_end[:, hs])

    blk = HG_FAST_BLOCK
    anchors = [lo + (blk // 2 if rev else blk // 2 - 1) for lo in range(0, tc, blk)]

    def mid_split_blocks():
        q_mid, k_mid, q_dec, k_end = [], [], [], []
        for lo, a in zip(range(0, tc, blk), anchors):
            rel = cum[lo:lo + blk] - cum[a:a + 1]
            q_mid.append(qs[lo:lo + blk] * jnp.exp2(rel))
            k_mid.append(kk[lo:lo + blk] * jnp.exp2(-rel))
            q_dec.append(q_mid[-1] * jnp.exp2(cum[a:a + 1]))
            k_end.append(k_mid[-1] * jnp.exp2(total - cum[a:a + 1]))
        q_mid, k_mid, q_dec, k_end = (jnp.concatenate(p, axis=0).astype(bf16) for p in (q_mid, k_mid, q_dec, k_end))
        same_block = (differ >> (blk.bit_length() - 1)) == 0
        keep = jnp.logical_and(same_block, (row <= col) if rev else (row >= col))
        intra = [jnp.where(keep, _dot_nt(q_mid[:, hs], k_mid[:, hs]), 0.0) for hs in heads]
        finish(add_levels(intra, blk), q_dec, k_end, None)

    def all_levels():
        intra = add_levels([jnp.zeros((tc, tc), f32)] * HG_HEADS, 1)
        q_dec = (qs * jnp.exp2(cum)).astype(bf16)
        k_end = (kk * jnp.exp2(total - cum)).astype(bf16)
        qk = qs * kk
        own = jnp.concatenate(
            [jnp.sum(qk[:, hs], axis=-1, keepdims=True) * vb[:, hs].astype(f32) for hs in heads], axis=-1)
        finish(intra, q_dec, k_end, own)

    return mid_split_blocks, all_levels


def _hgrn_kernel(slow_ref, *refs, has_init):
    if has_init:
        (qsf_ref, vf_ref, ff_ref, qsb_ref, vb_ref, fb_ref, s0f_ref, s0b_ref,
         of_ref, ob_ref, stf_ref, stb_ref) = refs
    else:
        (qsf_ref, vf_ref, ff_ref, qsb_ref, vb_ref, fb_ref,
         of_ref, ob_ref, sf_out_ref, sb_out_ref, stf_ref, stb_ref) = refs
    t = pl.program_id(1)
    n_seq = qsf_ref.shape[0]

    @pl.when(t == 0)
    def _():
        for b in range(n_seq):
            for hd in range(HG_HEADS):
                if has_init:
                    stf_ref[b, hd] = s0f_ref[b, hd].T
                    stb_ref[b, hd] = s0b_ref[b, hd].T
                else:
                    stf_ref[b, hd] = jnp.zeros((HG_HEAD_DIM, HG_HEAD_DIM), f32)
                    stb_ref[b, hd] = jnp.zeros((HG_HEAD_DIM, HG_HEAD_DIM), f32)

    chunks = [slice(lo, lo + HG_TILE) for lo in range(0, qsf_ref.shape[1], HG_TILE)]
    scans = []
    for r_f, r_b in zip(chunks, reversed(chunks)):
        for b in range(n_seq):
            scans.append(_hgrn_direction(qsf_ref[b, r_f], vf_ref[b, r_f], ff_ref[b, r_f], stf_ref.at[b], of_ref.at[b],
                                         r_f, rev=False))
            scans.append(_hgrn_direction(qsb_ref[b, r_b], vb_ref[b, r_b], fb_ref[b, r_b], stb_ref.at[b], ob_ref.at[b],
                                         r_b, rev=True))
    slow = slow_ref[pl.program_id(0) * pl.num_programs(1) + t] != 0

    @pl.when(jnp.logical_not(slow))
    def _():
        for mid_split_blocks, _ in scans:
            mid_split_blocks()

    @pl.when(slow)
    def _():
        for _, all_levels in scans:
            all_levels()

    if not has_init:
        @pl.when(t == pl.num_programs(1) - 1)
        def _():
            for b in range(n_seq):
                for hd in range(HG_HEADS):
                    sf_out_ref[b, hd] = stf_ref[b, hd].T
                    sb_out_ref[b, hd] = stb_ref[b, hd].T


def _hgrn(qs, vh, f_fwd, f_bwd, span_fwd, span_bwd, init_states):
    n_seqs, length, _ = qs.shape
    step = min(HG_STEP, length)
    nt = length // step
    groups = n_seqs // HG_SEQS
    has_init = init_states is not None
    per_step = lambda span: jnp.max(span[:, :, 0].reshape(n_seqs, nt, -1), axis=-1) > HG_FAST_SPAN_LOG2
    slow = jnp.logical_or(per_step(span_fwd), per_step(span_bwd)[:, ::-1])
    slow = jnp.any(slow.reshape(groups, HG_SEQS, nt), axis=1).astype(jnp.int32).reshape(-1)
    fwd = pl.BlockSpec((HG_SEQS, step, HG_WIDTH), lambda g, t, _: (g, t, 0))
    bwd = pl.BlockSpec((HG_SEQS, step, HG_WIDTH), lambda g, t, _: (g, nt - 1 - t, 0))
    state = pl.BlockSpec((HG_SEQS, HG_HEADS, HG_HEAD_DIM, HG_HEAD_DIM), lambda g, t, _: (g, 0, 0, 0))
    in_specs = [fwd, fwd, fwd, bwd, bwd, bwd]
    args = [qs, vh, f_fwd, qs, vh, f_bwd]
    out_specs = [fwd, bwd]
    out_shape = [jax.ShapeDtypeStruct(qs.shape, f32)] * 2
    if has_init:
        in_specs += [state, state]
        args += list(init_states)
    else:
        out_specs += [state, state]
        out_shape += [jax.ShapeDtypeStruct((n_seqs, HG_HEADS, HG_HEAD_DIM, HG_HEAD_DIM), f32)] * 2
    return pl.pallas_call(
        functools.partial(_hgrn_kernel, has_init=has_init),
        grid_spec=pltpu.PrefetchScalarGridSpec(
            num_scalar_prefetch=1,
            grid=(groups, nt),
            in_specs=in_specs,
            out_specs=out_specs,
            scratch_shapes=[pltpu.VMEM((HG_SEQS, HG_HEADS, HG_HEAD_DIM, HG_HEAD_DIM), f32)] * 2),
        out_shape=out_shape,
        compiler_params=pltpu.CompilerParams(
            dimension_semantics=("arbitrary", "arbitrary"), vmem_limit_bytes=HGRN_VMEM_BYTES),
        name="hgrn2",
    )(slow, *args)


def _attn_kernel(*refs, n_ctx, n_cast):
    kk_ref, vta_ref, bound_ref, fixed_ref = refs[-4:]
    o_ref, *cast_out = refs[len(refs) - 5 - n_cast:-4]
    inputs = refs[:len(refs) - 5 - n_cast]
    cast_in = inputs[len(inputs) - n_cast:]
    if n_ctx:
        (qa_ref, ka_ref, vt_ref, ckt_ref, cvt_ref, x_ref, of_ref, ob_ref, sg_ref, m_ref, wo_ref, hg_ref, qg_ref,
         g_ref, b_ref) = inputs[:len(inputs) - n_cast]
    else:
        (qa_ref, ka_ref, vt_ref, x_ref, of_ref, ob_ref, sg_ref, m_ref, wo_ref, hg_ref, qg_ref,
         g_ref, b_ref) = inputs[:len(inputs) - n_cast]
    for src_ref, dst_ref in zip(cast_in, cast_out):
        dst_ref[...] = src_ref[...].astype(bf16)
    tq = qa_ref.shape[1]
    n_keys = kk_ref.shape[0]

    @pl.when(pl.program_id(1) == 0)
    def _():
        def fill_keys(lo, k):
            n = k.shape[0]
            low = lax.broadcasted_iota(jnp.int32, k.shape, 1) < HEAD_DIM
            k_sw = pltpu.roll(k, HEAD_DIM, 1)
            kk_ref[lo:lo + n, 0:LANES] = jnp.where(low, k, k_sw).astype(bf16)
            kk_ref[lo:lo + n, LANES:2 * LANES] = jnp.where(low, k_sw, k).astype(bf16)

        def fill_values(lo, vt):
            n = vt.shape[1]
            ones_row = jnp.where(lax.broadcasted_iota(jnp.int32, (V_ROWS - HEAD_DIM, n), 0) == 0, 1.0, 0.0)
            for kv in range(N_KV_HEADS):
                vta_ref[kv * V_ROWS:kv * V_ROWS + HEAD_DIM, lo:lo + n] = (
                    vt[kv * HEAD_DIM:(kv + 1) * HEAD_DIM].astype(bf16))
                vta_ref[kv * V_ROWS + HEAD_DIM:(kv + 1) * V_ROWS, lo:lo + n] = ones_row.astype(bf16)

        if n_ctx:
            fill_keys(0, ckt_ref[0].T)
            fill_values(0, cvt_ref[0])
        fill_keys(n_ctx, ka_ref[0])
        fill_values(n_ctx, vt_ref[0])

        gain_max = jnp.max(jnp.abs(qg_ref[...]), axis=1, keepdims=True)
        all_small = None
        for kv in range(N_KV_HEADS):
            kt = kk_ref[:, kv * LANES:(kv + 1) * LANES].astype(f32)
            k_norm2 = jnp.max(0.5 * jnp.sum(kt * kt, axis=1, keepdims=True), axis=0, keepdims=True)
            bound = (LOG2_E * SCORE_BOUND_SLACK) * gain_max * jnp.sqrt(k_norm2)
            bound_ref[kv] = bound[0, 0]
            small = jnp.where(bound <= SCORE_BOUND_LIMIT, 1, 0)
            all_small = small if all_small is None else all_small * small
        fixed_ref[0] = all_small[0, 0]

    ts = min(Q_SUB, tq)
    low = lax.broadcasted_iota(jnp.int32, (ts, LANES), 1) < HEAD_DIM
    kc = min(KEY_CHUNK, n_keys)
    pairs_per_kv = N_HEADS // N_KV_HEADS // 2

    def attend(rows, fixed_shift):
        def masked_pair(tile):
            qp = qa_ref[0, rows, tile * LANES:(tile + 1) * LANES]
            zero = jnp.zeros_like(qp)
            return jnp.concatenate([jnp.where(low, qp, zero), jnp.where(low, zero, qp)], axis=0)

        q_pairs = [masked_pair(tile) for tile in range(N_HEADS // 2)]
        work = [(tile, lo) for tile in range(N_HEADS // 2) for lo in range(0, n_keys, kc)]

        def scores(tile, lo):
            kv = tile // pairs_per_kv
            return _dot_nt(kk_ref[lo:lo + kc, kv * LANES:(kv + 1) * LANES], q_pairs[tile])

        heads_t = []
        st_next = scores(*work[0])
        m = acc = None
        for i, (tile, lo) in enumerate(work):
            st = st_next
            if i + 1 < len(work):
                st_next = scores(*work[i + 1])
            kv = tile // pairs_per_kv
            values_t = vta_ref[kv * V_ROWS:(kv + 1) * V_ROWS, lo:lo + kc]
            if fixed_shift:
                e = jnp.exp2(st - bound_ref[kv]).astype(bf16)
                pv = jnp.dot(values_t, e, preferred_element_type=f32)
                acc = pv if acc is None else acc + pv
            else:
                m_chunk = jnp.max(st, axis=0, keepdims=True)
                m_new = m_chunk if m is None else jnp.maximum(m, m_chunk)
                e = jnp.exp2(st - m_new).astype(bf16)
                pv = jnp.dot(values_t, e, preferred_element_type=f32)
                acc = pv if acc is None else acc * jnp.exp2(m - m_new) + pv
                m = m_new
            if lo + kc == n_keys:
                on = (acc[0:HEAD_DIM] * (1.0 / acc[HEAD_DIM:HEAD_DIM + 1])).astype(bf16)
                heads_t += [on[:, :ts], on[:, ts:]]
                m = acc = None
        return jnp.concatenate(heads_t, axis=0)

    def project(rows, o_att_t):
        o_sum = of_ref[0, rows] + ob_ref[0, rows]
        normed = []
        for hd in range(HG_HEADS):
            oh = o_sum[:, hd * HG_HEAD_DIM:(hd + 1) * HG_HEAD_DIM]
            ms = jnp.mean(oh * oh, axis=-1, keepdims=True)
            normed.append(oh * lax.rsqrt(ms + RMS_EPS))
        o_hg = (jnp.concatenate(normed, axis=-1) * hg_ref[...] * sg_ref[0, rows]).astype(bf16)
        y = (jnp.dot(o_hg, wo_ref[0:HG_WIDTH, :], preferred_element_type=f32)
             + _dot_tn(o_att_t, wo_ref[HG_WIDTH:HG_WIDTH + ATT_WIDTH, :]))
        r = ALPHA * x_ref[0, rows] + m_ref[0, 5:6, :] * y
        o_ref[0, rows] = _layer_norm(r, g_ref[1:2, :], b_ref[1:2, :])

    def run(fixed_shift):
        for lo in range(0, tq, ts):
            rows = slice(lo, lo + ts)
            project(rows, attend(rows, fixed_shift))

    use_bound = fixed_ref[0] != 0
    pl.when(use_bound)(functools.partial(run, True))
    pl.when(jnp.logical_not(use_bound))(functools.partial(run, False))


def _attn(qa, ka, vt, ctx_kv_t, x, o_f, o_b, sg, mod, mod_group, w_out, hg_gain, q_gain, ln_g, ln_b, to_cast=()):
    groups, length, _ = qa.shape
    tq = min(Q_TILE, length)
    nt = length // tq
    n_ctx = 0 if ctx_kv_t is None else ctx_kv_t[0].shape[2]
    n_keys = n_ctx + length
    tok = lambda width: pl.BlockSpec((1, tq, width), lambda g, t: (g, t, 0))
    whole_t = lambda n: pl.BlockSpec((1, KV_WIDTH, n), lambda g, t: (g, 0, 0))
    in_specs = [tok(ATT_WIDTH), pl.BlockSpec((1, length, KV_WIDTH), lambda g, t: (g, 0, 0)), whole_t(length)]
    args = [qa, ka, vt]
    if n_ctx:
        in_specs += [whole_t(n_ctx), whole_t(n_ctx)]
        args += list(ctx_kv_t)
    in_specs += [
        tok(D_MODEL), tok(HG_WIDTH), tok(HG_WIDTH), tok(HG_WIDTH),
        pl.BlockSpec((1, N_MOD, D_MODEL), lambda g, t: (mod_group(g), 0, 0)),
        _resident((D_MODEL, D_MODEL)),
        _resident((1, HG_WIDTH)),
        _resident((1, ATT_WIDTH)),
        _resident((3, D_MODEL)),
        _resident((3, D_MODEL)),
    ]
    args += [x, o_f, o_b, sg, mod, w_out, hg_gain, q_gain, ln_g, ln_b]
    cast_specs, cast_shapes = _cast_specs(to_cast, nt, groups * nt)
    in_specs += cast_specs
    args += list(to_cast)
    out_specs = [tok(D_MODEL)] + cast_specs
    out_shape = [jax.ShapeDtypeStruct(x.shape, f32)] + cast_shapes
    return pl.pallas_call(
        functools.partial(_attn_kernel, n_ctx=n_ctx, n_cast=len(to_cast)),
        grid=(groups, nt),
        in_specs=in_specs,
        out_specs=out_specs,
        out_shape=out_shape,
        scratch_shapes=[pltpu.VMEM((n_keys, N_KV_HEADS * LANES), bf16),
                        pltpu.VMEM((N_KV_HEADS * V_ROWS, n_keys), bf16),
                        pltpu.SMEM((N_KV_HEADS,), f32),
                        pltpu.SMEM((1,), jnp.int32)],
        compiler_params=pltpu.CompilerParams(
            dimension_semantics=("arbitrary", "arbitrary"), vmem_limit_bytes=ATTN_VMEM_BYTES),
        name="attn_out",
    )(*args)


def _rope_tables(n_tokens):
    half = HEAD_DIM // 2
    t = jnp.arange(n_tokens)
    inv = ROPE_THETA ** (-jnp.arange(0, half, 2, dtype=f32) / half)
    ang_row = (t // GRID_W).astype(f32)[:, None] * inv
    ang_col = (t % GRID_W).astype(f32)[:, None] * inv
    cos = jnp.concatenate([jnp.cos(ang_row)] * 2 + [jnp.cos(ang_col)] * 2, axis=-1)
    sin = jnp.concatenate([-jnp.sin(ang_row), jnp.sin(ang_row), -jnp.sin(ang_col), jnp.sin(ang_col)], axis=-1)
    return jnp.tile(cos, (1, LANES // HEAD_DIM)), jnp.tile(sin, (1, LANES // HEAD_DIM))


def kernel(x_prompt, x_sample, cache_k, cache_v, state_hgrn_fwd, state_hgrn_bwd, c, c_ctx, w_mod, b_mod,
           w_ffn1_in, w_ffn1_out, w_ffn2_in, w_ffn2_out, w_in, w_out, q_norm_g, k_norm_g, hg_norm_g,
           lb_logits_fwd, lb_logits_bwd, ln_g, ln_b):
    assert w_mod.shape[0] == DEPTH and lb_logits_fwd.shape[0] == DEPTH + 1
    batch, seq, _ = x_prompt.shape
    dec_batch, dec_seq, _ = x_sample.shape
    past = cache_k.shape[2]

    ctx_row = dec_batch
    rows = -(-(dec_batch + 1) // BF16_SUBLANES) * BF16_SUBLANES
    cvecs = jnp.concatenate([c, c_ctx[None, :], jnp.zeros((rows - dec_batch - 1, D_MODEL), f32)], axis=0)
    mod = _modulation(cvecs, w_mod[0], b_mod[0]).reshape(rows, N_MOD, D_MODEL)

    to_bf16 = lambda w: w.astype(bf16)
    w1u, w1d = to_bf16(w_ffn1_in[0]), to_bf16(w_ffn1_out[0])
    gains = ln_g[0], ln_b[0]
    q_gain = jnp.tile(q_norm_g[0], N_HEADS).reshape(1, ATT_WIDTH)
    k_gain = jnp.tile(k_norm_g[0], N_KV_HEADS).reshape(1, KV_WIDTH)
    hg_gain = hg_norm_g[0].reshape(1, HG_WIDTH)
    head_of = jnp.arange(ATT_WIDTH) // HEAD_DIM
    ones_bd = (head_of[:, None] == head_of[None, :]).astype(bf16)

    def mixer_sublayer(x, mod_group, rope_tables, ctx_kv_t, init_states, hg_groups, to_cast=()):
        shape = x.shape
        seq_len = shape[0] * shape[1] // hg_groups
        per_seq = lambda a: a.reshape(hg_groups, seq_len, a.shape[-1])
        proj = _proj(x, mod, mod_group, w_in_b, lb_logits_fwd, lb_logits_bwd, q_gain, k_gain, ones_bd, rope_tables,
                     seq_len)
        qs, vh, f_f, f_b, sg, qa, ka = map(per_seq, proj[:7])
        vt, span_f, span_b = proj[7], proj[8].reshape(hg_groups, -1, LANES), proj[9].reshape(hg_groups, -1, LANES)
        scans = _hgrn(qs, vh, f_f, f_b, span_f, span_b, init_states)
        x, *cast = _attn(qa, ka, vt, ctx_kv_t, per_seq(x), scans[0], scans[1], sg, mod, mod_group, w_out_b, hg_gain,
                         q_gain, *gains, to_cast=to_cast)
        return x.reshape(shape), proj[10:], vt, scans[2:], cast

    from_cache = lambda t: t[:, 0].transpose(0, 2, 3, 1).reshape(dec_batch, KV_WIDTH, past)
    init_states = (state_hgrn_fwd[:, 0], state_hgrn_bwd[:, 0])
    latent_group = lambda g: g
    x_latent, w_in_b, w_out_b = _ffn(x_sample, mod, latent_group, w1u, w1d, *gains, mod_base=0, ln_row=0,
                                     to_cast=(w_in[0], w_out[0]))
    x_latent, _, _, _, (w2u, w2d) = mixer_sublayer(
        x_latent, latent_group, _rope_tables(dec_seq), (from_cache(cache_k), from_cache(cache_v)), init_states,
        dec_batch, to_cast=(w_ffn2_in[0], w_ffn2_out[0]))

    ctx_group = lambda g: ctx_row
    x_ctx = _ffn(x_prompt.reshape(1, batch * seq, D_MODEL), mod, ctx_group, w1u, w1d, *gains, mod_base=0, ln_row=0)
    x_ctx, (kt_new,), vt_new, states, _ = mixer_sublayer(x_ctx, ctx_group, None, None, None, batch)
    y_sample = _ffn(x_latent, mod, latent_group, w2u, w2d, *gains, mod_base=6, ln_row=2)
    y_prompt = _ffn(x_ctx, mod, ctx_group, w2u, w2d, *gains, mod_base=6, ln_row=2)
    y_prompt = y_prompt.reshape(batch, seq, D_MODEL)
    to_cache = lambda t: t.reshape(batch, DEPTH, N_KV_HEADS, HEAD_DIM, seq).transpose(0, 1, 4, 2, 3)
    new_cache_k, new_cache_v = to_cache(kt_new), to_cache(vt_new)
    new_state_fwd = states[0].reshape(batch, DEPTH, HG_HEADS, HG_HEAD_DIM, HG_HEAD_DIM)
    new_state_bwd = states[1].reshape(batch, DEPTH, HG_HEADS, HG_HEAD_DIM, HG_HEAD_DIM)

    return (y_prompt, y_sample, new_cache_k, new_cache_v, new_state_fwd, new_state_bwd)
```

```python
import functools

import jax
import jax.numpy as jnp
from jax import lax
from jax.experimental import pallas as pl
from jax.experimental.pallas import tpu as pltpu

f32 = jnp.float32
bf16 = jnp.bfloat16

D_MODEL = 1024
N_MOD = 9
HG_WIDTH = 512
HG_HEAD_DIM = 128
HG_HEADS = 4
ATT_WIDTH = 512
HEAD_DIM = 64
N_HEADS = 8
N_KV_HEADS = 2
KV_WIDTH = 128
IN_WIDTH = 5 * HG_WIDTH + ATT_WIDTH + 2 * KV_WIDTH
D_FF = 2816
GRID_W = 64
ROPE_PAIR = HEAD_DIM // 4
ROPE_THETA = 10000.0
DEPTH = 1
ALPHA = (2.0 * DEPTH) ** 0.25
LOG2_E = 1.4426950408889634
LN_EPS = 1e-6
RMS_EPS = 1e-6

LANES = 128
BF16_SUBLANES = 16
VMEM_LIMIT_BYTES = 56 * 1024 * 1024

FF_CHUNK = 256
FFN_TILE = 1024
FFN_SUB = 512
TOKEN_TILE = 512
PROJ_SUB = 256
HG_TILE = 128
HG_STEP = 512
HG_SEQS = 2
Q_TILE = 1024
Q_SUB = 256
KEY_CHUNK = 512
SCORE_BOUND_LIMIT = 40.0
SCORE_BOUND_SLACK = 1.01
V_ROWS = 80
HG_FAST_BLOCK = 64
HG_FAST_SPAN_LOG2 = 100.0


def _silu(x):
    return x * jax.nn.sigmoid(x)


def _layer_norm(r, g, b):
    mu = jnp.mean(r, axis=-1, keepdims=True)
    c = r - mu
    var = jnp.mean(c * c, axis=-1, keepdims=True)
    return c * lax.rsqrt(var + LN_EPS) * g + b


def _split_bf16(x, parts):
    out = []
    r = x
    for _ in range(parts - 1):
        p = r.astype(bf16)
        out.append(p)
        r = r - p.astype(f32)
    out.append(r.astype(bf16))
    return out


def _dot01(mat01, x, parts, *, mat_on_left):
    pieces = _split_bf16(x, parts)
    if mat_on_left:
        return jnp.dot(jnp.concatenate([mat01] * parts, axis=1), jnp.concatenate(pieces, axis=0),
                       preferred_element_type=f32)
    return jnp.dot(jnp.concatenate(pieces, axis=1), jnp.concatenate([mat01] * parts, axis=0),
                   preferred_element_type=f32)


def _dot_nt(a, b):
    return lax.dot_general(a, b, (((1,), (1,)), ((), ())), preferred_element_type=f32)


def _dot_tn(a, b):
    return lax.dot_general(a, b, (((0,), (0,)), ((), ())), preferred_element_type=f32)


def _mod_kernel(c_ref, w_ref, b_ref, o_ref):
    a = _silu(c_ref[...]).astype(bf16)
    o_ref[...] = jnp.dot(a, w_ref[...].astype(bf16), preferred_element_type=f32) + b_ref[...]


def _modulation(cvecs, w_mod, b_mod):
    rows = cvecs.shape[0]
    n_out = w_mod.shape[1]
    tn = D_MODEL
    return pl.pallas_call(
        _mod_kernel,
        grid=(n_out // tn,),
        in_specs=[
            pl.BlockSpec((rows, D_MODEL), lambda j: (0, 0)),
            pl.BlockSpec((D_MODEL, tn), lambda j: (0, j)),
            pl.BlockSpec((1, tn), lambda j: (0, j)),
        ],
        out_specs=pl.BlockSpec((rows, tn), lambda j: (0, j)),
        out_shape=jax.ShapeDtypeStruct((rows, n_out), f32),
        compiler_params=pltpu.CompilerParams(dimension_semantics=("arbitrary",)),
        name="modulation",
    )(cvecs, w_mod, b_mod.reshape(1, n_out))


def _ffn_kernel(x_ref, m_ref, wup_ref, wd_ref, g_ref, b_ref, *refs, mod_base, ln_row):
    n_cast = (len(refs) - 2) // 2
    cast_in, o_ref, cast_out, act_ref = refs[:n_cast], refs[n_cast], refs[n_cast + 1:-1], refs[-1]
    for src_ref, dst_ref in zip(cast_in, cast_out):
        dst_ref[...] = src_ref[...].astype(bf16)
    shift = m_ref[0, mod_base:mod_base + 1, :]
    scale = m_ref[0, mod_base + 1:mod_base + 2, :]
    gate = m_ref[0, mod_base + 2:mod_base + 3, :]
    subs = [slice(lo, lo + FFN_SUB) for lo in range(0, x_ref.shape[1], FFN_SUB)]
    xs = [x_ref[0, rows] for rows in subs]
    hs = [(x * (1.0 + scale) + shift).astype(bf16) for x in xs]
    for j in range(D_FF // FF_CHUNK):
        cols = slice(j * FF_CHUNK, (j + 1) * FF_CHUNK)
        for rows, h in zip(subs, hs):
            a = jnp.dot(h, wup_ref[:, cols], preferred_element_type=f32)
            u = jnp.dot(h, wup_ref[:, D_FF + j * FF_CHUNK:D_FF + (j + 1) * FF_CHUNK],
                        preferred_element_type=f32)
            act_ref[rows, cols] = (_silu(a) * u).astype(bf16)
    ys = [jnp.dot(act_ref[rows, :], wd_ref[...], preferred_element_type=f32) for rows in subs]
    for rows, x, y in zip(subs, xs, ys):
        r = ALPHA * x + 0.5 * gate * y
        o_ref[0, rows] = _layer_norm(r, g_ref[ln_row:ln_row + 1, :], b_ref[ln_row:ln_row + 1, :])


def _resident(shape):
    return pl.BlockSpec(shape, lambda *_: (0,) * len(shape), pipeline_mode=pl.Buffered(1))


def _cast_specs(to_cast, nt, steps):
    specs, shapes = [], []
    for w in to_cast:
        rows = next(r for r in range(BF16_SUBLANES, w.shape[0] + 1, BF16_SUBLANES)
                    if w.shape[0] % r == 0 and w.shape[0] // r <= steps)
        specs.append(pl.BlockSpec((rows, w.shape[1]),
                                  lambda g, t, last=w.shape[0] // rows - 1: (jnp.minimum(g * nt + t, last), 0)))
        shapes.append(jax.ShapeDtypeStruct(w.shape, bf16))
    return specs, shapes


def _ffn(x, mod, mod_group, w_up, wd, ln_g, ln_b, *, mod_base, ln_row, to_cast=()):
    groups, length, _ = x.shape
    tm = FFN_TILE
    nt = length // tm
    cast_specs, cast_shapes = _cast_specs(to_cast, nt, groups * nt)
    out = pl.pallas_call(
        functools.partial(_ffn_kernel, mod_base=mod_base, ln_row=ln_row),
        grid=(groups, nt),
        in_specs=[
            pl.BlockSpec((1, tm, D_MODEL), lambda g, t: (g, t, 0)),
            pl.BlockSpec((1, N_MOD, D_MODEL), lambda g, t: (mod_group(g), 0, 0)),
            _resident((D_MODEL, 2 * D_FF)),
            _resident((D_FF, D_MODEL)),
            _resident((3, D_MODEL)),
            _resident((3, D_MODEL)),
        ] + cast_specs,
        out_specs=[pl.BlockSpec((1, tm, D_MODEL), lambda g, t: (g, t, 0))] + cast_specs,
        out_shape=[jax.ShapeDtypeStruct(x.shape, f32)] + cast_shapes,
        scratch_shapes=[pltpu.VMEM((tm, D_FF), bf16)],
        compiler_params=pltpu.CompilerParams(
            dimension_semantics=("arbitrary", "arbitrary"), vmem_limit_bytes=VMEM_LIMIT_BYTES),
        name="ffn",
    )(x, mod, w_up, wd, ln_g, ln_b, *to_cast)
    return out if to_cast else out[0]


def _head_rms_norm(x, ones_bd, gain, parts):
    ss = _dot01(ones_bd, x * x, parts, mat_on_left=False)
    return x * lax.rsqrt(ss * (1.0 / HEAD_DIM) + RMS_EPS) * gain


def _rope(x, cos, sin_signed):
    width = x.shape[-1]
    lane = lax.broadcasted_iota(jnp.int32, x.shape, 1)
    from_right = pltpu.roll(x, width - ROPE_PAIR, 1)
    from_left = pltpu.roll(x, ROPE_PAIR, 1)
    partner = jnp.where((lane & (2 * ROPE_PAIR - 1)) < ROPE_PAIR, from_right, from_left)
    return x * cos + partner * sin_signed


def _proj_kernel(*refs, rope):
    if rope:
        (x_ref, m_ref, w_ref, lbf_ref, lbb_ref, qg_ref, kg_ref, bd_ref, cos_ref, sin_ref,
         qs_ref, vh_ref, ff_ref, fb_ref, sg_ref, qa_ref, ka_ref, vt_ref,
         spf_ref, spb_ref) = refs
    else:
        (x_ref, m_ref, w_ref, lbf_ref, lbb_ref, qg_ref, kg_ref, bd_ref,
         qs_ref, vh_ref, ff_ref, fb_ref, sg_ref, qa_ref, ka_ref, vt_ref,
         spf_ref, spb_ref, kt_ref) = refs
    shift = m_ref[0, 3:4, :]
    scale = m_ref[0, 4:5, :]
    subs = [slice(lo, lo + PROJ_SUB) for lo in range(0, x_ref.shape[1], PROJ_SUB)]
    hs = [(x_ref[0, rows] * (1.0 + scale) + shift).astype(bf16) for rows in subs]

    def cols(h, lo, width):
        return jnp.dot(h, w_ref[:, lo:lo + width], preferred_element_type=f32)

    def lower_bound(lb_ref):
        l0 = lb_ref[0:1, :]
        l1 = lb_ref[1:2, :]
        m = jnp.maximum(l0, l1)
        e0 = jnp.exp(l0 - m)
        e1 = jnp.exp(l1 - m)
        return e0 / (e0 + e1)

    half = HG_FAST_BLOCK // 2

    def forget_gate(lb_ref, lo, f_ref, span_ref):
        lb = lower_bound(lb_ref)
        for rows, h in zip(subs, hs):
            f = lb + (1.0 - lb) * jax.nn.sigmoid(cols(h, lo, HG_WIDTH))
            f_ref[0, rows] = f
            sums = jnp.sum(jnp.log2(f).reshape(PROJ_SUB // half, half, HG_WIDTH), axis=1)
            span_ref[0, rows.start // half:rows.stop // half] = jnp.broadcast_to(
                jnp.max(jnp.abs(sums), axis=-1, keepdims=True), (PROJ_SUB // half, LANES))

    base = 5 * HG_WIDTH
    piece = vt_ref.shape[2]
    for rows, h in zip(subs, hs):
        q = _head_rms_norm(cols(h, base, ATT_WIDTH), bd_ref[...], qg_ref[...], 1)
        k = _head_rms_norm(cols(h, base + ATT_WIDTH, KV_WIDTH), bd_ref[0:KV_WIDTH, 0:KV_WIDTH], kg_ref[...], 2)
        if rope:
            cos, sin = cos_ref[rows, :], sin_ref[rows, :]
            q = _rope(q, jnp.concatenate([cos] * (ATT_WIDTH // LANES), axis=1),
                      jnp.concatenate([sin] * (ATT_WIDTH // LANES), axis=1))
            k = _rope(k, cos, sin)
        qa_ref[0, rows] = (q * (HEAD_DIM ** -0.5 * LOG2_E)).astype(bf16)
        ka_ref[0, rows] = k
        v = cols(h, base + ATT_WIDTH + KV_WIDTH, KV_WIDTH)
        for lo in range(rows.start, rows.stop, min(piece, PROJ_SUB)):
            n = min(piece, PROJ_SUB)
            dst = (lo // piece, slice(None), slice(lo % piece, lo % piece + n))
            vt_ref[dst] = v[lo - rows.start:lo - rows.start + n].T
            if not rope:
                kt_ref[dst] = k[lo - rows.start:lo - rows.start + n].T

    for rows, h in zip(subs, hs):
        qs_ref[0, rows] = _silu(cols(h, 0, HG_WIDTH))
    for rows, h in zip(subs, hs):
        vh_ref[0, rows] = cols(h, HG_WIDTH, HG_WIDTH).astype(bf16)
    forget_gate(lbf_ref, 2 * HG_WIDTH, ff_ref, spf_ref)
    forget_gate(lbb_ref, 3 * HG_WIDTH, fb_ref, spb_ref)
    for rows, h in zip(subs, hs):
        sg_ref[0, rows] = _silu(cols(h, 4 * HG_WIDTH, HG_WIDTH))


def _proj(x, mod, mod_group, w_in, lb_f, lb_b, q_gain, k_gain, ones_bd, rope_tables, seq_len):
    groups, length, _ = x.shape
    tm = TOKEN_TILE
    rope = rope_tables is not None
    n_seq = groups * length // seq_len
    if seq_len >= tm:
        tiles_per_seq = seq_len // tm
        t_spec = pl.BlockSpec((1, KV_WIDTH, tm), lambda g, t: (g * (length // seq_len) + t // tiles_per_seq, 0,
                                                               t % tiles_per_seq))
    else:
        t_spec = pl.BlockSpec((tm // seq_len, KV_WIDTH, seq_len), lambda g, t: (g * (length // tm) + t, 0, 0))
    t_shape = jax.ShapeDtypeStruct((n_seq, KV_WIDTH, seq_len), f32)
    tok = lambda width: pl.BlockSpec((1, tm, width), lambda g, t: (g, t, 0))
    in_specs = [
        tok(D_MODEL),
        pl.BlockSpec((1, N_MOD, D_MODEL), lambda g, t: (mod_group(g), 0, 0)),
        _resident((D_MODEL, IN_WIDTH)),
        _resident((2, HG_WIDTH)),
        _resident((2, HG_WIDTH)),
        _resident((1, ATT_WIDTH)),
        _resident((1, KV_WIDTH)),
        _resident((ATT_WIDTH, ATT_WIDTH)),
    ]
    args = [x, mod, w_in, lb_f, lb_b, q_gain, k_gain, ones_bd]
    if rope:
        in_specs += [pl.BlockSpec((tm, LANES), lambda g, t: (t, 0))] * 2
        args += list(rope_tables)
    shape = lambda width, dt: jax.ShapeDtypeStruct((groups, length, width), dt)
    half = HG_FAST_BLOCK // 2
    span_spec = pl.BlockSpec((1, tm // half, LANES), lambda g, t: (g, t, 0))
    span_shape = jax.ShapeDtypeStruct((groups, length // half, LANES), f32)
    return pl.pallas_call(
        functools.partial(_proj_kernel, rope=rope),
        grid=(groups, length // tm),
        in_specs=in_specs,
        out_specs=([tok(HG_WIDTH)] * 5 + [tok(ATT_WIDTH), tok(KV_WIDTH), t_spec, span_spec, span_spec]
                   + ([] if rope else [t_spec])),
        out_shape=([shape(HG_WIDTH, f32), shape(HG_WIDTH, bf16)] + [shape(HG_WIDTH, f32)] * 3
                   + [shape(ATT_WIDTH, bf16), shape(KV_WIDTH, f32), t_shape, span_shape, span_shape]
                   + ([] if rope else [t_shape])),
        compiler_params=pltpu.CompilerParams(
            dimension_semantics=("arbitrary", "arbitrary"), vmem_limit_bytes=VMEM_LIMIT_BYTES),
        name="mixer_proj",
    )(*args)


def _hgrn_direction(qs, vb, f, st_ref, o_ref, rows, *, rev):
    tc = qs.shape[0]
    lf = jnp.log2(f)
    kk = 1.0 - f
    row = lax.broadcasted_iota(jnp.int32, (tc, tc), 0)
    col = lax.broadcasted_iota(jnp.int32, (tc, tc), 1)
    ordered = (row < col) if rev else (row > col)
    differ = row ^ col
    tri = jnp.where((col >= row) if rev else (col <= row), 1.0, 0.0).astype(bf16)
    cum = _dot01(tri, lf, 2, mat_on_left=True)
    total = cum[0:1] if rev else cum[tc - 1:tc]
    carry = jnp.exp2(total)
    query_half = 0 if rev else 1
    heads = [slice(hd * HG_HEAD_DIM, (hd + 1) * HG_HEAD_DIM) for hd in range(HG_HEADS)]

    def level_exponent(s, ridx):
        if s == 1:
            return jnp.where((ridx & 1) == query_half, lf, 0.0)
        if s == 2:
            nxt = pltpu.roll(lf, tc - 1, 0)
            prv = pltpu.roll(lf, 1, 0)
            m4 = ridx & 3
            if rev:
                return jnp.where(m4 == 0, lf + nxt, jnp.where(m4 == 1, lf, jnp.where(m4 == 2, 0.0, prv)))
            return jnp.where(m4 == 0, nxt, jnp.where(m4 == 1, 0.0, jnp.where(m4 == 2, lf, lf + prv)))
        blocks = []
        for lo in range(0, tc, 2 * s):
            anchor = lo + (s if rev else s - 1)
            blocks.append(cum[lo:lo + 2 * s] - cum[anchor:anchor + 1])
        d = jnp.concatenate(blocks, axis=0)
        is_query = ((ridx >> (s.bit_length() - 1)) & 1) == query_half
        return jnp.where(is_query, d, -d)

    def add_levels(intra, first):
        ridx = lax.broadcasted_iota(jnp.int32, qs.shape, 0)
        s = first
        while s < tc:
            shift = s.bit_length() - 1
            if 2 * s == tc:
                lower, upper = slice(0, s), slice(s, tc)
                anchor = cum[s:s + 1] if rev else cum[s - 1:s]
                q_rows, k_rows = (lower, upper) if rev else (upper, lower)
                halves = {q_rows: qs[q_rows] * jnp.exp2(cum[q_rows] - anchor),
                          k_rows: kk[k_rows] * jnp.exp2(anchor - cum[k_rows])}
                mixed = jnp.concatenate([halves[lower], halves[upper]], axis=0).astype(bf16)
            else:
                is_query = ((ridx >> shift) & 1) == query_half
                mixed = (jnp.where(is_query, qs, kk) * jnp.exp2(level_exponent(s, ridx))).astype(bf16)
            pair = jnp.logical_and((differ >> shift) == 1, ordered)
            for hd, hs in enumerate(heads):
                intra[hd] = jnp.where(pair, _dot_nt(mixed[:, hs], mixed[:, hs]), intra[hd])
            s *= 2
        return intra

    def finish(intra, q_dec, k_end, extra):
        for hd, hs in enumerate(heads):
            st = st_ref[hd]
            o = jnp.dot(jnp.concatenate([intra[hd].astype(bf16), q_dec[:, hs]], axis=1),
                        jnp.concatenate([vb[:, hs], st.T.astype(bf16)], axis=0), preferred_element_type=f32)
            o_ref[rows, hs] = o if extra is None else o + extra[:, hs]
            st_ref[hd] = st * carry[:, hs] + _dot_tn(vb[:, hs], k_end[:, hs])

    blk = HG_FAST_BLOCK
    anchors = [lo + (blk // 2 if rev else blk // 2 - 1) for lo in range(0, tc, blk)]

    def mid_split_blocks():
        q_mid, k_mid, q_dec, k_end = [], [], [], []
        for lo, a in zip(range(0, tc, blk), anchors):
            rel = cum[lo:lo + blk] - cum[a:a + 1]
            q_mid.append(qs[lo:lo + blk] * jnp.exp2(rel))
            k_mid.append(kk[lo:lo + blk] * jnp.exp2(-rel))
            q_dec.append(q_mid[-1] * jnp.exp2(cum[a:a + 1]))
            k_end.append(k_mid[-1] * jnp.exp2(total - cum[a:a + 1]))
        q_mid, k_mid, q_dec, k_end = (jnp.concatenate(p, axis=0).astype(bf16) for p in (q_mid, k_mid, q_dec, k_end))
        same_block = (differ >> (blk.bit_length() - 1)) == 0
        keep = jnp.logical_and(same_block, (row <= col) if rev else (row >= col))
        intra = [jnp.where(keep, _dot_nt(q_mid[:, hs], k_mid[:, hs]), 0.0) for hs in heads]
        finish(add_levels(intra, blk), q_dec, k_end, None)

    def all_levels():
        intra = add_levels([jnp.zeros((tc, tc), f32)] * HG_HEADS, 1)
        q_dec = (qs * jnp.exp2(cum)).astype(bf16)
        k_end = (kk * jnp.exp2(total - cum)).astype(bf16)
        qk = qs * kk
        own = jnp.concatenate(
            [jnp.sum(qk[:, hs], axis=-1, keepdims=True) * vb[:, hs].astype(f32) for hs in heads], axis=-1)
        finish(intra, q_dec, k_end, own)

    return mid_split_blocks, all_levels


def _hgrn_kernel(slow_ref, *refs, has_init):
    if has_init:
        (qsf_ref, vf_ref, ff_ref, qsb_ref, vb_ref, fb_ref, s0f_ref, s0b_ref,
         of_ref, ob_ref, stf_ref, stb_ref) = refs
    else:
        (qsf_ref, vf_ref, ff_ref, qsb_ref, vb_ref, fb_ref,
         of_ref, ob_ref, sf_out_ref, sb_out_ref, stf_ref, stb_ref) = refs
    t = pl.program_id(1)
    n_seq = qsf_ref.shape[0]

    @pl.when(t == 0)
    def _():
        for b in range(n_seq):
            for hd in range(HG_HEADS):
                if has_init:
                    stf_ref[b, hd] = s0f_ref[b, hd].T
                    stb_ref[b, hd] = s0b_ref[b, hd].T
                else:
                    stf_ref[b, hd] = jnp.zeros((HG_HEAD_DIM, HG_HEAD_DIM), f32)
                    stb_ref[b, hd] = jnp.zeros((HG_HEAD_DIM, HG_HEAD_DIM), f32)

    chunks = [slice(lo, lo + HG_TILE) for lo in range(0, qsf_ref.shape[1], HG_TILE)]
    scans = []
    for r_f, r_b in zip(chunks, reversed(chunks)):
        for b in range(n_seq):
            scans.append(_hgrn_direction(qsf_ref[b, r_f], vf_ref[b, r_f], ff_ref[b, r_f], stf_ref.at[b], of_ref.at[b],
                                         r_f, rev=False))
            scans.append(_hgrn_direction(qsb_ref[b, r_b], vb_ref[b, r_b], fb_ref[b, r_b], stb_ref.at[b], ob_ref.at[b],
                                         r_b, rev=True))
    slow = slow_ref[pl.program_id(0) * pl.num_programs(1) + t] != 0

    @pl.when(jnp.logical_not(slow))
    def _():
        for mid_split_blocks, _ in scans:
            mid_split_blocks()

    @pl.when(slow)
    def _():
        for _, all_levels in scans:
            all_levels()

    if not has_init:
        @pl.when(t == pl.num_programs(1) - 1)
        def _():
            for b in range(n_seq):
                for hd in range(HG_HEADS):
                    sf_out_ref[b, hd] = stf_ref[b, hd].T
                    sb_out_ref[b, hd] = stb_ref[b, hd].T


def _hgrn(qs, vh, f_fwd, f_bwd, span_fwd, span_bwd, init_states):
    n_seqs, length, _ = qs.shape
    step = min(HG_STEP, length)
    nt = length // step
    groups = n_seqs // HG_SEQS
    has_init = init_states is not None
    per_step = lambda span: jnp.max(span[:, :, 0].reshape(n_seqs, nt, -1), axis=-1) > HG_FAST_SPAN_LOG2
    slow = jnp.logical_or(per_step(span_fwd), per_step(span_bwd)[:, ::-1])
    slow = jnp.any(slow.reshape(groups, HG_SEQS, nt), axis=1).astype(jnp.int32).reshape(-1)
    fwd = pl.BlockSpec((HG_SEQS, step, HG_WIDTH), lambda g, t, _: (g, t, 0))
    bwd = pl.BlockSpec((HG_SEQS, step, HG_WIDTH), lambda g, t, _: (g, nt - 1 - t, 0))
    state = pl.BlockSpec((HG_SEQS, HG_HEADS, HG_HEAD_DIM, HG_HEAD_DIM), lambda g, t, _: (g, 0, 0, 0))
    in_specs = [fwd, fwd, fwd, bwd, bwd, bwd]
    args = [qs, vh, f_fwd, qs, vh, f_bwd]
    out_specs = [fwd, bwd]
    out_shape = [jax.ShapeDtypeStruct(qs.shape, f32)] * 2
    if has_init:
        in_specs += [state, state]
        args += list(init_states)
    else:
        out_specs += [state, state]
        out_shape += [jax.ShapeDtypeStruct((n_seqs, HG_HEADS, HG_HEAD_DIM, HG_HEAD_DIM), f32)] * 2
    return pl.pallas_call(
        functools.partial(_hgrn_kernel, has_init=has_init),
        grid_spec=pltpu.PrefetchScalarGridSpec(
            num_scalar_prefetch=1,
            grid=(groups, nt),
            in_specs=in_specs,
            out_specs=out_specs,
            scratch_shapes=[pltpu.VMEM((HG_SEQS, HG_HEADS, HG_HEAD_DIM, HG_HEAD_DIM), f32)] * 2),
        out_shape=out_shape,
        compiler_params=pltpu.CompilerParams(
            dimension_semantics=("arbitrary", "arbitrary"), vmem_limit_bytes=VMEM_LIMIT_BYTES),
        name="hgrn2",
    )(slow, *args)


def _attn_kernel(*refs, n_ctx, n_cast):
    kk_ref, vta_ref, bound_ref, fixed_ref = refs[-4:]
    o_ref, *cast_out = refs[len(refs) - 5 - n_cast:-4]
    inputs = refs[:len(refs) - 5 - n_cast]
    cast_in = inputs[len(inputs) - n_cast:]
    if n_ctx:
        (qa_ref, ka_ref, vt_ref, ckt_ref, cvt_ref, x_ref, of_ref, ob_ref, sg_ref, m_ref, wo_ref, hg_ref, qg_ref,
         g_ref, b_ref) = inputs[:len(inputs) - n_cast]
    else:
        (qa_ref, ka_ref, vt_ref, x_ref, of_ref, ob_ref, sg_ref, m_ref, wo_ref, hg_ref, qg_ref,
         g_ref, b_ref) = inputs[:len(inputs) - n_cast]
    for src_ref, dst_ref in zip(cast_in, cast_out):
        dst_ref[...] = src_ref[...].astype(bf16)
    tq = qa_ref.shape[1]
    n_keys = kk_ref.shape[0]

    @pl.when(pl.program_id(1) == 0)
    def _():
        def fill_keys(lo, k):
            n = k.shape[0]
            low = lax.broadcasted_iota(jnp.int32, k.shape, 1) < HEAD_DIM
            k_sw = pltpu.roll(k, HEAD_DIM, 1)
            kk_ref[lo:lo + n, 0:LANES] = jnp.where(low, k, k_sw).astype(bf16)
            kk_ref[lo:lo + n, LANES:2 * LANES] = jnp.where(low, k_sw, k).astype(bf16)

        def fill_values(lo, vt):
            n = vt.shape[1]
            ones_row = jnp.where(lax.broadcasted_iota(jnp.int32, (V_ROWS - HEAD_DIM, n), 0) == 0, 1.0, 0.0)
            for kv in range(N_KV_HEADS):
                vta_ref[kv * V_ROWS:kv * V_ROWS + HEAD_DIM, lo:lo + n] = (
                    vt[kv * HEAD_DIM:(kv + 1) * HEAD_DIM].astype(bf16))
                vta_ref[kv * V_ROWS + HEAD_DIM:(kv + 1) * V_ROWS, lo:lo + n] = ones_row.astype(bf16)

        if n_ctx:
            fill_keys(0, ckt_ref[0].T)
            fill_values(0, cvt_ref[0])
        fill_keys(n_ctx, ka_ref[0])
        fill_values(n_ctx, vt_ref[0])

        gain_max = jnp.max(jnp.abs(qg_ref[...]), axis=1, keepdims=True)
        all_small = None
        for kv in range(N_KV_HEADS):
            kt = kk_ref[:, kv * LANES:(kv + 1) * LANES].astype(f32)
            k_norm2 = jnp.max(0.5 * jnp.sum(kt * kt, axis=1, keepdims=True), axis=0, keepdims=True)
            bound = (LOG2_E * SCORE_BOUND_SLACK) * gain_max * jnp.sqrt(k_norm2)
            bound_ref[kv] = bound[0, 0]
            small = jnp.where(bound <= SCORE_BOUND_LIMIT, 1, 0)
            all_small = small if all_small is None else all_small * small
        fixed_ref[0] = all_small[0, 0]

    ts = min(Q_SUB, tq)
    low = lax.broadcasted_iota(jnp.int32, (ts, LANES), 1) < HEAD_DIM
    kc = min(KEY_CHUNK, n_keys)
    pairs_per_kv = N_HEADS // N_KV_HEADS // 2

    def attend(rows, fixed_shift):
        def masked_pair(tile):
            qp = qa_ref[0, rows, tile * LANES:(tile + 1) * LANES]
            zero = jnp.zeros_like(qp)
            return jnp.concatenate([jnp.where(low, qp, zero), jnp.where(low, zero, qp)], axis=0)

        q_pairs = [masked_pair(tile) for tile in range(N_HEADS // 2)]
        work = [(tile, lo) for tile in range(N_HEADS // 2) for lo in range(0, n_keys, kc)]

        def scores(tile, lo):
            kv = tile // pairs_per_kv
            return _dot_nt(kk_ref[lo:lo + kc, kv * LANES:(kv + 1) * LANES], q_pairs[tile])

        heads_t = []
        st_next = scores(*work[0])
        m = acc = None
        for i, (tile, lo) in enumerate(work):
            st = st_next
            if i + 1 < len(work):
                st_next = scores(*work[i + 1])
            kv = tile // pairs_per_kv
            values_t = vta_ref[kv * V_ROWS:(kv + 1) * V_ROWS, lo:lo + kc]
            if fixed_shift:
                e = jnp.exp2(st - bound_ref[kv]).astype(bf16)
                pv = jnp.dot(values_t, e, preferred_element_type=f32)
                acc = pv if acc is None else acc + pv
            else:
                m_chunk = jnp.max(st, axis=0, keepdims=True)
                m_new = m_chunk if m is None else jnp.maximum(m, m_chunk)
                e = jnp.exp2(st - m_new).astype(bf16)
                pv = jnp.dot(values_t, e, preferred_element_type=f32)
                acc = pv if acc is None else acc * jnp.exp2(m - m_new) + pv
                m = m_new
            if lo + kc == n_keys:
                on = (acc[0:HEAD_DIM] * (1.0 / acc[HEAD_DIM:HEAD_DIM + 1])).astype(bf16)
                heads_t += [on[:, :ts], on[:, ts:]]
                m = acc = None
        return jnp.concatenate(heads_t, axis=0)

    def project(rows, o_att_t):
        o_sum = of_ref[0, rows] + ob_ref[0, rows]
        normed = []
        for hd in range(HG_HEADS):
            oh = o_sum[:, hd * HG_HEAD_DIM:(hd + 1) * HG_HEAD_DIM]
            ms = jnp.mean(oh * oh, axis=-1, keepdims=True)
            normed.append(oh * lax.rsqrt(ms + RMS_EPS))
        o_hg = (jnp.concatenate(normed, axis=-1) * hg_ref[...] * sg_ref[0, rows]).astype(bf16)
        y = (jnp.dot(o_hg, wo_ref[0:HG_WIDTH, :], preferred_element_type=f32)
             + _dot_tn(o_att_t, wo_ref[HG_WIDTH:HG_WIDTH + ATT_WIDTH, :]))
        r = ALPHA * x_ref[0, rows] + m_ref[0, 5:6, :] * y
        o_ref[0, rows] = _layer_norm(r, g_ref[1:2, :], b_ref[1:2, :])

    def run(fixed_shift):
        for lo in range(0, tq, ts):
            rows = slice(lo, lo + ts)
            project(rows, attend(rows, fixed_shift))

    use_bound = fixed_ref[0] != 0
    pl.when(use_bound)(functools.partial(run, True))
    pl.when(jnp.logical_not(use_bound))(functools.partial(run, False))


def _attn(qa, ka, vt, ctx_kv_t, x, o_f, o_b, sg, mod, mod_group, w_out, hg_gain, q_gain, ln_g, ln_b, to_cast=()):
    groups, length, _ = qa.shape
    tq = min(Q_TILE, length)
    nt = length // tq
    n_ctx = 0 if ctx_kv_t is None else ctx_kv_t[0].shape[2]
    n_keys = n_ctx + length
    tok = lambda width: pl.BlockSpec((1, tq, width), lambda g, t: (g, t, 0))
    whole_t = lambda n: pl.BlockSpec((1, KV_WIDTH, n), lambda g, t: (g, 0, 0))
    in_specs = [tok(ATT_WIDTH), pl.BlockSpec((1, length, KV_WIDTH), lambda g, t: (g, 0, 0)), whole_t(length)]
    args = [qa, ka, vt]
    if n_ctx:
        in_specs += [whole_t(n_ctx), whole_t(n_ctx)]
        args += list(ctx_kv_t)
    in_specs += [
        tok(D_MODEL), tok(HG_WIDTH), tok(HG_WIDTH), tok(HG_WIDTH),
        pl.BlockSpec((1, N_MOD, D_MODEL), lambda g, t: (mod_group(g), 0, 0)),
        _resident((D_MODEL, D_MODEL)),
        _resident((1, HG_WIDTH)),
        _resident((1, ATT_WIDTH)),
        _resident((3, D_MODEL)),
        _resident((3, D_MODEL)),
    ]
    args += [x, o_f, o_b, sg, mod, w_out, hg_gain, q_gain, ln_g, ln_b]
    cast_specs, cast_shapes = _cast_specs(to_cast, nt, groups * nt)
    in_specs += cast_specs
    args += list(to_cast)
    out_specs = [tok(D_MODEL)] + cast_specs
    out_shape = [jax.ShapeDtypeStruct(x.shape, f32)] + cast_shapes
    return pl.pallas_call(
        functools.partial(_attn_kernel, n_ctx=n_ctx, n_cast=len(to_cast)),
        grid=(groups, nt),
        in_specs=in_specs,
        out_specs=out_specs,
        out_shape=out_shape,
        scratch_shapes=[pltpu.VMEM((n_keys, N_KV_HEADS * LANES), bf16),
                        pltpu.VMEM((N_KV_HEADS * V_ROWS, n_keys), bf16),
                        pltpu.SMEM((N_KV_HEADS,), f32),
                        pltpu.SMEM((1,), jnp.int32)],
        compiler_params=pltpu.CompilerParams(
            dimension_semantics=("arbitrary", "arbitrary"), vmem_limit_bytes=VMEM_LIMIT_BYTES),
        name="attn_out",
    )(*args)


def _rope_tables(n_tokens):
    half = HEAD_DIM // 2
    t = jnp.arange(n_tokens)
    inv = ROPE_THETA ** (-jnp.arange(0, half, 2, dtype=f32) / half)
    ang_row = (t // GRID_W).astype(f32)[:, None] * inv
    ang_col = (t % GRID_W).astype(f32)[:, None] * inv
    cos = jnp.concatenate([jnp.cos(ang_row)] * 2 + [jnp.cos(ang_col)] * 2, axis=-1)
    sin = jnp.concatenate([-jnp.sin(ang_row), jnp.sin(ang_row), -jnp.sin(ang_col), jnp.sin(ang_col)], axis=-1)
    return jnp.tile(cos, (1, LANES // HEAD_DIM)), jnp.tile(sin, (1, LANES // HEAD_DIM))


def kernel(x_prompt, x_sample, cache_k, cache_v, state_hgrn_fwd, state_hgrn_bwd, c, c_ctx, w_mod, b_mod,
           w_ffn1_in, w_ffn1_out, w_ffn2_in, w_ffn2_out, w_in, w_out, q_norm_g, k_norm_g, hg_norm_g,
           lb_logits_fwd, lb_logits_bwd, ln_g, ln_b):
    assert w_mod.shape[0] == DEPTH and lb_logits_fwd.shape[0] == DEPTH + 1
    batch, seq, _ = x_prompt.shape
    dec_batch, dec_seq, _ = x_sample.shape
    past = cache_k.shape[2]

    ctx_row = dec_batch
    rows = -(-(dec_batch + 1) // BF16_SUBLANES) * BF16_SUBLANES
    cvecs = jnp.concatenate([c, c_ctx[None, :], jnp.zeros((rows - dec_batch - 1, D_MODEL), f32)], axis=0)
    mod = _modulation(cvecs, w_mod[0], b_mod[0]).reshape(rows, N_MOD, D_MODEL)

    to_bf16 = lambda w: w.astype(bf16)
    w1u, w1d = to_bf16(w_ffn1_in[0]), to_bf16(w_ffn1_out[0])
    gains = ln_g[0], ln_b[0]
    q_gain = jnp.tile(q_norm_g[0], N_HEADS).reshape(1, ATT_WIDTH)
    k_gain = jnp.tile(k_norm_g[0], N_KV_HEADS).reshape(1, KV_WIDTH)
    hg_gain = hg_norm_g[0].reshape(1, HG_WIDTH)
    head_of = jnp.arange(ATT_WIDTH) // HEAD_DIM
    ones_bd = (head_of[:, None] == head_of[None, :]).astype(bf16)

    def mixer_sublayer(x, mod_group, rope_tables, ctx_kv_t, init_states, hg_groups, to_cast=()):
        shape = x.shape
        seq_len = shape[0] * shape[1] // hg_groups
        per_seq = lambda a: a.reshape(hg_groups, seq_len, a.shape[-1])
        proj = _proj(x, mod, mod_group, w_in_b, lb_logits_fwd, lb_logits_bwd, q_gain, k_gain, ones_bd, rope_tables,
                     seq_len)
        qs, vh, f_f, f_b, sg, qa, ka = map(per_seq, proj[:7])
        vt, span_f, span_b = proj[7], proj[8].reshape(hg_groups, -1, LANES), proj[9].reshape(hg_groups, -1, LANES)
        scans = _hgrn(qs, vh, f_f, f_b, span_f, span_b, init_states)
        x, *cast = _attn(qa, ka, vt, ctx_kv_t, per_seq(x), scans[0], scans[1], sg, mod, mod_group, w_out_b, hg_gain,
                         q_gain, *gains, to_cast=to_cast)
        return x.reshape(shape), proj[10:], vt, scans[2:], cast

    from_cache = lambda t: t[:, 0].transpose(0, 2, 3, 1).reshape(dec_batch, KV_WIDTH, past)
    init_states = (state_hgrn_fwd[:, 0], state_hgrn_bwd[:, 0])
    latent_group = lambda g: g
    x_latent, w_in_b, w_out_b = _ffn(x_sample, mod, latent_group, w1u, w1d, *gains, mod_base=0, ln_row=0,
                                     to_cast=(w_in[0], w_out[0]))
    x_latent, _, _, _, (w2u, w2d) = mixer_sublayer(
        x_latent, latent_group, _rope_tables(dec_seq), (from_cache(cache_k), from_cache(cache_v)), init_states,
        dec_batch, to_cast=(w_ffn2_in[0], w_ffn2_out[0]))

    ctx_group = lambda g: ctx_row
    x_ctx = _ffn(x_prompt.reshape(1, batch * seq, D_MODEL), mod, ctx_group, w1u, w1d, *gains, mod_base=0, ln_row=0)
    x_ctx, (kt_new,), vt_new, states, _ = mixer_sublayer(x_ctx, ctx_group, None, None, None, batch)
    y_sample = _ffn(x_latent, mod, latent_group, w2u, w2d, *gains, mod_base=6, ln_row=2)
    y_prompt = _ffn(x_ctx, mod, ctx_group, w2u, w2d, *gains, mod_base=6, ln_row=2)
    y_prompt = y_prompt.reshape(batch, seq, D_MODEL)
    to_cache = lambda t: t.reshape(batch, DEPTH, N_KV_HEADS, HEAD_DIM, seq).transpose(0, 1, 4, 2, 3)
    new_cache_k, new_cache_v = to_cache(kt_new), to_cache(vt_new)
    new_state_fwd = states[0].reshape(batch, DEPTH, HG_HEADS, HG_HEAD_DIM, HG_HEAD_DIM)
    new_state_bwd = states[1].reshape(batch, DEPTH, HG_HEADS, HG_HEAD_DIM, HG_HEAD_DIM)

    return (y_prompt, y_sample, new_cache_k, new_cache_v, new_state_fwd, new_state_bwd)
```

```python
import functools

import jax
import jax.numpy as jnp
from jax import lax
from jax.experimental import pallas as pl
from jax.experimental.pallas import tpu as pltpu

f32 = jnp.float32
bf16 = jnp.bfloat16

D_MODEL = 1024
N_MOD = 9
HG_WIDTH = 512
HG_HEAD_DIM = 128
HG_HEADS = 4
ATT_WIDTH = 512
HEAD_DIM = 64
N_HEADS = 8
N_KV_HEADS = 2
KV_WIDTH = 128
IN_WIDTH = 5 * HG_WIDTH + ATT_WIDTH + 2 * KV_WIDTH
D_FF = 2816
GRID_W = 64
ROPE_PAIR = HEAD_DIM // 4
ROPE_THETA = 10000.0
DEPTH = 1
ALPHA = (2.0 * DEPTH) ** 0.25
LOG2_E = 1.4426950408889634
LN_EPS = 1e-6
RMS_EPS = 1e-6

LANES = 128
BF16_SUBLANES = 16
VMEM_LIMIT_BYTES = 56 * 1024 * 1024

FF_CHUNK = 256
FFN_TILE = 1024
FFN_SUB = 512
TOKEN_TILE = 512
PROJ_SUB = 256
HG_TILE = 128
HG_STEP = 512
HG_SEQS = 2
Q_TILE = 512
Q_SUB = 256
KEY_CHUNK = 512
SCORE_BOUND_LIMIT = 40.0
SCORE_BOUND_SLACK = 1.01
V_ROWS = 80
HG_FAST_BLOCK = 64
HG_FAST_SPAN_LOG2 = 100.0


def _silu(x):
    return x * jax.nn.sigmoid(x)


def _layer_norm(r, g, b):
    mu = jnp.mean(r, axis=-1, keepdims=True)
    c = r - mu
    var = jnp.mean(c * c, axis=-1, keepdims=True)
    return c * lax.rsqrt(var + LN_EPS) * g + b


def _split_bf16(x, parts):
    out = []
    r = x
    for _ in range(parts - 1):
        p = r.astype(bf16)
        out.append(p)
        r = r - p.astype(f32)
    out.append(r.astype(bf16))
    return out


def _dot01(mat01, x, parts, *, mat_on_left):
    pieces = _split_bf16(x, parts)
    if mat_on_left:
        return jnp.dot(jnp.concatenate([mat01] * parts, axis=1), jnp.concatenate(pieces, axis=0),
                       preferred_element_type=f32)
    return jnp.dot(jnp.concatenate(pieces, axis=1), jnp.concatenate([mat01] * parts, axis=0),
                   preferred_element_type=f32)


def _dot_nt(a, b):
    return lax.dot_general(a, b, (((1,), (1,)), ((), ())), preferred_element_type=f32)


def _dot_tn(a, b):
    return lax.dot_general(a, b, (((0,), (0,)), ((), ())), preferred_element_type=f32)


def _mod_kernel(c_ref, w_ref, b_ref, *refs):
    n_cast = len(refs) // 2
    cast_in, o_ref, cast_out = refs[:n_cast], refs[n_cast], refs[n_cast + 1:]
    for src_ref, dst_ref in zip(cast_in, cast_out):
        dst_ref[...] = src_ref[...].astype(bf16)
    a = _silu(c_ref[...]).astype(bf16)
    o_ref[...] = jnp.dot(a, w_ref[...].astype(bf16), preferred_element_type=f32) + b_ref[...]


def _modulation(cvecs, w_mod, b_mod, to_cast=()):
    rows = cvecs.shape[0]
    n_out = w_mod.shape[1]
    tn = D_MODEL
    nt = n_out // tn
    cast_specs, cast_shapes = _cast_specs(to_cast, nt, nt)
    return pl.pallas_call(
        _mod_kernel,
        grid=(1, nt),
        in_specs=[
            pl.BlockSpec((rows, D_MODEL), lambda g, j: (0, 0)),
            pl.BlockSpec((D_MODEL, tn), lambda g, j: (0, j)),
            pl.BlockSpec((1, tn), lambda g, j: (0, j)),
        ] + cast_specs,
        out_specs=[pl.BlockSpec((rows, tn), lambda g, j: (0, j))] + cast_specs,
        out_shape=[jax.ShapeDtypeStruct((rows, n_out), f32)] + cast_shapes,
        compiler_params=pltpu.CompilerParams(dimension_semantics=("arbitrary", "arbitrary")),
        name="modulation",
    )(cvecs, w_mod, b_mod.reshape(1, n_out), *to_cast)


def _ffn_kernel(x_ref, m_ref, wup_ref, wd_ref, g_ref, b_ref, *refs, mod_base, ln_row):
    n_cast = (len(refs) - 2) // 2
    cast_in, o_ref, cast_out, act_ref = refs[:n_cast], refs[n_cast], refs[n_cast + 1:-1], refs[-1]
    for src_ref, dst_ref in zip(cast_in, cast_out):
        dst_ref[...] = src_ref[...].astype(bf16)
    shift = m_ref[0, mod_base:mod_base + 1, :]
    scale = m_ref[0, mod_base + 1:mod_base + 2, :]
    gate = m_ref[0, mod_base + 2:mod_base + 3, :]
    subs = [slice(lo, lo + FFN_SUB) for lo in range(0, x_ref.shape[1], FFN_SUB)]
    xs = [x_ref[0, rows] for rows in subs]
    hs = [(x * (1.0 + scale) + shift).astype(bf16) for x in xs]
    for j in range(D_FF // FF_CHUNK):
        cols = slice(j * FF_CHUNK, (j + 1) * FF_CHUNK)
        for rows, h in zip(subs, hs):
            a = jnp.dot(h, wup_ref[:, cols], preferred_element_type=f32)
            u = jnp.dot(h, wup_ref[:, D_FF + j * FF_CHUNK:D_FF + (j + 1) * FF_CHUNK],
                        preferred_element_type=f32)
            act_ref[rows, cols] = (_silu(a) * u).astype(bf16)
    ys = [jnp.dot(act_ref[rows, :], wd_ref[...], preferred_element_type=f32) for rows in subs]
    for rows, x, y in zip(subs, xs, ys):
        r = ALPHA * x + 0.5 * gate * y
        o_ref[0, rows] = _layer_norm(r, g_ref[ln_row:ln_row + 1, :], b_ref[ln_row:ln_row + 1, :])


def _resident(shape):
    return pl.BlockSpec(shape, lambda *_: (0,) * len(shape), pipeline_mode=pl.Buffered(1))


def _cast_specs(to_cast, nt, steps):
    specs, shapes = [], []
    for w in to_cast:
        rows = next(r for r in range(BF16_SUBLANES, w.shape[0] + 1, BF16_SUBLANES)
                    if w.shape[0] % r == 0 and w.shape[0] // r <= steps)
        specs.append(pl.BlockSpec((rows, w.shape[1]),
                                  lambda g, t, last=w.shape[0] // rows - 1: (jnp.minimum(g * nt + t, last), 0)))
        shapes.append(jax.ShapeDtypeStruct(w.shape, bf16))
    return specs, shapes


def _ffn(x, mod, mod_group, w_up, wd, ln_g, ln_b, *, mod_base, ln_row, to_cast=()):
    groups, length, _ = x.shape
    tm = FFN_TILE
    nt = length // tm
    cast_specs, cast_shapes = _cast_specs(to_cast, nt, groups * nt)
    out = pl.pallas_call(
        functools.partial(_ffn_kernel, mod_base=mod_base, ln_row=ln_row),
        grid=(groups, nt),
        in_specs=[
            pl.BlockSpec((1, tm, D_MODEL), lambda g, t: (g, t, 0)),
            pl.BlockSpec((1, N_MOD, D_MODEL), lambda g, t: (mod_group(g), 0, 0)),
            _resident((D_MODEL, 2 * D_FF)),
            _resident((D_FF, D_MODEL)),
            _resident((3, D_MODEL)),
            _resident((3, D_MODEL)),
        ] + cast_specs,
        out_specs=[pl.BlockSpec((1, tm, D_MODEL), lambda g, t: (g, t, 0))] + cast_specs,
        out_shape=[jax.ShapeDtypeStruct(x.shape, f32)] + cast_shapes,
        scratch_shapes=[pltpu.VMEM((tm, D_FF), bf16)],
        compiler_params=pltpu.CompilerParams(
            dimension_semantics=("arbitrary", "arbitrary"), vmem_limit_bytes=VMEM_LIMIT_BYTES),
        name="ffn",
    )(x, mod, w_up, wd, ln_g, ln_b, *to_cast)
    return out if to_cast else out[0]


def _head_rms_norm(x, ones_bd, gain, parts):
    ss = _dot01(ones_bd, x * x, parts, mat_on_left=False)
    return x * lax.rsqrt(ss * (1.0 / HEAD_DIM) + RMS_EPS) * gain


def _rope(x, cos, sin_signed):
    width = x.shape[-1]
    lane = lax.broadcasted_iota(jnp.int32, x.shape, 1)
    from_right = pltpu.roll(x, width - ROPE_PAIR, 1)
    from_left = pltpu.roll(x, ROPE_PAIR, 1)
    partner = jnp.where((lane & (2 * ROPE_PAIR - 1)) < ROPE_PAIR, from_right, from_left)
    return x * cos + partner * sin_signed


def _proj_kernel(*refs, rope):
    if rope:
        (x_ref, m_ref, w_ref, lbf_ref, lbb_ref, qg_ref, kg_ref, bd_ref, cos_ref, sin_ref,
         qs_ref, vh_ref, ff_ref, fb_ref, sg_ref, qa_ref, ka_ref, vt_ref,
         spf_ref, spb_ref) = refs
    else:
        (x_ref, m_ref, w_ref, lbf_ref, lbb_ref, qg_ref, kg_ref, bd_ref,
         qs_ref, vh_ref, ff_ref, fb_ref, sg_ref, qa_ref, ka_ref, vt_ref,
         spf_ref, spb_ref, kt_ref) = refs
    shift = m_ref[0, 3:4, :]
    scale = m_ref[0, 4:5, :]
    subs = [slice(lo, lo + PROJ_SUB) for lo in range(0, x_ref.shape[1], PROJ_SUB)]
    hs = [(x_ref[0, rows] * (1.0 + scale) + shift).astype(bf16) for rows in subs]

    def cols(h, lo, width):
        return jnp.dot(h, w_ref[:, lo:lo + width], preferred_element_type=f32)

    def lower_bound(lb_ref):
        l0 = lb_ref[0:1, :]
        l1 = lb_ref[1:2, :]
        m = jnp.maximum(l0, l1)
        e0 = jnp.exp(l0 - m)
        e1 = jnp.exp(l1 - m)
        return e0 / (e0 + e1)

    half = HG_FAST_BLOCK // 2

    def forget_gate(lb_ref, lo, f_ref, span_ref):
        lb = lower_bound(lb_ref)
        for rows, h in zip(subs, hs):
            f = lb + (1.0 - lb) * jax.nn.sigmoid(cols(h, lo, HG_WIDTH))
            f_ref[0, rows] = f
            sums = jnp.sum(jnp.log2(f).reshape(PROJ_SUB // half, half, HG_WIDTH), axis=1)
            span_ref[0, rows.start // half:rows.stop // half] = jnp.broadcast_to(
                jnp.max(jnp.abs(sums), axis=-1, keepdims=True), (PROJ_SUB // half, LANES))

    base = 5 * HG_WIDTH
    piece = vt_ref.shape[2]
    for rows, h in zip(subs, hs):
        q = _head_rms_norm(cols(h, base, ATT_WIDTH), bd_ref[...], qg_ref[...], 1)
        k = _head_rms_norm(cols(h, base + ATT_WIDTH, KV_WIDTH), bd_ref[0:KV_WIDTH, 0:KV_WIDTH], kg_ref[...], 2)
        if rope:
            cos, sin = cos_ref[rows, :], sin_ref[rows, :]
            q = _rope(q, jnp.concatenate([cos] * (ATT_WIDTH // LANES), axis=1),
                      jnp.concatenate([sin] * (ATT_WIDTH // LANES), axis=1))
            k = _rope(k, cos, sin)
        qa_ref[0, rows] = (q * (HEAD_DIM ** -0.5 * LOG2_E)).astype(bf16)
        ka_ref[0, rows] = k
        v = cols(h, base + ATT_WIDTH + KV_WIDTH, KV_WIDTH)
        for lo in range(rows.start, rows.stop, min(piece, PROJ_SUB)):
            n = min(piece, PROJ_SUB)
            dst = (lo // piece, slice(None), slice(lo % piece, lo % piece + n))
            vt_ref[dst] = v[lo - rows.start:lo - rows.start + n].T
            if not rope:
                kt_ref[dst] = k[lo - rows.start:lo - rows.start + n].T

    for rows, h in zip(subs, hs):
        qs_ref[0, rows] = _silu(cols(h, 0, HG_WIDTH))
    for rows, h in zip(subs, hs):
        vh_ref[0, rows] = cols(h, HG_WIDTH, HG_WIDTH).astype(bf16)
    forget_gate(lbf_ref, 2 * HG_WIDTH, ff_ref, spf_ref)
    forget_gate(lbb_ref, 3 * HG_WIDTH, fb_ref, spb_ref)
    for rows, h in zip(subs, hs):
        sg_ref[0, rows] = _silu(cols(h, 4 * HG_WIDTH, HG_WIDTH))


def _proj(x, mod, mod_group, w_in, lb_f, lb_b, q_gain, k_gain, ones_bd, rope_tables, seq_len):
    groups, length, _ = x.shape
    tm = TOKEN_TILE
    rope = rope_tables is not None
    n_seq = groups * length // seq_len
    if seq_len >= tm:
        tiles_per_seq = seq_len // tm
        t_spec = pl.BlockSpec((1, KV_WIDTH, tm), lambda g, t: (g * (length // seq_len) + t // tiles_per_seq, 0,
                                                               t % tiles_per_seq))
    else:
        t_spec = pl.BlockSpec((tm // seq_len, KV_WIDTH, seq_len), lambda g, t: (g * (length // tm) + t, 0, 0))
    t_shape = jax.ShapeDtypeStruct((n_seq, KV_WIDTH, seq_len), f32)
    tok = lambda width: pl.BlockSpec((1, tm, width), lambda g, t: (g, t, 0))
    in_specs = [
        tok(D_MODEL),
        pl.BlockSpec((1, N_MOD, D_MODEL), lambda g, t: (mod_group(g), 0, 0)),
        _resident((D_MODEL, IN_WIDTH)),
        _resident((2, HG_WIDTH)),
        _resident((2, HG_WIDTH)),
        _resident((1, ATT_WIDTH)),
        _resident((1, KV_WIDTH)),
        _resident((ATT_WIDTH, ATT_WIDTH)),
    ]
    args = [x, mod, w_in, lb_f, lb_b, q_gain, k_gain, ones_bd]
    if rope:
        in_specs += [pl.BlockSpec((tm, LANES), lambda g, t: (t, 0))] * 2
        args += list(rope_tables)
    shape = lambda width, dt: jax.ShapeDtypeStruct((groups, length, width), dt)
    half = HG_FAST_BLOCK // 2
    span_spec = pl.BlockSpec((1, tm // half, LANES), lambda g, t: (g, t, 0))
    span_shape = jax.ShapeDtypeStruct((groups, length // half, LANES), f32)
    return pl.pallas_call(
        functools.partial(_proj_kernel, rope=rope),
        grid=(groups, length // tm),
        in_specs=in_specs,
        out_specs=([tok(HG_WIDTH)] * 5 + [tok(ATT_WIDTH), tok(KV_WIDTH), t_spec, span_spec, span_spec]
                   + ([] if rope else [t_spec])),
        out_shape=([shape(HG_WIDTH, f32), shape(HG_WIDTH, bf16)] + [shape(HG_WIDTH, f32)] * 3
                   + [shape(ATT_WIDTH, bf16), shape(KV_WIDTH, f32), t_shape, span_shape, span_shape]
                   + ([] if rope else [t_shape])),
        compiler_params=pltpu.CompilerParams(
            dimension_semantics=("arbitrary", "arbitrary"), vmem_limit_bytes=VMEM_LIMIT_BYTES),
        name="mixer_proj",
    )(*args)


def _hgrn_direction(qs, vb, f, st_ref, o_ref, rows, *, rev):
    tc = qs.shape[0]
    lf = jnp.log2(f)
    kk = 1.0 - f
    row = lax.broadcasted_iota(jnp.int32, (tc, tc), 0)
    col = lax.broadcasted_iota(jnp.int32, (tc, tc), 1)
    ordered = (row < col) if rev else (row > col)
    differ = row ^ col
    tri = jnp.where((col >= row) if rev else (col <= row), 1.0, 0.0).astype(bf16)
    cum = _dot01(tri, lf, 2, mat_on_left=True)
    total = cum[0:1] if rev else cum[tc - 1:tc]
    carry = jnp.exp2(total)
    query_half = 0 if rev else 1
    heads = [slice(hd * HG_HEAD_DIM, (hd + 1) * HG_HEAD_DIM) for hd in range(HG_HEADS)]

    def level_exponent(s, ridx):
        if s == 1:
            return jnp.where((ridx & 1) == query_half, lf, 0.0)
        if s == 2:
            nxt = pltpu.roll(lf, tc - 1, 0)
            prv = pltpu.roll(lf, 1, 0)
            m4 = ridx & 3
            if rev:
                return jnp.where(m4 == 0, lf + nxt, jnp.where(m4 == 1, lf, jnp.where(m4 == 2, 0.0, prv)))
            return jnp.where(m4 == 0, nxt, jnp.where(m4 == 1, 0.0, jnp.where(m4 == 2, lf, lf + prv)))
        blocks = []
        for lo in range(0, tc, 2 * s):
            anchor = lo + (s if rev else s - 1)
            blocks.append(cum[lo:lo + 2 * s] - cum[anchor:anchor + 1])
        d = jnp.concatenate(blocks, axis=0)
        is_query = ((ridx >> (s.bit_length() - 1)) & 1) == query_half
        return jnp.where(is_query, d, -d)

    def add_levels(intra, first):
        ridx = lax.broadcasted_iota(jnp.int32, qs.shape, 0)
        s = first
        while s < tc:
            shift = s.bit_length() - 1
            if 2 * s == tc:
                lower, upper = slice(0, s), slice(s, tc)
                anchor = cum[s:s + 1] if rev else cum[s - 1:s]
                q_rows, k_rows = (lower, upper) if rev else (upper, lower)
                halves = {q_rows: qs[q_rows] * jnp.exp2(cum[q_rows] - anchor),
                          k_rows: kk[k_rows] * jnp.exp2(anchor - cum[k_rows])}
                mixed = jnp.concatenate([halves[lower], halves[upper]], axis=0).astype(bf16)
            else:
                is_query = ((ridx >> shift) & 1) == query_half
                mixed = (jnp.where(is_query, qs, kk) * jnp.exp2(level_exponent(s, ridx))).astype(bf16)
            pair = jnp.logical_and((differ >> shift) == 1, ordered)
            for hd, hs in enumerate(heads):
                intra[hd] = jnp.where(pair, _dot_nt(mixed[:, hs], mixed[:, hs]), intra[hd])
            s *= 2
        return intra

    def finish(intra, q_dec, k_end, extra):
        for hd, hs in enumerate(heads):
            st = st_ref[hd]
            o = jnp.dot(jnp.concatenate([intra[hd].astype(bf16), q_dec[:, hs]], axis=1),
                        jnp.concatenate([vb[:, hs], st.T.astype(bf16)], axis=0), preferred_element_type=f32)
            o_ref[rows, hs] = o if extra is None else o + extra[:, hs]
            st_ref[hd] = st * carry[:, hs] + _dot_tn(vb[:, hs], k_end[:, hs])

    blk = HG_FAST_BLOCK
    anchors = [lo + (blk // 2 if rev else blk // 2 - 1) for lo in range(0, tc, blk)]

    def mid_split_blocks():
        q_mid, k_mid, q_dec, k_end = [], [], [], []
        for lo, a in zip(range(0, tc, blk), anchors):
            rel = cum[lo:lo + blk] - cum[a:a + 1]
            q_mid.append(qs[lo:lo + blk] * jnp.exp2(rel))
            k_mid.append(kk[lo:lo + blk] * jnp.exp2(-rel))
            q_dec.append(q_mid[-1] * jnp.exp2(cum[a:a + 1]))
            k_end.append(k_mid[-1] * jnp.exp2(total - cum[a:a + 1]))
        q_mid, k_mid, q_dec, k_end = (jnp.concatenate(p, axis=0).astype(bf16) for p in (q_mid, k_mid, q_dec, k_end))
        same_block = (differ >> (blk.bit_length() - 1)) == 0
        keep = jnp.logical_and(same_block, (row <= col) if rev else (row >= col))
        intra = [jnp.where(keep, _dot_nt(q_mid[:, hs], k_mid[:, hs]), 0.0) for hs in heads]
        finish(add_levels(intra, blk), q_dec, k_end, None)

    def all_levels():
        intra = add_levels([jnp.zeros((tc, tc), f32)] * HG_HEADS, 1)
        q_dec = (qs * jnp.exp2(cum)).astype(bf16)
        k_end = (kk * jnp.exp2(total - cum)).astype(bf16)
        qk = qs * kk
        own = jnp.concatenate(
            [jnp.sum(qk[:, hs], axis=-1, keepdims=True) * vb[:, hs].astype(f32) for hs in heads], axis=-1)
        finish(intra, q_dec, k_end, own)

    return mid_split_blocks, all_levels


def _hgrn_kernel(slow_ref, *refs, has_init):
    if has_init:
        (qsf_ref, vf_ref, ff_ref, qsb_ref, vb_ref, fb_ref, s0f_ref, s0b_ref,
         of_ref, ob_ref, stf_ref, stb_ref) = refs
    else:
        (qsf_ref, vf_ref, ff_ref, qsb_ref, vb_ref, fb_ref,
         of_ref, ob_ref, sf_out_ref, sb_out_ref, stf_ref, stb_ref) = refs
    t = pl.program_id(1)
    n_seq = qsf_ref.shape[0]

    @pl.when(t == 0)
    def _():
        for b in range(n_seq):
            for hd in range(HG_HEADS):
                if has_init:
                    stf_ref[b, hd] = s0f_ref[b, hd].T
                    stb_ref[b, hd] = s0b_ref[b, hd].T
                else:
                    stf_ref[b, hd] = jnp.zeros((HG_HEAD_DIM, HG_HEAD_DIM), f32)
                    stb_ref[b, hd] = jnp.zeros((HG_HEAD_DIM, HG_HEAD_DIM), f32)

    chunks = [slice(lo, lo + HG_TILE) for lo in range(0, qsf_ref.shape[1], HG_TILE)]
    scans = []
    for r_f, r_b in zip(chunks, reversed(chunks)):
        for b in range(n_seq):
            scans.append(_hgrn_direction(qsf_ref[b, r_f], vf_ref[b, r_f], ff_ref[b, r_f], stf_ref.at[b], of_ref.at[b],
                                         r_f, rev=False))
            scans.append(_hgrn_direction(qsb_ref[b, r_b], vb_ref[b, r_b], fb_ref[b, r_b], stb_ref.at[b], ob_ref.at[b],
                                         r_b, rev=True))
    slow = slow_ref[pl.program_id(0) * pl.num_programs(1) + t] != 0

    @pl.when(jnp.logical_not(slow))
    def _():
        for mid_split_blocks, _ in scans:
            mid_split_blocks()

    @pl.when(slow)
    def _():
        for _, all_levels in scans:
            all_levels()

    if not has_init:
        @pl.when(t == pl.num_programs(1) - 1)
        def _():
            for b in range(n_seq):
                for hd in range(HG_HEADS):
                    sf_out_ref[b, hd] = stf_ref[b, hd].T
                    sb_out_ref[b, hd] = stb_ref[b, hd].T


def _hgrn(qs, vh, f_fwd, f_bwd, span_fwd, span_bwd, init_states):
    n_seqs, length, _ = qs.shape
    step = min(HG_STEP, length)
    nt = length // step
    groups = n_seqs // HG_SEQS
    has_init = init_states is not None
    per_step = lambda span: jnp.max(span[:, :, 0].reshape(n_seqs, nt, -1), axis=-1) > HG_FAST_SPAN_LOG2
    slow = jnp.logical_or(per_step(span_fwd), per_step(span_bwd)[:, ::-1])
    slow = jnp.any(slow.reshape(groups, HG_SEQS, nt), axis=1).astype(jnp.int32).reshape(-1)
    fwd = pl.BlockSpec((HG_SEQS, step, HG_WIDTH), lambda g, t, _: (g, t, 0))
    bwd = pl.BlockSpec((HG_SEQS, step, HG_WIDTH), lambda g, t, _: (g, nt - 1 - t, 0))
    state = pl.BlockSpec((HG_SEQS, HG_HEADS, HG_HEAD_DIM, HG_HEAD_DIM), lambda g, t, _: (g, 0, 0, 0))
    in_specs = [fwd, fwd, fwd, bwd, bwd, bwd]
    args = [qs, vh, f_fwd, qs, vh, f_bwd]
    out_specs = [fwd, bwd]
    out_shape = [jax.ShapeDtypeStruct(qs.shape, f32)] * 2
    if has_init:
        in_specs += [state, state]
        args += list(init_states)
    else:
        out_specs += [state, state]
        out_shape += [jax.ShapeDtypeStruct((n_seqs, HG_HEADS, HG_HEAD_DIM, HG_HEAD_DIM), f32)] * 2
    return pl.pallas_call(
        functools.partial(_hgrn_kernel, has_init=has_init),
        grid_spec=pltpu.PrefetchScalarGridSpec(
            num_scalar_prefetch=1,
            grid=(groups, nt),
            in_specs=in_specs,
            out_specs=out_specs,
            scratch_shapes=[pltpu.VMEM((HG_SEQS, HG_HEADS, HG_HEAD_DIM, HG_HEAD_DIM), f32)] * 2),
        out_shape=out_shape,
        compiler_params=pltpu.CompilerParams(
            dimension_semantics=("arbitrary", "arbitrary"), vmem_limit_bytes=VMEM_LIMIT_BYTES),
        name="hgrn2",
    )(slow, *args)


def _attn_kernel(*refs, n_ctx, n_cast):
    kk_ref, vta_ref, bound_ref, fixed_ref = refs[-4:]
    o_ref, *cast_out = refs[len(refs) - 5 - n_cast:-4]
    inputs = refs[:len(refs) - 5 - n_cast]
    cast_in = inputs[len(inputs) - n_cast:]
    if n_ctx:
        (qa_ref, ka_ref, vt_ref, ckt_ref, cvt_ref, x_ref, of_ref, ob_ref, sg_ref, m_ref, wo_ref, hg_ref, qg_ref,
         g_ref, b_ref) = inputs[:len(inputs) - n_cast]
    else:
        (qa_ref, ka_ref, vt_ref, x_ref, of_ref, ob_ref, sg_ref, m_ref, wo_ref, hg_ref, qg_ref,
         g_ref, b_ref) = inputs[:len(inputs) - n_cast]
    for src_ref, dst_ref in zip(cast_in, cast_out):
        dst_ref[...] = src_ref[...].astype(bf16)
    tq = qa_ref.shape[1]
    n_keys = kk_ref.shape[0]

    @pl.when(pl.program_id(1) == 0)
    def _():
        def fill_keys(lo, k):
            n = k.shape[0]
            low = lax.broadcasted_iota(jnp.int32, k.shape, 1) < HEAD_DIM
            k_sw = pltpu.roll(k, HEAD_DIM, 1)
            kk_ref[lo:lo + n, 0:LANES] = jnp.where(low, k, k_sw).astype(bf16)
            kk_ref[lo:lo + n, LANES:2 * LANES] = jnp.where(low, k_sw, k).astype(bf16)

        def fill_values(lo, vt):
            n = vt.shape[1]
            ones_row = jnp.where(lax.broadcasted_iota(jnp.int32, (V_ROWS - HEAD_DIM, n), 0) == 0, 1.0, 0.0)
            for kv in range(N_KV_HEADS):
                vta_ref[kv * V_ROWS:kv * V_ROWS + HEAD_DIM, lo:lo + n] = (
                    vt[kv * HEAD_DIM:(kv + 1) * HEAD_DIM].astype(bf16))
                vta_ref[kv * V_ROWS + HEAD_DIM:(kv + 1) * V_ROWS, lo:lo + n] = ones_row.astype(bf16)

        if n_ctx:
            fill_keys(0, ckt_ref[0].T)
            fill_values(0, cvt_ref[0])
        fill_keys(n_ctx, ka_ref[0])
        fill_values(n_ctx, vt_ref[0])

        gain_max = jnp.max(jnp.abs(qg_ref[...]), axis=1, keepdims=True)
        all_small = None
        for kv in range(N_KV_HEADS):
            kt = kk_ref[:, kv * LANES:(kv + 1) * LANES].astype(f32)
            k_norm2 = jnp.max(0.5 * jnp.sum(kt * kt, axis=1, keepdims=True), axis=0, keepdims=True)
            bound = (LOG2_E * SCORE_BOUND_SLACK) * gain_max * jnp.sqrt(k_norm2)
            bound_ref[kv] = bound[0, 0]
            small = jnp.where(bound <= SCORE_BOUND_LIMIT, 1, 0)
            all_small = small if all_small is None else all_small * small
        fixed_ref[0] = all_small[0, 0]

    ts = min(Q_SUB, tq)
    low = lax.broadcasted_iota(jnp.int32, (ts, LANES), 1) < HEAD_DIM
    kc = min(KEY_CHUNK, n_keys)
    pairs_per_kv = N_HEADS // N_KV_HEADS // 2

    def attend(rows, fixed_shift):
        def masked_pair(tile):
            qp = qa_ref[0, rows, tile * LANES:(tile + 1) * LANES]
            zero = jnp.zeros_like(qp)
            return jnp.concatenate([jnp.where(low, qp, zero), jnp.where(low, zero, qp)], axis=0)

        q_pairs = [masked_pair(tile) for tile in range(N_HEADS // 2)]
        work = [(tile, lo) for tile in range(N_HEADS // 2) for lo in range(0, n_keys, kc)]

        def scores(tile, lo):
            kv = tile // pairs_per_kv
            return _dot_nt(kk_ref[lo:lo + kc, kv * LANES:(kv + 1) * LANES], q_pairs[tile])

        heads_t = []
        st_next = scores(*work[0])
        m = acc = None
        for i, (tile, lo) in enumerate(work):
            st = st_next
            if i + 1 < len(work):
                st_next = scores(*work[i + 1])
            kv = tile // pairs_per_kv
            values_t = vta_ref[kv * V_ROWS:(kv + 1) * V_ROWS, lo:lo + kc]
            if fixed_shift:
                e = jnp.exp2(st - bound_ref[kv]).astype(bf16)
                pv = jnp.dot(values_t, e, preferred_element_type=f32)
                acc = pv if acc is None else acc + pv
            else:
                m_chunk = jnp.max(st, axis=0, keepdims=True)
                m_new = m_chunk if m is None else jnp.maximum(m, m_chunk)
                e = jnp.exp2(st - m_new).astype(bf16)
                pv = jnp.dot(values_t, e, preferred_element_type=f32)
                acc = pv if acc is None else acc * jnp.exp2(m - m_new) + pv
                m = m_new
            if lo + kc == n_keys:
                on = (acc[0:HEAD_DIM] * (1.0 / acc[HEAD_DIM:HEAD_DIM + 1])).astype(bf16)
                heads_t += [on[:, :ts], on[:, ts:]]
                m = acc = None
        return jnp.concatenate(heads_t, axis=0)

    def project(rows, o_att_t):
        o_sum = of_ref[0, rows] + ob_ref[0, rows]
        normed = []
        for hd in range(HG_HEADS):
            oh = o_sum[:, hd * HG_HEAD_DIM:(hd + 1) * HG_HEAD_DIM]
            ms = jnp.mean(oh * oh, axis=-1, keepdims=True)
            normed.append(oh * lax.rsqrt(ms + RMS_EPS))
        o_hg = (jnp.concatenate(normed, axis=-1) * hg_ref[...] * sg_ref[0, rows]).astype(bf16)
        y = (jnp.dot(o_hg, wo_ref[0:HG_WIDTH, :], preferred_element_type=f32)
             + _dot_tn(o_att_t, wo_ref[HG_WIDTH:HG_WIDTH + ATT_WIDTH, :]))
        r = ALPHA * x_ref[0, rows] + m_ref[0, 5:6, :] * y
        o_ref[0, rows] = _layer_norm(r, g_ref[1:2, :], b_ref[1:2, :])

    def run(fixed_shift):
        for lo in range(0, tq, ts):
            rows = slice(lo, lo + ts)
            project(rows, attend(rows, fixed_shift))

    use_bound = fixed_ref[0] != 0
    pl.when(use_bound)(functools.partial(run, True))
    pl.when(jnp.logical_not(use_bound))(functools.partial(run, False))


def _attn(qa, ka, vt, ctx_kv_t, x, o_f, o_b, sg, mod, mod_group, w_out, hg_gain, q_gain, ln_g, ln_b, to_cast=()):
    groups, length, _ = qa.shape
    tq = min(Q_TILE, length)
    nt = length // tq
    n_ctx = 0 if ctx_kv_t is None else ctx_kv_t[0].shape[2]
    n_keys = n_ctx + length
    tok = lambda width: pl.BlockSpec((1, tq, width), lambda g, t: (g, t, 0))
    whole_t = lambda n: pl.BlockSpec((1, KV_WIDTH, n), lambda g, t: (g, 0, 0))
    in_specs = [tok(ATT_WIDTH), pl.BlockSpec((1, length, KV_WIDTH), lambda g, t: (g, 0, 0)), whole_t(length)]
    args = [qa, ka, vt]
    if n_ctx:
        in_specs += [whole_t(n_ctx), whole_t(n_ctx)]
        args += list(ctx_kv_t)
    in_specs += [
        tok(D_MODEL), tok(HG_WIDTH), tok(HG_WIDTH), tok(HG_WIDTH),
        pl.BlockSpec((1, N_MOD, D_MODEL), lambda g, t: (mod_group(g), 0, 0)),
        _resident((D_MODEL, D_MODEL)),
        _resident((1, HG_WIDTH)),
        _resident((1, ATT_WIDTH)),
        _resident((3, D_MODEL)),
        _resident((3, D_MODEL)),
    ]
    args += [x, o_f, o_b, sg, mod, w_out, hg_gain, q_gain, ln_g, ln_b]
    cast_specs, cast_shapes = _cast_specs(to_cast, nt, groups * nt)
    in_specs += cast_specs
    args += list(to_cast)
    out_specs = [tok(D_MODEL)] + cast_specs
    out_shape = [jax.ShapeDtypeStruct(x.shape, f32)] + cast_shapes
    return pl.pallas_call(
        functools.partial(_attn_kernel, n_ctx=n_ctx, n_cast=len(to_cast)),
        grid=(groups, nt),
        in_specs=in_specs,
        out_specs=out_specs,
        out_shape=out_shape,
        scratch_shapes=[pltpu.VMEM((n_keys, N_KV_HEADS * LANES), bf16),
                        pltpu.VMEM((N_KV_HEADS * V_ROWS, n_keys), bf16),
                        pltpu.SMEM((N_KV_HEADS,), f32),
                        pltpu.SMEM((1,), jnp.int32)],
        compiler_params=pltpu.CompilerParams(
            dimension_semantics=("arbitrary", "arbitrary"), vmem_limit_bytes=VMEM_LIMIT_BYTES),
        name="attn_out",
    )(*args)


def _rope_tables(n_tokens):
    half = HEAD_DIM // 2
    t = jnp.arange(n_tokens)
    inv = ROPE_THETA ** (-jnp.arange(0, half, 2, dtype=f32) / half)
    ang_row = (t // GRID_W).astype(f32)[:, None] * inv
    ang_col = (t % GRID_W).astype(f32)[:, None] * inv
    cos = jnp.concatenate([jnp.cos(ang_row)] * 2 + [jnp.cos(ang_col)] * 2, axis=-1)
    sin = jnp.concatenate([-jnp.sin(ang_row), jnp.sin(ang_row), -jnp.sin(ang_col), jnp.sin(ang_col)], axis=-1)
    return jnp.tile(cos, (1, LANES // HEAD_DIM)), jnp.tile(sin, (1, LANES // HEAD_DIM))


def kernel(x_prompt, x_sample, cache_k, cache_v, state_hgrn_fwd, state_hgrn_bwd, c, c_ctx, w_mod, b_mod,
           w_ffn1_in, w_ffn1_out, w_ffn2_in, w_ffn2_out, w_in, w_out, q_norm_g, k_norm_g, hg_norm_g,
           lb_logits_fwd, lb_logits_bwd, ln_g, ln_b):
    assert w_mod.shape[0] == DEPTH and lb_logits_fwd.shape[0] == DEPTH + 1
    batch, seq, _ = x_prompt.shape
    dec_batch, dec_seq, _ = x_sample.shape
    past = cache_k.shape[2]

    ctx_row = dec_batch
    rows = -(-(dec_batch + 1) // BF16_SUBLANES) * BF16_SUBLANES
    cvecs = jnp.concatenate([c, c_ctx[None, :], jnp.zeros((rows - dec_batch - 1, D_MODEL), f32)], axis=0)
    mod, w1u, w1d = _modulation(cvecs, w_mod[0], b_mod[0], to_cast=(w_ffn1_in[0], w_ffn1_out[0]))
    mod = mod.reshape(rows, N_MOD, D_MODEL)
    gains = ln_g[0], ln_b[0]
    q_gain = jnp.tile(q_norm_g[0], N_HEADS).reshape(1, ATT_WIDTH)
    k_gain = jnp.tile(k_norm_g[0], N_KV_HEADS).reshape(1, KV_WIDTH)
    hg_gain = hg_norm_g[0].reshape(1, HG_WIDTH)
    head_of = jnp.arange(ATT_WIDTH) // HEAD_DIM
    ones_bd = (head_of[:, None] == head_of[None, :]).astype(bf16)

    def mixer_sublayer(x, mod_group, rope_tables, ctx_kv_t, init_states, hg_groups, to_cast=()):
        shape = x.shape
        seq_len = shape[0] * shape[1] // hg_groups
        per_seq = lambda a: a.reshape(hg_groups, seq_len, a.shape[-1])
        proj = _proj(x, mod, mod_group, w_in_b, lb_logits_fwd, lb_logits_bwd, q_gain, k_gain, ones_bd, rope_tables,
                     seq_len)
        qs, vh, f_f, f_b, sg, qa, ka = map(per_seq, proj[:7])
        vt, span_f, span_b = proj[7], proj[8].reshape(hg_groups, -1, LANES), proj[9].reshape(hg_groups, -1, LANES)
        scans = _hgrn(qs, vh, f_f, f_b, span_f, span_b, init_states)
        x, *cast = _attn(qa, ka, vt, ctx_kv_t, per_seq(x), scans[0], scans[1], sg, mod, mod_group, w_out_b, hg_gain,
                         q_gain, *gains, to_cast=to_cast)
        return x.reshape(shape), proj[10:], vt, scans[2:], cast

    from_cache = lambda t: t[:, 0].transpose(0, 2, 3, 1).reshape(dec_batch, KV_WIDTH, past)
    init_states = (state_hgrn_fwd[:, 0], state_hgrn_bwd[:, 0])
    latent_group = lambda g: g
    x_latent, w_in_b, w_out_b = _ffn(x_sample, mod, latent_group, w1u, w1d, *gains, mod_base=0, ln_row=0,
                                     to_cast=(w_in[0], w_out[0]))
    x_latent, _, _, _, (w2u, w2d) = mixer_sublayer(
        x_latent, latent_group, _rope_tables(dec_seq), (from_cache(cache_k), from_cache(cache_v)), init_states,
        dec_batch, to_cast=(w_ffn2_in[0], w_ffn2_out[0]))

    ctx_group = lambda g: ctx_row
    x_ctx = _ffn(x_prompt.reshape(1, batch * seq, D_MODEL), mod, ctx_group, w1u, w1d, *gains, mod_base=0, ln_row=0)
    x_ctx, (kt_new,), vt_new, states, _ = mixer_sublayer(x_ctx, ctx_group, None, None, None, batch)
    y_sample = _ffn(x_latent, mod, latent_group, w2u, w2d, *gains, mod_base=6, ln_row=2)
    y_prompt = _ffn(x_ctx, mod, ctx_group, w2u, w2d, *gains, mod_base=6, ln_row=2)
    y_prompt = y_prompt.reshape(batch, seq, D_MODEL)
    to_cache = lambda t: t.reshape(batch, DEPTH, N_KV_HEADS, HEAD_DIM, seq).transpose(0, 1, 4, 2, 3)
    new_cache_k, new_cache_v = to_cache(kt_new), to_cache(vt_new)
    new_state_fwd = states[0].reshape(batch, DEPTH, HG_HEADS, HG_HEAD_DIM, HG_HEAD_DIM)
    new_state_bwd = states[1].reshape(batch, DEPTH, HG_HEADS, HG_HEAD_DIM, HG_HEAD_DIM)

    return (y_prompt, y_sample, new_cache_k, new_cache_v, new_state_fwd, new_state_bwd)
```
